```python
import math
import jax, jax.numpy as jnp
from jax import lax
import numpy as np

D_MODEL = 1024
BATCH = 16
SEQ = 256
DEPTH = 4
DEC_BATCH = 2
DEC_SEQ = 2048
PAST_LEN = 512

GRID_W = 64
HEAD_DIM = 64
NA_HEADS = 4
NA_WIDTH = NA_HEADS * HEAD_DIM
NA_WIN_ROWS = 8
NA_WIN_COLS = 16
NA_QBLK_COLS = 16
NA_KBLK_COLS = NA_QBLK_COLS + NA_WIN_COLS
SC_WIDTH = 256
DA_HEADS = 4
DA_DIM = 64
DA_QK_WIDTH = 2 * DA_HEADS * DA_DIM
DA_V_WIDTH = DA_HEADS * 2 * DA_DIM
MIX_WIDTH = NA_WIDTH + SC_WIDTH + DA_V_WIDTH
IN_WIDTH = 3 * NA_WIDTH + 3 * SC_WIDTH + 2 * DA_QK_WIDTH + DA_V_WIDTH
IN_SPLITS = (NA_WIDTH, 2 * NA_WIDTH, 3 * NA_WIDTH,
             3 * NA_WIDTH + SC_WIDTH, 3 * NA_WIDTH + 2 * SC_WIDTH, 3 * NA_WIDTH + 3 * SC_WIDTH,
             3 * NA_WIDTH + 3 * SC_WIDTH + DA_QK_WIDTH, 3 * NA_WIDTH + 3 * SC_WIDTH + 2 * DA_QK_WIDTH)
QBLK = 128
ROPE_BASE = 10000.0
N_EXPERTS = 16
N_EXPERT_GROUPS = 4
EXPERTS_PER_GROUP = N_EXPERTS // N_EXPERT_GROUPS
TOP_K = 2
D_EXPERT = 256
DEEPNORM_ALPHA = (2 * DEPTH) ** 0.25
DEEPNORM_BETA = (8 * DEPTH) ** -0.25
LN_EPS = 1e-5

kernel_name = "hybrid_na_conv_diffattn_moe_diffusion_step"


def _layer_norm(x):
    xf = x.astype(jnp.float32)
    mu = jnp.mean(xf, -1, keepdims=True)
    var = jnp.mean(jnp.square(xf - mu), -1, keepdims=True)
    return ((xf - mu) * lax.rsqrt(var + LN_EPS)).astype(x.dtype)


def _rms_norm(x, g):
    xf = x.astype(jnp.float32)
    return (xf * lax.rsqrt(jnp.mean(xf * xf, -1, keepdims=True) + LN_EPS)).astype(x.dtype) * g


def _post_norm(x, y, g, b):
    return _layer_norm(DEEPNORM_ALPHA * x + y) * g + b


def _modulation(cond, w_mod, b_mod):
    m = jnp.einsum('nd,de->ne', jax.nn.silu(cond), w_mod) + b_mod
    return tuple(t[:, None, :] for t in jnp.split(m, 6, axis=-1))


def _in_proj(h, w):
    B, L, _ = h.shape
    u = jnp.einsum('bld,de->ble', h, w)
    na_q, na_k, na_v, sc_b, sc_c, sc_x, da_q, da_k, da_v = jnp.split(u, IN_SPLITS, axis=-1)
    na = lambda t: t.reshape(B, L, NA_HEADS, HEAD_DIM)
    return (na(na_q), na(na_k), na(na_v), sc_b, sc_c, sc_x,
            da_q.reshape(B, L, 2 * DA_HEADS, DA_DIM), da_k.reshape(B, L, 2 * DA_HEADS, DA_DIM),
            da_v.reshape(B, L, DA_HEADS, 2 * DA_DIM))


def _rope_1d(x, pos):
    half = x.shape[-1] // 2
    inv_freq = ROPE_BASE ** (-np.arange(half, dtype=np.float32) / half)
    ang = pos.astype(np.float32)[:, None] * inv_freq[None, :]
    cos = jnp.asarray(np.cos(ang))[:, None, :].astype(x.dtype)
    sin = jnp.asarray(np.sin(ang))[:, None, :].astype(x.dtype)
    x1, x2 = x[..., :half], x[..., half:]
    return jnp.concatenate([x1 * cos - x2 * sin, x2 * cos + x1 * sin], axis=-1)


def _axial_rope(x):
    S = x.shape[1]
    t = np.arange(S)
    half = x.shape[-1] // 2
    return jnp.concatenate([_rope_1d(x[..., :half], t // GRID_W),
                            _rope_1d(x[..., half:], t % GRID_W)], axis=-1)


def _query_blocks(q):
    B, L = q.shape[:2]
    return jnp.moveaxis(q.reshape(B, L // QBLK, QBLK, *q.shape[2:]), 1, 0)


def _merge_blocks(o):
    nb, B, qb = o.shape[:3]
    return jnp.moveaxis(o, 0, 1).reshape(B, nb * qb, *o.shape[3:])


def _softmax_attn(q, k, v):
    scale = q.shape[-1] ** -0.5
    def one(qb):
        s = jnp.einsum('bqhd,bkhd->bhqk', qb, k).astype(jnp.float32) * scale
        p = jax.nn.softmax(s, axis=-1).astype(v.dtype)
        return jnp.einsum('bhqk,bkhd->bqhd', p, v)
    o = _merge_blocks(lax.map(one, _query_blocks(q)))
    return o.reshape(o.shape[0], o.shape[1], -1)


def _neighbourhood_attn(q, k, v, ctx_k, ctx_v, rel_bias):
    B, S, H, dh = q.shape
    rows = S // GRID_W
    wr = min(NA_WIN_ROWS, rows)
    ncb = GRID_W // NA_QBLK_COLS
    qr = np.arange(rows)
    key_rows = np.clip(qr - wr // 2, 0, rows - wr)[:, None] + np.arange(wr)
    band = np.clip(np.arange(ncb) * NA_QBLK_COLS - NA_WIN_COLS // 2, 0, GRID_W - NA_KBLK_COLS)
    key_cols = band[:, None] + np.arange(NA_KBLK_COLS)
    qc = np.arange(GRID_W).reshape(ncb, NA_QBLK_COLS)
    win0 = np.clip(qc - NA_WIN_COLS // 2, 0, GRID_W - NA_WIN_COLS)
    in_win = (key_cols[:, None, :] >= win0[:, :, None]) & (key_cols[:, None, :] < win0[:, :, None] + NA_WIN_COLS)
    d_row = key_rows - qr[:, None] + NA_WIN_ROWS - 1
    d_col = np.clip(key_cols[:, None, :] - qc[:, :, None], 1 - NA_WIN_COLS, NA_WIN_COLS - 1) + NA_WIN_COLS - 1
    n_loc = wr * NA_KBLK_COLS
    bias = rel_bias.astype(jnp.float32)[:, d_row[:, None, None, :, None], d_col[None, :, :, None, :]]
    bias = jnp.where(in_win[None, None, :, :, None, :], bias, -jnp.inf)
    bias = bias.reshape(H, rows, ncb, NA_QBLK_COLS, n_loc)
    qg = q.reshape(B, rows, ncb, NA_QBLK_COLS, H, dh)
    ridx = key_rows[:, None, :, None]
    cidx = key_cols[None, :, None, :]
    kg = k.reshape(B, rows, GRID_W, H, dh)[:, ridx, cidx].reshape(B, rows, ncb, n_loc, H, dh)
    vg = v.reshape(B, rows, GRID_W, H, dh)[:, ridx, cidx].reshape(B, rows, ncb, n_loc, H, dh)
    scale = dh ** -0.5
    s_loc = jnp.einsum('brcqhd,brckhd->bhrcqk', qg, kg).astype(jnp.float32) * scale + bias
    s_ctx = jnp.einsum('brcqhd,bkhd->bhrcqk', qg, ctx_k).astype(jnp.float32) * scale
    p = jax.nn.softmax(jnp.concatenate([s_loc, s_ctx], axis=-1), axis=-1).astype(v.dtype)
    o = (jnp.einsum('bhrcqk,brckhd->brcqhd', p[..., :n_loc], vg)
         + jnp.einsum('bhrcqk,bkhd->brcqhd', p[..., n_loc:], ctx_v))
    return o.reshape(B, S, H * dh)


def _diff_attn(q, k, v, lam, lam_init, gain):
    B = q.shape[0]
    scale = DA_DIM ** -0.5
    def one(qb):
        s = jnp.einsum('bqhd,bkhd->bhqk', qb, k).astype(jnp.float32) * scale
        p = jax.nn.softmax(s, axis=-1).reshape(B, DA_HEADS, 2, QBLK, -1)
        a = (p[:, :, 0] - lam * p[:, :, 1]).astype(v.dtype)
        return jnp.einsum('bhqk,bkhd->bqhd', a, v)
    o = _merge_blocks(lax.map(one, _query_blocks(q)))
    o = _rms_norm(o, gain) * (1.0 - lam_init)
    return o.reshape(o.shape[0], o.shape[1], DA_V_WIDTH)


def _short_conv(u, w, b):
    up = jnp.pad(u, ((0, 0), (1, 1), (0, 0)))
    return up[:, :-2] * w[0] + up[:, 1:-1] * w[1] + up[:, 2:] * w[2] + b


def _moe(h, w_router, b_router, w1, w3, w2):
    B, L, _ = h.shape
    scores = jax.nn.sigmoid(jnp.einsum('bld,de->ble', h, w_router).astype(jnp.float32))
    grouped = (scores + b_router.astype(jnp.float32)).reshape(B, L, N_EXPERT_GROUPS, EXPERTS_PER_GROUP)
    group_score = jnp.sum(lax.top_k(grouped, TOP_K)[0], axis=-1)
    sel_group = jnp.argmax(group_score, axis=-1)
    in_group = (jnp.arange(N_EXPERT_GROUPS) == sel_group[..., None])[..., None]
    masked = jnp.where(in_group, grouped, -jnp.inf).reshape(B, L, N_EXPERTS)
    _, top_e = lax.top_k(masked, TOP_K)
    top_w = jnp.take_along_axis(scores, top_e, axis=-1)
    top_w = top_w / jnp.sum(top_w, -1, keepdims=True)
    gates = jnp.sum(jax.nn.one_hot(top_e, N_EXPERTS, dtype=jnp.float32) * top_w[..., None], axis=-2).astype(h.dtype)
    a = jnp.einsum('bld,edf->blef', h, w1)
    g = jnp.einsum('bld,edf->blef', h, w3)
    hid = jax.nn.silu(a) * g * gates[..., None]
    return jnp.einsum('blef,efd->bld', hid, w2)


def _layer(x, cond, attn_fn, w_mod, b_mod, w_in, sc_conv_w, sc_conv_b, w_out, ln1_g, ln1_b,
           w_router, b_router, w1, w3, w2, ln2_g, ln2_b):
    shift1, scale1, gate1, shift2, scale2, gate2 = _modulation(cond, w_mod, b_mod)
    h = _layer_norm(x) * (1 + scale1) + shift1
    na_q, na_k, na_v, sc_b, sc_c, sc_x, da_q, da_k, da_v = _in_proj(h, w_in)
    o_na, o_da, ctx_state = attn_fn(na_q, na_k, na_v, da_q, da_k, da_v)
    o_sc = sc_b * _short_conv(sc_c * sc_x, sc_conv_w, sc_conv_b)
    y = jnp.einsum('blm,md->bld', jnp.concatenate([o_na, o_sc, o_da], axis=-1), w_out)
    x = _post_norm(x, gate1 * y, ln1_g, ln1_b)
    h = _layer_norm(x) * (1 + scale2) + shift2
    x = _post_norm(x, gate2 * _moe(h, w_router, b_router, w1, w3, w2), ln2_g, ln2_b)
    return x, ctx_state


def setup_inputs(seed: int = 0) -> dict:
    key = jax.random.key(seed)
    def nrm(i, shape, std):
        return jax.random.normal(jax.random.fold_in(key, i), shape, jnp.float32) * std
    D = D_MODEL
    return {
        "x_prompt": nrm(0, (BATCH, SEQ, D), 1.0),
        "x_sample": nrm(1, (DEC_BATCH, DEC_SEQ, D), 1.0),
        "cache_na_k": nrm(2, (DEC_BATCH, DEPTH, PAST_LEN, NA_HEADS, HEAD_DIM), 1.0),
        "cache_na_v": nrm(3, (DEC_BATCH, DEPTH, PAST_LEN, NA_HEADS, HEAD_DIM), 1.0),
        "cache_da_k": nrm(4, (DEC_BATCH, DEPTH, PAST_LEN, 2 * DA_HEADS, DA_DIM), 1.0),
        "cache_da_v": nrm(5, (DEC_BATCH, DEPTH, PAST_LEN, DA_HEADS, 2 * DA_DIM), 1.0),
        "c": nrm(6, (DEC_BATCH, D), 1.0),
        "c_ctx": nrm(7, (D,), 1.0),
        "w_mod": nrm(8, (DEPTH, D, 6 * D), 0.5 * D ** -0.5),
        "b_mod": nrm(9, (DEPTH, 6 * D), 0.02),
        "w_in": nrm(10, (DEPTH, D, IN_WIDTH), D ** -0.5),
        "na_rel_bias": nrm(11, (DEPTH, NA_HEADS, 2 * NA_WIN_ROWS - 1, 2 * NA_WIN_COLS - 1), 0.1),
        "sc_conv_w": nrm(12, (DEPTH, 3, SC_WIDTH), 3 ** -0.5),
        "sc_conv_b": nrm(13, (DEPTH, SC_WIDTH), 0.02),
        "da_lambda": nrm(14, (DEPTH, 4, DA_DIM), 0.1),
        "da_norm_g": 1.0 + nrm(15, (DEPTH, 2 * DA_DIM), 0.02),
        "w_out": nrm(16, (DEPTH, MIX_WIDTH, D), DEEPNORM_BETA * MIX_WIDTH ** -0.5),
        "ln1_g": 1.0 + nrm(17, (DEPTH, D), 0.02),
        "ln1_b": nrm(18, (DEPTH, D), 0.02),
        "w_router": nrm(19, (D, N_EXPERTS), D ** -0.5),
        "b_router": nrm(20, (N_EXPERTS,), 0.01),
        "moe_w1": nrm(21, (DEPTH, N_EXPERTS, D, D_EXPERT), D ** -0.5),
        "moe_w3": nrm(22, (DEPTH, N_EXPERTS, D, D_EXPERT), D ** -0.5),
        "moe_w2": nrm(23, (DEPTH, N_EXPERTS, D_EXPERT, D), DEEPNORM_BETA * D_EXPERT ** -0.5),
        "ln2_g": 1.0 + nrm(24, (DEPTH, D), 0.02),
        "ln2_b": nrm(25, (DEPTH, D), 0.02),
    }


def reference(x_prompt, x_sample, cache_na_k, cache_na_v, cache_da_k, cache_da_v, c, c_ctx,
              w_mod, b_mod, w_in, na_rel_bias, sc_conv_w, sc_conv_b, da_lambda, da_norm_g, w_out,
              ln1_g, ln1_b, w_router, b_router, moe_w1, moe_w3, moe_w2, ln2_g, ln2_b):
    xp, xs = x_prompt, x_sample
    na_k_list, na_v_list, da_k_list, da_v_list = [], [], [], []
    for l in range(DEPTH):
        lam_init = 0.8 - 0.6 * math.exp(-0.3 * l)
        lp = da_lambda[l].astype(jnp.float32)
        lam = jnp.exp(jnp.sum(lp[0] * lp[1])) - jnp.exp(jnp.sum(lp[2] * lp[3])) + lam_init
        params = (w_mod[l], b_mod[l], w_in[l], sc_conv_w[l], sc_conv_b[l], w_out[l], ln1_g[l], ln1_b[l],
                  w_router, b_router, moe_w1[l], moe_w3[l], moe_w2[l], ln2_g[l], ln2_b[l])

        def ctx_attn(na_q, na_k, na_v, da_q, da_k, da_v):
            o_na = _softmax_attn(na_q, na_k, na_v)
            o_da = _diff_attn(da_q, da_k, da_v, lam, lam_init, da_norm_g[l])
            return o_na, o_da, (na_k, na_v, da_k, da_v)

        def lat_attn(na_q, na_k, na_v, da_q, da_k, da_v):
            o_na = _neighbourhood_attn(na_q, na_k, na_v, cache_na_k[:, l], cache_na_v[:, l], na_rel_bias[l])
            k_all = jnp.concatenate([_axial_rope(da_k), cache_da_k[:, l]], axis=1)
            v_all = jnp.concatenate([da_v, cache_da_v[:, l]], axis=1)
            o_da = _diff_attn(_axial_rope(da_q), k_all, v_all, lam, lam_init, da_norm_g[l])
            return o_na, o_da, None

        xp, (nk, nv, dk, dv) = _layer(xp, c_ctx[None, :], ctx_attn, *params)
        xs, _ = _layer(xs, c, lat_attn, *params)
        na_k_list.append(nk)
        na_v_list.append(nv)
        da_k_list.append(dk)
        da_v_list.append(dv)
    new_na_k = jnp.stack(na_k_list, axis=1)
    new_na_v = jnp.stack(na_v_list, axis=1)
    new_da_k = jnp.stack(da_k_list, axis=1)
    new_da_v = jnp.stack(da_v_list, axis=1)
    return (xp, xs, new_na_k, new_na_v, new_da_k, new_da_v)
```

```python
import functools
import math

import numpy as np
import jax
import jax.numpy as jnp
from jax import lax
from jax.experimental import pallas as pl
from jax.experimental.pallas import tpu as pltpu

F32 = jnp.float32
BF16 = jnp.bfloat16

D_MODEL = 1024
BATCH = 16
SEQ = 256
DEPTH = 4
DEC_BATCH = 2
DEC_SEQ = 2048
PAST_LEN = 512
GRID_W = 64
GRID_ROWS = DEC_SEQ // GRID_W
HEAD_DIM = 64
NA_HEADS = 4
NA_WIDTH = NA_HEADS * HEAD_DIM
NA_WIN_ROWS = 8
NA_WIN_COLS = 16
SC_WIDTH = 256
DA_HEADS = 4
DA_DIM = 64
DA_QK_WIDTH = 2 * DA_HEADS * DA_DIM
DA_V_WIDTH = DA_HEADS * 2 * DA_DIM
MIX_WIDTH = NA_WIDTH + SC_WIDTH + DA_V_WIDTH
IN_WIDTH = 3 * NA_WIDTH + 3 * SC_WIDTH + 2 * DA_QK_WIDTH + DA_V_WIDTH
OFF_NA_Q = 0
OFF_NA_K = NA_WIDTH
OFF_NA_V = 2 * NA_WIDTH
OFF_SC_B = 3 * NA_WIDTH
OFF_SC_C = OFF_SC_B + SC_WIDTH
OFF_SC_X = OFF_SC_C + SC_WIDTH
OFF_DA_Q = OFF_SC_X + SC_WIDTH
OFF_DA_K = OFF_DA_Q + DA_QK_WIDTH
OFF_DA_V = OFF_DA_K + DA_QK_WIDTH
ROPE_BASE = 10000.0
N_EXPERTS = 16
N_GROUPS = 4
PER_GROUP = N_EXPERTS // N_GROUPS
D_EXPERT = 256
ALPHA = (2 * DEPTH) ** 0.25
LN_EPS = 1e-5
QK_SCALE = HEAD_DIM ** -0.5

N_COND = 8
TM = 256
TM_MOE = 512
QT = 256
QT_ROWS = QT // GRID_W
NA_KEY_ROWS = 12
NA_KEYS = NA_KEY_ROWS * GRID_W
VMEM_LIMIT = 56 * 1024 * 1024


def _dot(a, b):
    return jnp.dot(a, b, preferred_element_type=F32)


def _dot_nt(a, b):
    return lax.dot_general(a, b, (((1,), (1,)), ((), ())), preferred_element_type=F32)


def _ln(x):
    mu = jnp.mean(x, -1, keepdims=True)
    xc = x - mu
    var = jnp.mean(xc * xc, -1, keepdims=True)
    return xc * lax.rsqrt(var + LN_EPS)


def _params(*sem):
    return pltpu.CompilerParams(dimension_semantics=sem, vmem_limit_bytes=VMEM_LIMIT)


def _mod_kernel(cond_ref, w_ref, b_ref, o_ref):
    c = cond_ref[...]
    s = (c * jax.nn.sigmoid(c)).astype(BF16)
    o_ref[...] = _dot(s, w_ref[...].astype(BF16)) + b_ref[...]


def _modulation(cond, w_mod, b_mod):
    tn = 1024
    return pl.pallas_call(
        _mod_kernel,
        grid=(DEPTH, 6 * D_MODEL // tn),
        in_specs=[
            pl.BlockSpec((N_COND, D_MODEL), lambda l, j: (0, 0)),
            pl.BlockSpec((None, D_MODEL, tn), lambda l, j: (l, 0, j)),
            pl.BlockSpec((None, 1, tn), lambda l, j: (l, 0, j)),
        ],
        out_specs=pl.BlockSpec((None, N_COND, tn), lambda l, j: (l, 0, j)),
        out_shape=jax.ShapeDtypeStruct((DEPTH, N_COND, 6 * D_MODEL), F32),
        compiler_params=_params("parallel", "parallel"),
        name="modulation",
    )(cond, w_mod, b_mod.reshape(DEPTH, 1, 6 * D_MODEL))


def _rope(t, cos, sin):
    lane = lax.broadcasted_iota(jnp.int32, t.shape, 1)
    first = (lane // 16) % 2 == 0
    n = t.shape[1]
    swapped = jnp.where(first, pltpu.roll(t, n - 16, 1), pltpu.roll(t, 16, 1))
    return t * cos + swapped * sin


def _inproj_kernel(*refs, latent):
    if latent:
        x_ref, mod_ref, w_ref, cos_ref, sin_ref, u_ref = refs
    else:
        x_ref, mod_ref, w_ref, u_ref, nak_ref, nav_ref, dak_ref, dav_ref = refs
    mod = mod_ref[...]
    h = _ln(x_ref[...]) * (1.0 + mod[1:2]) + mod[0:1]
    u = _dot(h.astype(BF16), w_ref[...])
    if latent:
        cos, sin = cos_ref[...], sin_ref[...]
        u_ref[:, :OFF_DA_Q] = u[:, :OFF_DA_Q].astype(BF16)
        u_ref[:, OFF_DA_Q:OFF_DA_K] = _rope(u[:, OFF_DA_Q:OFF_DA_K], cos, sin).astype(BF16)
        u_ref[:, OFF_DA_K:OFF_DA_V] = _rope(u[:, OFF_DA_K:OFF_DA_V], cos, sin).astype(BF16)
        u_ref[:, OFF_DA_V:] = u[:, OFF_DA_V:].astype(BF16)
    else:
        u_ref[...] = u.astype(BF16)
        nak_ref[...] = u[:, OFF_NA_K:OFF_NA_V]
        nav_ref[...] = u[:, OFF_NA_V:OFF_SC_B]
        dak_ref[...] = u[:, OFF_DA_K:OFF_DA_V]
        dav_ref[...] = u[:, OFF_DA_V:]


def _inproj(x, mod_l, w_in, l, latent, rope_tabs=None):
    n = x.shape[0]
    nt = n // TM
    per_batch = DEC_SEQ // TM
    if latent:
        mod_map = lambda i: (1 + i // per_batch, 0, 0)
    else:
        mod_map = lambda i: (0, 0, 0)
    in_specs = [
        pl.BlockSpec((TM, D_MODEL), lambda i: (i, 0)),
        pl.BlockSpec((None, 6, D_MODEL), mod_map),
        pl.BlockSpec((None, D_MODEL, IN_WIDTH), lambda i: (l, 0, 0)),
    ]
    args = [x, mod_l, w_in]
    out_specs = [pl.BlockSpec((TM, IN_WIDTH), lambda i: (i, 0))]
    out_shape = [jax.ShapeDtypeStruct((n, IN_WIDTH), BF16)]
    if latent:
        in_specs += [pl.BlockSpec((TM, DA_QK_WIDTH), lambda i: (i % per_batch, 0))] * 2
        args += list(rope_tabs)
    else:
        for width in (NA_WIDTH, NA_WIDTH, DA_QK_WIDTH, DA_V_WIDTH):
            out_specs.append(pl.BlockSpec((TM, width), lambda i: (i, 0)))
            out_shape.append(jax.ShapeDtypeStruct((n, width), F32))
    return pl.pallas_call(
        functools.partial(_inproj_kernel, latent=latent),
        grid=(nt,),
        in_specs=in_specs,
        out_specs=out_specs,
        out_shape=out_shape,
        compiler_params=_params("parallel"),
        name="inproj_latent" if latent else "inproj_context",
    )(*args)


def _lambda(lam_ref, lam_init):
    lp = lam_ref[...]
    s1 = jnp.sum(lp[0:1] * lp[1:2], axis=-1, keepdims=True)
    s2 = jnp.sum(lp[2:3] * lp[3:4], axis=-1, keepdims=True)
    return jnp.exp(s1) - jnp.exp(s2) + lam_init


def _softmax_pv(score_parts, value_parts):
    m = functools.reduce(jnp.maximum, [jnp.max(s, -1, keepdims=True) for s in score_parts])
    es = [jnp.exp(s - m) for s in score_parts]
    denom = functools.reduce(jnp.add, [jnp.sum(e, -1, keepdims=True) for e in es])
    o = functools.reduce(jnp.add, [_dot(e.astype(BF16), v) for e, v in zip(es, value_parts)])
    return o / denom


def _diff_head_norm(o1, o2, lam, gain, lam_init):
    o = o1 - lam * o2
    o = o * lax.rsqrt(jnp.mean(o * o, -1, keepdims=True) + LN_EPS)
    return o * gain * (1.0 - lam_init)


def _gated_conv(b, v, v_prev_row, v_next_row, w, bias):
    n = v.shape[0]
    row = lax.broadcasted_iota(jnp.int32, v.shape, 0)
    prev = jnp.where(row == 0, v_prev_row, pltpu.roll(v, 1, 0))
    nxt = jnp.where(row == n - 1, v_next_row, pltpu.roll(v, n - 1, 0))
    return b * (prev * w[0:1] + v * w[1:2] + nxt * w[2:3] + bias)


def _ctx_mix_kernel(u_ref, lam_ref, gain_ref, cw_ref, cb_ref, o_ref, *, lam_init):
    lam = _lambda(lam_ref, lam_init)
    gain = gain_ref[...]
    outs = []
    for h in range(NA_HEADS):
        q = u_ref[:, OFF_NA_Q + h * HEAD_DIM:OFF_NA_Q + (h + 1) * HEAD_DIM]
        k = u_ref[:, OFF_NA_K + h * HEAD_DIM:OFF_NA_K + (h + 1) * HEAD_DIM]
        v = u_ref[:, OFF_NA_V + h * HEAD_DIM:OFF_NA_V + (h + 1) * HEAD_DIM]
        outs.append(_softmax_pv([_dot_nt(q, k)], [v]))
    zero_row = jnp.zeros((1, SC_WIDTH), F32)
    vc = u_ref[:, OFF_SC_C:OFF_SC_X].astype(F32) * u_ref[:, OFF_SC_X:OFF_DA_Q].astype(F32)
    outs.append(_gated_conv(u_ref[:, OFF_SC_B:OFF_SC_C].astype(F32), vc, zero_row, zero_row,
                            cw_ref[...], cb_ref[...]))
    for j in range(DA_HEADS):
        v = u_ref[:, OFF_DA_V + j * 2 * DA_DIM:OFF_DA_V + (j + 1) * 2 * DA_DIM]
        o12 = []
        for mi in (2 * j, 2 * j + 1):
            q = u_ref[:, OFF_DA_Q + mi * DA_DIM:OFF_DA_Q + (mi + 1) * DA_DIM]
            k = u_ref[:, OFF_DA_K + mi * DA_DIM:OFF_DA_K + (mi + 1) * DA_DIM]
            o12.append(_softmax_pv([_dot_nt(q, k)], [v]))
        outs.append(_diff_head_norm(o12[0], o12[1], lam, gain, lam_init))
    o_ref[...] = jnp.concatenate(outs, axis=-1).astype(BF16)


def _ctx_mix(u, da_lambda, da_norm_g, conv_w, conv_b, l, lam_init):
    return pl.pallas_call(
        functools.partial(_ctx_mix_kernel, lam_init=lam_init),
        grid=(BATCH,),
        in_specs=[
            pl.BlockSpec((SEQ, IN_WIDTH), lambda b: (b, 0)),
            pl.BlockSpec((None, 4, DA_DIM), lambda b: (l, 0, 0)),
            pl.BlockSpec((None, 1, 2 * DA_DIM), lambda b: (l, 0, 0)),
            pl.BlockSpec((None, 3, SC_WIDTH), lambda b: (l, 0, 0)),
            pl.BlockSpec((None, 1, SC_WIDTH), lambda b: (l, 0, 0)),
        ],
        out_specs=pl.BlockSpec((SEQ, MIX_WIDTH), lambda b: (b, 0)),
        out_shape=jax.ShapeDtypeStruct((BATCH * SEQ, MIX_WIDTH), BF16),
        compiler_params=_params("parallel"),
        name="mix_context",
    )(u, da_lambda, da_norm_g.reshape(DEPTH, 1, 2 * DA_DIM), conv_w, conv_b.reshape(DEPTH, 1, SC_WIDTH))


def _lat_mix_kernel(uq_ref, nak_ref, nav_ref, sc_ref, dakv_ref, cnak_ref, cnav_ref, cdak_ref, cdav_ref,
                    bias_ref, lam_ref, gain_ref, cw_ref, cb_ref, o_ref, *, lam_init):
    qt = pl.program_id(1)
    n_qt = pl.num_programs(1)
    lam = _lambda(lam_ref, lam_init)
    gain = gain_ref[...]
    outs = []
    key_row0 = jnp.clip(qt * QT_ROWS - NA_WIN_ROWS // 2, 0, GRID_ROWS - NA_KEY_ROWS)
    k0 = pl.multiple_of(key_row0 * GRID_W, GRID_W)
    for h in range(NA_HEADS):
        hs = slice(h * HEAD_DIM, (h + 1) * HEAD_DIM)
        q = uq_ref[:, OFF_NA_Q + h * HEAD_DIM:OFF_NA_Q + (h + 1) * HEAD_DIM]
        k_loc = nak_ref[pl.ds(k0, NA_KEYS), hs]
        v_loc = nav_ref[pl.ds(k0, NA_KEYS), hs]
        s_loc = _dot_nt(q, k_loc) + bias_ref[h]
        s_ctx = _dot_nt(q, cnak_ref[:, hs])
        outs.append(_softmax_pv([s_loc, s_ctx], [v_loc, cnav_ref[:, hs]]))
    t0 = pl.multiple_of(qt * QT, QT)
    halo = 16
    before = sc_ref[pl.ds(pl.multiple_of(jnp.maximum(t0 - halo, 0), halo), halo), :].astype(F32)
    after = sc_ref[pl.ds(pl.multiple_of(jnp.minimum(t0 + QT, DEC_SEQ - halo), halo), halo), :].astype(F32)
    cur = sc_ref[pl.ds(t0, QT), :].astype(F32)
    v_prev = before[halo - 1:halo, SC_WIDTH:2 * SC_WIDTH] * before[halo - 1:halo, 2 * SC_WIDTH:]
    v_next = after[0:1, SC_WIDTH:2 * SC_WIDTH] * after[0:1, 2 * SC_WIDTH:]
    v_prev = jnp.where(qt > 0, v_prev, 0.0)
    v_next = jnp.where(qt < n_qt - 1, v_next, 0.0)
    outs.append(_gated_conv(cur[:, :SC_WIDTH], cur[:, SC_WIDTH:2 * SC_WIDTH] * cur[:, 2 * SC_WIDTH:],
                            v_prev, v_next, cw_ref[...], cb_ref[...]))
    for j in range(DA_HEADS):
        vs = slice(DA_QK_WIDTH + j * 2 * DA_DIM, DA_QK_WIDTH + (j + 1) * 2 * DA_DIM)
        v_lat = dakv_ref[:, vs]
        v_ctx = cdav_ref[:, j * 2 * DA_DIM:(j + 1) * 2 * DA_DIM]
        o12 = []
        for mi in (2 * j, 2 * j + 1):
            ms = slice(mi * DA_DIM, (mi + 1) * DA_DIM)
            q = uq_ref[:, OFF_DA_Q + mi * DA_DIM:OFF_DA_Q + (mi + 1) * DA_DIM]
            s_lat = _dot_nt(q, dakv_ref[:, ms])
            s_ctx = _dot_nt(q, cdak_ref[:, ms])
            o12.append(_softmax_pv([s_lat, s_ctx], [v_lat, v_ctx]))
        outs.append(_diff_head_norm(o12[0], o12[1], lam, gain, lam_init))
    o_ref[...] = jnp.concatenate(outs, axis=-1).astype(BF16)


def _lat_mix(u, caches, bias_tab, da_lambda, da_norm_g, conv_w, conv_b, l, lam_init):
    cnak, cnav, cdak, cdav = caches
    n_qt = DEC_SEQ // QT

    def bias_map(b, qt):
        return (l, 0, jnp.where(qt == 0, 0, jnp.where(qt == n_qt - 1, 2, 1)), 0, 0)

    return pl.pallas_call(
        functools.partial(_lat_mix_kernel, lam_init=lam_init),
        grid=(DEC_BATCH, n_qt),
        in_specs=[
            pl.BlockSpec((QT, IN_WIDTH), lambda b, qt: (b * n_qt + qt, 0)),
            pl.BlockSpec((DEC_SEQ, NA_WIDTH), lambda b, qt: (b, OFF_NA_K // NA_WIDTH)),
            pl.BlockSpec((DEC_SEQ, NA_WIDTH), lambda b, qt: (b, OFF_NA_V // NA_WIDTH)),
            pl.BlockSpec((DEC_SEQ, 3 * SC_WIDTH), lambda b, qt: (b, OFF_SC_B // (3 * SC_WIDTH))),
            pl.BlockSpec((DEC_SEQ, DA_QK_WIDTH + DA_V_WIDTH), lambda b, qt: (b, OFF_DA_K // (DA_QK_WIDTH + DA_V_WIDTH))),
            pl.BlockSpec((None, None, PAST_LEN, NA_WIDTH), lambda b, qt: (b, l, 0, 0)),
            pl.BlockSpec((None, None, PAST_LEN, NA_WIDTH), lambda b, qt: (b, l, 0, 0)),
            pl.BlockSpec((None, None, PAST_LEN, DA_QK_WIDTH), lambda b, qt: (b, l, 0, 0)),
            pl.BlockSpec((None, None, PAST_LEN, DA_V_WIDTH), lambda b, qt: (b, l, 0, 0)),
            pl.BlockSpec((None, NA_HEADS, None, QT, NA_KEYS), bias_map),
            pl.BlockSpec((None, 4, DA_DIM), lambda b, qt: (l, 0, 0)),
            pl.BlockSpec((None, 1, 2 * DA_DIM), lambda b, qt: (l, 0, 0)),
            pl.BlockSpec((None, 3, SC_WIDTH), lambda b, qt: (l, 0, 0)),
            pl.BlockSpec((None, 1, SC_WIDTH), lambda b, qt: (l, 0, 0)),
        ],
        out_specs=pl.BlockSpec((QT, MIX_WIDTH), lambda b, qt: (b * n_qt + qt, 0)),
        out_shape=jax.ShapeDtypeStruct((DEC_BATCH * DEC_SEQ, MIX_WIDTH), BF16),
        compiler_params=_params("parallel", "arbitrary"),
        name="mix_latent",
    )(u, u, u, u, u, cnak, cnav, cdak, cdav, bias_tab, da_lambda,
      da_norm_g.reshape(DEPTH, 1, 2 * DA_DIM), conv_w, conv_b.reshape(DEPTH, 1, SC_WIDTH))


def _first_max(vals):
    m = functools.reduce(jnp.maximum, vals)
    hot, taken = [], None
    for v in vals:
        is_max = v == m
        if taken is None:
            hot.append(is_max)
            taken = is_max
        else:
            hot.append(is_max & ~taken)
            taken = taken | is_max
    return hot, m


def _pick(hot, vals):
    out = vals[-1]
    for h, v in zip(hot[-2::-1], vals[-2::-1]):
        out = jnp.where(h, v, out)
    return out


def _router_gates(h2, h2_bf, wr_ref, br_ref):
    w = wr_ref[...]
    w_hi = w.astype(BF16)
    w_lo = (w - w_hi.astype(F32)).astype(BF16)
    h_lo = (h2 - h2_bf.astype(F32)).astype(BF16)
    z = _dot_nt(w_hi, h2_bf) + _dot_nt(w_hi, h_lo) + _dot_nt(w_lo, h2_bf)
    scores = jax.nn.sigmoid(z)
    biased = scores + br_ref[...]
    P = [biased[k * N_GROUPS:(k + 1) * N_GROUPS] for k in range(PER_GROUP)]
    S = [scores[k * N_GROUPS:(k + 1) * N_GROUPS] for k in range(PER_GROUP)]
    pair_sums = [P[i] + P[j] for i in range(PER_GROUP) for j in range(i + 1, PER_GROUP)]
    group_score = functools.reduce(jnp.maximum, pair_sums)
    sel, _ = _first_max([group_score[g:g + 1] for g in range(N_GROUPS)])
    c = [_pick(sel, [P[k][g:g + 1] for g in range(N_GROUPS)]) for k in range(PER_GROUP)]
    cs = [_pick(sel, [S[k][g:g + 1] for g in range(N_GROUPS)]) for k in range(PER_GROUP)]
    t1, _ = _first_max(c)
    t2, _ = _first_max([jnp.where(t, -jnp.inf, v) for t, v in zip(t1, c)])
    w1 = functools.reduce(jnp.add, [jnp.where(t, v, 0.0) for t, v in zip(t1, cs)])
    w2 = functools.reduce(jnp.add, [jnp.where(t, v, 0.0) for t, v in zip(t2, cs)])
    total = w1 + w2
    slot_gate = [jnp.where(a, w1 / total, jnp.where(b, w2 / total, 0.0)) for a, b in zip(t1, t2)]
    rows = [jnp.where(sel[g], slot_gate[k], 0.0) for g in range(N_GROUPS) for k in range(PER_GROUP)]
    rows.append(jnp.zeros((128 - N_EXPERTS, h2.shape[0]), F32))
    return jnp.concatenate(rows, axis=0)


def _post_kernel(mix_ref, x_ref, mod_ref, wout_ref, g_ref, b_ref, wr_ref, br_ref, x1_ref, h2_ref, gates_ref):
    mod = mod_ref[...]
    y = _dot(mix_ref[...], wout_ref[...])
    x1 = _ln(ALPHA * x_ref[...] + mod[2:3] * y) * g_ref[...] + b_ref[...]
    h2 = _ln(x1) * (1.0 + mod[4:5]) + mod[3:4]
    h2_bf = h2.astype(BF16)
    x1_ref[...] = x1
    h2_ref[...] = h2_bf
    gates_ref[...] = _router_gates(h2, h2_bf, wr_ref, br_ref).T


def _post(mix, x, mod_l, w_out, ln_g, ln_b, wr_t, br_t, l, latent):
    n = x.shape[0]
    per_batch = DEC_SEQ // TM
    mod_map = (lambda i: (1 + i // per_batch, 0, 0)) if latent else (lambda i: (0, 0, 0))
    return pl.pallas_call(
        _post_kernel,
        grid=(n // TM,),
        in_specs=[
            pl.BlockSpec((TM, MIX_WIDTH), lambda i: (i, 0)),
            pl.BlockSpec((TM, D_MODEL), lambda i: (i, 0)),
            pl.BlockSpec((None, 6, D_MODEL), mod_map),
            pl.BlockSpec((None, MIX_WIDTH, D_MODEL), lambda i: (l, 0, 0)),
            pl.BlockSpec((None, 1, D_MODEL), lambda i: (l, 0, 0)),
            pl.BlockSpec((None, 1, D_MODEL), lambda i: (l, 0, 0)),
            pl.BlockSpec((N_EXPERTS, D_MODEL), lambda i: (0, 0)),
            pl.BlockSpec((N_EXPERTS, 1), lambda i: (0, 0)),
        ],
        out_specs=[
            pl.BlockSpec((TM, D_MODEL), lambda i: (i, 0)),
            pl.BlockSpec((TM, D_MODEL), lambda i: (i, 0)),
            pl.BlockSpec((TM, 128), lambda i: (i, 0)),
        ],
        out_shape=[
            jax.ShapeDtypeStruct((n, D_MODEL), F32),
            jax.ShapeDtypeStruct((n, D_MODEL), BF16),
            jax.ShapeDtypeStruct((n, 128), F32),
        ],
        compiler_params=_params("parallel"),
        name="post_latent" if latent else "post_context",
    )(mix, x, mod_l, w_out, ln_g.reshape(DEPTH, 1, D_MODEL), ln_b.reshape(DEPTH, 1, D_MODEL), wr_t, br_t)


def _moe_kernel(h_ref, gates_ref, x1_ref, mod_ref, w1_ref, w3_ref, w2_ref, g_ref, b_ref, o_ref, acc_ref):
    e = pl.program_id(1)

    @pl.when(e == 0)
    def _():
        acc_ref[...] = jnp.zeros_like(acc_ref)

    h = h_ref[...]
    gates = gates_ref[...]
    lane = lax.broadcasted_iota(jnp.int32, gates.shape, 1)
    gate = jnp.sum(jnp.where(lane == e, gates, 0.0), axis=-1, keepdims=True)
    a = _dot(h, w1_ref[...])
    g = _dot(h, w3_ref[...])
    hid = a * jax.nn.sigmoid(a) * g * gate
    acc_ref[...] += _dot(hid.astype(BF16), w2_ref[...])

    @pl.when(e == pl.num_programs(1) - 1)
    def _():
        mod = mod_ref[...]
        o_ref[...] = _ln(ALPHA * x1_ref[...] + mod[5:6] * acc_ref[...]) * g_ref[...] + b_ref[...]


def _moe(h2, gates, x1, mod_l, w1, w3, w2, ln_g, ln_b, l, latent):
    n = x1.shape[0]
    per_batch = DEC_SEQ // TM_MOE
    mod_map = (lambda i, e: (1 + i // per_batch, 0, 0)) if latent else (lambda i, e: (0, 0, 0))
    return pl.pallas_call(
        _moe_kernel,
        grid=(n // TM_MOE, N_EXPERTS),
        in_specs=[
            pl.BlockSpec((TM_MOE, D_MODEL), lambda i, e: (i, 0)),
            pl.BlockSpec((TM_MOE, 128), lambda i, e: (i, 0)),
            pl.BlockSpec((TM_MOE, D_MODEL), lambda i, e: (i, 0)),
            pl.BlockSpec((None, 6, D_MODEL), mod_map),
            pl.BlockSpec((None, None, D_MODEL, D_EXPERT), lambda i, e: (l, e, 0, 0)),
            pl.BlockSpec((None, None, D_MODEL, D_EXPERT), lambda i, e: (l, e, 0, 0)),
            pl.BlockSpec((None, None, D_EXPERT, D_MODEL), lambda i, e: (l, e, 0, 0)),
            pl.BlockSpec((None, 1, D_MODEL), lambda i, e: (l, 0, 0)),
            pl.BlockSpec((None, 1, D_MODEL), lambda i, e: (l, 0, 0)),
        ],
        out_specs=pl.BlockSpec((TM_MOE, D_MODEL), lambda i, e: (i, 0)),
        out_shape=jax.ShapeDtypeStruct((n, D_MODEL), F32),
        scratch_shapes=[pltpu.VMEM((TM_MOE, D_MODEL), F32)],
        compiler_params=_params("parallel", "arbitrary"),
        name="moe_latent" if latent else "moe_context",
    )(h2, gates, x1, mod_l, w1, w3, w2, ln_g.reshape(DEPTH, 1, D_MODEL), ln_b.reshape(DEPTH, 1, D_MODEL))


def _rope_tables():
    t = np.arange(DEC_SEQ)
    half = DA_DIM // 4
    inv_freq = ROPE_BASE ** (-np.arange(half, dtype=np.float32) / half)
    cos, sin = [], []
    for pos in (t // GRID_W, t % GRID_W):
        ang = pos.astype(np.float32)[:, None] * inv_freq[None, :]
        c, s = np.cos(ang), np.sin(ang)
        cos += [c, c]
        sin += [-s, s]
    cos = np.tile(np.concatenate(cos, axis=1), (1, 2 * DA_HEADS))
    sin = np.tile(np.concatenate(sin, axis=1), (1, 2 * DA_HEADS))
    return jnp.asarray(cos, F32), jnp.asarray(sin, F32)


def _na_bias_tables(na_rel_bias):
    variants = [(0, 0), (2 * QT_ROWS, 2 * QT_ROWS - NA_WIN_ROWS // 2), (GRID_ROWS - QT_ROWS, GRID_ROWS - NA_KEY_ROWS)]
    qi = np.arange(QT) // GRID_W
    qc = np.arange(QT) % GRID_W
    ki = np.arange(NA_KEYS) // GRID_W
    kc = np.arange(NA_KEYS) % GRID_W
    d_rows, d_cols, valids = [], [], []
    for r0, key_row0 in variants:
        qr = r0 + qi
        kr = key_row0 + ki
        win_r0 = np.clip(qr - NA_WIN_ROWS // 2, 0, GRID_ROWS - NA_WIN_ROWS)
        win_c0 = np.clip(qc - NA_WIN_COLS // 2, 0, GRID_W - NA_WIN_COLS)
        ok_r = (kr[None, :] >= win_r0[:, None]) & (kr[None, :] < win_r0[:, None] + NA_WIN_ROWS)
        ok_c = (kc[None, :] >= win_c0[:, None]) & (kc[None, :] < win_c0[:, None] + NA_WIN_COLS)
        d_rows.append(np.clip(kr[None, :] - qr[:, None] + NA_WIN_ROWS - 1, 0, 2 * NA_WIN_ROWS - 2))
        d_cols.append(np.clip(kc[None, :] - qc[:, None], 1 - NA_WIN_COLS, NA_WIN_COLS - 1) + NA_WIN_COLS - 1)
        valids.append(ok_r & ok_c)
    d_row, d_col, valid = np.stack(d_rows), np.stack(d_cols), np.stack(valids)
    bias = na_rel_bias.astype(F32)[:, :, d_row, d_col]
    return jnp.where(valid[None, None], bias, -jnp.inf)


def kernel(x_prompt, x_sample, cache_na_k, cache_na_v, cache_da_k, cache_da_v, c, c_ctx, w_mod, b_mod, w_in,
           na_rel_bias, sc_conv_w, sc_conv_b, da_lambda, da_norm_g, w_out, ln1_g, ln1_b, w_router, b_router,
           moe_w1, moe_w3, moe_w2, ln2_g, ln2_b):
    xp = x_prompt.reshape(BATCH * SEQ, D_MODEL)
    xs = x_sample.reshape(DEC_BATCH * DEC_SEQ, D_MODEL)

    cond = jnp.concatenate([c_ctx[None, :], c, jnp.zeros((N_COND - 1 - DEC_BATCH, D_MODEL), F32)], axis=0)
    mod = _modulation(cond, w_mod, b_mod).reshape(DEPTH, N_COND, 6, D_MODEL)

    col_scale = np.ones((IN_WIDTH,), np.float32)
    col_scale[OFF_NA_Q:OFF_NA_K] = QK_SCALE
    col_scale[OFF_DA_Q:OFF_DA_K] = QK_SCALE
    w_in_bf = (w_in * col_scale).astype(BF16)
    w_out_bf = w_out.astype(BF16)
    w1_bf, w3_bf, w2_bf = moe_w1.astype(BF16), moe_w3.astype(BF16), moe_w2.astype(BF16)

    slot_major = np.arange(N_EXPERTS).reshape(N_GROUPS, PER_GROUP).T.reshape(-1)
    wr_t = w_router.T[slot_major]
    br_t = b_router.astype(F32)[slot_major].reshape(N_EXPERTS, 1)

    caches = (
        cache_na_k.reshape(DEC_BATCH, DEPTH, PAST_LEN, NA_WIDTH).astype(BF16),
        cache_na_v.reshape(DEC_BATCH, DEPTH, PAST_LEN, NA_WIDTH).astype(BF16),
        cache_da_k.reshape(DEC_BATCH, DEPTH, PAST_LEN, DA_QK_WIDTH).astype(BF16),
        cache_da_v.reshape(DEC_BATCH, DEPTH, PAST_LEN, DA_V_WIDTH).astype(BF16),
    )
    rope_tabs = _rope_tables()
    bias_tab = _na_bias_tables(na_rel_bias)

    new_caches = [[], [], [], []]
    for l in range(DEPTH):
        lam_init = 0.8 - 0.6 * math.exp(-0.3 * l)
        mod_l = mod[l]
        u_p, nak, nav, dak, dav = _inproj(xp, mod_l, w_in_bf, l, latent=False)
        for lst, t in zip(new_caches, (nak, nav, dak, dav)):
            lst.append(t)
        u_s, = _inproj(xs, mod_l, w_in_bf, l, latent=True, rope_tabs=rope_tabs)
        mix_p = _ctx_mix(u_p, da_lambda, da_norm_g, sc_conv_w, sc_conv_b, l, lam_init)
        mix_s = _lat_mix(u_s, caches, bias_tab, da_lambda, da_norm_g, sc_conv_w, sc_conv_b, l, lam_init)
        x1p, h2p, gp = _post(mix_p, xp, mod_l, w_out_bf, ln1_g, ln1_b, wr_t, br_t, l, latent=False)
        x1s, h2s, gs = _post(mix_s, xs, mod_l, w_out_bf, ln1_g, ln1_b, wr_t, br_t, l, latent=True)
        xp = _moe(h2p, gp, x1p, mod_l, w1_bf, w3_bf, w2_bf, ln2_g, ln2_b, l, latent=False)
        xs = _moe(h2s, gs, x1s, mod_l, w1_bf, w3_bf, w2_bf, ln2_g, ln2_b, l, latent=True)

    def stack(parts, heads, dim):
        return jnp.stack([t.reshape(BATCH, SEQ, heads, dim) for t in parts], axis=1)

    return (xp.reshape(BATCH, SEQ, D_MODEL), xs.reshape(DEC_BATCH, DEC_SEQ, D_MODEL),
            stack(new_caches[0], NA_HEADS, HEAD_DIM), stack(new_caches[1], NA_HEADS, HEAD_DIM),
            stack(new_caches[2], 2 * DA_HEADS, DA_DIM), stack(new_caches[3], DA_HEADS, 2 * DA_DIM))
```

```python
import functools
import math

import numpy as np
import jax
import jax.numpy as jnp
from jax import lax
from jax.experimental import pallas as pl
from jax.experimental.pallas import tpu as pltpu

F32 = jnp.float32
BF16 = jnp.bfloat16

D_MODEL = 1024
BATCH = 16
SEQ = 256
DEPTH = 4
DEC_BATCH = 2
DEC_SEQ = 2048
PAST_LEN = 512
GRID_W = 64
GRID_ROWS = DEC_SEQ // GRID_W
HEAD_DIM = 64
NA_HEADS = 4
NA_WIDTH = NA_HEADS * HEAD_DIM
NA_WIN_ROWS = 8
NA_WIN_COLS = 16
SC_WIDTH = 256
DA_HEADS = 4
DA_DIM = 64
DA_QK_WIDTH = 2 * DA_HEADS * DA_DIM
DA_V_WIDTH = DA_HEADS * 2 * DA_DIM
MIX_WIDTH = NA_WIDTH + SC_WIDTH + DA_V_WIDTH
IN_WIDTH = 3 * NA_WIDTH + 3 * SC_WIDTH + 2 * DA_QK_WIDTH + DA_V_WIDTH
OFF_NA_Q = 0
OFF_NA_K = NA_WIDTH
OFF_NA_V = 2 * NA_WIDTH
OFF_SC_B = 3 * NA_WIDTH
OFF_SC_C = OFF_SC_B + SC_WIDTH
OFF_SC_X = OFF_SC_C + SC_WIDTH
OFF_DA_Q = OFF_SC_X + SC_WIDTH
OFF_DA_K = OFF_DA_Q + DA_QK_WIDTH
OFF_DA_V = OFF_DA_K + DA_QK_WIDTH
ROPE_BASE = 10000.0
N_EXPERTS = 16
N_GROUPS = 4
PER_GROUP = N_EXPERTS // N_GROUPS
D_EXPERT = 256
ALPHA = (2 * DEPTH) ** 0.25
LN_EPS = 1e-5
QK_SCALE = HEAD_DIM ** -0.5

N_COND = 8
TM = 256
TM_MOE = 512
QT = 256
QT_ROWS = QT // GRID_W
NA_KEY_ROWS = 12
NA_KEYS = NA_KEY_ROWS * GRID_W
VMEM_LIMIT = 56 * 1024 * 1024


def _dot(a, b):
    return jnp.dot(a, b, preferred_element_type=F32)


def _dot_nt(a, b):
    return lax.dot_general(a, b, (((1,), (1,)), ((), ())), preferred_element_type=F32)


def _ln(x):
    mu = jnp.mean(x, -1, keepdims=True)
    xc = x - mu
    var = jnp.mean(xc * xc, -1, keepdims=True)
    return xc * lax.rsqrt(var + LN_EPS)


def _params(*sem):
    return pltpu.CompilerParams(dimension_semantics=sem, vmem_limit_bytes=VMEM_LIMIT)


def _mod_kernel(cond_ref, w_ref, b_ref, o_ref):
    c = cond_ref[...]
    s = (c * jax.nn.sigmoid(c)).astype(BF16)
    o_ref[...] = _dot(s, w_ref[...].astype(BF16)) + b_ref[...]


def _modulation(cond, w_mod, b_mod):
    tn = 1024
    return pl.pallas_call(
        _mod_kernel,
        grid=(DEPTH, 6 * D_MODEL // tn),
        in_specs=[
            pl.BlockSpec((N_COND, D_MODEL), lambda l, j: (0, 0)),
            pl.BlockSpec((None, D_MODEL, tn), lambda l, j: (l, 0, j)),
            pl.BlockSpec((None, 1, tn), lambda l, j: (l, 0, j)),
        ],
        out_specs=pl.BlockSpec((None, N_COND, tn), lambda l, j: (l, 0, j)),
        out_shape=jax.ShapeDtypeStruct((DEPTH, N_COND, 6 * D_MODEL), F32),
        compiler_params=_params("parallel", "parallel"),
        name="modulation",
    )(cond, w_mod, b_mod.reshape(DEPTH, 1, 6 * D_MODEL))


def _rope(t, cos, sin):
    lane = lax.broadcasted_iota(jnp.int32, t.shape, 1)
    first = (lane // 16) % 2 == 0
    n = t.shape[1]
    swapped = jnp.where(first, pltpu.roll(t, n - 16, 1), pltpu.roll(t, 16, 1))
    return t * cos + swapped * sin


def _inproj_kernel(*refs, latent):
    if latent:
        x_ref, mod_ref, w_ref, cos_ref, sin_ref, u_ref = refs
    else:
        x_ref, mod_ref, w_ref, u_ref, nak_ref, nav_ref, dak_ref, dav_ref = refs
    mod = mod_ref[...]
    h = _ln(x_ref[...]) * (1.0 + mod[1:2]) + mod[0:1]
    u = _dot(h.astype(BF16), w_ref[...])
    if latent:
        cos, sin = cos_ref[...], sin_ref[...]
        u_ref[:, :OFF_DA_Q] = u[:, :OFF_DA_Q].astype(BF16)
        u_ref[:, OFF_DA_Q:OFF_DA_K] = _rope(u[:, OFF_DA_Q:OFF_DA_K], cos, sin).astype(BF16)
        u_ref[:, OFF_DA_K:OFF_DA_V] = _rope(u[:, OFF_DA_K:OFF_DA_V], cos, sin).astype(BF16)
        u_ref[:, OFF_DA_V:] = u[:, OFF_DA_V:].astype(BF16)
    else:
        u_ref[...] = u.astype(BF16)
        nak_ref[...] = u[:, OFF_NA_K:OFF_NA_V]
        nav_ref[...] = u[:, OFF_NA_V:OFF_SC_B]
        dak_ref[...] = u[:, OFF_DA_K:OFF_DA_V]
        dav_ref[...] = u[:, OFF_DA_V:]


def _inproj(x, mod_l, w_in, l, latent, rope_tabs=None):
    n = x.shape[0]
    nt = n // TM
    per_batch = DEC_SEQ // TM
    if latent:
        mod_map = lambda i: (1 + i // per_batch, 0, 0)
    else:
        mod_map = lambda i: (0, 0, 0)
    in_specs = [
        pl.BlockSpec((TM, D_MODEL), lambda i: (i, 0)),
        pl.BlockSpec((None, 6, D_MODEL), mod_map),
        pl.BlockSpec((None, D_MODEL, IN_WIDTH), lambda i: (l, 0, 0)),
    ]
    args = [x, mod_l, w_in]
    out_specs = [pl.BlockSpec((TM, IN_WIDTH), lambda i: (i, 0))]
    out_shape = [jax.ShapeDtypeStruct((n, IN_WIDTH), BF16)]
    if latent:
        in_specs += [pl.BlockSpec((TM, DA_QK_WIDTH), lambda i: (i % per_batch, 0))] * 2
        args += list(rope_tabs)
    else:
        for width in (NA_WIDTH, NA_WIDTH, DA_QK_WIDTH, DA_V_WIDTH):
            out_specs.append(pl.BlockSpec((TM, width), lambda i: (i, 0)))
            out_shape.append(jax.ShapeDtypeStruct((n, width), F32))
    return pl.pallas_call(
        functools.partial(_inproj_kernel, latent=latent),
        grid=(nt,),
        in_specs=in_specs,
        out_specs=out_specs,
        out_shape=out_shape,
        compiler_params=_params("parallel"),
        name="inproj_latent" if latent else "inproj_context",
    )(*args)


def _lambda(lam_ref, lam_init):
    lp = lam_ref[...]
    s1 = jnp.sum(lp[0:1] * lp[1:2], axis=-1, keepdims=True)
    s2 = jnp.sum(lp[2:3] * lp[3:4], axis=-1, keepdims=True)
    return jnp.exp(s1) - jnp.exp(s2) + lam_init


def _softmax_pv(score_parts, value_parts):
    m = functools.reduce(jnp.maximum, [jnp.max(s, -1, keepdims=True) for s in score_parts])
    es = [jnp.exp(s - m) for s in score_parts]
    denom = functools.reduce(jnp.add, [jnp.sum(e, -1, keepdims=True) for e in es])
    o = functools.reduce(jnp.add, [_dot(e.astype(BF16), v) for e, v in zip(es, value_parts)])
    return o / denom


def _diff_head_norm(o1, o2, lam, gain, lam_init):
    o = o1 - lam * o2
    o = o * lax.rsqrt(jnp.mean(o * o, -1, keepdims=True) + LN_EPS)
    return o * gain * (1.0 - lam_init)


def _gated_conv(b, v, v_prev_row, v_next_row, w, bias):
    n = v.shape[0]
    row = lax.broadcasted_iota(jnp.int32, v.shape, 0)
    prev = jnp.where(row == 0, v_prev_row, pltpu.roll(v, 1, 0))
    nxt = jnp.where(row == n - 1, v_next_row, pltpu.roll(v, n - 1, 0))
    return b * (prev * w[0:1] + v * w[1:2] + nxt * w[2:3] + bias)


def _ctx_mix_kernel(u_ref, lam_ref, gain_ref, cw_ref, cb_ref, o_ref, *, lam_init):
    lam = _lambda(lam_ref, lam_init)
    gain = gain_ref[...]
    outs = []
    for h in range(NA_HEADS):
        q = u_ref[:, OFF_NA_Q + h * HEAD_DIM:OFF_NA_Q + (h + 1) * HEAD_DIM]
        k = u_ref[:, OFF_NA_K + h * HEAD_DIM:OFF_NA_K + (h + 1) * HEAD_DIM]
        v = u_ref[:, OFF_NA_V + h * HEAD_DIM:OFF_NA_V + (h + 1) * HEAD_DIM]
        outs.append(_softmax_pv([_dot_nt(q, k)], [v]))
    zero_row = jnp.zeros((1, SC_WIDTH), F32)
    vc = u_ref[:, OFF_SC_C:OFF_SC_X].astype(F32) * u_ref[:, OFF_SC_X:OFF_DA_Q].astype(F32)
    outs.append(_gated_conv(u_ref[:, OFF_SC_B:OFF_SC_C].astype(F32), vc, zero_row, zero_row,
                            cw_ref[...], cb_ref[...]))
    for j in range(DA_HEADS):
        v = u_ref[:, OFF_DA_V + j * 2 * DA_DIM:OFF_DA_V + (j + 1) * 2 * DA_DIM]
        o12 = []
        for mi in (2 * j, 2 * j + 1):
            q = u_ref[:, OFF_DA_Q + mi * DA_DIM:OFF_DA_Q + (mi + 1) * DA_DIM]
            k = u_ref[:, OFF_DA_K + mi * DA_DIM:OFF_DA_K + (mi + 1) * DA_DIM]
            o12.append(_softmax_pv([_dot_nt(q, k)], [v]))
        outs.append(_diff_head_norm(o12[0], o12[1], lam, gain, lam_init))
    o_ref[...] = jnp.concatenate(outs, axis=-1).astype(BF16)


def _ctx_mix(u, da_lambda, da_norm_g, conv_w, conv_b, l, lam_init):
    return pl.pallas_call(
        functools.partial(_ctx_mix_kernel, lam_init=lam_init),
        grid=(BATCH,),
        in_specs=[
            pl.BlockSpec((SEQ, IN_WIDTH), lambda b: (b, 0)),
            pl.BlockSpec((None, 4, DA_DIM), lambda b: (l, 0, 0)),
            pl.BlockSpec((None, 1, 2 * DA_DIM), lambda b: (l, 0, 0)),
            pl.BlockSpec((None, 3, SC_WIDTH), lambda b: (l, 0, 0)),
            pl.BlockSpec((None, 1, SC_WIDTH), lambda b: (l, 0, 0)),
        ],
        out_specs=pl.BlockSpec((SEQ, MIX_WIDTH), lambda b: (b, 0)),
        out_shape=jax.ShapeDtypeStruct((BATCH * SEQ, MIX_WIDTH), BF16),
        compiler_params=_params("parallel"),
        name="mix_context",
    )(u, da_lambda, da_norm_g.reshape(DEPTH, 1, 2 * DA_DIM), conv_w, conv_b.reshape(DEPTH, 1, SC_WIDTH))


def _lat_mix_kernel(uq_ref, nak_ref, nav_ref, sc_ref, dakv_ref, cnak_ref, cnav_ref, cdak_ref, cdav_ref,
                    bias_ref, lam_ref, gain_ref, cw_ref, cb_ref, o_ref, *, lam_init):
    qt = pl.program_id(1)
    n_qt = pl.num_programs(1)
    lam = _lambda(lam_ref, lam_init)
    gain = gain_ref[...]
    outs = []
    key_row0 = jnp.clip(qt * QT_ROWS - NA_WIN_ROWS // 2, 0, GRID_ROWS - NA_KEY_ROWS)
    k0 = pl.multiple_of(key_row0 * GRID_W, GRID_W)
    for h in range(NA_HEADS):
        hs = slice(h * HEAD_DIM, (h + 1) * HEAD_DIM)
        q = uq_ref[:, OFF_NA_Q + h * HEAD_DIM:OFF_NA_Q + (h + 1) * HEAD_DIM]
        k_loc = nak_ref[pl.ds(k0, NA_KEYS), hs]
        v_loc = nav_ref[pl.ds(k0, NA_KEYS), hs]
        s_loc = _dot_nt(q, k_loc) + bias_ref[h]
        s_ctx = _dot_nt(q, cnak_ref[:, hs])
        outs.append(_softmax_pv([s_loc, s_ctx], [v_loc, cnav_ref[:, hs]]))
    t0 = pl.multiple_of(qt * QT, QT)
    halo = 16
    before = sc_ref[pl.ds(pl.multiple_of(jnp.maximum(t0 - halo, 0), halo), halo), :].astype(F32)
    after = sc_ref[pl.ds(pl.multiple_of(jnp.minimum(t0 + QT, DEC_SEQ - halo), halo), halo), :].astype(F32)
    cur = sc_ref[pl.ds(t0, QT), :].astype(F32)
    v_prev = before[halo - 1:halo, SC_WIDTH:2 * SC_WIDTH] * before[halo - 1:halo, 2 * SC_WIDTH:]
    v_next = after[0:1, SC_WIDTH:2 * SC_WIDTH] * after[0:1, 2 * SC_WIDTH:]
    v_prev = jnp.where(qt > 0, v_prev, 0.0)
    v_next = jnp.where(qt < n_qt - 1, v_next, 0.0)
    outs.append(_gated_conv(cur[:, :SC_WIDTH], cur[:, SC_WIDTH:2 * SC_WIDTH] * cur[:, 2 * SC_WIDTH:],
                            v_prev, v_next, cw_ref[...], cb_ref[...]))
    for j in range(DA_HEADS):
        vs = slice(DA_QK_WIDTH + j * 2 * DA_DIM, DA_QK_WIDTH + (j + 1) * 2 * DA_DIM)
        v_lat = dakv_ref[:, vs]
        v_ctx = cdav_ref[:, j * 2 * DA_DIM:(j + 1) * 2 * DA_DIM]
        o12 = []
        for mi in (2 * j, 2 * j + 1):
            ms = slice(mi * DA_DIM, (mi + 1) * DA_DIM)
            q = uq_ref[:, OFF_DA_Q + mi * DA_DIM:OFF_DA_Q + (mi + 1) * DA_DIM]
            s_lat = _dot_nt(q, dakv_ref[:, ms])
            s_ctx = _dot_nt(q, cdak_ref[:, ms])
            o12.append(_softmax_pv([s_lat, s_ctx], [v_lat, v_ctx]))
        outs.append(_diff_head_norm(o12[0], o12[1], lam, gain, lam_init))
    o_ref[...] = jnp.concatenate(outs, axis=-1).astype(BF16)


def _lat_mix(u, caches, bias_tab, da_lambda, da_norm_g, conv_w, conv_b, l, lam_init):
    cnak, cnav, cdak, cdav = caches
    n_qt = DEC_SEQ // QT

    def bias_map(b, qt):
        return (l, 0, jnp.where(qt == 0, 0, jnp.where(qt == n_qt - 1, 2, 1)), 0, 0)

    return pl.pallas_call(
        functools.partial(_lat_mix_kernel, lam_init=lam_init),
        grid=(DEC_BATCH, n_qt),
        in_specs=[
            pl.BlockSpec((QT, IN_WIDTH), lambda b, qt: (b * n_qt + qt, 0)),
            pl.BlockSpec((DEC_SEQ, NA_WIDTH), lambda b, qt: (b, OFF_NA_K // NA_WIDTH)),
            pl.BlockSpec((DEC_SEQ, NA_WIDTH), lambda b, qt: (b, OFF_NA_V // NA_WIDTH)),
            pl.BlockSpec((DEC_SEQ, 3 * SC_WIDTH), lambda b, qt: (b, OFF_SC_B // (3 * SC_WIDTH))),
            pl.BlockSpec((DEC_SEQ, DA_QK_WIDTH + DA_V_WIDTH), lambda b, qt: (b, OFF_DA_K // (DA_QK_WIDTH + DA_V_WIDTH))),
            pl.BlockSpec((None, None, PAST_LEN, NA_WIDTH), lambda b, qt: (b, l, 0, 0)),
            pl.BlockSpec((None, None, PAST_LEN, NA_WIDTH), lambda b, qt: (b, l, 0, 0)),
            pl.BlockSpec((None, None, PAST_LEN, DA_QK_WIDTH), lambda b, qt: (b, l, 0, 0)),
            pl.BlockSpec((None, None, PAST_LEN, DA_V_WIDTH), lambda b, qt: (b, l, 0, 0)),
            pl.BlockSpec((None, NA_HEADS, None, QT, NA_KEYS), bias_map),
            pl.BlockSpec((None, 4, DA_DIM), lambda b, qt: (l, 0, 0)),
            pl.BlockSpec((None, 1, 2 * DA_DIM), lambda b, qt: (l, 0, 0)),
            pl.BlockSpec((None, 3, SC_WIDTH), lambda b, qt: (l, 0, 0)),
            pl.BlockSpec((None, 1, SC_WIDTH), lambda b, qt: (l, 0, 0)),
        ],
        out_specs=pl.BlockSpec((QT, MIX_WIDTH), lambda b, qt: (b * n_qt + qt, 0)),
        out_shape=jax.ShapeDtypeStruct((DEC_BATCH * DEC_SEQ, MIX_WIDTH), BF16),
        compiler_params=_params("parallel", "arbitrary"),
        name="mix_latent",
    )(u, u, u, u, u, cnak, cnav, cdak, cdav, bias_tab, da_lambda,
      da_norm_g.reshape(DEPTH, 1, 2 * DA_DIM), conv_w, conv_b.reshape(DEPTH, 1, SC_WIDTH))


def _first_max(vals):
    m = functools.reduce(jnp.maximum, vals)
    hot, taken = [], None
    for v in vals:
        is_max = v == m
        if taken is None:
            hot.append(is_max)
            taken = is_max
        else:
            hot.append(is_max & ~taken)
            taken = taken | is_max
    return hot, m


def _pick(hot, vals):
    out = vals[-1]
    for h, v in zip(hot[-2::-1], vals[-2::-1]):
        out = jnp.where(h, v, out)
    return out


def _router_gates(h2, h2_bf, wr_ref, br_ref):
    w = wr_ref[...]
    w_hi = w.astype(BF16)
    w_lo = (w - w_hi.astype(F32)).astype(BF16)
    h_lo = (h2 - h2_bf.astype(F32)).astype(BF16)
    z = _dot_nt(w_hi, h2_bf) + _dot_nt(w_hi, h_lo) + _dot_nt(w_lo, h2_bf)
    scores = jax.nn.sigmoid(z)
    biased = scores + br_ref[...]
    P = [biased[k * N_GROUPS:(k + 1) * N_GROUPS] for k in range(PER_GROUP)]
    S = [scores[k * N_GROUPS:(k + 1) * N_GROUPS] for k in range(PER_GROUP)]
    pair_sums = [P[i] + P[j] for i in range(PER_GROUP) for j in range(i + 1, PER_GROUP)]
    group_score = functools.reduce(jnp.maximum, pair_sums)
    sel, _ = _first_max([group_score[g:g + 1] for g in range(N_GROUPS)])
    c = [_pick(sel, [P[k][g:g + 1] for g in range(N_GROUPS)]) for k in range(PER_GROUP)]
    cs = [_pick(sel, [S[k][g:g + 1] for g in range(N_GROUPS)]) for k in range(PER_GROUP)]
    t1, _ = _first_max(c)
    t2, _ = _first_max([jnp.where(t, -jnp.inf, v) for t, v in zip(t1, c)])
    w1 = functools.reduce(jnp.add, [jnp.where(t, v, 0.0) for t, v in zip(t1, cs)])
    w2 = functools.reduce(jnp.add, [jnp.where(t, v, 0.0) for t, v in zip(t2, cs)])
    total = w1 + w2
    slot_gate = [jnp.where(a, w1 / total, jnp.where(b, w2 / total, 0.0)) for a, b in zip(t1, t2)]
    rows = [jnp.where(sel[g], slot_gate[k], 0.0) for g in range(N_GROUPS) for k in range(PER_GROUP)]
    rows.append(jnp.zeros((128 - N_EXPERTS, h2.shape[0]), F32))
    return jnp.concatenate(rows, axis=0)


def _post_kernel(mix_ref, x_ref, mod_ref, wout_ref, g_ref, b_ref, wr_ref, br_ref, x1_ref, h2_ref, gates_ref):
    mod = mod_ref[...]
    y = _dot(mix_ref[...], wout_ref[...])
    x1 = _ln(ALPHA * x_ref[...] + mod[2:3] * y) * g_ref[...] + b_ref[...]
    h2 = _ln(x1) * (1.0 + mod[4:5]) + mod[3:4]
    h2_bf = h2.astype(BF16)
    x1_ref[...] = x1
    h2_ref[...] = h2_bf
    gates_ref[...] = _router_gates(h2, h2_bf, wr_ref, br_ref).T


def _post(mix, x, mod_l, w_out, ln_g, ln_b, wr_t, br_t, l, latent):
    n = x.shape[0]
    per_batch = DEC_SEQ // TM
    mod_map = (lambda i: (1 + i // per_batch, 0, 0)) if latent else (lambda i: (0, 0, 0))
    return pl.pallas_call(
        _post_kernel,
        grid=(n // TM,),
        in_specs=[
            pl.BlockSpec((TM, MIX_WIDTH), lambda i: (i, 0)),
            pl.BlockSpec((TM, D_MODEL), lambda i: (i, 0)),
            pl.BlockSpec((None, 6, D_MODEL), mod_map),
            pl.BlockSpec((None, MIX_WIDTH, D_MODEL), lambda i: (l, 0, 0)),
            pl.BlockSpec((None, 1, D_MODEL), lambda i: (l, 0, 0)),
            pl.BlockSpec((None, 1, D_MODEL), lambda i: (l, 0, 0)),
            pl.BlockSpec((N_EXPERTS, D_MODEL), lambda i: (0, 0)),
            pl.BlockSpec((N_EXPERTS, 1), lambda i: (0, 0)),
        ],
        out_specs=[
            pl.BlockSpec((TM, D_MODEL), lambda i: (i, 0)),
            pl.BlockSpec((TM, D_MODEL), lambda i: (i, 0)),
            pl.BlockSpec((TM, 128), lambda i: (i, 0)),
        ],
        out_shape=[
            jax.ShapeDtypeStruct((n, D_MODEL), F32),
            jax.ShapeDtypeStruct((n, D_MODEL), BF16),
            jax.ShapeDtypeStruct((n, 128), F32),
        ],
        compiler_params=_params("parallel"),
        name="post_latent" if latent else "post_context",
    )(mix, x, mod_l, w_out, ln_g.reshape(DEPTH, 1, D_MODEL), ln_b.reshape(DEPTH, 1, D_MODEL), wr_t, br_t)


def _moe_kernel(h_ref, gates_ref, x1_ref, mod_ref, w1_ref, w3_ref, w2_ref, g_ref, b_ref, o_ref, acc_ref):
    e = pl.program_id(1)

    @pl.when(e == 0)
    def _():
        acc_ref[...] = jnp.zeros_like(acc_ref)

    h = h_ref[...]
    gates = gates_ref[...]
    lane = lax.broadcasted_iota(jnp.int32, gates.shape, 1)
    gate = jnp.sum(jnp.where(lane == e, gates, 0.0), axis=-1, keepdims=True)
    a = _dot(h, w1_ref[...])
    g = _dot(h, w3_ref[...])
    hid = a * jax.nn.sigmoid(a) * g * gate
    acc_ref[...] += _dot(hid.astype(BF16), w2_ref[...])

    @pl.when(e == pl.num_programs(1) - 1)
    def _():
        mod = mod_ref[...]
        o_ref[...] = _ln(ALPHA * x1_ref[...] + mod[5:6] * acc_ref[...]) * g_ref[...] + b_ref[...]


def _moe(h2, gates, x1, mod_l, w1, w3, w2, ln_g, ln_b, l, latent):
    n = x1.shape[0]
    per_batch = DEC_SEQ // TM_MOE
    mod_map = (lambda i, e: (1 + i // per_batch, 0, 0)) if latent else (lambda i, e: (0, 0, 0))
    return pl.pallas_call(
        _moe_kernel,
        grid=(n // TM_MOE, N_EXPERTS),
        in_specs=[
            pl.BlockSpec((TM_MOE, D_MODEL), lambda i, e: (i, 0)),
            pl.BlockSpec((TM_MOE, 128), lambda i, e: (i, 0)),
            pl.BlockSpec((TM_MOE, D_MODEL), lambda i, e: (i, 0)),
            pl.BlockSpec((None, 6, D_MODEL), mod_map),
            pl.BlockSpec((None, None, D_MODEL, D_EXPERT), lambda i, e: (l, e, 0, 0)),
            pl.BlockSpec((None, None, D_MODEL, D_EXPERT), lambda i, e: (l, e, 0, 0)),
            pl.BlockSpec((None, None, D_EXPERT, D_MODEL), lambda i, e: (l, e, 0, 0)),
            pl.BlockSpec((None, 1, D_MODEL), lambda i, e: (l, 0, 0)),
            pl.BlockSpec((None, 1, D_MODEL), lambda i, e: (l, 0, 0)),
        ],
        out_specs=pl.BlockSpec((TM_MOE, D_MODEL), lambda i, e: (i, 0)),
        out_shape=jax.ShapeDtypeStruct((n, D_MODEL), F32),
        scratch_shapes=[pltpu.VMEM((TM_MOE, D_MODEL), F32)],
        compiler_params=_params("parallel", "arbitrary"),
        name="moe_latent" if latent else "moe_context",
    )(h2, gates, x1, mod_l, w1, w3, w2, ln_g.reshape(DEPTH, 1, D_MODEL), ln_b.reshape(DEPTH, 1, D_MODEL))


def _rope_tables():
    t = np.arange(DEC_SEQ)
    half = DA_DIM // 4
    inv_freq = ROPE_BASE ** (-np.arange(half, dtype=np.float32) / half)
    cos, sin = [], []
    for pos in (t // GRID_W, t % GRID_W):
        ang = pos.astype(np.float32)[:, None] * inv_freq[None, :]
        c, s = np.cos(ang), np.sin(ang)
        cos += [c, c]
        sin += [-s, s]
    cos = np.tile(np.concatenate(cos, axis=1), (1, 2 * DA_HEADS))
    sin = np.tile(np.concatenate(sin, axis=1), (1, 2 * DA_HEADS))
    return jnp.asarray(cos, F32), jnp.asarray(sin, F32)


def _na_bias_tables(na_rel_bias):
    variants = [(0, 0), (2 * QT_ROWS, 2 * QT_ROWS - NA_WIN_ROWS // 2), (GRID_ROWS - QT_ROWS, GRID_ROWS - NA_KEY_ROWS)]
    qi = np.arange(QT) // GRID_W
    qc = np.arange(QT) % GRID_W
    ki = np.arange(NA_KEYS) // GRID_W
    kc = np.arange(NA_KEYS) % GRID_W
    n_dr, n_dc = 2 * NA_WIN_ROWS - 1, 2 * NA_WIN_COLS - 1
    valids = []
    row_sel = np.zeros((len(variants), QT_ROWS, NA_KEY_ROWS, n_dr), np.float32)
    for v, (r0, key_row0) in enumerate(variants):
        qr = r0 + qi
        kr = key_row0 + ki
        win_r0 = np.clip(qr - NA_WIN_ROWS // 2, 0, GRID_ROWS - NA_WIN_ROWS)
        win_c0 = np.clip(qc - NA_WIN_COLS // 2, 0, GRID_W - NA_WIN_COLS)
        ok_r = (kr[None, :] >= win_r0[:, None]) & (kr[None, :] < win_r0[:, None] + NA_WIN_ROWS)
        ok_c = (kc[None, :] >= win_c0[:, None]) & (kc[None, :] < win_c0[:, None] + NA_WIN_COLS)
        valids.append(ok_r & ok_c)
        for a in range(QT_ROWS):
            for b in range(NA_KEY_ROWS):
                d = (key_row0 + b) - (r0 + a) + NA_WIN_ROWS - 1
                if 0 <= d < n_dr:
                    row_sel[v, a, b, d] = 1.0
    cols = np.arange(GRID_W)
    d_col = np.clip(cols[None, :] - cols[:, None], 1 - NA_WIN_COLS, NA_WIN_COLS - 1) + NA_WIN_COLS - 1
    col_sel = (d_col[None] == np.arange(n_dc)[:, None, None]).astype(np.float32)
    valid = np.stack(valids)
    hi = lax.Precision.HIGHEST
    by_col = jnp.einsum('lhab,bqc->lhaqc', na_rel_bias.astype(F32), col_sel, precision=hi)
    bias = jnp.einsum('vika,lhaqc->lhviqkc', row_sel, by_col, precision=hi)
    bias = bias.reshape(DEPTH, NA_HEADS, len(variants), QT, NA_KEYS)
    return jnp.where(valid[None, None], bias, -jnp.inf)


def kernel(x_prompt, x_sample, cache_na_k, cache_na_v, cache_da_k, cache_da_v, c, c_ctx, w_mod, b_mod, w_in,
           na_rel_bias, sc_conv_w, sc_conv_b, da_lambda, da_norm_g, w_out, ln1_g, ln1_b, w_router, b_router,
           moe_w1, moe_w3, moe_w2, ln2_g, ln2_b):
    xp = x_prompt.reshape(BATCH * SEQ, D_MODEL)
    xs = x_sample.reshape(DEC_BATCH * DEC_SEQ, D_MODEL)

    cond = jnp.concatenate([c_ctx[None, :], c, jnp.zeros((N_COND - 1 - DEC_BATCH, D_MODEL), F32)], axis=0)
    mod = _modulation(cond, w_mod, b_mod).reshape(DEPTH, N_COND, 6, D_MODEL)

    col_scale = np.ones((IN_WIDTH,), np.float32)
    col_scale[OFF_NA_Q:OFF_NA_K] = QK_SCALE
    col_scale[OFF_DA_Q:OFF_DA_K] = QK_SCALE
    w_in_bf = (w_in * col_scale).astype(BF16)
    w_out_bf = w_out.astype(BF16)
    w1_bf, w3_bf, w2_bf = moe_w1.astype(BF16), moe_w3.astype(BF16), moe_w2.astype(BF16)

    slot_major = np.arange(N_EXPERTS).reshape(N_GROUPS, PER_GROUP).T.reshape(-1)
    wr_t = w_router.T[slot_major]
    br_t = b_router.astype(F32)[slot_major].reshape(N_EXPERTS, 1)

    caches = (
        cache_na_k.reshape(DEC_BATCH, DEPTH, PAST_LEN, NA_WIDTH).astype(BF16),
        cache_na_v.reshape(DEC_BATCH, DEPTH, PAST_LEN, NA_WIDTH).astype(BF16),
        cache_da_k.reshape(DEC_BATCH, DEPTH, PAST_LEN, DA_QK_WIDTH).astype(BF16),
        cache_da_v.reshape(DEC_BATCH, DEPTH, PAST_LEN, DA_V_WIDTH).astype(BF16),
    )
    rope_tabs = _rope_tables()
    bias_tab = _na_bias_tables(na_rel_bias)

    new_caches = [[], [], [], []]
    for l in range(DEPTH):
        lam_init = 0.8 - 0.6 * math.exp(-0.3 * l)
        mod_l = mod[l]
        u_p, nak, nav, dak, dav = _inproj(xp, mod_l, w_in_bf, l, latent=False)
        for lst, t in zip(new_caches, (nak, nav, dak, dav)):
            lst.append(t)
        u_s, = _inproj(xs, mod_l, w_in_bf, l, latent=True, rope_tabs=rope_tabs)
        mix_p = _ctx_mix(u_p, da_lambda, da_norm_g, sc_conv_w, sc_conv_b, l, lam_init)
        mix_s = _lat_mix(u_s, caches, bias_tab, da_lambda, da_norm_g, sc_conv_w, sc_conv_b, l, lam_init)
        x1p, h2p, gp = _post(mix_p, xp, mod_l, w_out_bf, ln1_g, ln1_b, wr_t, br_t, l, latent=False)
        x1s, h2s, gs = _post(mix_s, xs, mod_l, w_out_bf, ln1_g, ln1_b, wr_t, br_t, l, latent=True)
        xp = _moe(h2p, gp, x1p, mod_l, w1_bf, w3_bf, w2_bf, ln2_g, ln2_b, l, latent=False)
        xs = _moe(h2s, gs, x1s, mod_l, w1_bf, w3_bf, w2_bf, ln2_g, ln2_b, l, latent=True)

    def stack(parts, heads, dim):
        return jnp.stack([t.reshape(BATCH, SEQ, heads, dim) for t in parts], axis=1)

    return (xp.reshape(BATCH, SEQ, D_MODEL), xs.reshape(DEC_BATCH, DEC_SEQ, D_MODEL),
            stack(new_caches[0], NA_HEADS, HEAD_DIM), stack(new_caches[1], NA_HEADS, HEAD_DIM),
            stack(new_caches[2], 2 * DA_HEADS, DA_DIM), stack(new_caches[3], DA_HEADS, 2 * DA_DIM))
```

```python
import functools
import math

import numpy as np
import jax
import jax.numpy as jnp
from jax import lax
from jax.experimental import pallas as pl
from jax.experimental.pallas import tpu as pltpu

F32 = jnp.float32
BF16 = jnp.bfloat16

D_MODEL = 1024
BATCH = 16
SEQ = 256
DEPTH = 4
DEC_BATCH = 2
DEC_SEQ = 2048
PAST_LEN = 512
GRID_W = 64
GRID_ROWS = DEC_SEQ // GRID_W
HEAD_DIM = 64
NA_HEADS = 4
NA_WIDTH = NA_HEADS * HEAD_DIM
NA_WIN_ROWS = 8
NA_WIN_COLS = 16
SC_WIDTH = 256
DA_HEADS = 4
DA_DIM = 64
DA_QK_WIDTH = 2 * DA_HEADS * DA_DIM
DA_V_WIDTH = DA_HEADS * 2 * DA_DIM
MIX_WIDTH = NA_WIDTH + SC_WIDTH + DA_V_WIDTH
IN_WIDTH = 3 * NA_WIDTH + 3 * SC_WIDTH + 2 * DA_QK_WIDTH + DA_V_WIDTH
OFF_NA_Q = 0
OFF_NA_K = NA_WIDTH
OFF_NA_V = 2 * NA_WIDTH
OFF_SC_B = 3 * NA_WIDTH
OFF_SC_C = OFF_SC_B + SC_WIDTH
OFF_SC_X = OFF_SC_C + SC_WIDTH
OFF_DA_Q = OFF_SC_X + SC_WIDTH
OFF_DA_K = OFF_DA_Q + DA_QK_WIDTH
OFF_DA_V = OFF_DA_K + DA_QK_WIDTH
ROPE_BASE = 10000.0
N_EXPERTS = 16
N_GROUPS = 4
PER_GROUP = N_EXPERTS // N_GROUPS
D_EXPERT = 256
ALPHA = (2 * DEPTH) ** 0.25
LN_EPS = 1e-5
QK_SCALE = HEAD_DIM ** -0.5

N_COND = 8
TM = 256
TM_POST = 512
TM_MOE = 512
QT = 256
QT_ROWS = QT // GRID_W
NA_KEY_ROWS = 12
NA_KEYS = NA_KEY_ROWS * GRID_W
VMEM_LIMIT = 56 * 1024 * 1024


def _dot(a, b):
    return jnp.dot(a, b, preferred_element_type=F32)


def _dot_nt(a, b):
    return lax.dot_general(a, b, (((1,), (1,)), ((), ())), preferred_element_type=F32)


def _ln(x):
    mu = jnp.mean(x, -1, keepdims=True)
    xc = x - mu
    var = jnp.mean(xc * xc, -1, keepdims=True)
    return xc * lax.rsqrt(var + LN_EPS)


def _params(*sem):
    return pltpu.CompilerParams(dimension_semantics=sem, vmem_limit_bytes=VMEM_LIMIT)


def _mod_kernel(cond_ref, w_ref, b_ref, o_ref):
    c = cond_ref[...]
    s = (c * jax.nn.sigmoid(c)).astype(BF16)
    o_ref[...] = _dot(s, w_ref[...].astype(BF16)) + b_ref[...]


def _modulation(cond, w_mod, b_mod):
    tn = 1024
    return pl.pallas_call(
        _mod_kernel,
        grid=(DEPTH, 6 * D_MODEL // tn),
        in_specs=[
            pl.BlockSpec((N_COND, D_MODEL), lambda l, j: (0, 0)),
            pl.BlockSpec((None, D_MODEL, tn), lambda l, j: (l, 0, j)),
            pl.BlockSpec((None, 1, tn), lambda l, j: (l, 0, j)),
        ],
        out_specs=pl.BlockSpec((None, N_COND, tn), lambda l, j: (l, 0, j)),
        out_shape=jax.ShapeDtypeStruct((DEPTH, N_COND, 6 * D_MODEL), F32),
        compiler_params=_params("parallel", "parallel"),
        name="modulation",
    )(cond, w_mod, b_mod.reshape(DEPTH, 1, 6 * D_MODEL))


def _rope(t, cos, sin):
    lane = lax.broadcasted_iota(jnp.int32, t.shape, 1)
    first = (lane // 16) % 2 == 0
    n = t.shape[1]
    swapped = jnp.where(first, pltpu.roll(t, n - 16, 1), pltpu.roll(t, 16, 1))
    return t * cos + swapped * sin


def _inproj_kernel(*refs, latent):
    if latent:
        x_ref, mod_ref, w_ref, cos_ref, sin_ref, u_ref = refs
    else:
        x_ref, mod_ref, w_ref = refs[:3]
        u_ref, nak_ref, nav_ref, dak_ref, dav_ref = refs[-5:]
    mod = mod_ref[...]
    h = _ln(x_ref[...]) * (1.0 + mod[1:2]) + mod[0:1]
    u = _dot(h.astype(BF16), w_ref[...])
    if latent:
        cos, sin = cos_ref[...], sin_ref[...]
        u_ref[:, :OFF_DA_Q] = u[:, :OFF_DA_Q].astype(BF16)
        u_ref[:, OFF_DA_Q:OFF_DA_K] = _rope(u[:, OFF_DA_Q:OFF_DA_K], cos, sin).astype(BF16)
        u_ref[:, OFF_DA_K:OFF_DA_V] = _rope(u[:, OFF_DA_K:OFF_DA_V], cos, sin).astype(BF16)
        u_ref[:, OFF_DA_V:] = u[:, OFF_DA_V:].astype(BF16)
    else:
        u_ref[...] = u.astype(BF16)
        nak_ref[...] = u[:, OFF_NA_K:OFF_NA_V]
        nav_ref[...] = u[:, OFF_NA_V:OFF_SC_B]
        dak_ref[...] = u[:, OFF_DA_K:OFF_DA_V]
        dav_ref[...] = u[:, OFF_DA_V:]


def _inproj(x, mod_l, w_in, l, latent, rope_tabs=None, cache_bufs=None):
    n = x.shape[0]
    nt = n // TM
    per_batch = DEC_SEQ // TM
    if latent:
        mod_map = lambda i: (1 + i // per_batch, 0, 0)
    else:
        mod_map = lambda i: (0, 0, 0)
    in_specs = [
        pl.BlockSpec((TM, D_MODEL), lambda i: (i, 0)),
        pl.BlockSpec((None, 6, D_MODEL), mod_map),
        pl.BlockSpec((None, D_MODEL, IN_WIDTH), lambda i: (l, 0, 0)),
    ]
    args = [x, mod_l, w_in]
    out_specs = [pl.BlockSpec((TM, IN_WIDTH), lambda i: (i, 0))]
    out_shape = [jax.ShapeDtypeStruct((n, IN_WIDTH), BF16)]
    aliases = {}
    if latent:
        in_specs += [pl.BlockSpec((TM, DA_QK_WIDTH), lambda i: (i % per_batch, 0))] * 2
        args += list(rope_tabs)
    else:
        assert TM == SEQ
        for width in (NA_WIDTH, NA_WIDTH, DA_QK_WIDTH, DA_V_WIDTH):
            out_specs.append(pl.BlockSpec((None, None, SEQ, width), lambda i: (i, l, 0, 0)))
            out_shape.append(jax.ShapeDtypeStruct((BATCH, DEPTH, SEQ, width), F32))
        if cache_bufs is not None:
            aliases = {len(args) + k: 1 + k for k in range(4)}
            in_specs += [pl.BlockSpec(memory_space=pl.ANY)] * 4
            args += list(cache_bufs)
    return pl.pallas_call(
        functools.partial(_inproj_kernel, latent=latent),
        grid=(nt,),
        in_specs=in_specs,
        out_specs=out_specs,
        out_shape=out_shape,
        input_output_aliases=aliases,
        compiler_params=_params("parallel"),
        name="inproj_latent" if latent else "inproj_context",
    )(*args)


def _lambda(lam_ref, lam_init):
    lp = lam_ref[...]
    s1 = jnp.sum(lp[0:1] * lp[1:2], axis=-1, keepdims=True)
    s2 = jnp.sum(lp[2:3] * lp[3:4], axis=-1, keepdims=True)
    return jnp.exp(s1) - jnp.exp(s2) + lam_init


def _softmax_pv(score_parts, value_parts):
    m = functools.reduce(jnp.maximum, [jnp.max(s, -1, keepdims=True) for s in score_parts])
    es = [jnp.exp(s - m) for s in score_parts]
    denom = functools.reduce(jnp.add, [jnp.sum(e, -1, keepdims=True) for e in es])
    o = functools.reduce(jnp.add, [_dot(e.astype(BF16), v) for e, v in zip(es, value_parts)])
    return o / denom


def _diff_head_norm(o1, o2, lam, gain, lam_init):
    o = o1 - lam * o2
    o = o * lax.rsqrt(jnp.mean(o * o, -1, keepdims=True) + LN_EPS)
    return o * gain * (1.0 - lam_init)


def _gated_conv(b, v, v_prev_row, v_next_row, w, bias):
    n = v.shape[0]
    row = lax.broadcasted_iota(jnp.int32, v.shape, 0)
    prev = jnp.where(row == 0, v_prev_row, pltpu.roll(v, 1, 0))
    nxt = jnp.where(row == n - 1, v_next_row, pltpu.roll(v, n - 1, 0))
    return b * (prev * w[0:1] + v * w[1:2] + nxt * w[2:3] + bias)


def _ctx_mix_kernel(u_ref, lam_ref, gain_ref, cw_ref, cb_ref, o_ref, *, lam_init):
    lam = _lambda(lam_ref, lam_init)
    gain = gain_ref[...]
    outs = []
    for h in range(NA_HEADS):
        q = u_ref[:, OFF_NA_Q + h * HEAD_DIM:OFF_NA_Q + (h + 1) * HEAD_DIM]
        k = u_ref[:, OFF_NA_K + h * HEAD_DIM:OFF_NA_K + (h + 1) * HEAD_DIM]
        v = u_ref[:, OFF_NA_V + h * HEAD_DIM:OFF_NA_V + (h + 1) * HEAD_DIM]
        outs.append(_softmax_pv([_dot_nt(q, k)], [v]))
    zero_row = jnp.zeros((1, SC_WIDTH), F32)
    vc = u_ref[:, OFF_SC_C:OFF_SC_X].astype(F32) * u_ref[:, OFF_SC_X:OFF_DA_Q].astype(F32)
    outs.append(_gated_conv(u_ref[:, OFF_SC_B:OFF_SC_C].astype(F32), vc, zero_row, zero_row,
                            cw_ref[...], cb_ref[...]))
    for j in range(DA_HEADS):
        v = u_ref[:, OFF_DA_V + j * 2 * DA_DIM:OFF_DA_V + (j + 1) * 2 * DA_DIM]
        o12 = []
        for mi in (2 * j, 2 * j + 1):
            q = u_ref[:, OFF_DA_Q + mi * DA_DIM:OFF_DA_Q + (mi + 1) * DA_DIM]
            k = u_ref[:, OFF_DA_K + mi * DA_DIM:OFF_DA_K + (mi + 1) * DA_DIM]
            o12.append(_softmax_pv([_dot_nt(q, k)], [v]))
        outs.append(_diff_head_norm(o12[0], o12[1], lam, gain, lam_init))
    o_ref[...] = jnp.concatenate(outs, axis=-1).astype(BF16)


def _ctx_mix(u, da_lambda, da_norm_g, conv_w, conv_b, l, lam_init):
    return pl.pallas_call(
        functools.partial(_ctx_mix_kernel, lam_init=lam_init),
        grid=(BATCH,),
        in_specs=[
            pl.BlockSpec((SEQ, IN_WIDTH), lambda b: (b, 0)),
            pl.BlockSpec((None, 4, DA_DIM), lambda b: (l, 0, 0)),
            pl.BlockSpec((None, 1, 2 * DA_DIM), lambda b: (l, 0, 0)),
            pl.BlockSpec((None, 3, SC_WIDTH), lambda b: (l, 0, 0)),
            pl.BlockSpec((None, 1, SC_WIDTH), lambda b: (l, 0, 0)),
        ],
        out_specs=pl.BlockSpec((SEQ, MIX_WIDTH), lambda b: (b, 0)),
        out_shape=jax.ShapeDtypeStruct((BATCH * SEQ, MIX_WIDTH), BF16),
        compiler_params=_params("parallel"),
        name="mix_context",
    )(u, da_lambda, da_norm_g.reshape(DEPTH, 1, 2 * DA_DIM), conv_w, conv_b.reshape(DEPTH, 1, SC_WIDTH))


def _lat_mix_kernel(uq_ref, nak_ref, nav_ref, sc_ref, dakv_ref, cnak_ref, cnav_ref, cdak_ref, cdav_ref,
                    bias_ref, lam_ref, gain_ref, cw_ref, cb_ref, o_ref, *, lam_init):
    qt = pl.program_id(1)
    n_qt = pl.num_programs(1)
    lam = _lambda(lam_ref, lam_init)
    gain = gain_ref[...]
    outs = []
    key_row0 = jnp.clip(qt * QT_ROWS - NA_WIN_ROWS // 2, 0, GRID_ROWS - NA_KEY_ROWS)
    k0 = pl.multiple_of(key_row0 * GRID_W, GRID_W)
    for h in range(NA_HEADS):
        hs = slice(h * HEAD_DIM, (h + 1) * HEAD_DIM)
        q = uq_ref[:, OFF_NA_Q + h * HEAD_DIM:OFF_NA_Q + (h + 1) * HEAD_DIM]
        k_loc = nak_ref[pl.ds(k0, NA_KEYS), hs]
        v_loc = nav_ref[pl.ds(k0, NA_KEYS), hs]
        s_loc = _dot_nt(q, k_loc) + bias_ref[h]
        s_ctx = _dot_nt(q, cnak_ref[:, hs])
        outs.append(_softmax_pv([s_loc, s_ctx], [v_loc, cnav_ref[:, hs]]))
    t0 = pl.multiple_of(qt * QT, QT)
    halo = 16
    before = sc_ref[pl.ds(pl.multiple_of(jnp.maximum(t0 - halo, 0), halo), halo), :].astype(F32)
    after = sc_ref[pl.ds(pl.multiple_of(jnp.minimum(t0 + QT, DEC_SEQ - halo), halo), halo), :].astype(F32)
    cur = sc_ref[pl.ds(t0, QT), :].astype(F32)
    v_prev = before[halo - 1:halo, SC_WIDTH:2 * SC_WIDTH] * before[halo - 1:halo, 2 * SC_WIDTH:]
    v_next = after[0:1, SC_WIDTH:2 * SC_WIDTH] * after[0:1, 2 * SC_WIDTH:]
    v_prev = jnp.where(qt > 0, v_prev, 0.0)
    v_next = jnp.where(qt < n_qt - 1, v_next, 0.0)
    outs.append(_gated_conv(cur[:, :SC_WIDTH], cur[:, SC_WIDTH:2 * SC_WIDTH] * cur[:, 2 * SC_WIDTH:],
                            v_prev, v_next, cw_ref[...], cb_ref[...]))
    for j in range(DA_HEADS):
        vs = slice(DA_QK_WIDTH + j * 2 * DA_DIM, DA_QK_WIDTH + (j + 1) * 2 * DA_DIM)
        v_lat = dakv_ref[:, vs]
        v_ctx = cdav_ref[:, j * 2 * DA_DIM:(j + 1) * 2 * DA_DIM]
        o12 = []
        for mi in (2 * j, 2 * j + 1):
            ms = slice(mi * DA_DIM, (mi + 1) * DA_DIM)
            q = uq_ref[:, OFF_DA_Q + mi * DA_DIM:OFF_DA_Q + (mi + 1) * DA_DIM]
            s_lat = _dot_nt(q, dakv_ref[:, ms])
            s_ctx = _dot_nt(q, cdak_ref[:, ms])
            o12.append(_softmax_pv([s_lat, s_ctx], [v_lat, v_ctx]))
        outs.append(_diff_head_norm(o12[0], o12[1], lam, gain, lam_init))
    o_ref[...] = jnp.concatenate(outs, axis=-1).astype(BF16)


def _lat_mix(u, caches, bias_tab, da_lambda, da_norm_g, conv_w, conv_b, l, lam_init):
    cnak, cnav, cdak, cdav = caches
    n_qt = DEC_SEQ // QT

    def bias_map(b, qt):
        return (l, 0, jnp.where(qt == 0, 0, jnp.where(qt == n_qt - 1, 2, 1)), 0, 0)

    return pl.pallas_call(
        functools.partial(_lat_mix_kernel, lam_init=lam_init),
        grid=(DEC_BATCH, n_qt),
        in_specs=[
            pl.BlockSpec((QT, IN_WIDTH), lambda b, qt: (b * n_qt + qt, 0)),
            pl.BlockSpec((DEC_SEQ, NA_WIDTH), lambda b, qt: (b, OFF_NA_K // NA_WIDTH)),
            pl.BlockSpec((DEC_SEQ, NA_WIDTH), lambda b, qt: (b, OFF_NA_V // NA_WIDTH)),
            pl.BlockSpec((DEC_SEQ, 3 * SC_WIDTH), lambda b, qt: (b, OFF_SC_B // (3 * SC_WIDTH))),
            pl.BlockSpec((DEC_SEQ, DA_QK_WIDTH + DA_V_WIDTH), lambda b, qt: (b, OFF_DA_K // (DA_QK_WIDTH + DA_V_WIDTH))),
            pl.BlockSpec((None, None, PAST_LEN, NA_WIDTH), lambda b, qt: (b, l, 0, 0)),
            pl.BlockSpec((None, None, PAST_LEN, NA_WIDTH), lambda b, qt: (b, l, 0, 0)),
            pl.BlockSpec((None, None, PAST_LEN, DA_QK_WIDTH), lambda b, qt: (b, l, 0, 0)),
            pl.BlockSpec((None, None, PAST_LEN, DA_V_WIDTH), lambda b, qt: (b, l, 0, 0)),
            pl.BlockSpec((None, NA_HEADS, None, QT, NA_KEYS), bias_map),
            pl.BlockSpec((None, 4, DA_DIM), lambda b, qt: (l, 0, 0)),
            pl.BlockSpec((None, 1, 2 * DA_DIM), lambda b, qt: (l, 0, 0)),
            pl.BlockSpec((None, 3, SC_WIDTH), lambda b, qt: (l, 0, 0)),
            pl.BlockSpec((None, 1, SC_WIDTH), lambda b, qt: (l, 0, 0)),
        ],
        out_specs=pl.BlockSpec((QT, MIX_WIDTH), lambda b, qt: (b * n_qt + qt, 0)),
        out_shape=jax.ShapeDtypeStruct((DEC_BATCH * DEC_SEQ, MIX_WIDTH), BF16),
        compiler_params=_params("parallel", "arbitrary"),
        name="mix_latent",
    )(u, u, u, u, u, cnak, cnav, cdak, cdav, bias_tab, da_lambda,
      da_norm_g.reshape(DEPTH, 1, 2 * DA_DIM), conv_w, conv_b.reshape(DEPTH, 1, SC_WIDTH))


def _first_max(vals):
    m = functools.reduce(jnp.maximum, vals)
    hot, taken = [], None
    for v in vals:
        is_max = v == m
        if taken is None:
            hot.append(is_max)
            taken = is_max
        else:
            hot.append(is_max & ~taken)
            taken = taken | is_max
    return hot, m


def _pick(hot, vals):
    out = vals[-1]
    for h, v in zip(hot[-2::-1], vals[-2::-1]):
        out = jnp.where(h, v, out)
    return out


def _router_gates(h2, h2_bf, wr_ref, br_ref):
    w = wr_ref[...]
    w_hi = w.astype(BF16)
    w_lo = (w - w_hi.astype(F32)).astype(BF16)
    h_lo = (h2 - h2_bf.astype(F32)).astype(BF16)
    z = _dot_nt(w_hi, h2_bf) + _dot_nt(w_hi, h_lo) + _dot_nt(w_lo, h2_bf)
    scores = jax.nn.sigmoid(z)
    biased = scores + br_ref[...]
    P = [biased[k * N_GROUPS:(k + 1) * N_GROUPS] for k in range(PER_GROUP)]
    S = [scores[k * N_GROUPS:(k + 1) * N_GROUPS] for k in range(PER_GROUP)]
    pair_sums = [P[i] + P[j] for i in range(PER_GROUP) for j in range(i + 1, PER_GROUP)]
    group_score = functools.reduce(jnp.maximum, pair_sums)
    sel, _ = _first_max([group_score[g:g + 1] for g in range(N_GROUPS)])
    c = [_pick(sel, [P[k][g:g + 1] for g in range(N_GROUPS)]) for k in range(PER_GROUP)]
    cs = [_pick(sel, [S[k][g:g + 1] for g in range(N_GROUPS)]) for k in range(PER_GROUP)]
    t1, _ = _first_max(c)
    t2, _ = _first_max([jnp.where(t, -jnp.inf, v) for t, v in zip(t1, c)])
    w1 = functools.reduce(jnp.add, [jnp.where(t, v, 0.0) for t, v in zip(t1, cs)])
    w2 = functools.reduce(jnp.add, [jnp.where(t, v, 0.0) for t, v in zip(t2, cs)])
    total = w1 + w2
    slot_gate = [jnp.where(a, w1 / total, jnp.where(b, w2 / total, 0.0)) for a, b in zip(t1, t2)]
    rows = [jnp.where(sel[g], slot_gate[k], 0.0) for g in range(N_GROUPS) for k in range(PER_GROUP)]
    rows.append(jnp.zeros((128 - N_EXPERTS, h2.shape[0]), F32))
    return jnp.concatenate(rows, axis=0)


def _post_kernel(mix_ref, x_ref, mod_ref, wout_ref, g_ref, b_ref, wr_ref, br_ref, x1_ref, h2_ref, gates_ref):
    mod = mod_ref[...]
    y = _dot(mix_ref[...], wout_ref[...])
    x1 = _ln(ALPHA * x_ref[...] + mod[2:3] * y) * g_ref[...] + b_ref[...]
    h2 = _ln(x1) * (1.0 + mod[4:5]) + mod[3:4]
    h2_bf = h2.astype(BF16)
    x1_ref[...] = x1
    h2_ref[...] = h2_bf
    gates_ref[...] = _router_gates(h2, h2_bf, wr_ref, br_ref).T


def _post(mix, x, mod_l, w_out, ln_g, ln_b, wr_t, br_t, l, latent):
    n = x.shape[0]
    per_batch = DEC_SEQ // TM_POST
    mod_map = (lambda i: (1 + i // per_batch, 0, 0)) if latent else (lambda i: (0, 0, 0))
    return pl.pallas_call(
        _post_kernel,
        grid=(n // TM_POST,),
        in_specs=[
            pl.BlockSpec((TM_POST, MIX_WIDTH), lambda i: (i, 0)),
            pl.BlockSpec((TM_POST, D_MODEL), lambda i: (i, 0)),
            pl.BlockSpec((None, 6, D_MODEL), mod_map),
            pl.BlockSpec((None, MIX_WIDTH, D_MODEL), lambda i: (l, 0, 0)),
            pl.BlockSpec((None, 1, D_MODEL), lambda i: (l, 0, 0)),
            pl.BlockSpec((None, 1, D_MODEL), lambda i: (l, 0, 0)),
            pl.BlockSpec((N_EXPERTS, D_MODEL), lambda i: (0, 0)),
            pl.BlockSpec((N_EXPERTS, 1), lambda i: (0, 0)),
        ],
        out_specs=[
            pl.BlockSpec((TM_POST, D_MODEL), lambda i: (i, 0)),
            pl.BlockSpec((TM_POST, D_MODEL), lambda i: (i, 0)),
            pl.BlockSpec((TM_POST, 128), lambda i: (i, 0)),
        ],
        out_shape=[
            jax.ShapeDtypeStruct((n, D_MODEL), F32),
            jax.ShapeDtypeStruct((n, D_MODEL), BF16),
            jax.ShapeDtypeStruct((n, 128), F32),
        ],
        compiler_params=_params("parallel"),
        name="post_latent" if latent else "post_context",
    )(mix, x, mod_l, w_out, ln_g.reshape(DEPTH, 1, D_MODEL), ln_b.reshape(DEPTH, 1, D_MODEL), wr_t, br_t)


def _moe_kernel(h_ref, gates_ref, x1_ref, mod_ref, w1_ref, w3_ref, w2_ref, g_ref, b_ref, o_ref, acc_ref):
    e = pl.program_id(1)

    @pl.when(e == 0)
    def _():
        acc_ref[...] = jnp.zeros_like(acc_ref)

    h = h_ref[...]
    gates = gates_ref[...]
    lane = lax.broadcasted_iota(jnp.int32, gates.shape, 1)
    gate = jnp.sum(jnp.where(lane == e, gates, 0.0), axis=-1, keepdims=True)
    a = _dot(h, w1_ref[...])
    g = _dot(h, w3_ref[...])
    hid = a * jax.nn.sigmoid(a) * g * gate
    acc_ref[...] += _dot(hid.astype(BF16), w2_ref[...])

    @pl.when(e == pl.num_programs(1) - 1)
    def _():
        mod = mod_ref[...]
        o_ref[...] = _ln(ALPHA * x1_ref[...] + mod[5:6] * acc_ref[...]) * g_ref[...] + b_ref[...]


def _moe(h2, gates, x1, mod_l, w1, w3, w2, ln_g, ln_b, l, latent):
    n = x1.shape[0]
    per_batch = DEC_SEQ // TM_MOE
    mod_map = (lambda i, e: (1 + i // per_batch, 0, 0)) if latent else (lambda i, e: (0, 0, 0))
    return pl.pallas_call(
        _moe_kernel,
        grid=(n // TM_MOE, N_EXPERTS),
        in_specs=[
            pl.BlockSpec((TM_MOE, D_MODEL), lambda i, e: (i, 0)),
            pl.BlockSpec((TM_MOE, 128), lambda i, e: (i, 0)),
            pl.BlockSpec((TM_MOE, D_MODEL), lambda i, e: (i, 0)),
            pl.BlockSpec((None, 6, D_MODEL), mod_map),
            pl.BlockSpec((None, None, D_MODEL, D_EXPERT), lambda i, e: (l, e, 0, 0)),
            pl.BlockSpec((None, None, D_MODEL, D_EXPERT), lambda i, e: (l, e, 0, 0)),
            pl.BlockSpec((None, None, D_EXPERT, D_MODEL), lambda i, e: (l, e, 0, 0)),
            pl.BlockSpec((None, 1, D_MODEL), lambda i, e: (l, 0, 0)),
            pl.BlockSpec((None, 1, D_MODEL), lambda i, e: (l, 0, 0)),
        ],
        out_specs=pl.BlockSpec((TM_MOE, D_MODEL), lambda i, e: (i, 0)),
        out_shape=jax.ShapeDtypeStruct((n, D_MODEL), F32),
        scratch_shapes=[pltpu.VMEM((TM_MOE, D_MODEL), F32)],
        compiler_params=_params("parallel", "arbitrary"),
        name="moe_latent" if latent else "moe_context",
    )(h2, gates, x1, mod_l, w1, w3, w2, ln_g.reshape(DEPTH, 1, D_MODEL), ln_b.reshape(DEPTH, 1, D_MODEL))


def _rope_tables():
    t = np.arange(DEC_SEQ)
    half = DA_DIM // 4
    inv_freq = ROPE_BASE ** (-np.arange(half, dtype=np.float32) / half)
    cos, sin = [], []
    for pos in (t // GRID_W, t % GRID_W):
        ang = pos.astype(np.float32)[:, None] * inv_freq[None, :]
        c, s = np.cos(ang), np.sin(ang)
        cos += [c, c]
        sin += [-s, s]
    cos = np.tile(np.concatenate(cos, axis=1), (1, 2 * DA_HEADS))
    sin = np.tile(np.concatenate(sin, axis=1), (1, 2 * DA_HEADS))
    return jnp.asarray(cos, F32), jnp.asarray(sin, F32)


_NA_TILE_KINDS = ((0, 0), (2 * QT_ROWS, 2 * QT_ROWS - NA_WIN_ROWS // 2), (GRID_ROWS - QT_ROWS, GRID_ROWS - NA_KEY_ROWS))


def _na_bias_kernel(by_ref, o_ref):
    qc = lax.broadcasted_iota(jnp.int32, (GRID_W, GRID_W), 0)
    kc = lax.broadcasted_iota(jnp.int32, (GRID_W, GRID_W), 1)
    win_c0 = jnp.clip(qc - NA_WIN_COLS // 2, 0, GRID_W - NA_WIN_COLS)
    in_cols = (kc >= win_c0) & (kc < win_c0 + NA_WIN_COLS)
    masked = jnp.full((GRID_W, GRID_W), -jnp.inf, F32)
    for kind, (r0, key_row0) in enumerate(_NA_TILE_KINDS):
        for a in range(QT_ROWS):
            qr = r0 + a
            win_r0 = min(max(qr - NA_WIN_ROWS // 2, 0), GRID_ROWS - NA_WIN_ROWS)
            blocks = []
            for b in range(NA_KEY_ROWS):
                kr = key_row0 + b
                if win_r0 <= kr < win_r0 + NA_WIN_ROWS:
                    blocks.append(jnp.where(in_cols, by_ref[kr - qr + NA_WIN_ROWS - 1], masked))
                else:
                    blocks.append(masked)
            o_ref[kind, a * GRID_W:(a + 1) * GRID_W, :] = jnp.concatenate(blocks, axis=1)


def _na_bias_tables(na_rel_bias):
    n_dr, n_dc = 2 * NA_WIN_ROWS - 1, 2 * NA_WIN_COLS - 1
    cols = np.arange(GRID_W)
    d_col = np.clip(cols[None, :] - cols[:, None], 1 - NA_WIN_COLS, NA_WIN_COLS - 1) + NA_WIN_COLS - 1
    col_sel = (d_col[None] == np.arange(n_dc)[:, None, None]).astype(np.float32)
    by_col = jnp.einsum('lhab,bqc->lhaqc', na_rel_bias.astype(F32), col_sel, precision=lax.Precision.HIGHEST)
    return pl.pallas_call(
        _na_bias_kernel,
        grid=(DEPTH, NA_HEADS),
        in_specs=[pl.BlockSpec((None, None, n_dr, GRID_W, GRID_W), lambda l, h: (l, h, 0, 0, 0))],
        out_specs=pl.BlockSpec((None, None, len(_NA_TILE_KINDS), QT, NA_KEYS), lambda l, h: (l, h, 0, 0, 0)),
        out_shape=jax.ShapeDtypeStruct((DEPTH, NA_HEADS, len(_NA_TILE_KINDS), QT, NA_KEYS), F32),
        compiler_params=_params("parallel", "parallel"),
        name="na_bias_table",
    )(by_col)


def kernel(x_prompt, x_sample, cache_na_k, cache_na_v, cache_da_k, cache_da_v, c, c_ctx, w_mod, b_mod, w_in,
           na_rel_bias, sc_conv_w, sc_conv_b, da_lambda, da_norm_g, w_out, ln1_g, ln1_b, w_router, b_router,
           moe_w1, moe_w3, moe_w2, ln2_g, ln2_b):
    xp = x_prompt.reshape(BATCH * SEQ, D_MODEL)
    xs = x_sample.reshape(DEC_BATCH * DEC_SEQ, D_MODEL)

    cond = jnp.concatenate([c_ctx[None, :], c, jnp.zeros((N_COND - 1 - DEC_BATCH, D_MODEL), F32)], axis=0)
    mod = _modulation(cond, w_mod, b_mod).reshape(DEPTH, N_COND, 6, D_MODEL)

    col_scale = np.ones((IN_WIDTH,), np.float32)
    col_scale[OFF_NA_Q:OFF_NA_K] = QK_SCALE
    col_scale[OFF_DA_Q:OFF_DA_K] = QK_SCALE
    w_in_bf = (w_in * col_scale).astype(BF16)
    w_out_bf = w_out.astype(BF16)
    w1_bf, w3_bf, w2_bf = moe_w1.astype(BF16), moe_w3.astype(BF16), moe_w2.astype(BF16)

    slot_major = np.arange(N_EXPERTS).reshape(N_GROUPS, PER_GROUP).T.reshape(-1)
    wr_t = w_router.T[slot_major]
    br_t = b_router.astype(F32)[slot_major].reshape(N_EXPERTS, 1)

    caches = (
        cache_na_k.reshape(DEC_BATCH, DEPTH, PAST_LEN, NA_WIDTH).astype(BF16),
        cache_na_v.reshape(DEC_BATCH, DEPTH, PAST_LEN, NA_WIDTH).astype(BF16),
        cache_da_k.reshape(DEC_BATCH, DEPTH, PAST_LEN, DA_QK_WIDTH).astype(BF16),
        cache_da_v.reshape(DEC_BATCH, DEPTH, PAST_LEN, DA_V_WIDTH).astype(BF16),
    )
    rope_tabs = _rope_tables()
    bias_tab = _na_bias_tables(na_rel_bias)

    new_caches = None
    for l in range(DEPTH):
        lam_init = 0.8 - 0.6 * math.exp(-0.3 * l)
        mod_l = mod[l]
        u_p, *new_caches = _inproj(xp, mod_l, w_in_bf, l, latent=False, cache_bufs=new_caches)
        u_s, = _inproj(xs, mod_l, w_in_bf, l, latent=True, rope_tabs=rope_tabs)
        mix_p = _ctx_mix(u_p, da_lambda, da_norm_g, sc_conv_w, sc_conv_b, l, lam_init)
        mix_s = _lat_mix(u_s, caches, bias_tab, da_lambda, da_norm_g, sc_conv_w, sc_conv_b, l, lam_init)
        x1p, h2p, gp = _post(mix_p, xp, mod_l, w_out_bf, ln1_g, ln1_b, wr_t, br_t, l, latent=False)
        x1s, h2s, gs = _post(mix_s, xs, mod_l, w_out_bf, ln1_g, ln1_b, wr_t, br_t, l, latent=True)
        xp = _moe(h2p, gp, x1p, mod_l, w1_bf, w3_bf, w2_bf, ln2_g, ln2_b, l, latent=False)
        xs = _moe(h2s, gs, x1s, mod_l, w1_bf, w3_bf, w2_bf, ln2_g, ln2_b, l, latent=True)

    nak, nav, dak, dav = new_caches
    return (xp.reshape(BATCH, SEQ, D_MODEL), xs.reshape(DEC_BATCH, DEC_SEQ, D_MODEL),
            nak.reshape(BATCH, DEPTH, SEQ, NA_HEADS, HEAD_DIM), nav.reshape(BATCH, DEPTH, SEQ, NA_HEADS, HEAD_DIM),
            dak.reshape(BATCH, DEPTH, SEQ, 2 * DA_HEADS, DA_DIM), dav.reshape(BATCH, DEPTH, SEQ, DA_HEADS, 2 * DA_DIM))
```

```python
import functools
import math

import numpy as np
import jax
import jax.numpy as jnp
from jax import lax
from jax.experimental import pallas as pl
from jax.experimental.pallas import tpu as pltpu

F32 = jnp.float32
BF16 = jnp.bfloat16

D_MODEL = 1024
BATCH = 16
SEQ = 256
DEPTH = 4
DEC_BATCH = 2
DEC_SEQ = 2048
PAST_LEN = 512
GRID_W = 64
GRID_ROWS = DEC_SEQ // GRID_W
HEAD_DIM = 64
NA_HEADS = 4
NA_WIDTH = NA_HEADS * HEAD_DIM
NA_WIN_ROWS = 8
NA_WIN_COLS = 16
SC_WIDTH = 256
DA_HEADS = 4
DA_DIM = 64
DA_QK_WIDTH = 2 * DA_HEADS * DA_DIM
DA_V_WIDTH = DA_HEADS * 2 * DA_DIM
MIX_WIDTH = NA_WIDTH + SC_WIDTH + DA_V_WIDTH
IN_WIDTH = 3 * NA_WIDTH + 3 * SC_WIDTH + 2 * DA_QK_WIDTH + DA_V_WIDTH
OFF_NA_Q = 0
OFF_NA_K = NA_WIDTH
OFF_NA_V = 2 * NA_WIDTH
OFF_SC_B = 3 * NA_WIDTH
OFF_SC_C = OFF_SC_B + SC_WIDTH
OFF_SC_X = OFF_SC_C + SC_WIDTH
OFF_DA_Q = OFF_SC_X + SC_WIDTH
OFF_DA_K = OFF_DA_Q + DA_QK_WIDTH
OFF_DA_V = OFF_DA_K + DA_QK_WIDTH
ROPE_BASE = 10000.0
N_EXPERTS = 16
N_GROUPS = 4
PER_GROUP = N_EXPERTS // N_GROUPS
D_EXPERT = 256
ALPHA = (2 * DEPTH) ** 0.25
LN_EPS = 1e-5
QK_SCALE = HEAD_DIM ** -0.5

N_COND = 8
TM = 256
TM_MOE = 512
TM_POST = TM_MOE
MOE_CHUNK = 128
MOE_SORTED = TM_MOE + (N_GROUPS - 1) * MOE_CHUNK
QT = 256
QT_ROWS = QT // GRID_W
NA_KEY_ROWS = 12
NA_KEYS = NA_KEY_ROWS * GRID_W
VMEM_LIMIT = 56 * 1024 * 1024


def _dot(a, b):
    return jnp.dot(a, b, preferred_element_type=F32)


def _dot_nt(a, b):
    return lax.dot_general(a, b, (((1,), (1,)), ((), ())), preferred_element_type=F32)


def _ln(x):
    mu = jnp.mean(x, -1, keepdims=True)
    xc = x - mu
    var = jnp.mean(xc * xc, -1, keepdims=True)
    return xc * lax.rsqrt(var + LN_EPS)


def _params(*sem):
    return pltpu.CompilerParams(dimension_semantics=sem, vmem_limit_bytes=VMEM_LIMIT)


def _mod_kernel(cond_ref, w_ref, b_ref, o_ref):
    c = cond_ref[...]
    s = (c * jax.nn.sigmoid(c)).astype(BF16)
    o_ref[...] = _dot(s, w_ref[...].astype(BF16)) + b_ref[...]


def _modulation(cond, w_mod, b_mod):
    tn = 1024
    return pl.pallas_call(
        _mod_kernel,
        grid=(DEPTH, 6 * D_MODEL // tn),
        in_specs=[
            pl.BlockSpec((N_COND, D_MODEL), lambda l, j: (0, 0)),
            pl.BlockSpec((None, D_MODEL, tn), lambda l, j: (l, 0, j)),
            pl.BlockSpec((None, 1, tn), lambda l, j: (l, 0, j)),
        ],
        out_specs=pl.BlockSpec((None, N_COND, tn), lambda l, j: (l, 0, j)),
        out_shape=jax.ShapeDtypeStruct((DEPTH, N_COND, 6 * D_MODEL), F32),
        compiler_params=_params("parallel", "parallel"),
        name="modulation",
    )(cond, w_mod, b_mod.reshape(DEPTH, 1, 6 * D_MODEL))


def _rope(t, cos, sin):
    lane = lax.broadcasted_iota(jnp.int32, t.shape, 1)
    first = (lane // 16) % 2 == 0
    n = t.shape[1]
    swapped = jnp.where(first, pltpu.roll(t, n - 16, 1), pltpu.roll(t, 16, 1))
    return t * cos + swapped * sin


def _inproj_kernel(*refs, latent):
    if latent:
        x_ref, mod_ref, w_ref, cos_ref, sin_ref, u_ref = refs
    else:
        x_ref, mod_ref, w_ref = refs[:3]
        u_ref, nak_ref, nav_ref, dak_ref, dav_ref = refs[-5:]
    mod = mod_ref[...]
    h = _ln(x_ref[...]) * (1.0 + mod[1:2]) + mod[0:1]
    u = _dot(h.astype(BF16), w_ref[...])
    if latent:
        cos, sin = cos_ref[...], sin_ref[...]
        u_ref[:, :OFF_DA_Q] = u[:, :OFF_DA_Q].astype(BF16)
        u_ref[:, OFF_DA_Q:OFF_DA_K] = _rope(u[:, OFF_DA_Q:OFF_DA_K], cos, sin).astype(BF16)
        u_ref[:, OFF_DA_K:OFF_DA_V] = _rope(u[:, OFF_DA_K:OFF_DA_V], cos, sin).astype(BF16)
        u_ref[:, OFF_DA_V:] = u[:, OFF_DA_V:].astype(BF16)
    else:
        u_ref[...] = u.astype(BF16)
        nak_ref[...] = u[:, OFF_NA_K:OFF_NA_V]
        nav_ref[...] = u[:, OFF_NA_V:OFF_SC_B]
        dak_ref[...] = u[:, OFF_DA_K:OFF_DA_V]
        dav_ref[...] = u[:, OFF_DA_V:]


def _inproj(x, mod_l, w_in, l, latent, rope_tabs=None, cache_bufs=None):
    n = x.shape[0]
    nt = n // TM
    per_batch = DEC_SEQ // TM
    if latent:
        mod_map = lambda i: (1 + i // per_batch, 0, 0)
    else:
        mod_map = lambda i: (0, 0, 0)
    in_specs = [
        pl.BlockSpec((TM, D_MODEL), lambda i: (i, 0)),
        pl.BlockSpec((None, 6, D_MODEL), mod_map),
        pl.BlockSpec((None, D_MODEL, IN_WIDTH), lambda i: (l, 0, 0)),
    ]
    args = [x, mod_l, w_in]
    out_specs = [pl.BlockSpec((TM, IN_WIDTH), lambda i: (i, 0))]
    out_shape = [jax.ShapeDtypeStruct((n, IN_WIDTH), BF16)]
    aliases = {}
    if latent:
        in_specs += [pl.BlockSpec((TM, DA_QK_WIDTH), lambda i: (i % per_batch, 0))] * 2
        args += list(rope_tabs)
    else:
        assert TM == SEQ
        for width in (NA_WIDTH, NA_WIDTH, DA_QK_WIDTH, DA_V_WIDTH):
            out_specs.append(pl.BlockSpec((None, None, SEQ, width), lambda i: (i, l, 0, 0)))
            out_shape.append(jax.ShapeDtypeStruct((BATCH, DEPTH, SEQ, width), F32))
        if cache_bufs is not None:
            aliases = {len(args) + k: 1 + k for k in range(4)}
            in_specs += [pl.BlockSpec(memory_space=pl.ANY)] * 4
            args += list(cache_bufs)
    return pl.pallas_call(
        functools.partial(_inproj_kernel, latent=latent),
        grid=(nt,),
        in_specs=in_specs,
        out_specs=out_specs,
        out_shape=out_shape,
        input_output_aliases=aliases,
        compiler_params=_params("parallel"),
        name="inproj_latent" if latent else "inproj_context",
    )(*args)


def _lambda(lam_ref, lam_init):
    lp = lam_ref[...]
    s1 = jnp.sum(lp[0:1] * lp[1:2], axis=-1, keepdims=True)
    s2 = jnp.sum(lp[2:3] * lp[3:4], axis=-1, keepdims=True)
    return jnp.exp(s1) - jnp.exp(s2) + lam_init


def _softmax_pv(score_parts, value_parts):
    m = functools.reduce(jnp.maximum, [jnp.max(s, -1, keepdims=True) for s in score_parts])
    es = [jnp.exp(s - m) for s in score_parts]
    denom = functools.reduce(jnp.add, [jnp.sum(e, -1, keepdims=True) for e in es])
    o = functools.reduce(jnp.add, [_dot(e.astype(BF16), v) for e, v in zip(es, value_parts)])
    return o / denom


def _diff_head_norm(o1, o2, lam, gain, lam_init):
    o = o1 - lam * o2
    o = o * lax.rsqrt(jnp.mean(o * o, -1, keepdims=True) + LN_EPS)
    return o * gain * (1.0 - lam_init)


def _gated_conv(b, v, v_prev_row, v_next_row, w, bias):
    n = v.shape[0]
    row = lax.broadcasted_iota(jnp.int32, v.shape, 0)
    prev = jnp.where(row == 0, v_prev_row, pltpu.roll(v, 1, 0))
    nxt = jnp.where(row == n - 1, v_next_row, pltpu.roll(v, n - 1, 0))
    return b * (prev * w[0:1] + v * w[1:2] + nxt * w[2:3] + bias)


def _ctx_mix_kernel(u_ref, lam_ref, gain_ref, cw_ref, cb_ref, o_ref, *, lam_init):
    lam = _lambda(lam_ref, lam_init)
    gain = gain_ref[...]
    outs = []
    for h in range(NA_HEADS):
        q = u_ref[:, OFF_NA_Q + h * HEAD_DIM:OFF_NA_Q + (h + 1) * HEAD_DIM]
        k = u_ref[:, OFF_NA_K + h * HEAD_DIM:OFF_NA_K + (h + 1) * HEAD_DIM]
        v = u_ref[:, OFF_NA_V + h * HEAD_DIM:OFF_NA_V + (h + 1) * HEAD_DIM]
        outs.append(_softmax_pv([_dot_nt(q, k)], [v]))
    zero_row = jnp.zeros((1, SC_WIDTH), F32)
    vc = u_ref[:, OFF_SC_C:OFF_SC_X].astype(F32) * u_ref[:, OFF_SC_X:OFF_DA_Q].astype(F32)
    outs.append(_gated_conv(u_ref[:, OFF_SC_B:OFF_SC_C].astype(F32), vc, zero_row, zero_row,
                            cw_ref[...], cb_ref[...]))
    for j in range(DA_HEADS):
        v = u_ref[:, OFF_DA_V + j * 2 * DA_DIM:OFF_DA_V + (j + 1) * 2 * DA_DIM]
        o12 = []
        for mi in (2 * j, 2 * j + 1):
            q = u_ref[:, OFF_DA_Q + mi * DA_DIM:OFF_DA_Q + (mi + 1) * DA_DIM]
            k = u_ref[:, OFF_DA_K + mi * DA_DIM:OFF_DA_K + (mi + 1) * DA_DIM]
            o12.append(_softmax_pv([_dot_nt(q, k)], [v]))
        outs.append(_diff_head_norm(o12[0], o12[1], lam, gain, lam_init))
    o_ref[...] = jnp.concatenate(outs, axis=-1).astype(BF16)


def _ctx_mix(u, da_lambda, da_norm_g, conv_w, conv_b, l, lam_init):
    return pl.pallas_call(
        functools.partial(_ctx_mix_kernel, lam_init=lam_init),
        grid=(BATCH,),
        in_specs=[
            pl.BlockSpec((SEQ, IN_WIDTH), lambda b: (b, 0)),
            pl.BlockSpec((None, 4, DA_DIM), lambda b: (l, 0, 0)),
            pl.BlockSpec((None, 1, 2 * DA_DIM), lambda b: (l, 0, 0)),
            pl.BlockSpec((None, 3, SC_WIDTH), lambda b: (l, 0, 0)),
            pl.BlockSpec((None, 1, SC_WIDTH), lambda b: (l, 0, 0)),
        ],
        out_specs=pl.BlockSpec((SEQ, MIX_WIDTH), lambda b: (b, 0)),
        out_shape=jax.ShapeDtypeStruct((BATCH * SEQ, MIX_WIDTH), BF16),
        compiler_params=_params("parallel"),
        name="mix_context",
    )(u, da_lambda, da_norm_g.reshape(DEPTH, 1, 2 * DA_DIM), conv_w, conv_b.reshape(DEPTH, 1, SC_WIDTH))


def _lat_mix_kernel(uq_ref, nak_ref, nav_ref, sc_ref, dakv_ref, cnak_ref, cnav_ref, cdak_ref, cdav_ref,
                    bias_ref, lam_ref, gain_ref, cw_ref, cb_ref, o_ref, *, lam_init):
    qt = pl.program_id(1)
    n_qt = pl.num_programs(1)
    lam = _lambda(lam_ref, lam_init)
    gain = gain_ref[...]
    outs = []
    key_row0 = jnp.clip(qt * QT_ROWS - NA_WIN_ROWS // 2, 0, GRID_ROWS - NA_KEY_ROWS)
    k0 = pl.multiple_of(key_row0 * GRID_W, GRID_W)
    for h in range(NA_HEADS):
        hs = slice(h * HEAD_DIM, (h + 1) * HEAD_DIM)
        q = uq_ref[:, OFF_NA_Q + h * HEAD_DIM:OFF_NA_Q + (h + 1) * HEAD_DIM]
        k_loc = nak_ref[pl.ds(k0, NA_KEYS), hs]
        v_loc = nav_ref[pl.ds(k0, NA_KEYS), hs]
        s_loc = _dot_nt(q, k_loc) + bias_ref[h]
        s_ctx = _dot_nt(q, cnak_ref[:, hs])
        outs.append(_softmax_pv([s_loc, s_ctx], [v_loc, cnav_ref[:, hs]]))
    t0 = pl.multiple_of(qt * QT, QT)
    halo = 16
    before = sc_ref[pl.ds(pl.multiple_of(jnp.maximum(t0 - halo, 0), halo), halo), :].astype(F32)
    after = sc_ref[pl.ds(pl.multiple_of(jnp.minimum(t0 + QT, DEC_SEQ - halo), halo), halo), :].astype(F32)
    cur = sc_ref[pl.ds(t0, QT), :].astype(F32)
    v_prev = before[halo - 1:halo, SC_WIDTH:2 * SC_WIDTH] * before[halo - 1:halo, 2 * SC_WIDTH:]
    v_next = after[0:1, SC_WIDTH:2 * SC_WIDTH] * after[0:1, 2 * SC_WIDTH:]
    v_prev = jnp.where(qt > 0, v_prev, 0.0)
    v_next = jnp.where(qt < n_qt - 1, v_next, 0.0)
    outs.append(_gated_conv(cur[:, :SC_WIDTH], cur[:, SC_WIDTH:2 * SC_WIDTH] * cur[:, 2 * SC_WIDTH:],
                            v_prev, v_next, cw_ref[...], cb_ref[...]))
    for j in range(DA_HEADS):
        vs = slice(DA_QK_WIDTH + j * 2 * DA_DIM, DA_QK_WIDTH + (j + 1) * 2 * DA_DIM)
        v_lat = dakv_ref[:, vs]
        v_ctx = cdav_ref[:, j * 2 * DA_DIM:(j + 1) * 2 * DA_DIM]
        o12 = []
        for mi in (2 * j, 2 * j + 1):
            ms = slice(mi * DA_DIM, (mi + 1) * DA_DIM)
            q = uq_ref[:, OFF_DA_Q + mi * DA_DIM:OFF_DA_Q + (mi + 1) * DA_DIM]
            s_lat = _dot_nt(q, dakv_ref[:, ms])
            s_ctx = _dot_nt(q, cdak_ref[:, ms])
            o12.append(_softmax_pv([s_lat, s_ctx], [v_lat, v_ctx]))
        outs.append(_diff_head_norm(o12[0], o12[1], lam, gain, lam_init))
    o_ref[...] = jnp.concatenate(outs, axis=-1).astype(BF16)


def _lat_mix(u, caches, bias_tab, da_lambda, da_norm_g, conv_w, conv_b, l, lam_init):
    cnak, cnav, cdak, cdav = caches
    n_qt = DEC_SEQ // QT

    def bias_map(b, qt):
        return (l, 0, jnp.where(qt == 0, 0, jnp.where(qt == n_qt - 1, 2, 1)), 0, 0)

    return pl.pallas_call(
        functools.partial(_lat_mix_kernel, lam_init=lam_init),
        grid=(DEC_BATCH, n_qt),
        in_specs=[
            pl.BlockSpec((QT, IN_WIDTH), lambda b, qt: (b * n_qt + qt, 0)),
            pl.BlockSpec((DEC_SEQ, NA_WIDTH), lambda b, qt: (b, OFF_NA_K // NA_WIDTH)),
            pl.BlockSpec((DEC_SEQ, NA_WIDTH), lambda b, qt: (b, OFF_NA_V // NA_WIDTH)),
            pl.BlockSpec((DEC_SEQ, 3 * SC_WIDTH), lambda b, qt: (b, OFF_SC_B // (3 * SC_WIDTH))),
            pl.BlockSpec((DEC_SEQ, DA_QK_WIDTH + DA_V_WIDTH), lambda b, qt: (b, OFF_DA_K // (DA_QK_WIDTH + DA_V_WIDTH))),
            pl.BlockSpec((None, None, PAST_LEN, NA_WIDTH), lambda b, qt: (b, l, 0, 0)),
            pl.BlockSpec((None, None, PAST_LEN, NA_WIDTH), lambda b, qt: (b, l, 0, 0)),
            pl.BlockSpec((None, None, PAST_LEN, DA_QK_WIDTH), lambda b, qt: (b, l, 0, 0)),
            pl.BlockSpec((None, None, PAST_LEN, DA_V_WIDTH), lambda b, qt: (b, l, 0, 0)),
            pl.BlockSpec((None, NA_HEADS, None, QT, NA_KEYS), bias_map),
            pl.BlockSpec((None, 4, DA_DIM), lambda b, qt: (l, 0, 0)),
            pl.BlockSpec((None, 1, 2 * DA_DIM), lambda b, qt: (l, 0, 0)),
            pl.BlockSpec((None, 3, SC_WIDTH), lambda b, qt: (l, 0, 0)),
            pl.BlockSpec((None, 1, SC_WIDTH), lambda b, qt: (l, 0, 0)),
        ],
        out_specs=pl.BlockSpec((QT, MIX_WIDTH), lambda b, qt: (b * n_qt + qt, 0)),
        out_shape=jax.ShapeDtypeStruct((DEC_BATCH * DEC_SEQ, MIX_WIDTH), BF16),
        compiler_params=_params("parallel", "arbitrary"),
        name="mix_latent",
    )(u, u, u, u, u, cnak, cnav, cdak, cdav, bias_tab, da_lambda,
      da_norm_g.reshape(DEPTH, 1, 2 * DA_DIM), conv_w, conv_b.reshape(DEPTH, 1, SC_WIDTH))


def _first_max(vals):
    m = functools.reduce(jnp.maximum, vals)
    hot, taken = [], None
    for v in vals:
        is_max = v == m
        if taken is None:
            hot.append(is_max)
            taken = is_max
        else:
            hot.append(is_max & ~taken)
            taken = taken | is_max
    return hot, m


def _pick(hot, vals):
    out = vals[-1]
    for h, v in zip(hot[-2::-1], vals[-2::-1]):
        out = jnp.where(h, v, out)
    return out


def _route(h2, h2_bf, wr_ref, br_ref):
    tm = h2.shape[0]
    w = wr_ref[...]
    w_hi = w.astype(BF16)
    w_lo = (w - w_hi.astype(F32)).astype(BF16)
    h_lo = (h2 - h2_bf.astype(F32)).astype(BF16)
    z = _dot_nt(w_hi, h2_bf) + _dot_nt(w_hi, h_lo) + _dot_nt(w_lo, h2_bf)
    scores = jax.nn.sigmoid(z)
    biased = scores + br_ref[...]
    P = [biased[k * N_GROUPS:(k + 1) * N_GROUPS] for k in range(PER_GROUP)]
    S = [scores[k * N_GROUPS:(k + 1) * N_GROUPS] for k in range(PER_GROUP)]
    pair_sums = [P[i] + P[j] for i in range(PER_GROUP) for j in range(i + 1, PER_GROUP)]
    group_score = functools.reduce(jnp.maximum, pair_sums)
    sel, _ = _first_max([group_score[g:g + 1] for g in range(N_GROUPS)])
    c = [_pick(sel, [P[k][g:g + 1] for g in range(N_GROUPS)]) for k in range(PER_GROUP)]
    cs = [_pick(sel, [S[k][g:g + 1] for g in range(N_GROUPS)]) for k in range(PER_GROUP)]
    t1, _ = _first_max(c)
    t2, _ = _first_max([jnp.where(t, -jnp.inf, v) for t, v in zip(t1, c)])
    w1 = functools.reduce(jnp.add, [jnp.where(t, v, 0.0) for t, v in zip(t1, cs)])
    w2 = functools.reduce(jnp.add, [jnp.where(t, v, 0.0) for t, v in zip(t2, cs)])
    total = w1 + w2
    slot_gate = [jnp.where(a, w1 / total, jnp.where(b, w2 / total, 0.0)) for a, b in zip(t1, t2)]
    onehot = jnp.concatenate([s.astype(F32) for s in sel] + [jnp.zeros((8 - N_GROUPS, tm), F32)], axis=0)
    earlier = (lax.broadcasted_iota(jnp.int32, (tm, tm), 0) < lax.broadcasted_iota(jnp.int32, (tm, tm), 1))
    rank = _dot(onehot.astype(BF16), earlier.astype(BF16))
    count = jnp.sum(onehot, axis=1, keepdims=True)
    n_chunks = jnp.floor((count + (MOE_CHUNK - 1)) * (1.0 / MOE_CHUNK))
    pos = jnp.zeros((1, tm), F32)
    start = jnp.zeros((1, 1), F32)
    for g in range(N_GROUPS):
        pos = jnp.where(sel[g], start + rank[g:g + 1], pos)
        start = start + n_chunks[g:g + 1] * MOE_CHUNK
    rows = [pos] + slot_gate + [jnp.zeros((128 - 1 - PER_GROUP, tm), F32)]
    return jnp.concatenate(rows, axis=0), n_chunks


def _post_kernel(mix_ref, x_ref, mod_ref, wout_ref, g_ref, b_ref, wr_ref, br_ref,
                 x1_ref, h2_ref, tok_ref, pos_ref, nch_ref):
    mod = mod_ref[...]
    y = _dot(mix_ref[...], wout_ref[...])
    x1 = _ln(ALPHA * x_ref[...] + mod[2:3] * y) * g_ref[...] + b_ref[...]
    h2 = _ln(x1) * (1.0 + mod[4:5]) + mod[3:4]
    h2_bf = h2.astype(BF16)
    x1_ref[...] = x1
    h2_ref[...] = h2_bf
    rows, n_chunks = _route(h2, h2_bf, wr_ref, br_ref)
    tok_ref[...] = rows.T
    pos_ref[...] = rows[0:8]
    nch_ref[...] = jnp.broadcast_to(n_chunks, (8, 128)).astype(jnp.int32)


def _post(mix, x, mod_l, w_out, ln_g, ln_b, wr_t, br_t, l, latent):
    n = x.shape[0]
    per_batch = DEC_SEQ // TM_POST
    mod_map = (lambda i: (1 + i // per_batch, 0, 0)) if latent else (lambda i: (0, 0, 0))
    return pl.pallas_call(
        _post_kernel,
        grid=(n // TM_POST,),
        in_specs=[
            pl.BlockSpec((TM_POST, MIX_WIDTH), lambda i: (i, 0)),
            pl.BlockSpec((TM_POST, D_MODEL), lambda i: (i, 0)),
            pl.BlockSpec((None, 6, D_MODEL), mod_map),
            pl.BlockSpec((None, MIX_WIDTH, D_MODEL), lambda i: (l, 0, 0)),
            pl.BlockSpec((None, 1, D_MODEL), lambda i: (l, 0, 0)),
            pl.BlockSpec((None, 1, D_MODEL), lambda i: (l, 0, 0)),
            pl.BlockSpec((N_EXPERTS, D_MODEL), lambda i: (0, 0)),
            pl.BlockSpec((N_EXPERTS, 1), lambda i: (0, 0)),
        ],
        out_specs=[
            pl.BlockSpec((TM_POST, D_MODEL), lambda i: (i, 0)),
            pl.BlockSpec((TM_POST, D_MODEL), lambda i: (i, 0)),
            pl.BlockSpec((TM_POST, 128), lambda i: (i, 0)),
            pl.BlockSpec((None, 8, TM_POST), lambda i: (i, 0, 0)),
            pl.BlockSpec((None, 8, 128), lambda i: (i, 0, 0)),
        ],
        out_shape=[
            jax.ShapeDtypeStruct((n, D_MODEL), F32),
            jax.ShapeDtypeStruct((n, D_MODEL), BF16),
            jax.ShapeDtypeStruct((n, 128), F32),
            jax.ShapeDtypeStruct((n // TM_POST, 8, TM_POST), F32),
            jax.ShapeDtypeStruct((n // TM_POST, 8, 128), jnp.int32),
        ],
        compiler_params=_params("parallel"),
        name="post_latent" if latent else "post_context",
    )(mix, x, mod_l, w_out, ln_g.reshape(DEPTH, 1, D_MODEL), ln_b.reshape(DEPTH, 1, D_MODEL), wr_t, br_t)


def _moe_kernel(nch_ref, h_ref, tok_ref, pos_ref, x1_ref, mod_ref, w1_ref, w3_ref, w2_ref, g_ref, b_ref, o_ref,
                hs_ref, gs_ref, ys_ref):
    i = pl.program_id(0)
    tok = tok_ref[...]
    pos_lane = pos_ref[0:1, :]
    pos_col = tok[:, 0:1]
    slot = lax.broadcasted_iota(jnp.int32, (MOE_SORTED, TM_MOE), 0).astype(F32)
    sort = (slot == pos_lane).astype(BF16)
    tok_hi = tok.astype(BF16)
    tok_lo = (tok - tok_hi.astype(F32)).astype(BF16)
    z = _dot(sort, jnp.concatenate([h_ref[...], tok_hi, tok_lo], axis=1))
    hs_ref[...] = z[:, :D_MODEL].astype(BF16)
    gs_ref[...] = z[:, D_MODEL:D_MODEL + 128] + z[:, D_MODEL + 128:]
    ys_ref[...] = jnp.zeros_like(ys_ref)

    first = 0
    for g in range(N_GROUPS):
        n = nch_ref[i * N_GROUPS + g]

        def chunk(c, carry, g=g, first=first):
            r0 = pl.multiple_of((first + c) * MOE_CHUNK, MOE_CHUNK)
            rows = hs_ref[pl.ds(r0, MOE_CHUNK), :]
            gates = gs_ref[pl.ds(r0, MOE_CHUNK), :]
            acc = jnp.zeros((MOE_CHUNK, D_MODEL), F32)
            for k in range(PER_GROUP):
                e = g * PER_GROUP + k
                a = _dot(rows, w1_ref[e])
                b = _dot(rows, w3_ref[e])
                hid = a * jax.nn.sigmoid(a) * b * gates[:, 1 + k:2 + k]
                acc = acc + _dot(hid.astype(BF16), w2_ref[e])
            ys_ref[pl.ds(r0, MOE_CHUNK), :] = acc.astype(BF16)
            return carry

        lax.fori_loop(0, n, chunk, 0)
        first = first + n

    unsort = (lax.broadcasted_iota(jnp.int32, (TM_MOE, MOE_SORTED), 1).astype(F32) == pos_col).astype(BF16)
    y = _dot(unsort, ys_ref[...])
    mod = mod_ref[...]
    o_ref[...] = _ln(ALPHA * x1_ref[...] + mod[5:6] * y) * g_ref[...] + b_ref[...]


def _moe(h2, tok, pos, nch, x1, mod_l, w1, w3, w2, ln_g, ln_b, l, latent):
    n = x1.shape[0]
    per_batch = DEC_SEQ // TM_MOE
    mod_map = (lambda i, s: (1 + i // per_batch, 0, 0)) if latent else (lambda i, s: (0, 0, 0))
    resident = pl.Buffered(1)
    grid_spec = pltpu.PrefetchScalarGridSpec(
        num_scalar_prefetch=1,
        grid=(n // TM_MOE,),
        in_specs=[
            pl.BlockSpec((TM_MOE, D_MODEL), lambda i, s: (i, 0)),
            pl.BlockSpec((TM_MOE, 128), lambda i, s: (i, 0)),
            pl.BlockSpec((None, 8, TM_MOE), lambda i, s: (i, 0, 0)),
            pl.BlockSpec((TM_MOE, D_MODEL), lambda i, s: (i, 0)),
            pl.BlockSpec((None, 6, D_MODEL), mod_map),
            pl.BlockSpec((None, N_EXPERTS, D_MODEL, D_EXPERT), lambda i, s: (l, 0, 0, 0), pipeline_mode=resident),
            pl.BlockSpec((None, N_EXPERTS, D_MODEL, D_EXPERT), lambda i, s: (l, 0, 0, 0), pipeline_mode=resident),
            pl.BlockSpec((None, N_EXPERTS, D_EXPERT, D_MODEL), lambda i, s: (l, 0, 0, 0), pipeline_mode=resident),
            pl.BlockSpec((None, 1, D_MODEL), lambda i, s: (l, 0, 0)),
            pl.BlockSpec((None, 1, D_MODEL), lambda i, s: (l, 0, 0)),
        ],
        out_specs=pl.BlockSpec((TM_MOE, D_MODEL), lambda i, s: (i, 0)),
        scratch_shapes=[
            pltpu.VMEM((MOE_SORTED, D_MODEL), BF16),
            pltpu.VMEM((MOE_SORTED, 128), F32),
            pltpu.VMEM((MOE_SORTED, D_MODEL), BF16),
        ],
    )
    return pl.pallas_call(
        _moe_kernel,
        grid_spec=grid_spec,
        out_shape=jax.ShapeDtypeStruct((n, D_MODEL), F32),
        compiler_params=_params("arbitrary"),
        name="moe_latent" if latent else "moe_context",
    )(nch[:, :N_GROUPS, 0].reshape(-1), h2, tok, pos, x1, mod_l, w1, w3, w2,
      ln_g.reshape(DEPTH, 1, D_MODEL), ln_b.reshape(DEPTH, 1, D_MODEL))


def _rope_tables():
    t = np.arange(DEC_SEQ)
    half = DA_DIM // 4
    inv_freq = ROPE_BASE ** (-np.arange(half, dtype=np.float32) / half)
    cos, sin = [], []
    for pos in (t // GRID_W, t % GRID_W):
        ang = pos.astype(np.float32)[:, None] * inv_freq[None, :]
        c, s = np.cos(ang), np.sin(ang)
        cos += [c, c]
        sin += [-s, s]
    cos = np.tile(np.concatenate(cos, axis=1), (1, 2 * DA_HEADS))
    sin = np.tile(np.concatenate(sin, axis=1), (1, 2 * DA_HEADS))
    return jnp.asarray(cos, F32), jnp.asarray(sin, F32)


_NA_TILE_KINDS = ((0, 0), (2 * QT_ROWS, 2 * QT_ROWS - NA_WIN_ROWS // 2), (GRID_ROWS - QT_ROWS, GRID_ROWS - NA_KEY_ROWS))


def _na_bias_kernel(by_ref, o_ref):
    qc = lax.broadcasted_iota(jnp.int32, (GRID_W, GRID_W), 0)
    kc = lax.broadcasted_iota(jnp.int32, (GRID_W, GRID_W), 1)
    win_c0 = jnp.clip(qc - NA_WIN_COLS // 2, 0, GRID_W - NA_WIN_COLS)
    in_cols = (kc >= win_c0) & (kc < win_c0 + NA_WIN_COLS)
    masked = jnp.full((GRID_W, GRID_W), -jnp.inf, F32)
    for kind, (r0, key_row0) in enumerate(_NA_TILE_KINDS):
        for a in range(QT_ROWS):
            qr = r0 + a
            win_r0 = min(max(qr - NA_WIN_ROWS // 2, 0), GRID_ROWS - NA_WIN_ROWS)
            blocks = []
            for b in range(NA_KEY_ROWS):
                kr = key_row0 + b
                if win_r0 <= kr < win_r0 + NA_WIN_ROWS:
                    blocks.append(jnp.where(in_cols, by_ref[kr - qr + NA_WIN_ROWS - 1], masked))
                else:
                    blocks.append(masked)
            o_ref[kind, a * GRID_W:(a + 1) * GRID_W, :] = jnp.concatenate(blocks, axis=1)


def _na_bias_tables(na_rel_bias):
    n_dr, n_dc = 2 * NA_WIN_ROWS - 1, 2 * NA_WIN_COLS - 1
    cols = np.arange(GRID_W)
    d_col = np.clip(cols[None, :] - cols[:, None], 1 - NA_WIN_COLS, NA_WIN_COLS - 1) + NA_WIN_COLS - 1
    col_sel = (d_col[None] == np.arange(n_dc)[:, None, None]).astype(np.float32)
    by_col = jnp.einsum('lhab,bqc->lhaqc', na_rel_bias.astype(F32), col_sel, precision=lax.Precision.HIGHEST)
    return pl.pallas_call(
        _na_bias_kernel,
        grid=(DEPTH, NA_HEADS),
        in_specs=[pl.BlockSpec((None, None, n_dr, GRID_W, GRID_W), lambda l, h: (l, h, 0, 0, 0))],
        out_specs=pl.BlockSpec((None, None, len(_NA_TILE_KINDS), QT, NA_KEYS), lambda l, h: (l, h, 0, 0, 0)),
        out_shape=jax.ShapeDtypeStruct((DEPTH, NA_HEADS, len(_NA_TILE_KINDS), QT, NA_KEYS), F32),
        compiler_params=_params("parallel", "parallel"),
        name="na_bias_table",
    )(by_col)


def kernel(x_prompt, x_sample, cache_na_k, cache_na_v, cache_da_k, cache_da_v, c, c_ctx, w_mod, b_mod, w_in,
           na_rel_bias, sc_conv_w, sc_conv_b, da_lambda, da_norm_g, w_out, ln1_g, ln1_b, w_router, b_router,
           moe_w1, moe_w3, moe_w2, ln2_g, ln2_b):
    xp = x_prompt.reshape(BATCH * SEQ, D_MODEL)
    xs = x_sample.reshape(DEC_BATCH * DEC_SEQ, D_MODEL)

    cond = jnp.concatenate([c_ctx[None, :], c, jnp.zeros((N_COND - 1 - DEC_BATCH, D_MODEL), F32)], axis=0)
    mod = _modulation(cond, w_mod, b_mod).reshape(DEPTH, N_COND, 6, D_MODEL)

    col_scale = np.ones((IN_WIDTH,), np.float32)
    col_scale[OFF_NA_Q:OFF_NA_K] = QK_SCALE
    col_scale[OFF_DA_Q:OFF_DA_K] = QK_SCALE
    w_in_bf = (w_in * col_scale).astype(BF16)
    w_out_bf = w_out.astype(BF16)
    w1_bf, w3_bf, w2_bf = moe_w1.astype(BF16), moe_w3.astype(BF16), moe_w2.astype(BF16)

    slot_major = np.arange(N_EXPERTS).reshape(N_GROUPS, PER_GROUP).T.reshape(-1)
    wr_t = w_router.T[slot_major]
    br_t = b_router.astype(F32)[slot_major].reshape(N_EXPERTS, 1)

    caches = (
        cache_na_k.reshape(DEC_BATCH, DEPTH, PAST_LEN, NA_WIDTH).astype(BF16),
        cache_na_v.reshape(DEC_BATCH, DEPTH, PAST_LEN, NA_WIDTH).astype(BF16),
        cache_da_k.reshape(DEC_BATCH, DEPTH, PAST_LEN, DA_QK_WIDTH).astype(BF16),
        cache_da_v.reshape(DEC_BATCH, DEPTH, PAST_LEN, DA_V_WIDTH).astype(BF16),
    )
    rope_tabs = _rope_tables()
    bias_tab = _na_bias_tables(na_rel_bias)

    new_caches = None
    for l in range(DEPTH):
        lam_init = 0.8 - 0.6 * math.exp(-0.3 * l)
        mod_l = mod[l]
        u_p, *new_caches = _inproj(xp, mod_l, w_in_bf, l, latent=False, cache_bufs=new_caches)
        u_s, = _inproj(xs, mod_l, w_in_bf, l, latent=True, rope_tabs=rope_tabs)
        mix_p = _ctx_mix(u_p, da_lambda, da_norm_g, sc_conv_w, sc_conv_b, l, lam_init)
        mix_s = _lat_mix(u_s, caches, bias_tab, da_lambda, da_norm_g, sc_conv_w, sc_conv_b, l, lam_init)
        x1p, h2p, *route_p = _post(mix_p, xp, mod_l, w_out_bf, ln1_g, ln1_b, wr_t, br_t, l, latent=False)
        x1s, h2s, *route_s = _post(mix_s, xs, mod_l, w_out_bf, ln1_g, ln1_b, wr_t, br_t, l, latent=True)
        xp = _moe(h2p, *route_p, x1p, mod_l, w1_bf, w3_bf, w2_bf, ln2_g, ln2_b, l, latent=False)
        xs = _moe(h2s, *route_s, x1s, mod_l, w1_bf, w3_bf, w2_bf, ln2_g, ln2_b, l, latent=True)

    nak, nav, dak, dav = new_caches
    return (xp.reshape(BATCH, SEQ, D_MODEL), xs.reshape(DEC_BATCH, DEC_SEQ, D_MODEL),
            nak.reshape(BATCH, DEPTH, SEQ, NA_HEADS, HEAD_DIM), nav.reshape(BATCH, DEPTH, SEQ, NA_HEADS, HEAD_DIM),
            dak.reshape(BATCH, DEPTH, SEQ, 2 * DA_HEADS, DA_DIM), dav.reshape(BATCH, DEPTH, SEQ, DA_HEADS, 2 * DA_DIM))
```

```python
import functools
import math

import numpy as np
import jax
import jax.numpy as jnp
from jax import lax
from jax.experimental import pallas as pl
from jax.experimental.pallas import tpu as pltpu

F32 = jnp.float32
BF16 = jnp.bfloat16

D_MODEL = 1024
BATCH = 16
SEQ = 256
DEPTH = 4
DEC_BATCH = 2
DEC_SEQ = 2048
PAST_LEN = 512
GRID_W = 64
GRID_ROWS = DEC_SEQ // GRID_W
HEAD_DIM = 64
NA_HEADS = 4
NA_WIDTH = NA_HEADS * HEAD_DIM
NA_WIN_ROWS = 8
NA_WIN_COLS = 16
SC_WIDTH = 256
DA_HEADS = 4
DA_DIM = 64
DA_QK_WIDTH = 2 * DA_HEADS * DA_DIM
DA_V_WIDTH = DA_HEADS * 2 * DA_DIM
MIX_WIDTH = NA_WIDTH + SC_WIDTH + DA_V_WIDTH
IN_WIDTH = 3 * NA_WIDTH + 3 * SC_WIDTH + 2 * DA_QK_WIDTH + DA_V_WIDTH
OFF_NA_Q = 0
OFF_NA_K = NA_WIDTH
OFF_NA_V = 2 * NA_WIDTH
OFF_SC_B = 3 * NA_WIDTH
OFF_SC_C = OFF_SC_B + SC_WIDTH
OFF_SC_X = OFF_SC_C + SC_WIDTH
OFF_DA_Q = OFF_SC_X + SC_WIDTH
OFF_DA_K = OFF_DA_Q + DA_QK_WIDTH
OFF_DA_V = OFF_DA_K + DA_QK_WIDTH
ROPE_BASE = 10000.0
N_EXPERTS = 16
N_GROUPS = 4
PER_GROUP = N_EXPERTS // N_GROUPS
D_EXPERT = 256
ALPHA = (2 * DEPTH) ** 0.25
LN_EPS = 1e-5
LOG2E = math.log2(math.e)
Q_SCALE = HEAD_DIM ** -0.5 * LOG2E

N_COND = 8
TM = 512
SEQ_PER_TILE = TM // SEQ
TM_MOE = 512
TM_POST = TM_MOE
MOE_CHUNK = 128
MOE_SORTED = TM_MOE + (N_GROUPS - 1) * MOE_CHUNK
QT = 256
QT_ROWS = QT // GRID_W
NA_KEY_ROWS = 12
NA_KEYS = NA_KEY_ROWS * GRID_W
VMEM_LIMIT = 56 * 1024 * 1024


def _dot(a, b):
    return jnp.dot(a, b, preferred_element_type=F32)


def _dot_nt(a, b):
    return lax.dot_general(a, b, (((1,), (1,)), ((), ())), preferred_element_type=F32)


def _ln(x):
    mu = jnp.mean(x, -1, keepdims=True)
    xc = x - mu
    var = jnp.mean(xc * xc, -1, keepdims=True)
    return xc * lax.rsqrt(var + LN_EPS)


def _params(*sem):
    return pltpu.CompilerParams(dimension_semantics=sem, vmem_limit_bytes=VMEM_LIMIT)


def _mod_kernel(cond_ref, w_ref, b_ref, o_ref):
    c = cond_ref[...]
    s = (c * jax.nn.sigmoid(c)).astype(BF16)
    o_ref[...] = _dot(s, w_ref[...].astype(BF16)) + b_ref[...]


def _modulation(cond, w_mod, b_mod):
    tn = 1024
    return pl.pallas_call(
        _mod_kernel,
        grid=(DEPTH, 6 * D_MODEL // tn),
        in_specs=[
            pl.BlockSpec((N_COND, D_MODEL), lambda l, j: (0, 0)),
            pl.BlockSpec((None, D_MODEL, tn), lambda l, j: (l, 0, j)),
            pl.BlockSpec((None, 1, tn), lambda l, j: (l, 0, j)),
        ],
        out_specs=pl.BlockSpec((None, N_COND, tn), lambda l, j: (l, 0, j)),
        out_shape=jax.ShapeDtypeStruct((DEPTH, N_COND, 6 * D_MODEL), F32),
        compiler_params=_params("parallel", "parallel"),
        name="modulation",
    )(cond, w_mod, b_mod.reshape(DEPTH, 1, 6 * D_MODEL))


def _rope(t, cos, sin):
    lane = lax.broadcasted_iota(jnp.int32, t.shape, 1)
    first = (lane // 16) % 2 == 0
    n = t.shape[1]
    swapped = jnp.where(first, pltpu.roll(t, n - 16, 1), pltpu.roll(t, 16, 1))
    return t * cos + swapped * sin


def _inproj_kernel(*refs, latent):
    if latent:
        x_ref, mod_ref, w_ref, cos_ref, sin_ref, u_ref = refs
    else:
        x_ref, mod_ref, w_ref = refs[:3]
        u_ref, nak_ref, nav_ref, dak_ref, dav_ref = refs[-5:]
    mod = mod_ref[...]
    h = _ln(x_ref[...]) * (1.0 + mod[1:2]) + mod[0:1]
    u = _dot(h.astype(BF16), w_ref[...])
    da_q, da_k = u[:, OFF_DA_Q:OFF_DA_K], u[:, OFF_DA_K:OFF_DA_V]
    if latent:
        cos, sin = cos_ref[...], sin_ref[...]
        da_q, da_k = _rope(da_q, cos, sin), _rope(da_k, cos, sin)
    u_ref[:, :OFF_NA_K] = (u[:, :OFF_NA_K] * Q_SCALE).astype(BF16)
    u_ref[:, OFF_NA_K:OFF_DA_Q] = u[:, OFF_NA_K:OFF_DA_Q].astype(BF16)
    u_ref[:, OFF_DA_Q:OFF_DA_K] = (da_q * Q_SCALE).astype(BF16)
    u_ref[:, OFF_DA_K:OFF_DA_V] = da_k.astype(BF16)
    u_ref[:, OFF_DA_V:] = u[:, OFF_DA_V:].astype(BF16)
    if not latent:
        for s in range(SEQ_PER_TILE):
            rows = slice(s * SEQ, (s + 1) * SEQ)
            nak_ref[s] = u[rows, OFF_NA_K:OFF_NA_V]
            nav_ref[s] = u[rows, OFF_NA_V:OFF_SC_B]
            dak_ref[s] = u[rows, OFF_DA_K:OFF_DA_V]
            dav_ref[s] = u[rows, OFF_DA_V:]


def _inproj(x, mod_l, w_in, l, latent, rope_tabs=None, cache_bufs=None):
    n = x.shape[0]
    nt = n // TM
    per_batch = DEC_SEQ // TM
    if latent:
        mod_map = lambda i: (1 + i // per_batch, 0, 0)
    else:
        mod_map = lambda i: (0, 0, 0)
    in_specs = [
        pl.BlockSpec((TM, D_MODEL), lambda i: (i, 0)),
        pl.BlockSpec((None, 6, D_MODEL), mod_map),
        pl.BlockSpec((None, D_MODEL, IN_WIDTH), lambda i: (l, 0, 0)),
    ]
    args = [x, mod_l, w_in]
    out_specs = [pl.BlockSpec((TM, IN_WIDTH), lambda i: (i, 0))]
    out_shape = [jax.ShapeDtypeStruct((n, IN_WIDTH), BF16)]
    aliases = {}
    if latent:
        in_specs += [pl.BlockSpec((TM, DA_QK_WIDTH), lambda i: (i % per_batch, 0))] * 2
        args += list(rope_tabs)
    else:
        for width in (NA_WIDTH, NA_WIDTH, DA_QK_WIDTH, DA_V_WIDTH):
            out_specs.append(pl.BlockSpec((SEQ_PER_TILE, None, SEQ, width), lambda i: (i, l, 0, 0)))
            out_shape.append(jax.ShapeDtypeStruct((BATCH, DEPTH, SEQ, width), F32))
        if cache_bufs is not None:
            aliases = {len(args) + k: 1 + k for k in range(4)}
            in_specs += [pl.BlockSpec(memory_space=pl.ANY)] * 4
            args += list(cache_bufs)
    return pl.pallas_call(
        functools.partial(_inproj_kernel, latent=latent),
        grid=(nt,),
        in_specs=in_specs,
        out_specs=out_specs,
        out_shape=out_shape,
        input_output_aliases=aliases,
        compiler_params=_params("parallel"),
        name="inproj_latent" if latent else "inproj_context",
    )(*args)


def _lambda(lam_ref, lam_init):
    lp = lam_ref[...]
    s1 = jnp.sum(lp[0:1] * lp[1:2], axis=-1, keepdims=True)
    s2 = jnp.sum(lp[2:3] * lp[3:4], axis=-1, keepdims=True)
    return jnp.exp(s1) - jnp.exp(s2) + lam_init


def _with_ones(v):
    return jnp.concatenate([v, jnp.ones((v.shape[0], max(v.shape[1], 64)), v.dtype)], axis=1)


def _softmax_pv(score_parts, v_ones, d):
    m = functools.reduce(jnp.maximum, [jnp.max(s, -1, keepdims=True) for s in score_parts])
    e = [jnp.exp2(s - m).astype(BF16) for s in score_parts]
    o = _dot(e[0] if len(e) == 1 else jnp.concatenate(e, axis=1), v_ones)
    return o[:, :d] / o[:, d:d + 1]


def _diff_head_norm(o1, o2, lam, gain, lam_init):
    o = o1 - lam * o2
    o = o * lax.rsqrt(jnp.mean(o * o, -1, keepdims=True) + LN_EPS)
    return o * gain * (1.0 - lam_init)


def _gated_conv(b, v, v_prev_row, v_next_row, w, bias):
    n = v.shape[0]
    row = lax.broadcasted_iota(jnp.int32, v.shape, 0)
    prev = jnp.where(row == 0, v_prev_row, pltpu.roll(v, 1, 0))
    nxt = jnp.where(row == n - 1, v_next_row, pltpu.roll(v, n - 1, 0))
    return b * (prev * w[0:1] + v * w[1:2] + nxt * w[2:3] + bias)


def _ctx_mix_kernel(u_ref, lam_ref, gain_ref, cw_ref, cb_ref, o_ref, *, lam_init):
    lam = _lambda(lam_ref, lam_init)
    gain = gain_ref[...]
    outs = []
    for h in range(NA_HEADS):
        q = u_ref[:, OFF_NA_Q + h * HEAD_DIM:OFF_NA_Q + (h + 1) * HEAD_DIM]
        k = u_ref[:, OFF_NA_K + h * HEAD_DIM:OFF_NA_K + (h + 1) * HEAD_DIM]
        v = u_ref[:, OFF_NA_V + h * HEAD_DIM:OFF_NA_V + (h + 1) * HEAD_DIM]
        outs.append(_softmax_pv([_dot_nt(q, k)], _with_ones(v), HEAD_DIM))
    zero_row = jnp.zeros((1, SC_WIDTH), F32)
    vc = u_ref[:, OFF_SC_C:OFF_SC_X].astype(F32) * u_ref[:, OFF_SC_X:OFF_DA_Q].astype(F32)
    outs.append(_gated_conv(u_ref[:, OFF_SC_B:OFF_SC_C].astype(F32), vc, zero_row, zero_row,
                            cw_ref[...], cb_ref[...]))
    for j in range(DA_HEADS):
        v = _with_ones(u_ref[:, OFF_DA_V + j * 2 * DA_DIM:OFF_DA_V + (j + 1) * 2 * DA_DIM])
        o12 = []
        for mi in (2 * j, 2 * j + 1):
            q = u_ref[:, OFF_DA_Q + mi * DA_DIM:OFF_DA_Q + (mi + 1) * DA_DIM]
            k = u_ref[:, OFF_DA_K + mi * DA_DIM:OFF_DA_K + (mi + 1) * DA_DIM]
            o12.append(_softmax_pv([_dot_nt(q, k)], v, 2 * DA_DIM))
        outs.append(_diff_head_norm(o12[0], o12[1], lam, gain, lam_init))
    o_ref[...] = jnp.concatenate(outs, axis=-1).astype(BF16)


def _ctx_mix(u, da_lambda, da_norm_g, conv_w, conv_b, l, lam_init):
    return pl.pallas_call(
        functools.partial(_ctx_mix_kernel, lam_init=lam_init),
        grid=(BATCH,),
        in_specs=[
            pl.BlockSpec((SEQ, IN_WIDTH), lambda b: (b, 0)),
            pl.BlockSpec((None, 4, DA_DIM), lambda b: (l, 0, 0)),
            pl.BlockSpec((None, 1, 2 * DA_DIM), lambda b: (l, 0, 0)),
            pl.BlockSpec((None, 3, SC_WIDTH), lambda b: (l, 0, 0)),
            pl.BlockSpec((None, 1, SC_WIDTH), lambda b: (l, 0, 0)),
        ],
        out_specs=pl.BlockSpec((SEQ, MIX_WIDTH), lambda b: (b, 0)),
        out_shape=jax.ShapeDtypeStruct((BATCH * SEQ, MIX_WIDTH), BF16),
        compiler_params=_params("parallel"),
        name="mix_context",
    )(u, da_lambda, da_norm_g.reshape(DEPTH, 1, 2 * DA_DIM), conv_w, conv_b.reshape(DEPTH, 1, SC_WIDTH))


def _lat_mix_kernel(uq_ref, nak_ref, nav_ref, sc_ref, dakv_ref, cnak_ref, cnav_ref, cdak_ref, cdav_ref,
                    bias_ref, lam_ref, gain_ref, cw_ref, cb_ref, o_ref, kall_ref, vall_ref, cnav1_ref, *, lam_init):
    qt = pl.program_id(1)
    n_qt = pl.num_programs(1)
    lam = _lambda(lam_ref, lam_init)
    gain = gain_ref[...]

    @pl.when(qt == 0)
    def _():
        kall_ref[:DEC_SEQ, :] = dakv_ref[:, :DA_QK_WIDTH]
        kall_ref[DEC_SEQ:, :] = cdak_ref[...]
        for j in range(DA_HEADS):
            vs = slice(j * 2 * DA_DIM, (j + 1) * 2 * DA_DIM)
            cols = slice(j * 4 * DA_DIM, (j + 1) * 4 * DA_DIM)
            vall_ref[:DEC_SEQ, cols] = _with_ones(dakv_ref[:, DA_QK_WIDTH + j * 2 * DA_DIM:DA_QK_WIDTH + (j + 1) * 2 * DA_DIM])
            vall_ref[DEC_SEQ:, cols] = _with_ones(cdav_ref[:, vs])
        for h in range(NA_HEADS):
            cnav1_ref[:, h * 2 * HEAD_DIM:(h + 1) * 2 * HEAD_DIM] = _with_ones(cnav_ref[:, h * HEAD_DIM:(h + 1) * HEAD_DIM])

    outs = []
    key_row0 = jnp.clip(qt * QT_ROWS - NA_WIN_ROWS // 2, 0, GRID_ROWS - NA_KEY_ROWS)
    k0 = pl.multiple_of(key_row0 * GRID_W, GRID_W)
    for h in range(NA_HEADS):
        hs = slice(h * HEAD_DIM, (h + 1) * HEAD_DIM)
        q = uq_ref[:, OFF_NA_Q + h * HEAD_DIM:OFF_NA_Q + (h + 1) * HEAD_DIM]
        k_loc = nak_ref[pl.ds(k0, NA_KEYS), hs]
        v_ones = jnp.concatenate([_with_ones(nav_ref[pl.ds(k0, NA_KEYS), hs]),
                                  cnav1_ref[:, h * 2 * HEAD_DIM:(h + 1) * 2 * HEAD_DIM]], axis=0)
        s_loc = _dot_nt(q, k_loc) + bias_ref[h]
        s_ctx = _dot_nt(q, cnak_ref[:, hs])
        outs.append(_softmax_pv([s_loc, s_ctx], v_ones, HEAD_DIM))
    t0 = pl.multiple_of(qt * QT, QT)
    halo = 16
    before = sc_ref[pl.ds(pl.multiple_of(jnp.maximum(t0 - halo, 0), halo), halo), :].astype(F32)
    after = sc_ref[pl.ds(pl.multiple_of(jnp.minimum(t0 + QT, DEC_SEQ - halo), halo), halo), :].astype(F32)
    cur = sc_ref[pl.ds(t0, QT), :].astype(F32)
    v_prev = before[halo - 1:halo, SC_WIDTH:2 * SC_WIDTH] * before[halo - 1:halo, 2 * SC_WIDTH:]
    v_next = after[0:1, SC_WIDTH:2 * SC_WIDTH] * after[0:1, 2 * SC_WIDTH:]
    v_prev = jnp.where(qt > 0, v_prev, 0.0)
    v_next = jnp.where(qt < n_qt - 1, v_next, 0.0)
    outs.append(_gated_conv(cur[:, :SC_WIDTH], cur[:, SC_WIDTH:2 * SC_WIDTH] * cur[:, 2 * SC_WIDTH:],
                            v_prev, v_next, cw_ref[...], cb_ref[...]))
    for j in range(DA_HEADS):
        v_ones = vall_ref[:, j * 4 * DA_DIM:(j + 1) * 4 * DA_DIM]
        o12 = []
        for mi in (2 * j, 2 * j + 1):
            q = uq_ref[:, OFF_DA_Q + mi * DA_DIM:OFF_DA_Q + (mi + 1) * DA_DIM]
            s = _dot_nt(q, kall_ref[:, mi * DA_DIM:(mi + 1) * DA_DIM])
            o12.append(_softmax_pv([s], v_ones, 2 * DA_DIM))
        outs.append(_diff_head_norm(o12[0], o12[1], lam, gain, lam_init))
    o_ref[...] = jnp.concatenate(outs, axis=-1).astype(BF16)


def _lat_mix(u, caches, bias_tab, da_lambda, da_norm_g, conv_w, conv_b, l, lam_init):
    cnak, cnav, cdak, cdav = caches
    n_qt = DEC_SEQ // QT

    def bias_map(b, qt):
        return (l, 0, jnp.where(qt == 0, 0, jnp.where(qt == n_qt - 1, 2, 1)), 0, 0)

    return pl.pallas_call(
        functools.partial(_lat_mix_kernel, lam_init=lam_init),
        grid=(DEC_BATCH, n_qt),
        in_specs=[
            pl.BlockSpec((QT, IN_WIDTH), lambda b, qt: (b * n_qt + qt, 0)),
            pl.BlockSpec((DEC_SEQ, NA_WIDTH), lambda b, qt: (b, OFF_NA_K // NA_WIDTH)),
            pl.BlockSpec((DEC_SEQ, NA_WIDTH), lambda b, qt: (b, OFF_NA_V // NA_WIDTH)),
            pl.BlockSpec((DEC_SEQ, 3 * SC_WIDTH), lambda b, qt: (b, OFF_SC_B // (3 * SC_WIDTH))),
            pl.BlockSpec((DEC_SEQ, DA_QK_WIDTH + DA_V_WIDTH), lambda b, qt: (b, OFF_DA_K // (DA_QK_WIDTH + DA_V_WIDTH))),
            pl.BlockSpec((None, None, PAST_LEN, NA_WIDTH), lambda b, qt: (b, l, 0, 0)),
            pl.BlockSpec((None, None, PAST_LEN, NA_WIDTH), lambda b, qt: (b, l, 0, 0)),
            pl.BlockSpec((None, None, PAST_LEN, DA_QK_WIDTH), lambda b, qt: (b, l, 0, 0)),
            pl.BlockSpec((None, None, PAST_LEN, DA_V_WIDTH), lambda b, qt: (b, l, 0, 0)),
            pl.BlockSpec((None, NA_HEADS, None, QT, NA_KEYS), bias_map),
            pl.BlockSpec((None, 4, DA_DIM), lambda b, qt: (l, 0, 0)),
            pl.BlockSpec((None, 1, 2 * DA_DIM), lambda b, qt: (l, 0, 0)),
            pl.BlockSpec((None, 3, SC_WIDTH), lambda b, qt: (l, 0, 0)),
            pl.BlockSpec((None, 1, SC_WIDTH), lambda b, qt: (l, 0, 0)),
        ],
        out_specs=pl.BlockSpec((QT, MIX_WIDTH), lambda b, qt: (b * n_qt + qt, 0)),
        out_shape=jax.ShapeDtypeStruct((DEC_BATCH * DEC_SEQ, MIX_WIDTH), BF16),
        scratch_shapes=[
            pltpu.VMEM((DEC_SEQ + PAST_LEN, DA_QK_WIDTH), BF16),
            pltpu.VMEM((DEC_SEQ + PAST_LEN, 2 * DA_V_WIDTH), BF16),
            pltpu.VMEM((PAST_LEN, 2 * NA_WIDTH), BF16),
        ],
        compiler_params=_params("parallel", "arbitrary"),
        name="mix_latent",
    )(u, u, u, u, u, cnak, cnav, cdak, cdav, bias_tab, da_lambda,
      da_norm_g.reshape(DEPTH, 1, 2 * DA_DIM), conv_w, conv_b.reshape(DEPTH, 1, SC_WIDTH))


def _first_max(vals):
    m = functools.reduce(jnp.maximum, vals)
    hot, taken = [], None
    for v in vals:
        is_max = v == m
        if taken is None:
            hot.append(is_max)
            taken = is_max
        else:
            hot.append(is_max & ~taken)
            taken = taken | is_max
    return hot, m


def _pick(hot, vals):
    out = vals[-1]
    for h, v in zip(hot[-2::-1], vals[-2::-1]):
        out = jnp.where(h, v, out)
    return out


def _route(h2, h2_bf, wr_ref, br_ref):
    tm = h2.shape[0]
    w = wr_ref[...]
    w_hi = w.astype(BF16)
    w_lo = (w - w_hi.astype(F32)).astype(BF16)
    h_lo = (h2 - h2_bf.astype(F32)).astype(BF16)
    z = _dot_nt(w_hi, h2_bf) + _dot_nt(w_hi, h_lo) + _dot_nt(w_lo, h2_bf)
    scores = jax.nn.sigmoid(z)
    biased = scores + br_ref[...]
    P = [biased[k * N_GROUPS:(k + 1) * N_GROUPS] for k in range(PER_GROUP)]
    S = [scores[k * N_GROUPS:(k + 1) * N_GROUPS] for k in range(PER_GROUP)]
    pair_sums = [P[i] + P[j] for i in range(PER_GROUP) for j in range(i + 1, PER_GROUP)]
    group_score = functools.reduce(jnp.maximum, pair_sums)
    sel, _ = _first_max([group_score[g:g + 1] for g in range(N_GROUPS)])
    c = [_pick(sel, [P[k][g:g + 1] for g in range(N_GROUPS)]) for k in range(PER_GROUP)]
    cs = [_pick(sel, [S[k][g:g + 1] for g in range(N_GROUPS)]) for k in range(PER_GROUP)]
    t1, _ = _first_max(c)
    t2, _ = _first_max([jnp.where(t, -jnp.inf, v) for t, v in zip(t1, c)])
    w1 = functools.reduce(jnp.add, [jnp.where(t, v, 0.0) for t, v in zip(t1, cs)])
    w2 = functools.reduce(jnp.add, [jnp.where(t, v, 0.0) for t, v in zip(t2, cs)])
    total = w1 + w2
    slot_gate = [jnp.where(a, w1 / total, jnp.where(b, w2 / total, 0.0)) for a, b in zip(t1, t2)]
    onehot = jnp.concatenate([s.astype(F32) for s in sel] + [jnp.zeros((8 - N_GROUPS, tm), F32)], axis=0)
    earlier = (lax.broadcasted_iota(jnp.int32, (tm, tm), 0) < lax.broadcasted_iota(jnp.int32, (tm, tm), 1))
    rank = _dot(onehot.astype(BF16), earlier.astype(BF16))
    count = jnp.sum(onehot, axis=1, keepdims=True)
    n_chunks = jnp.floor((count + (MOE_CHUNK - 1)) * (1.0 / MOE_CHUNK))
    pos = jnp.zeros((1, tm), F32)
    start = jnp.zeros((1, 1), F32)
    for g in range(N_GROUPS):
        pos = jnp.where(sel[g], start + rank[g:g + 1], pos)
        start = start + n_chunks[g:g + 1] * MOE_CHUNK
    rows = [pos] + slot_gate + [jnp.zeros((128 - 1 - PER_GROUP, tm), F32)]
    return jnp.concatenate(rows, axis=0), n_chunks


def _post_kernel(mix_ref, x_ref, mod_ref, wout_ref, g_ref, b_ref, wr_ref, br_ref,
                 x1_ref, h2_ref, tok_ref, pos_ref, nch_ref):
    mod = mod_ref[...]
    y = _dot(mix_ref[...], wout_ref[...])
    x1 = _ln(ALPHA * x_ref[...] + mod[2:3] * y) * g_ref[...] + b_ref[...]
    h2 = _ln(x1) * (1.0 + mod[4:5]) + mod[3:4]
    h2_bf = h2.astype(BF16)
    x1_ref[...] = x1
    h2_ref[...] = h2_bf
    rows, n_chunks = _route(h2, h2_bf, wr_ref, br_ref)
    tok_ref[...] = rows.T
    pos_ref[...] = rows[0:8]
    nch_ref[...] = jnp.broadcast_to(n_chunks, (8, 128)).astype(jnp.int32)


def _post(mix, x, mod_l, w_out, ln_g, ln_b, wr_t, br_t, l, latent):
    n = x.shape[0]
    per_batch = DEC_SEQ // TM_POST
    mod_map = (lambda i: (1 + i // per_batch, 0, 0)) if latent else (lambda i: (0, 0, 0))
    return pl.pallas_call(
        _post_kernel,
        grid=(n // TM_POST,),
        in_specs=[
            pl.BlockSpec((TM_POST, MIX_WIDTH), lambda i: (i, 0)),
            pl.BlockSpec((TM_POST, D_MODEL), lambda i: (i, 0)),
            pl.BlockSpec((None, 6, D_MODEL), mod_map),
            pl.BlockSpec((None, MIX_WIDTH, D_MODEL), lambda i: (l, 0, 0)),
            pl.BlockSpec((None, 1, D_MODEL), lambda i: (l, 0, 0)),
            pl.BlockSpec((None, 1, D_MODEL), lambda i: (l, 0, 0)),
            pl.BlockSpec((N_EXPERTS, D_MODEL), lambda i: (0, 0)),
            pl.BlockSpec((N_EXPERTS, 1), lambda i: (0, 0)),
        ],
        out_specs=[
            pl.BlockSpec((TM_POST, D_MODEL), lambda i: (i, 0)),
            pl.BlockSpec((TM_POST, D_MODEL), lambda i: (i, 0)),
            pl.BlockSpec((TM_POST, 128), lambda i: (i, 0)),
            pl.BlockSpec((None, 8, TM_POST), lambda i: (i, 0, 0)),
            pl.BlockSpec((None, 8, 128), lambda i: (i, 0, 0)),
        ],
        out_shape=[
            jax.ShapeDtypeStruct((n, D_MODEL), F32),
            jax.ShapeDtypeStruct((n, D_MODEL), BF16),
            jax.ShapeDtypeStruct((n, 128), F32),
            jax.ShapeDtypeStruct((n // TM_POST, 8, TM_POST), F32),
            jax.ShapeDtypeStruct((n // TM_POST, 8, 128), jnp.int32),
        ],
        compiler_params=_params("parallel"),
        name="post_latent" if latent else "post_context",
    )(mix, x, mod_l, w_out, ln_g.reshape(DEPTH, 1, D_MODEL), ln_b.reshape(DEPTH, 1, D_MODEL), wr_t, br_t)


def _moe_kernel(nch_ref, h_ref, tok_ref, pos_ref, x1_ref, mod_ref, w1_ref, w3_ref, w2_ref, g_ref, b_ref, o_ref,
                hs_ref, gs_ref, ys_ref):
    i = pl.program_id(0)
    tok = tok_ref[...]
    pos_lane = pos_ref[0:1, :]
    pos_col = tok[:, 0:1]
    slot = lax.broadcasted_iota(jnp.int32, (MOE_SORTED, TM_MOE), 0).astype(F32)
    sort = (slot == pos_lane).astype(BF16)
    tok_hi = tok.astype(BF16)
    tok_lo = (tok - tok_hi.astype(F32)).astype(BF16)
    z = _dot(sort, jnp.concatenate([h_ref[...], tok_hi, tok_lo], axis=1))
    hs_ref[...] = z[:, :D_MODEL].astype(BF16)
    gs_ref[...] = z[:, D_MODEL:D_MODEL + 128] + z[:, D_MODEL + 128:]
    ys_ref[...] = jnp.zeros_like(ys_ref)

    first = 0
    for g in range(N_GROUPS):
        n = nch_ref[i * N_GROUPS + g]

        def chunk(c, carry, g=g, first=first):
            r0 = pl.multiple_of((first + c) * MOE_CHUNK, MOE_CHUNK)
            rows = hs_ref[pl.ds(r0, MOE_CHUNK), :]
            gates = gs_ref[pl.ds(r0, MOE_CHUNK), :]
            acc = jnp.zeros((MOE_CHUNK, D_MODEL), F32)
            for k in range(PER_GROUP):
                e = g * PER_GROUP + k
                a = _dot(rows, w1_ref[e])
                b = _dot(rows, w3_ref[e])
                hid = a * jax.nn.sigmoid(a) * b * gates[:, 1 + k:2 + k]
                acc = acc + _dot(hid.astype(BF16), w2_ref[e])
            ys_ref[pl.ds(r0, MOE_CHUNK), :] = acc.astype(BF16)
            return carry

        lax.fori_loop(0, n, chunk, 0)
        first = first + n

    unsort = (lax.broadcasted_iota(jnp.int32, (TM_MOE, MOE_SORTED), 1).astype(F32) == pos_col).astype(BF16)
    y = _dot(unsort, ys_ref[...])
    mod = mod_ref[...]
    o_ref[...] = _ln(ALPHA * x1_ref[...] + mod[5:6] * y) * g_ref[...] + b_ref[...]


def _moe(h2, tok, pos, nch, x1, mod_l, w1, w3, w2, ln_g, ln_b, l, latent):
    n = x1.shape[0]
    per_batch = DEC_SEQ // TM_MOE
    mod_map = (lambda i, s: (1 + i // per_batch, 0, 0)) if latent else (lambda i, s: (0, 0, 0))
    resident = pl.Buffered(1)
    grid_spec = pltpu.PrefetchScalarGridSpec(
        num_scalar_prefetch=1,
        grid=(n // TM_MOE,),
        in_specs=[
            pl.BlockSpec((TM_MOE, D_MODEL), lambda i, s: (i, 0)),
            pl.BlockSpec((TM_MOE, 128), lambda i, s: (i, 0)),
            pl.BlockSpec((None, 8, TM_MOE), lambda i, s: (i, 0, 0)),
            pl.BlockSpec((TM_MOE, D_MODEL), lambda i, s: (i, 0)),
            pl.BlockSpec((None, 6, D_MODEL), mod_map),
            pl.BlockSpec((None, N_EXPERTS, D_MODEL, D_EXPERT), lambda i, s: (l, 0, 0, 0), pipeline_mode=resident),
            pl.BlockSpec((None, N_EXPERTS, D_MODEL, D_EXPERT), lambda i, s: (l, 0, 0, 0), pipeline_mode=resident),
            pl.BlockSpec((None, N_EXPERTS, D_EXPERT, D_MODEL), lambda i, s: (l, 0, 0, 0), pipeline_mode=resident),
            pl.BlockSpec((None, 1, D_MODEL), lambda i, s: (l, 0, 0)),
            pl.BlockSpec((None, 1, D_MODEL), lambda i, s: (l, 0, 0)),
        ],
        out_specs=pl.BlockSpec((TM_MOE, D_MODEL), lambda i, s: (i, 0)),
        scratch_shapes=[
            pltpu.VMEM((MOE_SORTED, D_MODEL), BF16),
            pltpu.VMEM((MOE_SORTED, 128), F32),
            pltpu.VMEM((MOE_SORTED, D_MODEL), BF16),
        ],
    )
    return pl.pallas_call(
        _moe_kernel,
        grid_spec=grid_spec,
        out_shape=jax.ShapeDtypeStruct((n, D_MODEL), F32),
        compiler_params=_params("arbitrary"),
        name="moe_latent" if latent else "moe_context",
    )(nch[:, :N_GROUPS, 0].reshape(-1), h2, tok, pos, x1, mod_l, w1, w3, w2,
      ln_g.reshape(DEPTH, 1, D_MODEL), ln_b.reshape(DEPTH, 1, D_MODEL))


def _rope_tables():
    t = np.arange(DEC_SEQ)
    half = DA_DIM // 4
    inv_freq = ROPE_BASE ** (-np.arange(half, dtype=np.float32) / half)
    cos, sin = [], []
    for pos in (t // GRID_W, t % GRID_W):
        ang = pos.astype(np.float32)[:, None] * inv_freq[None, :]
        c, s = np.cos(ang), np.sin(ang)
        cos += [c, c]
        sin += [-s, s]
    cos = np.tile(np.concatenate(cos, axis=1), (1, 2 * DA_HEADS))
    sin = np.tile(np.concatenate(sin, axis=1), (1, 2 * DA_HEADS))
    return jnp.asarray(cos, F32), jnp.asarray(sin, F32)


_NA_TILE_KINDS = ((0, 0), (2 * QT_ROWS, 2 * QT_ROWS - NA_WIN_ROWS // 2), (GRID_ROWS - QT_ROWS, GRID_ROWS - NA_KEY_ROWS))


def _na_bias_kernel(by_ref, o_ref):
    qc = lax.broadcasted_iota(jnp.int32, (GRID_W, GRID_W), 0)
    kc = lax.broadcasted_iota(jnp.int32, (GRID_W, GRID_W), 1)
    win_c0 = jnp.clip(qc - NA_WIN_COLS // 2, 0, GRID_W - NA_WIN_COLS)
    in_cols = (kc >= win_c0) & (kc < win_c0 + NA_WIN_COLS)
    masked = jnp.full((GRID_W, GRID_W), -jnp.inf, F32)
    for kind, (r0, key_row0) in enumerate(_NA_TILE_KINDS):
        for a in range(QT_ROWS):
            qr = r0 + a
            win_r0 = min(max(qr - NA_WIN_ROWS // 2, 0), GRID_ROWS - NA_WIN_ROWS)
            blocks = []
            for b in range(NA_KEY_ROWS):
                kr = key_row0 + b
                if win_r0 <= kr < win_r0 + NA_WIN_ROWS:
                    blocks.append(jnp.where(in_cols, by_ref[kr - qr + NA_WIN_ROWS - 1] * LOG2E, masked))
                else:
                    blocks.append(masked)
            o_ref[kind, a * GRID_W:(a + 1) * GRID_W, :] = jnp.concatenate(blocks, axis=1)


def _na_bias_tables(na_rel_bias):
    n_dr, n_dc = 2 * NA_WIN_ROWS - 1, 2 * NA_WIN_COLS - 1
    cols = np.arange(GRID_W)
    d_col = np.clip(cols[None, :] - cols[:, None], 1 - NA_WIN_COLS, NA_WIN_COLS - 1) + NA_WIN_COLS - 1
    col_sel = (d_col[None] == np.arange(n_dc)[:, None, None]).astype(np.float32)
    by_col = jnp.einsum('lhab,bqc->lhaqc', na_rel_bias.astype(F32), col_sel, precision=lax.Precision.HIGHEST)
    return pl.pallas_call(
        _na_bias_kernel,
        grid=(DEPTH, NA_HEADS),
        in_specs=[pl.BlockSpec((None, None, n_dr, GRID_W, GRID_W), lambda l, h: (l, h, 0, 0, 0))],
        out_specs=pl.BlockSpec((None, None, len(_NA_TILE_KINDS), QT, NA_KEYS), lambda l, h: (l, h, 0, 0, 0)),
        out_shape=jax.ShapeDtypeStruct((DEPTH, NA_HEADS, len(_NA_TILE_KINDS), QT, NA_KEYS), F32),
        compiler_params=_params("parallel", "parallel"),
        name="na_bias_table",
    )(by_col)


def kernel(x_prompt, x_sample, cache_na_k, cache_na_v, cache_da_k, cache_da_v, c, c_ctx, w_mod, b_mod, w_in,
           na_rel_bias, sc_conv_w, sc_conv_b, da_lambda, da_norm_g, w_out, ln1_g, ln1_b, w_router, b_router,
           moe_w1, moe_w3, moe_w2, ln2_g, ln2_b):
    xp = x_prompt.reshape(BATCH * SEQ, D_MODEL)
    xs = x_sample.reshape(DEC_BATCH * DEC_SEQ, D_MODEL)

    cond = jnp.concatenate([c_ctx[None, :], c, jnp.zeros((N_COND - 1 - DEC_BATCH, D_MODEL), F32)], axis=0)
    mod = _modulation(cond, w_mod, b_mod).reshape(DEPTH, N_COND, 6, D_MODEL)

    w_in_bf = w_in.astype(BF16)
    w_out_bf = w_out.astype(BF16)
    w1_bf, w3_bf, w2_bf = moe_w1.astype(BF16), moe_w3.astype(BF16), moe_w2.astype(BF16)

    slot_major = np.arange(N_EXPERTS).reshape(N_GROUPS, PER_GROUP).T.reshape(-1)
    wr_t = w_router.T[slot_major]
    br_t = b_router.astype(F32)[slot_major].reshape(N_EXPERTS, 1)

    caches = (
        cache_na_k.reshape(DEC_BATCH, DEPTH, PAST_LEN, NA_WIDTH).astype(BF16),
        cache_na_v.reshape(DEC_BATCH, DEPTH, PAST_LEN, NA_WIDTH).astype(BF16),
        cache_da_k.reshape(DEC_BATCH, DEPTH, PAST_LEN, DA_QK_WIDTH).astype(BF16),
        cache_da_v.reshape(DEC_BATCH, DEPTH, PAST_LEN, DA_V_WIDTH).astype(BF16),
    )
    rope_tabs = _rope_tables()
    bias_tab = _na_bias_tables(na_rel_bias)

    new_caches = None
    for l in range(DEPTH):
        lam_init = 0.8 - 0.6 * math.exp(-0.3 * l)
        mod_l = mod[l]
        u_p, *new_caches = _inproj(xp, mod_l, w_in_bf, l, latent=False, cache_bufs=new_caches)
        u_s, = _inproj(xs, mod_l, w_in_bf, l, latent=True, rope_tabs=rope_tabs)
        mix_p = _ctx_mix(u_p, da_lambda, da_norm_g, sc_conv_w, sc_conv_b, l, lam_init)
        mix_s = _lat_mix(u_s, caches, bias_tab, da_lambda, da_norm_g, sc_conv_w, sc_conv_b, l, lam_init)
        x1p, h2p, *route_p = _post(mix_p, xp, mod_l, w_out_bf, ln1_g, ln1_b, wr_t, br_t, l, latent=False)
        x1s, h2s, *route_s = _post(mix_s, xs, mod_l, w_out_bf, ln1_g, ln1_b, wr_t, br_t, l, latent=True)
        xp = _moe(h2p, *route_p, x1p, mod_l, w1_bf, w3_bf, w2_bf, ln2_g, ln2_b, l, latent=False)
        xs = _moe(h2s, *route_s, x1s, mod_l, w1_bf, w3_bf, w2_bf, ln2_g, ln2_b, l, latent=True)

    nak, nav, dak, dav = new_caches
    return (xp.reshape(BATCH, SEQ, D_MODEL), xs.reshape(DEC_BATCH, DEC_SEQ, D_MODEL),
            nak.reshape(BATCH, DEPTH, SEQ, NA_HEADS, HEAD_DIM), nav.reshape(BATCH, DEPTH, SEQ, NA_HEADS, HEAD_DIM),
            dak.reshape(BATCH, DEPTH, SEQ, 2 * DA_HEADS, DA_DIM), dav.reshape(BATCH, DEPTH, SEQ, DA_HEADS, 2 * DA_DIM))
```

```python
import functools
import math

import numpy as np
import jax
import jax.numpy as jnp
from jax import lax
from jax.experimental import pallas as pl
from jax.experimental.pallas import tpu as pltpu

F32 = jnp.float32
BF16 = jnp.bfloat16

D_MODEL = 1024
BATCH = 16
SEQ = 256
DEPTH = 4
DEC_BATCH = 2
DEC_SEQ = 2048
PAST_LEN = 512
GRID_W = 64
GRID_ROWS = DEC_SEQ // GRID_W
HEAD_DIM = 64
NA_HEADS = 4
NA_WIDTH = NA_HEADS * HEAD_DIM
NA_WIN_ROWS = 8
NA_WIN_COLS = 16
SC_WIDTH = 256
DA_HEADS = 4
DA_DIM = 64
DA_QK_WIDTH = 2 * DA_HEADS * DA_DIM
DA_V_WIDTH = DA_HEADS * 2 * DA_DIM
MIX_WIDTH = NA_WIDTH + SC_WIDTH + DA_V_WIDTH
IN_WIDTH = 3 * NA_WIDTH + 3 * SC_WIDTH + 2 * DA_QK_WIDTH + DA_V_WIDTH
OFF_NA_Q = 0
OFF_NA_K = NA_WIDTH
OFF_NA_V = 2 * NA_WIDTH
OFF_SC_B = 3 * NA_WIDTH
OFF_SC_C = OFF_SC_B + SC_WIDTH
OFF_SC_X = OFF_SC_C + SC_WIDTH
OFF_DA_Q = OFF_SC_X + SC_WIDTH
OFF_DA_K = OFF_DA_Q + DA_QK_WIDTH
OFF_DA_V = OFF_DA_K + DA_QK_WIDTH
ROPE_BASE = 10000.0
N_EXPERTS = 16
N_GROUPS = 4
PER_GROUP = N_EXPERTS // N_GROUPS
D_EXPERT = 256
ALPHA = (2 * DEPTH) ** 0.25
LN_EPS = 1e-5
LOG2E = math.log2(math.e)
Q_SCALE = HEAD_DIM ** -0.5 * LOG2E

N_COND = 8
TM = 512
SEQ_PER_TILE = TM // SEQ
CTX_SEQ_PER_STEP = 2
TM_MOE = 512
TM_POST = TM_MOE
MOE_CHUNK = 64
MOE_BLOCK = 4
MOE_SORTED = TM_MOE + (N_GROUPS - 1) * MOE_CHUNK
QT = 256
QT_ROWS = QT // GRID_W
NA_KEY_ROWS = 12
NA_KEYS = NA_KEY_ROWS * GRID_W
VMEM_LIMIT = 56 * 1024 * 1024


def _dot(a, b):
    return jnp.dot(a, b, preferred_element_type=F32)


def _dot_nt(a, b):
    return lax.dot_general(a, b, (((1,), (1,)), ((), ())), preferred_element_type=F32)


def _ln(x):
    mu = jnp.mean(x, -1, keepdims=True)
    xc = x - mu
    var = jnp.mean(xc * xc, -1, keepdims=True)
    return xc * lax.rsqrt(var + LN_EPS)


def _params(*sem):
    return pltpu.CompilerParams(dimension_semantics=sem, vmem_limit_bytes=VMEM_LIMIT)


def _mod_kernel(cond_ref, w_ref, b_ref, o_ref):
    c = cond_ref[...]
    s = (c * jax.nn.sigmoid(c)).astype(BF16)
    o_ref[...] = _dot(s, w_ref[...].astype(BF16)) + b_ref[...]


def _modulation(cond, w_mod, b_mod):
    tn = 1024
    return pl.pallas_call(
        _mod_kernel,
        grid=(DEPTH, 6 * D_MODEL // tn),
        in_specs=[
            pl.BlockSpec((N_COND, D_MODEL), lambda l, j: (0, 0)),
            pl.BlockSpec((None, D_MODEL, tn), lambda l, j: (l, 0, j)),
            pl.BlockSpec((None, 1, tn), lambda l, j: (l, 0, j)),
        ],
        out_specs=pl.BlockSpec((None, N_COND, tn), lambda l, j: (l, 0, j)),
        out_shape=jax.ShapeDtypeStruct((DEPTH, N_COND, 6 * D_MODEL), F32),
        compiler_params=_params("parallel", "parallel"),
        name="modulation",
    )(cond, w_mod, b_mod.reshape(DEPTH, 1, 6 * D_MODEL))


def _rope(t, cos, sin):
    lane = lax.broadcasted_iota(jnp.int32, t.shape, 1)
    first = (lane // 16) % 2 == 0
    n = t.shape[1]
    swapped = jnp.where(first, pltpu.roll(t, n - 16, 1), pltpu.roll(t, 16, 1))
    return t * cos + swapped * sin


def _inproj_kernel(*refs, latent):
    if latent:
        x_ref, mod_ref, w_ref, cos_ref, sin_ref, u_ref = refs
    else:
        x_ref, mod_ref, w_ref = refs[:3]
        u_ref, nak_ref, nav_ref, dak_ref, dav_ref = refs[-5:]
    mod = mod_ref[...]
    h = _ln(x_ref[...]) * (1.0 + mod[1:2]) + mod[0:1]
    u = _dot(h.astype(BF16), w_ref[...])
    da_q, da_k = u[:, OFF_DA_Q:OFF_DA_K], u[:, OFF_DA_K:OFF_DA_V]
    if latent:
        cos, sin = cos_ref[...], sin_ref[...]
        da_q, da_k = _rope(da_q, cos, sin), _rope(da_k, cos, sin)
    u_ref[:, :OFF_NA_K] = (u[:, :OFF_NA_K] * Q_SCALE).astype(BF16)
    u_ref[:, OFF_NA_K:OFF_DA_Q] = u[:, OFF_NA_K:OFF_DA_Q].astype(BF16)
    u_ref[:, OFF_DA_Q:OFF_DA_K] = (da_q * Q_SCALE).astype(BF16)
    u_ref[:, OFF_DA_K:OFF_DA_V] = da_k.astype(BF16)
    u_ref[:, OFF_DA_V:] = u[:, OFF_DA_V:].astype(BF16)
    if not latent:
        for s in range(SEQ_PER_TILE):
            rows = slice(s * SEQ, (s + 1) * SEQ)
            nak_ref[s] = u[rows, OFF_NA_K:OFF_NA_V]
            nav_ref[s] = u[rows, OFF_NA_V:OFF_SC_B]
            dak_ref[s] = u[rows, OFF_DA_K:OFF_DA_V]
            dav_ref[s] = u[rows, OFF_DA_V:]


def _inproj(x, mod_l, w_in, l, latent, rope_tabs=None, cache_bufs=None):
    n = x.shape[0]
    nt = n // TM
    per_batch = DEC_SEQ // TM
    if latent:
        mod_map = lambda i: (1 + i // per_batch, 0, 0)
    else:
        mod_map = lambda i: (0, 0, 0)
    in_specs = [
        pl.BlockSpec((TM, D_MODEL), lambda i: (i, 0)),
        pl.BlockSpec((None, 6, D_MODEL), mod_map),
        pl.BlockSpec((None, D_MODEL, IN_WIDTH), lambda i: (l, 0, 0)),
    ]
    args = [x, mod_l, w_in]
    out_specs = [pl.BlockSpec((TM, IN_WIDTH), lambda i: (i, 0))]
    out_shape = [jax.ShapeDtypeStruct((n, IN_WIDTH), BF16)]
    aliases = {}
    if latent:
        in_specs += [pl.BlockSpec((TM, DA_QK_WIDTH), lambda i: (i % per_batch, 0))] * 2
        args += list(rope_tabs)
    else:
        for width in (NA_WIDTH, NA_WIDTH, DA_QK_WIDTH, DA_V_WIDTH):
            out_specs.append(pl.BlockSpec((SEQ_PER_TILE, None, SEQ, width), lambda i: (i, l, 0, 0)))
            out_shape.append(jax.ShapeDtypeStruct((BATCH, DEPTH, SEQ, width), F32))
        if cache_bufs is not None:
            aliases = {len(args) + k: 1 + k for k in range(4)}
            in_specs += [pl.BlockSpec(memory_space=pl.ANY)] * 4
            args += list(cache_bufs)
    return pl.pallas_call(
        functools.partial(_inproj_kernel, latent=latent),
        grid=(nt,),
        in_specs=in_specs,
        out_specs=out_specs,
        out_shape=out_shape,
        input_output_aliases=aliases,
        compiler_params=_params("parallel"),
        name="inproj_latent" if latent else "inproj_context",
    )(*args)


def _lambda(lam_ref, lam_init):
    lp = lam_ref[...]
    s1 = jnp.sum(lp[0:1] * lp[1:2], axis=-1, keepdims=True)
    s2 = jnp.sum(lp[2:3] * lp[3:4], axis=-1, keepdims=True)
    return jnp.exp(s1) - jnp.exp(s2) + lam_init


def _with_ones(v):
    return jnp.concatenate([v, jnp.ones((v.shape[0], max(v.shape[1], 64)), v.dtype)], axis=1)


def _softmax_pv(score_parts, v_ones, d):
    m = functools.reduce(jnp.maximum, [jnp.max(s, -1, keepdims=True) for s in score_parts])
    e = [jnp.exp2(s - m).astype(BF16) for s in score_parts]
    o = _dot(e[0] if len(e) == 1 else jnp.concatenate(e, axis=1), v_ones)
    return o[:, :d] / o[:, d:d + 1]


def _softmax_pv_small(s, v):
    e = jnp.exp2(s - jnp.max(s, -1, keepdims=True))
    return _dot(e.astype(BF16), v) / jnp.sum(e, -1, keepdims=True)


def _diff_head_norm(o1, o2, lam, gain, lam_init):
    o = o1 - lam * o2
    o = o * lax.rsqrt(jnp.mean(o * o, -1, keepdims=True) + LN_EPS)
    return o * gain * (1.0 - lam_init)


def _gated_conv(b, v, v_prev_row, v_next_row, w, bias):
    n = v.shape[0]
    row = lax.broadcasted_iota(jnp.int32, v.shape, 0)
    prev = jnp.where(row == 0, v_prev_row, pltpu.roll(v, 1, 0))
    nxt = jnp.where(row == n - 1, v_next_row, pltpu.roll(v, n - 1, 0))
    return b * (prev * w[0:1] + v * w[1:2] + nxt * w[2:3] + bias)


def _ctx_mix_kernel(u_ref, lam_ref, gain_ref, cw_ref, cb_ref, o_ref, *, lam_init):
    lam = _lambda(lam_ref, lam_init)
    gain = gain_ref[...]
    for s in range(CTX_SEQ_PER_STEP):
        rows = slice(s * SEQ, (s + 1) * SEQ)
        outs = []
        for h in range(NA_HEADS):
            q = u_ref[rows, OFF_NA_Q + h * HEAD_DIM:OFF_NA_Q + (h + 1) * HEAD_DIM]
            k = u_ref[rows, OFF_NA_K + h * HEAD_DIM:OFF_NA_K + (h + 1) * HEAD_DIM]
            v = u_ref[rows, OFF_NA_V + h * HEAD_DIM:OFF_NA_V + (h + 1) * HEAD_DIM]
            outs.append(_softmax_pv_small(_dot_nt(q, k), v))
        zero_row = jnp.zeros((1, SC_WIDTH), F32)
        vc = u_ref[rows, OFF_SC_C:OFF_SC_X].astype(F32) * u_ref[rows, OFF_SC_X:OFF_DA_Q].astype(F32)
        outs.append(_gated_conv(u_ref[rows, OFF_SC_B:OFF_SC_C].astype(F32), vc, zero_row, zero_row,
                                cw_ref[...], cb_ref[...]))
        for j in range(DA_HEADS):
            v = u_ref[rows, OFF_DA_V + j * 2 * DA_DIM:OFF_DA_V + (j + 1) * 2 * DA_DIM]
            o12 = []
            for mi in (2 * j, 2 * j + 1):
                q = u_ref[rows, OFF_DA_Q + mi * DA_DIM:OFF_DA_Q + (mi + 1) * DA_DIM]
                k = u_ref[rows, OFF_DA_K + mi * DA_DIM:OFF_DA_K + (mi + 1) * DA_DIM]
                o12.append(_softmax_pv_small(_dot_nt(q, k), v))
            outs.append(_diff_head_norm(o12[0], o12[1], lam, gain, lam_init))
        o_ref[rows, :] = jnp.concatenate(outs, axis=-1).astype(BF16)


def _ctx_mix(u, da_lambda, da_norm_g, conv_w, conv_b, l, lam_init):
    step_rows = CTX_SEQ_PER_STEP * SEQ
    return pl.pallas_call(
        functools.partial(_ctx_mix_kernel, lam_init=lam_init),
        grid=(BATCH // CTX_SEQ_PER_STEP,),
        in_specs=[
            pl.BlockSpec((step_rows, IN_WIDTH), lambda b: (b, 0)),
            pl.BlockSpec((None, 4, DA_DIM), lambda b: (l, 0, 0)),
            pl.BlockSpec((None, 1, 2 * DA_DIM), lambda b: (l, 0, 0)),
            pl.BlockSpec((None, 3, SC_WIDTH), lambda b: (l, 0, 0)),
            pl.BlockSpec((None, 1, SC_WIDTH), lambda b: (l, 0, 0)),
        ],
        out_specs=pl.BlockSpec((step_rows, MIX_WIDTH), lambda b: (b, 0)),
        out_shape=jax.ShapeDtypeStruct((BATCH * SEQ, MIX_WIDTH), BF16),
        compiler_params=_params("parallel"),
        name="mix_context",
    )(u, da_lambda, da_norm_g.reshape(DEPTH, 1, 2 * DA_DIM), conv_w, conv_b.reshape(DEPTH, 1, SC_WIDTH))


def _lat_mix_kernel(uq_ref, nak_ref, nav_ref, sc_ref, dakv_ref, cnak_ref, cnav_ref, cdak_ref, cdav_ref,
                    bias_ref, lam_ref, gain_ref, cw_ref, cb_ref, o_ref, kall_ref, vall_ref, cnav1_ref, *, lam_init):
    qt = pl.program_id(1)
    n_qt = pl.num_programs(1)
    lam = _lambda(lam_ref, lam_init)
    gain = gain_ref[...]

    @pl.when(qt == 0)
    def _():
        kall_ref[:DEC_SEQ, :] = dakv_ref[:, :DA_QK_WIDTH]
        kall_ref[DEC_SEQ:, :] = cdak_ref[...]
        for j in range(DA_HEADS):
            vs = slice(j * 2 * DA_DIM, (j + 1) * 2 * DA_DIM)
            cols = slice(j * 4 * DA_DIM, (j + 1) * 4 * DA_DIM)
            vall_ref[:DEC_SEQ, cols] = _with_ones(dakv_ref[:, DA_QK_WIDTH + j * 2 * DA_DIM:DA_QK_WIDTH + (j + 1) * 2 * DA_DIM])
            vall_ref[DEC_SEQ:, cols] = _with_ones(cdav_ref[:, vs])
        for h in range(NA_HEADS):
            cnav1_ref[:, h * 2 * HEAD_DIM:(h + 1) * 2 * HEAD_DIM] = _with_ones(cnav_ref[:, h * HEAD_DIM:(h + 1) * HEAD_DIM])

    outs = []
    key_row0 = jnp.clip(qt * QT_ROWS - NA_WIN_ROWS // 2, 0, GRID_ROWS - NA_KEY_ROWS)
    k0 = pl.multiple_of(key_row0 * GRID_W, GRID_W)
    for h in range(NA_HEADS):
        hs = slice(h * HEAD_DIM, (h + 1) * HEAD_DIM)
        q = uq_ref[:, OFF_NA_Q + h * HEAD_DIM:OFF_NA_Q + (h + 1) * HEAD_DIM]
        k_loc = nak_ref[pl.ds(k0, NA_KEYS), hs]
        v_ones = jnp.concatenate([_with_ones(nav_ref[pl.ds(k0, NA_KEYS), hs]),
                                  cnav1_ref[:, h * 2 * HEAD_DIM:(h + 1) * 2 * HEAD_DIM]], axis=0)
        s_loc = _dot_nt(q, k_loc) + bias_ref[h]
        s_ctx = _dot_nt(q, cnak_ref[:, hs])
        outs.append(_softmax_pv([s_loc, s_ctx], v_ones, HEAD_DIM))
    t0 = pl.multiple_of(qt * QT, QT)
    halo = 16
    before = sc_ref[pl.ds(pl.multiple_of(jnp.maximum(t0 - halo, 0), halo), halo), :].astype(F32)
    after = sc_ref[pl.ds(pl.multiple_of(jnp.minimum(t0 + QT, DEC_SEQ - halo), halo), halo), :].astype(F32)
    cur = sc_ref[pl.ds(t0, QT), :].astype(F32)
    v_prev = before[halo - 1:halo, SC_WIDTH:2 * SC_WIDTH] * before[halo - 1:halo, 2 * SC_WIDTH:]
    v_next = after[0:1, SC_WIDTH:2 * SC_WIDTH] * after[0:1, 2 * SC_WIDTH:]
    v_prev = jnp.where(qt > 0, v_prev, 0.0)
    v_next = jnp.where(qt < n_qt - 1, v_next, 0.0)
    outs.append(_gated_conv(cur[:, :SC_WIDTH], cur[:, SC_WIDTH:2 * SC_WIDTH] * cur[:, 2 * SC_WIDTH:],
                            v_prev, v_next, cw_ref[...], cb_ref[...]))
    for j in range(DA_HEADS):
        v_ones = vall_ref[:, j * 4 * DA_DIM:(j + 1) * 4 * DA_DIM]
        o12 = []
        for mi in (2 * j, 2 * j + 1):
            q = uq_ref[:, OFF_DA_Q + mi * DA_DIM:OFF_DA_Q + (mi + 1) * DA_DIM]
            s = _dot_nt(q, kall_ref[:, mi * DA_DIM:(mi + 1) * DA_DIM])
            o12.append(_softmax_pv([s], v_ones, 2 * DA_DIM))
        outs.append(_diff_head_norm(o12[0], o12[1], lam, gain, lam_init))
    o_ref[...] = jnp.concatenate(outs, axis=-1).astype(BF16)


def _lat_mix(u, caches, bias_tab, da_lambda, da_norm_g, conv_w, conv_b, l, lam_init):
    cnak, cnav, cdak, cdav = caches
    n_qt = DEC_SEQ // QT

    def bias_map(b, qt):
        return (l, 0, jnp.where(qt == 0, 0, jnp.where(qt == n_qt - 1, 2, 1)), 0, 0)

    return pl.pallas_call(
        functools.partial(_lat_mix_kernel, lam_init=lam_init),
        grid=(DEC_BATCH, n_qt),
        in_specs=[
            pl.BlockSpec((QT, IN_WIDTH), lambda b, qt: (b * n_qt + qt, 0)),
            pl.BlockSpec((DEC_SEQ, NA_WIDTH), lambda b, qt: (b, OFF_NA_K // NA_WIDTH)),
            pl.BlockSpec((DEC_SEQ, NA_WIDTH), lambda b, qt: (b, OFF_NA_V // NA_WIDTH)),
            pl.BlockSpec((DEC_SEQ, 3 * SC_WIDTH), lambda b, qt: (b, OFF_SC_B // (3 * SC_WIDTH))),
            pl.BlockSpec((DEC_SEQ, DA_QK_WIDTH + DA_V_WIDTH), lambda b, qt: (b, OFF_DA_K // (DA_QK_WIDTH + DA_V_WIDTH))),
            pl.BlockSpec((None, None, PAST_LEN, NA_WIDTH), lambda b, qt: (b, l, 0, 0)),
            pl.BlockSpec((None, None, PAST_LEN, NA_WIDTH), lambda b, qt: (b, l, 0, 0)),
            pl.BlockSpec((None, None, PAST_LEN, DA_QK_WIDTH), lambda b, qt: (b, l, 0, 0)),
            pl.BlockSpec((None, None, PAST_LEN, DA_V_WIDTH), lambda b, qt: (b, l, 0, 0)),
            pl.BlockSpec((None, NA_HEADS, None, QT, NA_KEYS), bias_map),
            pl.BlockSpec((None, 4, DA_DIM), lambda b, qt: (l, 0, 0)),
            pl.BlockSpec((None, 1, 2 * DA_DIM), lambda b, qt: (l, 0, 0)),
            pl.BlockSpec((None, 3, SC_WIDTH), lambda b, qt: (l, 0, 0)),
            pl.BlockSpec((None, 1, SC_WIDTH), lambda b, qt: (l, 0, 0)),
        ],
        out_specs=pl.BlockSpec((QT, MIX_WIDTH), lambda b, qt: (b * n_qt + qt, 0)),
        out_shape=jax.ShapeDtypeStruct((DEC_BATCH * DEC_SEQ, MIX_WIDTH), BF16),
        scratch_shapes=[
            pltpu.VMEM((DEC_SEQ + PAST_LEN, DA_QK_WIDTH), BF16),
            pltpu.VMEM((DEC_SEQ + PAST_LEN, 2 * DA_V_WIDTH), BF16),
            pltpu.VMEM((PAST_LEN, 2 * NA_WIDTH), BF16),
        ],
        compiler_params=_params("parallel", "arbitrary"),
        name="mix_latent",
    )(u, u, u, u, u, cnak, cnav, cdak, cdav, bias_tab, da_lambda,
      da_norm_g.reshape(DEPTH, 1, 2 * DA_DIM), conv_w, conv_b.reshape(DEPTH, 1, SC_WIDTH))


def _first_max(vals):
    m = functools.reduce(jnp.maximum, vals)
    hot, taken = [], None
    for v in vals:
        is_max = v == m
        if taken is None:
            hot.append(is_max)
            taken = is_max
        else:
            hot.append(is_max & ~taken)
            taken = taken | is_max
    return hot, m


def _pick(hot, vals):
    out = vals[-1]
    for h, v in zip(hot[-2::-1], vals[-2::-1]):
        out = jnp.where(h, v, out)
    return out


def _route(h2, h2_bf, wr_ref, br_ref):
    tm = h2.shape[0]
    w = wr_ref[...]
    w_hi = w.astype(BF16)
    w_lo = (w - w_hi.astype(F32)).astype(BF16)
    h_lo = (h2 - h2_bf.astype(F32)).astype(BF16)
    z = _dot_nt(w_hi, h2_bf) + _dot_nt(w_hi, h_lo) + _dot_nt(w_lo, h2_bf)
    scores = jax.nn.sigmoid(z)
    biased = scores + br_ref[...]
    P = [biased[k * N_GROUPS:(k + 1) * N_GROUPS] for k in range(PER_GROUP)]
    S = [scores[k * N_GROUPS:(k + 1) * N_GROUPS] for k in range(PER_GROUP)]
    pair_sums = [P[i] + P[j] for i in range(PER_GROUP) for j in range(i + 1, PER_GROUP)]
    group_score = functools.reduce(jnp.maximum, pair_sums)
    sel, _ = _first_max([group_score[g:g + 1] for g in range(N_GROUPS)])
    c = [_pick(sel, [P[k][g:g + 1] for g in range(N_GROUPS)]) for k in range(PER_GROUP)]
    cs = [_pick(sel, [S[k][g:g + 1] for g in range(N_GROUPS)]) for k in range(PER_GROUP)]
    t1, _ = _first_max(c)
    t2, _ = _first_max([jnp.where(t, -jnp.inf, v) for t, v in zip(t1, c)])
    w1 = functools.reduce(jnp.add, [jnp.where(t, v, 0.0) for t, v in zip(t1, cs)])
    w2 = functools.reduce(jnp.add, [jnp.where(t, v, 0.0) for t, v in zip(t2, cs)])
    total = w1 + w2
    slot_gate = [jnp.where(a, w1 / total, jnp.where(b, w2 / total, 0.0)) for a, b in zip(t1, t2)]
    onehot = jnp.concatenate([s.astype(F32) for s in sel] + [jnp.zeros((8 - N_GROUPS, tm), F32)], axis=0)
    earlier = (lax.broadcasted_iota(jnp.int32, (tm, tm), 0) < lax.broadcasted_iota(jnp.int32, (tm, tm), 1))
    rank = _dot(onehot.astype(BF16), earlier.astype(BF16))
    count = jnp.sum(onehot, axis=1, keepdims=True)
    n_chunks = jnp.floor((count + (MOE_CHUNK - 1)) * (1.0 / MOE_CHUNK))
    pos = jnp.zeros((1, tm), F32)
    start = jnp.zeros((1, 1), F32)
    for g in range(N_GROUPS):
        pos = jnp.where(sel[g], start + rank[g:g + 1], pos)
        start = start + n_chunks[g:g + 1] * MOE_CHUNK
    rows = [pos] + slot_gate + [jnp.zeros((128 - 1 - PER_GROUP, tm), F32)]
    return jnp.concatenate(rows, axis=0), n_chunks


def _post_kernel(mix_ref, x_ref, mod_ref, wout_ref, g_ref, b_ref, wr_ref, br_ref,
                 x1_ref, h2_ref, tok_ref, pos_ref, nch_ref):
    mod = mod_ref[...]
    y = _dot(mix_ref[...], wout_ref[...])
    x1 = _ln(ALPHA * x_ref[...] + mod[2:3] * y) * g_ref[...] + b_ref[...]
    h2 = _ln(x1) * (1.0 + mod[4:5]) + mod[3:4]
    h2_bf = h2.astype(BF16)
    x1_ref[...] = x1
    h2_ref[...] = h2_bf
    rows, n_chunks = _route(h2, h2_bf, wr_ref, br_ref)
    tok_ref[...] = rows.T
    pos_ref[...] = rows[0:8]
    nch_ref[...] = jnp.broadcast_to(n_chunks, (8, 128)).astype(jnp.int32)


def _post(mix, x, mod_l, w_out, ln_g, ln_b, wr_t, br_t, l, latent):
    n = x.shape[0]
    per_batch = DEC_SEQ // TM_POST
    mod_map = (lambda i: (1 + i // per_batch, 0, 0)) if latent else (lambda i: (0, 0, 0))
    return pl.pallas_call(
        _post_kernel,
        grid=(n // TM_POST,),
        in_specs=[
            pl.BlockSpec((TM_POST, MIX_WIDTH), lambda i: (i, 0)),
            pl.BlockSpec((TM_POST, D_MODEL), lambda i: (i, 0)),
            pl.BlockSpec((None, 6, D_MODEL), mod_map),
            pl.BlockSpec((None, MIX_WIDTH, D_MODEL), lambda i: (l, 0, 0)),
            pl.BlockSpec((None, 1, D_MODEL), lambda i: (l, 0, 0)),
            pl.BlockSpec((None, 1, D_MODEL), lambda i: (l, 0, 0)),
            pl.BlockSpec((N_EXPERTS, D_MODEL), lambda i: (0, 0)),
            pl.BlockSpec((N_EXPERTS, 1), lambda i: (0, 0)),
        ],
        out_specs=[
            pl.BlockSpec((TM_POST, D_MODEL), lambda i: (i, 0)),
            pl.BlockSpec((TM_POST, D_MODEL), lambda i: (i, 0)),
            pl.BlockSpec((TM_POST, 128), lambda i: (i, 0)),
            pl.BlockSpec((None, 8, TM_POST), lambda i: (i, 0, 0)),
            pl.BlockSpec((None, 8, 128), lambda i: (i, 0, 0)),
        ],
        out_shape=[
            jax.ShapeDtypeStruct((n, D_MODEL), F32),
            jax.ShapeDtypeStruct((n, D_MODEL), BF16),
            jax.ShapeDtypeStruct((n, 128), F32),
            jax.ShapeDtypeStruct((n // TM_POST, 8, TM_POST), F32),
            jax.ShapeDtypeStruct((n // TM_POST, 8, 128), jnp.int32),
        ],
        compiler_params=_params("parallel"),
        name="post_latent" if latent else "post_context",
    )(mix, x, mod_l, w_out, ln_g.reshape(DEPTH, 1, D_MODEL), ln_b.reshape(DEPTH, 1, D_MODEL), wr_t, br_t)


def _moe_kernel(nch_ref, h_ref, tok_ref, pos_ref, x1_ref, mod_ref, w1_ref, w3_ref, w2_ref, g_ref, b_ref, o_ref,
                hs_ref, gs_ref, ys_ref):
    i = pl.program_id(0)
    tok = tok_ref[...]
    pos_lane = pos_ref[0:1, :]
    pos_col = tok[:, 0:1]
    slot = lax.broadcasted_iota(jnp.int32, (MOE_SORTED, TM_MOE), 0).astype(F32)
    sort = (slot == pos_lane).astype(BF16)
    tok_hi = tok.astype(BF16)
    tok_lo = (tok - tok_hi.astype(F32)).astype(BF16)
    z = _dot(sort, jnp.concatenate([h_ref[...], tok_hi, tok_lo], axis=1))
    hs_ref[...] = z[:, :D_MODEL].astype(BF16)
    gs_ref[...] = z[:, D_MODEL:D_MODEL + 128] + z[:, D_MODEL + 128:]
    ys_ref[...] = jnp.zeros_like(ys_ref)

    def experts(g, chunk0, n_rows):
        r0 = pl.multiple_of(chunk0 * MOE_CHUNK, MOE_CHUNK)
        rows = hs_ref[pl.ds(r0, n_rows), :]
        gates = gs_ref[pl.ds(r0, n_rows), :]
        acc = jnp.zeros((n_rows, D_MODEL), F32)
        for k in range(PER_GROUP):
            e = g * PER_GROUP + k
            a = _dot(rows, w1_ref[e])
            b = _dot(rows, w3_ref[e])
            hid = a * jax.nn.sigmoid(a) * b * gates[:, 1 + k:2 + k]
            acc = acc + _dot(hid.astype(BF16), w2_ref[e])
        ys_ref[pl.ds(r0, n_rows), :] = acc.astype(BF16)

    def group(g, first):
        n = nch_ref[i * N_GROUPS + g]
        n_blocks = n // MOE_BLOCK
        rest = n - n_blocks * MOE_BLOCK

        def block(c, carry):
            experts(g, first + c * MOE_BLOCK, MOE_BLOCK * MOE_CHUNK)
            return carry

        lax.fori_loop(0, n_blocks, block, 0)
        for m in range(1, MOE_BLOCK):
            @pl.when(rest == m)
            def _(m=m):
                experts(g, first + n_blocks * MOE_BLOCK, m * MOE_CHUNK)
        return first + n

    lax.fori_loop(0, N_GROUPS, group, 0)

    unsort = (lax.broadcasted_iota(jnp.int32, (TM_MOE, MOE_SORTED), 1).astype(F32) == pos_col).astype(BF16)
    y = _dot(unsort, ys_ref[...])
    mod = mod_ref[...]
    o_ref[...] = _ln(ALPHA * x1_ref[...] + mod[5:6] * y) * g_ref[...] + b_ref[...]


def _moe(h2, tok, pos, nch, x1, mod_l, w1, w3, w2, ln_g, ln_b, l, latent):
    n = x1.shape[0]
    per_batch = DEC_SEQ // TM_MOE
    mod_map = (lambda i, s: (1 + i // per_batch, 0, 0)) if latent else (lambda i, s: (0, 0, 0))
    resident = pl.Buffered(1)
    grid_spec = pltpu.PrefetchScalarGridSpec(
        num_scalar_prefetch=1,
        grid=(n // TM_MOE,),
        in_specs=[
            pl.BlockSpec((TM_MOE, D_MODEL), lambda i, s: (i, 0)),
            pl.BlockSpec((TM_MOE, 128), lambda i, s: (i, 0)),
            pl.BlockSpec((None, 8, TM_MOE), lambda i, s: (i, 0, 0)),
            pl.BlockSpec((TM_MOE, D_MODEL), lambda i, s: (i, 0)),
            pl.BlockSpec((None, 6, D_MODEL), mod_map),
            pl.BlockSpec((None, N_EXPERTS, D_MODEL, D_EXPERT), lambda i, s: (l, 0, 0, 0), pipeline_mode=resident),
            pl.BlockSpec((None, N_EXPERTS, D_MODEL, D_EXPERT), lambda i, s: (l, 0, 0, 0), pipeline_mode=resident),
            pl.BlockSpec((None, N_EXPERTS, D_EXPERT, D_MODEL), lambda i, s: (l, 0, 0, 0), pipeline_mode=resident),
            pl.BlockSpec((None, 1, D_MODEL), lambda i, s: (l, 0, 0)),
            pl.BlockSpec((None, 1, D_MODEL), lambda i, s: (l, 0, 0)),
        ],
        out_specs=pl.BlockSpec((TM_MOE, D_MODEL), lambda i, s: (i, 0)),
        scratch_shapes=[
            pltpu.VMEM((MOE_SORTED, D_MODEL), BF16),
            pltpu.VMEM((MOE_SORTED, 128), F32),
            pltpu.VMEM((MOE_SORTED, D_MODEL), BF16),
        ],
    )
    return pl.pallas_call(
        _moe_kernel,
        grid_spec=grid_spec,
        out_shape=jax.ShapeDtypeStruct((n, D_MODEL), F32),
        compiler_params=_params("arbitrary"),
        name="moe_latent" if latent else "moe_context",
    )(nch[:, :N_GROUPS, 0].reshape(-1), h2, tok, pos, x1, mod_l, w1, w3, w2,
      ln_g.reshape(DEPTH, 1, D_MODEL), ln_b.reshape(DEPTH, 1, D_MODEL))


def _rope_tables():
    t = np.arange(DEC_SEQ)
    half = DA_DIM // 4
    inv_freq = ROPE_BASE ** (-np.arange(half, dtype=np.float32) / half)
    cos, sin = [], []
    for pos in (t // GRID_W, t % GRID_W):
        ang = pos.astype(np.float32)[:, None] * inv_freq[None, :]
        c, s = np.cos(ang), np.sin(ang)
        cos += [c, c]
        sin += [-s, s]
    cos = np.tile(np.concatenate(cos, axis=1), (1, 2 * DA_HEADS))
    sin = np.tile(np.concatenate(sin, axis=1), (1, 2 * DA_HEADS))
    return jnp.asarray(cos, F32), jnp.asarray(sin, F32)


_NA_TILE_KINDS = ((0, 0), (2 * QT_ROWS, 2 * QT_ROWS - NA_WIN_ROWS // 2), (GRID_ROWS - QT_ROWS, GRID_ROWS - NA_KEY_ROWS))


def _na_bias_kernel(by_ref, o_ref):
    qc = lax.broadcasted_iota(jnp.int32, (GRID_W, GRID_W), 0)
    kc = lax.broadcasted_iota(jnp.int32, (GRID_W, GRID_W), 1)
    win_c0 = jnp.clip(qc - NA_WIN_COLS // 2, 0, GRID_W - NA_WIN_COLS)
    in_cols = (kc >= win_c0) & (kc < win_c0 + NA_WIN_COLS)
    masked = jnp.full((GRID_W, GRID_W), -jnp.inf, F32)
    for kind, (r0, key_row0) in enumerate(_NA_TILE_KINDS):
        for a in range(QT_ROWS):
            qr = r0 + a
            win_r0 = min(max(qr - NA_WIN_ROWS // 2, 0), GRID_ROWS - NA_WIN_ROWS)
            blocks = []
            for b in range(NA_KEY_ROWS):
                kr = key_row0 + b
                if win_r0 <= kr < win_r0 + NA_WIN_ROWS:
                    blocks.append(jnp.where(in_cols, by_ref[kr - qr + NA_WIN_ROWS - 1] * LOG2E, masked))
                else:
                    blocks.append(masked)
            o_ref[kind, a * GRID_W:(a + 1) * GRID_W, :] = jnp.concatenate(blocks, axis=1)


def _na_bias_tables(na_rel_bias):
    n_dr, n_dc = 2 * NA_WIN_ROWS - 1, 2 * NA_WIN_COLS - 1
    cols = np.arange(GRID_W)
    d_col = np.clip(cols[None, :] - cols[:, None], 1 - NA_WIN_COLS, NA_WIN_COLS - 1) + NA_WIN_COLS - 1
    col_sel = (d_col[None] == np.arange(n_dc)[:, None, None]).astype(np.float32)
    by_col = jnp.einsum('lhab,bqc->lhaqc', na_rel_bias.astype(F32), col_sel, precision=lax.Precision.HIGHEST)
    return pl.pallas_call(
        _na_bias_kernel,
        grid=(DEPTH, NA_HEADS),
        in_specs=[pl.BlockSpec((None, None, n_dr, GRID_W, GRID_W), lambda l, h: (l, h, 0, 0, 0))],
        out_specs=pl.BlockSpec((None, None, len(_NA_TILE_KINDS), QT, NA_KEYS), lambda l, h: (l, h, 0, 0, 0)),
        out_shape=jax.ShapeDtypeStruct((DEPTH, NA_HEADS, len(_NA_TILE_KINDS), QT, NA_KEYS), F32),
        compiler_params=_params("parallel", "parallel"),
        name="na_bias_table",
    )(by_col)


def kernel(x_prompt, x_sample, cache_na_k, cache_na_v, cache_da_k, cache_da_v, c, c_ctx, w_mod, b_mod, w_in,
           na_rel_bias, sc_conv_w, sc_conv_b, da_lambda, da_norm_g, w_out, ln1_g, ln1_b, w_router, b_router,
           moe_w1, moe_w3, moe_w2, ln2_g, ln2_b):
    xp = x_prompt.reshape(BATCH * SEQ, D_MODEL)
    xs = x_sample.reshape(DEC_BATCH * DEC_SEQ, D_MODEL)

    cond = jnp.concatenate([c_ctx[None, :], c, jnp.zeros((N_COND - 1 - DEC_BATCH, D_MODEL), F32)], axis=0)
    mod = _modulation(cond, w_mod, b_mod).reshape(DEPTH, N_COND, 6, D_MODEL)

    w_in_bf = w_in.astype(BF16)
    w_out_bf = w_out.astype(BF16)
    w1_bf, w3_bf, w2_bf = moe_w1.astype(BF16), moe_w3.astype(BF16), moe_w2.astype(BF16)

    slot_major = np.arange(N_EXPERTS).reshape(N_GROUPS, PER_GROUP).T.reshape(-1)
    wr_t = w_router.T[slot_major]
    br_t = b_router.astype(F32)[slot_major].reshape(N_EXPERTS, 1)

    caches = (
        cache_na_k.reshape(DEC_BATCH, DEPTH, PAST_LEN, NA_WIDTH).astype(BF16),
        cache_na_v.reshape(DEC_BATCH, DEPTH, PAST_LEN, NA_WIDTH).astype(BF16),
        cache_da_k.reshape(DEC_BATCH, DEPTH, PAST_LEN, DA_QK_WIDTH).astype(BF16),
        cache_da_v.reshape(DEC_BATCH, DEPTH, PAST_LEN, DA_V_WIDTH).astype(BF16),
    )
    rope_tabs = _rope_tables()
    bias_tab = _na_bias_tables(na_rel_bias)

    new_caches = None
    for l in range(DEPTH):
        lam_init = 0.8 - 0.6 * math.exp(-0.3 * l)
        mod_l = mod[l]
        u_p, *new_caches = _inproj(xp, mod_l, w_in_bf, l, latent=False, cache_bufs=new_caches)
        u_s, = _inproj(xs, mod_l, w_in_bf, l, latent=True, rope_tabs=rope_tabs)
        mix_p = _ctx_mix(u_p, da_lambda, da_norm_g, sc_conv_w, sc_conv_b, l, lam_init)
        mix_s = _lat_mix(u_s, caches, bias_tab, da_lambda, da_norm_g, sc_conv_w, sc_conv_b, l, lam_init)
        x1p, h2p, *route_p = _post(mix_p, xp, mod_l, w_out_bf, ln1_g, ln1_b, wr_t, br_t, l, latent=False)
        x1s, h2s, *route_s = _post(mix_s, xs, mod_l, w_out_bf, ln1_g, ln1_b, wr_t, br_t, l, latent=True)
        xp = _moe(h2p, *route_p, x1p, mod_l, w1_bf, w3_bf, w2_bf, ln2_g, ln2_b, l, latent=False)
        xs = _moe(h2s, *route_s, x1s, mod_l, w1_bf, w3_bf, w2_bf, ln2_g, ln2_b, l, latent=True)

    nak, nav, dak, dav = new_caches
    return (xp.reshape(BATCH, SEQ, D_MODEL), xs.reshape(DEC_BATCH, DEC_SEQ, D_MODEL),
            nak.reshape(BATCH, DEPTH, SEQ, NA_HEADS, HEAD_DIM), nav.reshape(BATCH, DEPTH, SEQ, NA_HEADS, HEAD_DIM),
            dak.reshape(BATCH, DEPTH, SEQ, 2 * DA_HEADS, DA_DIM), dav.reshape(BATCH, DEPTH, SEQ, DA_HEADS, 2 * DA_DIM))
```

```python
import functools
import math

import numpy as np
import jax
import jax.numpy as jnp
from jax import lax
from jax.experimental import pallas as pl
from jax.experimental.pallas import tpu as pltpu

F32 = jnp.float32
BF16 = jnp.bfloat16

D_MODEL = 1024
BATCH = 16
SEQ = 256
DEPTH = 4
DEC_BATCH = 2
DEC_SEQ = 2048
PAST_LEN = 512
GRID_W = 64
GRID_ROWS = DEC_SEQ // GRID_W
HEAD_DIM = 64
NA_HEADS = 4
NA_WIDTH = NA_HEADS * HEAD_DIM
NA_WIN_ROWS = 8
NA_WIN_COLS = 16
SC_WIDTH = 256
DA_HEADS = 4
DA_DIM = 64
DA_QK_WIDTH = 2 * DA_HEADS * DA_DIM
DA_V_WIDTH = DA_HEADS * 2 * DA_DIM
MIX_WIDTH = NA_WIDTH + SC_WIDTH + DA_V_WIDTH
IN_WIDTH = 3 * NA_WIDTH + 3 * SC_WIDTH + 2 * DA_QK_WIDTH + DA_V_WIDTH
OFF_NA_Q = 0
OFF_NA_K = NA_WIDTH
OFF_NA_V = 2 * NA_WIDTH
OFF_SC_B = 3 * NA_WIDTH
OFF_SC_C = OFF_SC_B + SC_WIDTH
OFF_SC_X = OFF_SC_C + SC_WIDTH
OFF_DA_Q = OFF_SC_X + SC_WIDTH
OFF_DA_K = OFF_DA_Q + DA_QK_WIDTH
OFF_DA_V = OFF_DA_K + DA_QK_WIDTH
ROPE_BASE = 10000.0
N_EXPERTS = 16
N_GROUPS = 4
PER_GROUP = N_EXPERTS // N_GROUPS
D_EXPERT = 256
ALPHA = (2 * DEPTH) ** 0.25
LN_EPS = 1e-5
LOG2E = math.log2(math.e)
Q_SCALE = HEAD_DIM ** -0.5 * LOG2E

N_COND = 8
TM = 512
SEQ_PER_TILE = TM // SEQ
CTX_TILES = BATCH * SEQ // TM
CTX_SEQ_PER_STEP = 2
TM_MOE = 512
TM_POST = TM_MOE
MOE_CHUNK = 64
MOE_BLOCK = 4
MOE_SORTED = TM_MOE + (N_GROUPS - 1) * MOE_CHUNK
assert TM == TM_MOE
QT = 256
QT_ROWS = QT // GRID_W
NA_KEY_ROWS = 12
NA_KEYS = NA_KEY_ROWS * GRID_W
VMEM_LIMIT = 56 * 1024 * 1024


def _dot(a, b):
    return jnp.dot(a, b, preferred_element_type=F32)


def _dot_nt(a, b):
    return lax.dot_general(a, b, (((1,), (1,)), ((), ())), preferred_element_type=F32)


def _ln(x):
    mu = jnp.mean(x, -1, keepdims=True)
    xc = x - mu
    var = jnp.mean(xc * xc, -1, keepdims=True)
    return xc * lax.rsqrt(var + LN_EPS)


def _params(*sem):
    return pltpu.CompilerParams(dimension_semantics=sem, vmem_limit_bytes=VMEM_LIMIT)


def _mod_kernel(cond_ref, w_ref, b_ref, o_ref):
    c = cond_ref[...]
    s = (c * jax.nn.sigmoid(c)).astype(BF16)
    o_ref[...] = _dot(s, w_ref[...].astype(BF16)) + b_ref[...]


def _modulation(cond, w_mod, b_mod):
    tn = 1024
    return pl.pallas_call(
        _mod_kernel,
        grid=(DEPTH, 6 * D_MODEL // tn),
        in_specs=[
            pl.BlockSpec((N_COND, D_MODEL), lambda l, j: (0, 0)),
            pl.BlockSpec((None, D_MODEL, tn), lambda l, j: (l, 0, j)),
            pl.BlockSpec((None, 1, tn), lambda l, j: (l, 0, j)),
        ],
        out_specs=pl.BlockSpec((None, N_COND, tn), lambda l, j: (l, 0, j)),
        out_shape=jax.ShapeDtypeStruct((DEPTH, N_COND, 6 * D_MODEL), F32),
        compiler_params=_params("parallel", "parallel"),
        name="modulation",
    )(cond, w_mod, b_mod.reshape(DEPTH, 1, 6 * D_MODEL))


def _rope(t, cos, sin):
    lane = lax.broadcasted_iota(jnp.int32, t.shape, 1)
    first = (lane // 16) % 2 == 0
    n = t.shape[1]
    swapped = jnp.where(first, pltpu.roll(t, n - 16, 1), pltpu.roll(t, 16, 1))
    return t * cos + swapped * sin


def _tile_stream(i, n_ctx_tiles, lat_tiles_per_batch):
    lat = jnp.maximum(i - n_ctx_tiles, 0)
    return jnp.minimum(i, n_ctx_tiles - 1), lat, jnp.where(i < n_ctx_tiles, 0, 1 + lat // lat_tiles_per_batch)


def _inproj_kernel(*refs):
    xp_ref, xs_ref, mod_ref, w_ref, cos_ref, sin_ref = refs[:6]
    up_ref, us_ref, nak_ref, nav_ref, dak_ref, dav_ref, wbf_ref = refs[-7:]
    i = pl.program_id(0)
    is_ctx = i < CTX_TILES

    @pl.when(i == 0)
    def _():
        wbf_ref[...] = w_ref[...].astype(BF16)

    mod = mod_ref[...]
    x = jnp.where(is_ctx, xp_ref[...], xs_ref[...])
    h = _ln(x) * (1.0 + mod[1:2]) + mod[0:1]
    u = _dot(h.astype(BF16), wbf_ref[...])
    cos, sin = cos_ref[...], sin_ref[...]
    da_q = _rope(u[:, OFF_DA_Q:OFF_DA_K], cos, sin)
    da_k = _rope(u[:, OFF_DA_K:OFF_DA_V], cos, sin)
    u_bf = jnp.concatenate([
        (u[:, :OFF_NA_K] * Q_SCALE).astype(BF16), u[:, OFF_NA_K:OFF_DA_Q].astype(BF16),
        (da_q * Q_SCALE).astype(BF16), da_k.astype(BF16), u[:, OFF_DA_V:].astype(BF16)], axis=1)

    @pl.when(is_ctx)
    def _():
        up_ref[...] = u_bf
        for s in range(SEQ_PER_TILE):
            rows = slice(s * SEQ, (s + 1) * SEQ)
            nak_ref[s] = u[rows, OFF_NA_K:OFF_NA_V]
            nav_ref[s] = u[rows, OFF_NA_V:OFF_SC_B]
            dak_ref[s] = u[rows, OFF_DA_K:OFF_DA_V]
            dav_ref[s] = u[rows, OFF_DA_V:]

    @pl.when(jnp.logical_not(is_ctx))
    def _():
        us_ref[...] = u_bf


def _inproj(xp, xs, mod_l, w_in, l, rope_tabs, cache_bufs=None):
    route = functools.partial(_tile_stream, n_ctx_tiles=CTX_TILES, lat_tiles_per_batch=DEC_SEQ // TM)
    in_specs = [
        pl.BlockSpec((TM, D_MODEL), lambda i: (route(i)[0], 0)),
        pl.BlockSpec((TM, D_MODEL), lambda i: (route(i)[1], 0)),
        pl.BlockSpec((None, 6, D_MODEL), lambda i: (route(i)[2], 0, 0)),
        pl.BlockSpec((None, D_MODEL, IN_WIDTH), lambda i: (l, 0, 0), pipeline_mode=pl.Buffered(1)),
        pl.BlockSpec((TM, DA_QK_WIDTH), lambda i: (jnp.where(i < CTX_TILES, 0, 1 + route(i)[1] % (DEC_SEQ // TM)), 0)),
        pl.BlockSpec((TM, DA_QK_WIDTH), lambda i: (jnp.where(i < CTX_TILES, 0, 1 + route(i)[1] % (DEC_SEQ // TM)), 0)),
    ]
    args = [xp, xs, mod_l, w_in, *rope_tabs]
    out_specs = [pl.BlockSpec((TM, IN_WIDTH), lambda i: (route(i)[0], 0)),
                 pl.BlockSpec((TM, IN_WIDTH), lambda i: (route(i)[1], 0))]
    out_shape = [jax.ShapeDtypeStruct((xp.shape[0], IN_WIDTH), BF16), jax.ShapeDtypeStruct((xs.shape[0], IN_WIDTH), BF16)]
    for width in (NA_WIDTH, NA_WIDTH, DA_QK_WIDTH, DA_V_WIDTH):
        out_specs.append(pl.BlockSpec((SEQ_PER_TILE, None, SEQ, width), lambda i: (route(i)[0], l, 0, 0)))
        out_shape.append(jax.ShapeDtypeStruct((BATCH, DEPTH, SEQ, width), F32))
    aliases = {}
    if cache_bufs is not None:
        aliases = {len(args) + k: 2 + k for k in range(4)}
        in_specs += [pl.BlockSpec(memory_space=pl.ANY)] * 4
        args += list(cache_bufs)
    return pl.pallas_call(
        _inproj_kernel,
        grid=(CTX_TILES + xs.shape[0] // TM,),
        in_specs=in_specs,
        out_specs=out_specs,
        out_shape=out_shape,
        scratch_shapes=[pltpu.VMEM((D_MODEL, IN_WIDTH), BF16)],
        input_output_aliases=aliases,
        compiler_params=_params("arbitrary"),
        name="inproj",
    )(*args)


def _lambda(lam_ref, lam_init):
    lp = lam_ref[...]
    s1 = jnp.sum(lp[0:1] * lp[1:2], axis=-1, keepdims=True)
    s2 = jnp.sum(lp[2:3] * lp[3:4], axis=-1, keepdims=True)
    return jnp.exp(s1) - jnp.exp(s2) + lam_init


def _with_ones(v):
    return jnp.concatenate([v, jnp.ones((v.shape[0], max(v.shape[1], 64)), v.dtype)], axis=1)


def _softmax_pv(score_parts, v_ones, d):
    m = functools.reduce(jnp.maximum, [jnp.max(s, -1, keepdims=True) for s in score_parts])
    e = [jnp.exp2(s - m).astype(BF16) for s in score_parts]
    o = _dot(e[0] if len(e) == 1 else jnp.concatenate(e, axis=1), v_ones)
    return o[:, :d] / o[:, d:d + 1]


def _softmax_pv_small(s, v):
    e = jnp.exp2(s - jnp.max(s, -1, keepdims=True))
    return _dot(e.astype(BF16), v) / jnp.sum(e, -1, keepdims=True)


def _diff_head_norm(o1, o2, lam, gain, lam_init):
    o = o1 - lam * o2
    o = o * lax.rsqrt(jnp.mean(o * o, -1, keepdims=True) + LN_EPS)
    return o * gain * (1.0 - lam_init)


def _gated_conv(b, v, v_prev_row, v_next_row, w, bias):
    n = v.shape[0]
    row = lax.broadcasted_iota(jnp.int32, v.shape, 0)
    prev = jnp.where(row == 0, v_prev_row, pltpu.roll(v, 1, 0))
    nxt = jnp.where(row == n - 1, v_next_row, pltpu.roll(v, n - 1, 0))
    return b * (prev * w[0:1] + v * w[1:2] + nxt * w[2:3] + bias)


def _ctx_mix_kernel(u_ref, lam_ref, gain_ref, cw_ref, cb_ref, o_ref, *, lam_init):
    lam = _lambda(lam_ref, lam_init)
    gain = gain_ref[...]
    for s in range(CTX_SEQ_PER_STEP):
        rows = slice(s * SEQ, (s + 1) * SEQ)
        outs = []
        for h in range(NA_HEADS):
            q = u_ref[rows, OFF_NA_Q + h * HEAD_DIM:OFF_NA_Q + (h + 1) * HEAD_DIM]
            k = u_ref[rows, OFF_NA_K + h * HEAD_DIM:OFF_NA_K + (h + 1) * HEAD_DIM]
            v = u_ref[rows, OFF_NA_V + h * HEAD_DIM:OFF_NA_V + (h + 1) * HEAD_DIM]
            outs.append(_softmax_pv_small(_dot_nt(q, k), v))
        zero_row = jnp.zeros((1, SC_WIDTH), F32)
        vc = u_ref[rows, OFF_SC_C:OFF_SC_X].astype(F32) * u_ref[rows, OFF_SC_X:OFF_DA_Q].astype(F32)
        outs.append(_gated_conv(u_ref[rows, OFF_SC_B:OFF_SC_C].astype(F32), vc, zero_row, zero_row,
                                cw_ref[...], cb_ref[...]))
        for j in range(DA_HEADS):
            v = u_ref[rows, OFF_DA_V + j * 2 * DA_DIM:OFF_DA_V + (j + 1) * 2 * DA_DIM]
            o12 = []
            for mi in (2 * j, 2 * j + 1):
                q = u_ref[rows, OFF_DA_Q + mi * DA_DIM:OFF_DA_Q + (mi + 1) * DA_DIM]
                k = u_ref[rows, OFF_DA_K + mi * DA_DIM:OFF_DA_K + (mi + 1) * DA_DIM]
                o12.append(_softmax_pv_small(_dot_nt(q, k), v))
            outs.append(_diff_head_norm(o12[0], o12[1], lam, gain, lam_init))
        o_ref[rows, :] = jnp.concatenate(outs, axis=-1).astype(BF16)


def _ctx_mix(u, da_lambda, da_norm_g, conv_w, conv_b, l, lam_init):
    step_rows = CTX_SEQ_PER_STEP * SEQ
    return pl.pallas_call(
        functools.partial(_ctx_mix_kernel, lam_init=lam_init),
        grid=(BATCH // CTX_SEQ_PER_STEP,),
        in_specs=[
            pl.BlockSpec((step_rows, IN_WIDTH), lambda b: (b, 0)),
            pl.BlockSpec((None, 4, DA_DIM), lambda b: (l, 0, 0)),
            pl.BlockSpec((None, 1, 2 * DA_DIM), lambda b: (l, 0, 0)),
            pl.BlockSpec((None, 3, SC_WIDTH), lambda b: (l, 0, 0)),
            pl.BlockSpec((None, 1, SC_WIDTH), lambda b: (l, 0, 0)),
        ],
        out_specs=pl.BlockSpec((step_rows, MIX_WIDTH), lambda b: (b, 0)),
        out_shape=jax.ShapeDtypeStruct((BATCH * SEQ, MIX_WIDTH), BF16),
        compiler_params=_params("parallel"),
        name="mix_context",
    )(u, da_lambda, da_norm_g.reshape(DEPTH, 1, 2 * DA_DIM), conv_w, conv_b.reshape(DEPTH, 1, SC_WIDTH))


def _lat_mix_kernel(uq_ref, nak_ref, nav_ref, sc_ref, dakv_ref, cnak_ref, cnav_ref, cdak_ref, cdav_ref,
                    bias_ref, lam_ref, gain_ref, cw_ref, cb_ref, o_ref, kall_ref, vall_ref, cnav1_ref, *, lam_init):
    qt = pl.program_id(1)
    n_qt = pl.num_programs(1)
    lam = _lambda(lam_ref, lam_init)
    gain = gain_ref[...]

    @pl.when(qt == 0)
    def _():
        kall_ref[:DEC_SEQ, :] = dakv_ref[:, :DA_QK_WIDTH]
        kall_ref[DEC_SEQ:, :] = cdak_ref[...]
        for j in range(DA_HEADS):
            vs = slice(j * 2 * DA_DIM, (j + 1) * 2 * DA_DIM)
            cols = slice(j * 4 * DA_DIM, (j + 1) * 4 * DA_DIM)
            vall_ref[:DEC_SEQ, cols] = _with_ones(dakv_ref[:, DA_QK_WIDTH + j * 2 * DA_DIM:DA_QK_WIDTH + (j + 1) * 2 * DA_DIM])
            vall_ref[DEC_SEQ:, cols] = _with_ones(cdav_ref[:, vs])
        for h in range(NA_HEADS):
            cnav1_ref[:, h * 2 * HEAD_DIM:(h + 1) * 2 * HEAD_DIM] = _with_ones(cnav_ref[:, h * HEAD_DIM:(h + 1) * HEAD_DIM])

    outs = []
    key_row0 = jnp.clip(qt * QT_ROWS - NA_WIN_ROWS // 2, 0, GRID_ROWS - NA_KEY_ROWS)
    k0 = pl.multiple_of(key_row0 * GRID_W, GRID_W)
    for h in range(NA_HEADS):
        hs = slice(h * HEAD_DIM, (h + 1) * HEAD_DIM)
        q = uq_ref[:, OFF_NA_Q + h * HEAD_DIM:OFF_NA_Q + (h + 1) * HEAD_DIM]
        k_loc = nak_ref[pl.ds(k0, NA_KEYS), hs]
        v_ones = jnp.concatenate([_with_ones(nav_ref[pl.ds(k0, NA_KEYS), hs]),
                                  cnav1_ref[:, h * 2 * HEAD_DIM:(h + 1) * 2 * HEAD_DIM]], axis=0)
        s_loc = _dot_nt(q, k_loc) + bias_ref[h]
        s_ctx = _dot_nt(q, cnak_ref[:, hs])
        outs.append(_softmax_pv([s_loc, s_ctx], v_ones, HEAD_DIM))
    t0 = pl.multiple_of(qt * QT, QT)
    halo = 16
    before = sc_ref[pl.ds(pl.multiple_of(jnp.maximum(t0 - halo, 0), halo), halo), :].astype(F32)
    after = sc_ref[pl.ds(pl.multiple_of(jnp.minimum(t0 + QT, DEC_SEQ - halo), halo), halo), :].astype(F32)
    cur = sc_ref[pl.ds(t0, QT), :].astype(F32)
    v_prev = before[halo - 1:halo, SC_WIDTH:2 * SC_WIDTH] * before[halo - 1:halo, 2 * SC_WIDTH:]
    v_next = after[0:1, SC_WIDTH:2 * SC_WIDTH] * after[0:1, 2 * SC_WIDTH:]
    v_prev = jnp.where(qt > 0, v_prev, 0.0)
    v_next = jnp.where(qt < n_qt - 1, v_next, 0.0)
    outs.append(_gated_conv(cur[:, :SC_WIDTH], cur[:, SC_WIDTH:2 * SC_WIDTH] * cur[:, 2 * SC_WIDTH:],
                            v_prev, v_next, cw_ref[...], cb_ref[...]))
    for j in range(DA_HEADS):
        v_ones = vall_ref[:, j * 4 * DA_DIM:(j + 1) * 4 * DA_DIM]
        o12 = []
        for mi in (2 * j, 2 * j + 1):
            q = uq_ref[:, OFF_DA_Q + mi * DA_DIM:OFF_DA_Q + (mi + 1) * DA_DIM]
            s = _dot_nt(q, kall_ref[:, mi * DA_DIM:(mi + 1) * DA_DIM])
            o12.append(_softmax_pv([s], v_ones, 2 * DA_DIM))
        outs.append(_diff_head_norm(o12[0], o12[1], lam, gain, lam_init))
    o_ref[...] = jnp.concatenate(outs, axis=-1).astype(BF16)


def _lat_mix(u, caches, bias_tab, da_lambda, da_norm_g, conv_w, conv_b, l, lam_init):
    cnak, cnav, cdak, cdav = caches
    n_qt = DEC_SEQ // QT

    def bias_map(b, qt):
        return (l, 0, jnp.where(qt == 0, 0, jnp.where(qt == n_qt - 1, 2, 1)), 0, 0)

    return pl.pallas_call(
        functools.partial(_lat_mix_kernel, lam_init=lam_init),
        grid=(DEC_BATCH, n_qt),
        in_specs=[
            pl.BlockSpec((QT, IN_WIDTH), lambda b, qt: (b * n_qt + qt, 0)),
            pl.BlockSpec((DEC_SEQ, NA_WIDTH), lambda b, qt: (b, OFF_NA_K // NA_WIDTH)),
            pl.BlockSpec((DEC_SEQ, NA_WIDTH), lambda b, qt: (b, OFF_NA_V // NA_WIDTH)),
            pl.BlockSpec((DEC_SEQ, 3 * SC_WIDTH), lambda b, qt: (b, OFF_SC_B // (3 * SC_WIDTH))),
            pl.BlockSpec((DEC_SEQ, DA_QK_WIDTH + DA_V_WIDTH), lambda b, qt: (b, OFF_DA_K // (DA_QK_WIDTH + DA_V_WIDTH))),
            pl.BlockSpec((None, None, PAST_LEN, NA_WIDTH), lambda b, qt: (b, l, 0, 0)),
            pl.BlockSpec((None, None, PAST_LEN, NA_WIDTH), lambda b, qt: (b, l, 0, 0)),
            pl.BlockSpec((None, None, PAST_LEN, DA_QK_WIDTH), lambda b, qt: (b, l, 0, 0)),
            pl.BlockSpec((None, None, PAST_LEN, DA_V_WIDTH), lambda b, qt: (b, l, 0, 0)),
            pl.BlockSpec((None, NA_HEADS, None, QT, NA_KEYS), bias_map),
            pl.BlockSpec((None, 4, DA_DIM), lambda b, qt: (l, 0, 0)),
            pl.BlockSpec((None, 1, 2 * DA_DIM), lambda b, qt: (l, 0, 0)),
            pl.BlockSpec((None, 3, SC_WIDTH), lambda b, qt: (l, 0, 0)),
            pl.BlockSpec((None, 1, SC_WIDTH), lambda b, qt: (l, 0, 0)),
        ],
        out_specs=pl.BlockSpec((QT, MIX_WIDTH), lambda b, qt: (b * n_qt + qt, 0)),
        out_shape=jax.ShapeDtypeStruct((DEC_BATCH * DEC_SEQ, MIX_WIDTH), BF16),
        scratch_shapes=[
            pltpu.VMEM((DEC_SEQ + PAST_LEN, DA_QK_WIDTH), BF16),
            pltpu.VMEM((DEC_SEQ + PAST_LEN, 2 * DA_V_WIDTH), BF16),
            pltpu.VMEM((PAST_LEN, 2 * NA_WIDTH), BF16),
        ],
        compiler_params=_params("parallel", "arbitrary"),
        name="mix_latent",
    )(u, u, u, u, u, cnak, cnav, cdak, cdav, bias_tab, da_lambda,
      da_norm_g.reshape(DEPTH, 1, 2 * DA_DIM), conv_w, conv_b.reshape(DEPTH, 1, SC_WIDTH))


def _first_max(vals):
    m = functools.reduce(jnp.maximum, vals)
    hot, taken = [], None
    for v in vals:
        is_max = v == m
        if taken is None:
            hot.append(is_max)
            taken = is_max
        else:
            hot.append(is_max & ~taken)
            taken = taken | is_max
    return hot, m


def _pick(hot, vals):
    out = vals[-1]
    for h, v in zip(hot[-2::-1], vals[-2::-1]):
        out = jnp.where(h, v, out)
    return out


def _route(h2, h2_bf, wr_ref, br_ref):
    tm = h2.shape[0]
    w = wr_ref[...]
    w_hi = w.astype(BF16)
    w_lo = (w - w_hi.astype(F32)).astype(BF16)
    h_lo = (h2 - h2_bf.astype(F32)).astype(BF16)
    z = _dot_nt(w_hi, h2_bf) + _dot_nt(w_hi, h_lo) + _dot_nt(w_lo, h2_bf)
    scores = jax.nn.sigmoid(z)
    biased = scores + br_ref[...]
    P = [biased[k * N_GROUPS:(k + 1) * N_GROUPS] for k in range(PER_GROUP)]
    S = [scores[k * N_GROUPS:(k + 1) * N_GROUPS] for k in range(PER_GROUP)]
    pair_sums = [P[i] + P[j] for i in range(PER_GROUP) for j in range(i + 1, PER_GROUP)]
    group_score = functools.reduce(jnp.maximum, pair_sums)
    sel, _ = _first_max([group_score[g:g + 1] for g in range(N_GROUPS)])
    c = [_pick(sel, [P[k][g:g + 1] for g in range(N_GROUPS)]) for k in range(PER_GROUP)]
    cs = [_pick(sel, [S[k][g:g + 1] for g in range(N_GROUPS)]) for k in range(PER_GROUP)]
    t1, _ = _first_max(c)
    t2, _ = _first_max([jnp.where(t, -jnp.inf, v) for t, v in zip(t1, c)])
    w1 = functools.reduce(jnp.add, [jnp.where(t, v, 0.0) for t, v in zip(t1, cs)])
    w2 = functools.reduce(jnp.add, [jnp.where(t, v, 0.0) for t, v in zip(t2, cs)])
    total = w1 + w2
    slot_gate = [jnp.where(a, w1 / total, jnp.where(b, w2 / total, 0.0)) for a, b in zip(t1, t2)]
    onehot = jnp.concatenate([s.astype(F32) for s in sel] + [jnp.zeros((8 - N_GROUPS, tm), F32)], axis=0)
    earlier = (lax.broadcasted_iota(jnp.int32, (tm, tm), 0) < lax.broadcasted_iota(jnp.int32, (tm, tm), 1))
    rank = _dot(onehot.astype(BF16), earlier.astype(BF16))
    count = jnp.sum(onehot, axis=1, keepdims=True)
    n_chunks = jnp.floor((count + (MOE_CHUNK - 1)) * (1.0 / MOE_CHUNK))
    pos = jnp.zeros((1, tm), F32)
    start = jnp.zeros((1, 1), F32)
    for g in range(N_GROUPS):
        pos = jnp.where(sel[g], start + rank[g:g + 1], pos)
        start = start + n_chunks[g:g + 1] * MOE_CHUNK
    rows = [pos] + slot_gate + [jnp.zeros((128 - 1 - PER_GROUP, tm), F32)]
    return jnp.concatenate(rows, axis=0), n_chunks


def _post_kernel(mixp_ref, mixs_ref, xp_ref, xs_ref, mod_ref, wout_ref, g_ref, b_ref, wr_ref, br_ref,
                 x1_ref, h2_ref, tok_ref, pos_ref, nch_ref, wbf_ref):
    i = pl.program_id(0)
    is_ctx = i < CTX_TILES

    @pl.when(i == 0)
    def _():
        wbf_ref[...] = wout_ref[...].astype(BF16)

    mod = mod_ref[...]
    y = _dot(jnp.where(is_ctx, mixp_ref[...], mixs_ref[...]), wbf_ref[...])
    x = jnp.where(is_ctx, xp_ref[...], xs_ref[...])
    x1 = _ln(ALPHA * x + mod[2:3] * y) * g_ref[...] + b_ref[...]
    h2 = _ln(x1) * (1.0 + mod[4:5]) + mod[3:4]
    h2_bf = h2.astype(BF16)
    x1_ref[...] = x1
    h2_ref[...] = h2_bf
    rows, n_chunks = _route(h2, h2_bf, wr_ref, br_ref)
    tok_ref[...] = rows.T
    pos_ref[...] = rows[0:8]
    nch_ref[...] = jnp.broadcast_to(n_chunks, (8, 128)).astype(jnp.int32)


def _post(mix_p, mix_s, xp, xs, mod_l, w_out, ln_g, ln_b, wr_t, br_t, l):
    n = xp.shape[0] + xs.shape[0]
    route = functools.partial(_tile_stream, n_ctx_tiles=CTX_TILES, lat_tiles_per_batch=DEC_SEQ // TM_POST)
    return pl.pallas_call(
        _post_kernel,
        grid=(n // TM_POST,),
        in_specs=[
            pl.BlockSpec((TM_POST, MIX_WIDTH), lambda i: (route(i)[0], 0)),
            pl.BlockSpec((TM_POST, MIX_WIDTH), lambda i: (route(i)[1], 0)),
            pl.BlockSpec((TM_POST, D_MODEL), lambda i: (route(i)[0], 0)),
            pl.BlockSpec((TM_POST, D_MODEL), lambda i: (route(i)[1], 0)),
            pl.BlockSpec((None, 6, D_MODEL), lambda i: (route(i)[2], 0, 0)),
            pl.BlockSpec((None, MIX_WIDTH, D_MODEL), lambda i: (l, 0, 0), pipeline_mode=pl.Buffered(1)),
            pl.BlockSpec((None, 1, D_MODEL), lambda i: (l, 0, 0)),
            pl.BlockSpec((None, 1, D_MODEL), lambda i: (l, 0, 0)),
            pl.BlockSpec((N_EXPERTS, D_MODEL), lambda i: (0, 0)),
            pl.BlockSpec((N_EXPERTS, 1), lambda i: (0, 0)),
        ],
        out_specs=[
            pl.BlockSpec((TM_POST, D_MODEL), lambda i: (i, 0)),
            pl.BlockSpec((TM_POST, D_MODEL), lambda i: (i, 0)),
            pl.BlockSpec((TM_POST, 128), lambda i: (i, 0)),
            pl.BlockSpec((None, 8, TM_POST), lambda i: (i, 0, 0)),
            pl.BlockSpec((None, 8, 128), lambda i: (i, 0, 0)),
        ],
        out_shape=[
            jax.ShapeDtypeStruct((n, D_MODEL), F32),
            jax.ShapeDtypeStruct((n, D_MODEL), BF16),
            jax.ShapeDtypeStruct((n, 128), F32),
            jax.ShapeDtypeStruct((n // TM_POST, 8, TM_POST), F32),
            jax.ShapeDtypeStruct((n // TM_POST, 8, 128), jnp.int32),
        ],
        scratch_shapes=[pltpu.VMEM((MIX_WIDTH, D_MODEL), BF16)],
        compiler_params=_params("arbitrary"),
        name="post",
    )(mix_p, mix_s, xp, xs, mod_l, w_out, ln_g.reshape(DEPTH, 1, D_MODEL), ln_b.reshape(DEPTH, 1, D_MODEL), wr_t, br_t)


def _moe_kernel(nch_ref, h_ref, tok_ref, pos_ref, x1_ref, mod_ref, w1f_ref, w3f_ref, w2f_ref, g_ref, b_ref,
                op_ref, os_ref, w1_ref, w3_ref, w2_ref, hs_ref, gs_ref, ys_ref):
    step = pl.program_id(0)

    @pl.when(step < N_EXPERTS)
    def _():
        w1_ref[step] = w1f_ref[...].astype(BF16)
        w3_ref[step] = w3f_ref[...].astype(BF16)
        w2_ref[step] = w2f_ref[...].astype(BF16)

    @pl.when(step >= N_EXPERTS)
    def _():
        i = step - N_EXPERTS
        out = _moe_tile(i, nch_ref, h_ref, tok_ref, pos_ref, x1_ref, mod_ref, w1_ref, w3_ref, w2_ref, g_ref, b_ref,
                        hs_ref, gs_ref, ys_ref)

        @pl.when(i < CTX_TILES)
        def _():
            op_ref[...] = out

        @pl.when(i >= CTX_TILES)
        def _():
            os_ref[...] = out


def _moe_tile(i, nch_ref, h_ref, tok_ref, pos_ref, x1_ref, mod_ref, w1_ref, w3_ref, w2_ref, g_ref, b_ref,
              hs_ref, gs_ref, ys_ref):
    tok = tok_ref[...]
    pos_lane = pos_ref[0:1, :]
    pos_col = tok[:, 0:1]
    slot = lax.broadcasted_iota(jnp.int32, (MOE_SORTED, TM_MOE), 0).astype(F32)
    sort = (slot == pos_lane).astype(BF16)
    tok_hi = tok.astype(BF16)
    tok_lo = (tok - tok_hi.astype(F32)).astype(BF16)
    z = _dot(sort, jnp.concatenate([h_ref[...], tok_hi, tok_lo], axis=1))
    hs_ref[...] = z[:, :D_MODEL].astype(BF16)
    gs_ref[...] = z[:, D_MODEL:D_MODEL + 128] + z[:, D_MODEL + 128:]
    ys_ref[...] = jnp.zeros_like(ys_ref)

    def experts(g, chunk0, n_rows):
        r0 = pl.multiple_of(chunk0 * MOE_CHUNK, MOE_CHUNK)
        rows = hs_ref[pl.ds(r0, n_rows), :]
        gates = gs_ref[pl.ds(r0, n_rows), :]
        acc = jnp.zeros((n_rows, D_MODEL), F32)
        for k in range(PER_GROUP):
            e = g * PER_GROUP + k
            a = _dot(rows, w1_ref[e])
            b = _dot(rows, w3_ref[e])
            hid = a * jax.nn.sigmoid(a) * b * gates[:, 1 + k:2 + k]
            acc = acc + _dot(hid.astype(BF16), w2_ref[e])
        ys_ref[pl.ds(r0, n_rows), :] = acc.astype(BF16)

    def group(g, first):
        n = nch_ref[i * N_GROUPS + g]
        n_blocks = n // MOE_BLOCK
        rest = n - n_blocks * MOE_BLOCK

        def block(c, carry):
            experts(g, first + c * MOE_BLOCK, MOE_BLOCK * MOE_CHUNK)
            return carry

        lax.fori_loop(0, n_blocks, block, 0)
        for m in range(1, MOE_BLOCK):
            @pl.when(rest == m)
            def _(m=m):
                experts(g, first + n_blocks * MOE_BLOCK, m * MOE_CHUNK)
        return first + n

    lax.fori_loop(0, N_GROUPS, group, 0)

    unsort = (lax.broadcasted_iota(jnp.int32, (TM_MOE, MOE_SORTED), 1).astype(F32) == pos_col).astype(BF16)
    y = _dot(unsort, ys_ref[...])
    mod = mod_ref[...]
    return _ln(ALPHA * x1_ref[...] + mod[5:6] * y) * g_ref[...] + b_ref[...]


def _moe(h2, tok, pos, nch, x1, mod_l, w1, w3, w2, ln_g, ln_b, l):
    n_tiles = x1.shape[0] // TM_MOE
    n_ctx = CTX_TILES * TM_MOE
    per_batch = DEC_SEQ // TM_MOE

    def tile(step):
        return jnp.maximum(step - N_EXPERTS, 0)

    def expert(step):
        return jnp.minimum(step, N_EXPERTS - 1)

    def mod_map(step, s):
        return (_tile_stream(tile(step), CTX_TILES, per_batch)[2], 0, 0)

    grid_spec = pltpu.PrefetchScalarGridSpec(
        num_scalar_prefetch=1,
        grid=(N_EXPERTS + n_tiles,),
        in_specs=[
            pl.BlockSpec((TM_MOE, D_MODEL), lambda step, s: (tile(step), 0)),
            pl.BlockSpec((TM_MOE, 128), lambda step, s: (tile(step), 0)),
            pl.BlockSpec((None, 8, TM_MOE), lambda step, s: (tile(step), 0, 0)),
            pl.BlockSpec((TM_MOE, D_MODEL), lambda step, s: (tile(step), 0)),
            pl.BlockSpec((None, 6, D_MODEL), mod_map),
            pl.BlockSpec((None, None, D_MODEL, D_EXPERT), lambda step, s: (l, expert(step), 0, 0)),
            pl.BlockSpec((None, None, D_MODEL, D_EXPERT), lambda step, s: (l, expert(step), 0, 0)),
            pl.BlockSpec((None, None, D_EXPERT, D_MODEL), lambda step, s: (l, expert(step), 0, 0)),
            pl.BlockSpec((None, 1, D_MODEL), lambda step, s: (l, 0, 0)),
            pl.BlockSpec((None, 1, D_MODEL), lambda step, s: (l, 0, 0)),
        ],
        out_specs=[
            pl.BlockSpec((TM_MOE, D_MODEL), lambda step, s: (jnp.minimum(tile(step), CTX_TILES - 1), 0)),
            pl.BlockSpec((TM_MOE, D_MODEL), lambda step, s: (jnp.maximum(tile(step) - CTX_TILES, 0), 0)),
        ],
        scratch_shapes=[
            pltpu.VMEM((N_EXPERTS, D_MODEL, D_EXPERT), BF16),
            pltpu.VMEM((N_EXPERTS, D_MODEL, D_EXPERT), BF16),
            pltpu.VMEM((N_EXPERTS, D_EXPERT, D_MODEL), BF16),
            pltpu.VMEM((MOE_SORTED, D_MODEL), BF16),
            pltpu.VMEM((MOE_SORTED, 128), F32),
            pltpu.VMEM((MOE_SORTED, D_MODEL), BF16),
        ],
    )
    return pl.pallas_call(
        _moe_kernel,
        grid_spec=grid_spec,
        out_shape=[jax.ShapeDtypeStruct((n_ctx, D_MODEL), F32),
                   jax.ShapeDtypeStruct((x1.shape[0] - n_ctx, D_MODEL), F32)],
        compiler_params=_params("arbitrary"),
        name="moe",
    )(nch[:, :N_GROUPS, 0].reshape(-1), h2, tok, pos, x1, mod_l, w1, w3, w2,
      ln_g.reshape(DEPTH, 1, D_MODEL), ln_b.reshape(DEPTH, 1, D_MODEL))


def _rope_tables():
    t = np.arange(DEC_SEQ)
    half = DA_DIM // 4
    inv_freq = ROPE_BASE ** (-np.arange(half, dtype=np.float32) / half)
    cos, sin = [], []
    for pos in (t // GRID_W, t % GRID_W):
        ang = pos.astype(np.float32)[:, None] * inv_freq[None, :]
        c, s = np.cos(ang), np.sin(ang)
        cos += [c, c]
        sin += [-s, s]
    cos = np.tile(np.concatenate(cos, axis=1), (1, 2 * DA_HEADS))
    sin = np.tile(np.concatenate(sin, axis=1), (1, 2 * DA_HEADS))
    cos = np.concatenate([np.ones((TM, DA_QK_WIDTH), np.float32), cos], axis=0)
    sin = np.concatenate([np.zeros((TM, DA_QK_WIDTH), np.float32), sin], axis=0)
    return jnp.asarray(cos, F32), jnp.asarray(sin, F32)


_NA_TILE_KINDS = ((0, 0), (2 * QT_ROWS, 2 * QT_ROWS - NA_WIN_ROWS // 2), (GRID_ROWS - QT_ROWS, GRID_ROWS - NA_KEY_ROWS))


def _na_bias_kernel(by_ref, o_ref):
    qc = lax.broadcasted_iota(jnp.int32, (GRID_W, GRID_W), 0)
    kc = lax.broadcasted_iota(jnp.int32, (GRID_W, GRID_W), 1)
    win_c0 = jnp.clip(qc - NA_WIN_COLS // 2, 0, GRID_W - NA_WIN_COLS)
    in_cols = (kc >= win_c0) & (kc < win_c0 + NA_WIN_COLS)
    masked = jnp.full((GRID_W, GRID_W), -jnp.inf, F32)
    for kind, (r0, key_row0) in enumerate(_NA_TILE_KINDS):
        for a in range(QT_ROWS):
            qr = r0 + a
            win_r0 = min(max(qr - NA_WIN_ROWS // 2, 0), GRID_ROWS - NA_WIN_ROWS)
            blocks = []
            for b in range(NA_KEY_ROWS):
                kr = key_row0 + b
                if win_r0 <= kr < win_r0 + NA_WIN_ROWS:
                    blocks.append(jnp.where(in_cols, by_ref[kr - qr + NA_WIN_ROWS - 1] * LOG2E, masked))
                else:
                    blocks.append(masked)
            o_ref[kind, a * GRID_W:(a + 1) * GRID_W, :] = jnp.concatenate(blocks, axis=1)


def _na_bias_tables(na_rel_bias):
    n_dr, n_dc = 2 * NA_WIN_ROWS - 1, 2 * NA_WIN_COLS - 1
    cols = np.arange(GRID_W)
    d_col = np.clip(cols[None, :] - cols[:, None], 1 - NA_WIN_COLS, NA_WIN_COLS - 1) + NA_WIN_COLS - 1
    col_sel = (d_col[None] == np.arange(n_dc)[:, None, None]).astype(np.float32)
    by_col = jnp.einsum('lhab,bqc->lhaqc', na_rel_bias.astype(F32), col_sel, precision=lax.Precision.HIGHEST)
    return pl.pallas_call(
        _na_bias_kernel,
        grid=(DEPTH, NA_HEADS),
        in_specs=[pl.BlockSpec((None, None, n_dr, GRID_W, GRID_W), lambda l, h: (l, h, 0, 0, 0))],
        out_specs=pl.BlockSpec((None, None, len(_NA_TILE_KINDS), QT, NA_KEYS), lambda l, h: (l, h, 0, 0, 0)),
        out_shape=jax.ShapeDtypeStruct((DEPTH, NA_HEADS, len(_NA_TILE_KINDS), QT, NA_KEYS), F32),
        compiler_params=_params("parallel", "parallel"),
        name="na_bias_table",
    )(by_col)


def kernel(x_prompt, x_sample, cache_na_k, cache_na_v, cache_da_k, cache_da_v, c, c_ctx, w_mod, b_mod, w_in,
           na_rel_bias, sc_conv_w, sc_conv_b, da_lambda, da_norm_g, w_out, ln1_g, ln1_b, w_router, b_router,
           moe_w1, moe_w3, moe_w2, ln2_g, ln2_b):
    xp = x_prompt.reshape(BATCH * SEQ, D_MODEL)
    xs = x_sample.reshape(DEC_BATCH * DEC_SEQ, D_MODEL)

    cond = jnp.concatenate([c_ctx[None, :], c, jnp.zeros((N_COND - 1 - DEC_BATCH, D_MODEL), F32)], axis=0)
    mod = _modulation(cond, w_mod, b_mod).reshape(DEPTH, N_COND, 6, D_MODEL)

    slot_major = np.arange(N_EXPERTS).reshape(N_GROUPS, PER_GROUP).T.reshape(-1)
    wr_t = w_router.T[slot_major]
    br_t = b_router.astype(F32)[slot_major].reshape(N_EXPERTS, 1)

    caches = (
        cache_na_k.reshape(DEC_BATCH, DEPTH, PAST_LEN, NA_WIDTH).astype(BF16),
        cache_na_v.reshape(DEC_BATCH, DEPTH, PAST_LEN, NA_WIDTH).astype(BF16),
        cache_da_k.reshape(DEC_BATCH, DEPTH, PAST_LEN, DA_QK_WIDTH).astype(BF16),
        cache_da_v.reshape(DEC_BATCH, DEPTH, PAST_LEN, DA_V_WIDTH).astype(BF16),
    )
    rope_tabs = _rope_tables()
    bias_tab = _na_bias_tables(na_rel_bias)

    new_caches = None
    for l in range(DEPTH):
        lam_init = 0.8 - 0.6 * math.exp(-0.3 * l)
        mod_l = mod[l]
        u_p, u_s, *new_caches = _inproj(xp, xs, mod_l, w_in, l, rope_tabs, cache_bufs=new_caches)
        mix_p = _ctx_mix(u_p, da_lambda, da_norm_g, sc_conv_w, sc_conv_b, l, lam_init)
        mix_s = _lat_mix(u_s, caches, bias_tab, da_lambda, da_norm_g, sc_conv_w, sc_conv_b, l, lam_init)
        x1, h2, *route = _post(mix_p, mix_s, xp, xs, mod_l, w_out, ln1_g, ln1_b, wr_t, br_t, l)
        xp, xs = _moe(h2, *route, x1, mod_l, moe_w1, moe_w3, moe_w2, ln2_g, ln2_b, l)

    nak, nav, dak, dav = new_caches
    return (xp.reshape(BATCH, SEQ, D_MODEL), xs.reshape(DEC_BATCH, DEC_SEQ, D_MODEL),
            nak.reshape(BATCH, DEPTH, SEQ, NA_HEADS, HEAD_DIM), nav.reshape(BATCH, DEPTH, SEQ, NA_HEADS, HEAD_DIM),
            dak.reshape(BATCH, DEPTH, SEQ, 2 * DA_HEADS, DA_DIM), dav.reshape(BATCH, DEPTH, SEQ, DA_HEADS, 2 * DA_DIM))
```

```python
import functools
import math

import numpy as np
import jax
import jax.numpy as jnp
from jax import lax
from jax.experimental import pallas as pl
from jax.experimental.pallas import tpu as pltpu

F32 = jnp.float32
BF16 = jnp.bfloat16

D_MODEL = 1024
BATCH = 16
SEQ = 256
DEPTH = 4
DEC_BATCH = 2
DEC_SEQ = 2048
PAST_LEN = 512
GRID_W = 64
GRID_ROWS = DEC_SEQ // GRID_W
HEAD_DIM = 64
NA_HEADS = 4
NA_WIDTH = NA_HEADS * HEAD_DIM
NA_WIN_ROWS = 8
NA_WIN_COLS = 16
SC_WIDTH = 256
DA_HEADS = 4
DA_DIM = 64
DA_QK_WIDTH = 2 * DA_HEADS * DA_DIM
DA_V_WIDTH = DA_HEADS * 2 * DA_DIM
MIX_WIDTH = NA_WIDTH + SC_WIDTH + DA_V_WIDTH
IN_WIDTH = 3 * NA_WIDTH + 3 * SC_WIDTH + 2 * DA_QK_WIDTH + DA_V_WIDTH
OFF_NA_Q = 0
OFF_NA_K = NA_WIDTH
OFF_NA_V = 2 * NA_WIDTH
OFF_SC_B = 3 * NA_WIDTH
OFF_SC_C = OFF_SC_B + SC_WIDTH
OFF_SC_X = OFF_SC_C + SC_WIDTH
OFF_DA_Q = OFF_SC_X + SC_WIDTH
OFF_DA_K = OFF_DA_Q + DA_QK_WIDTH
OFF_DA_V = OFF_DA_K + DA_QK_WIDTH
ROPE_BASE = 10000.0
N_EXPERTS = 16
N_GROUPS = 4
PER_GROUP = N_EXPERTS // N_GROUPS
D_EXPERT = 256
ALPHA = (2 * DEPTH) ** 0.25
LN_EPS = 1e-5
LOG2E = math.log2(math.e)
Q_SCALE = HEAD_DIM ** -0.5 * LOG2E

N_COND = 8
TM = 512
SEQ_PER_TILE = TM // SEQ
CTX_TILES = BATCH * SEQ // TM
CTX_SEQ_PER_STEP = 2
TM_MOE = 512
TM_POST = TM_MOE
MOE_CHUNK = 64
MOE_BLOCK = 4
MOE_SORTED = TM_MOE + (N_GROUPS - 1) * MOE_CHUNK
assert TM == TM_MOE
QT = 256
QT_ROWS = QT // GRID_W
NA_KEY_ROWS = 12
NA_KEYS = NA_KEY_ROWS * GRID_W
VMEM_LIMIT = 56 * 1024 * 1024


def _dot(a, b):
    return jnp.dot(a, b, preferred_element_type=F32)


def _dot_nt(a, b):
    return lax.dot_general(a, b, (((1,), (1,)), ((), ())), preferred_element_type=F32)


def _ln(x):
    mu = jnp.mean(x, -1, keepdims=True)
    xc = x - mu
    var = jnp.mean(xc * xc, -1, keepdims=True)
    return xc * lax.rsqrt(var + LN_EPS)


def _params(*sem):
    return pltpu.CompilerParams(dimension_semantics=sem, vmem_limit_bytes=VMEM_LIMIT)


def _mod_kernel(cond_ref, w_ref, b_ref, o_ref):
    c = cond_ref[...]
    s = (c * jax.nn.sigmoid(c)).astype(BF16)
    o_ref[...] = _dot(s, w_ref[...].astype(BF16)) + b_ref[...]


def _modulation(cond, w_mod, b_mod):
    tn = 1024
    return pl.pallas_call(
        _mod_kernel,
        grid=(DEPTH, 6 * D_MODEL // tn),
        in_specs=[
            pl.BlockSpec((N_COND, D_MODEL), lambda l, j: (0, 0)),
            pl.BlockSpec((None, D_MODEL, tn), lambda l, j: (l, 0, j)),
            pl.BlockSpec((None, 1, tn), lambda l, j: (l, 0, j)),
        ],
        out_specs=pl.BlockSpec((None, N_COND, tn), lambda l, j: (l, 0, j)),
        out_shape=jax.ShapeDtypeStruct((DEPTH, N_COND, 6 * D_MODEL), F32),
        compiler_params=_params("parallel", "parallel"),
        name="modulation",
    )(cond, w_mod, b_mod.reshape(DEPTH, 1, 6 * D_MODEL))


def _rope(t, cos, sin):
    lane = lax.broadcasted_iota(jnp.int32, t.shape, 1)
    first = (lane // 16) % 2 == 0
    n = t.shape[1]
    swapped = jnp.where(first, pltpu.roll(t, n - 16, 1), pltpu.roll(t, 16, 1))
    return t * cos + swapped * sin


def _tile_stream(i, n_ctx_tiles, lat_tiles_per_batch):
    lat = jnp.maximum(i - n_ctx_tiles, 0)
    return jnp.minimum(i, n_ctx_tiles - 1), lat, jnp.where(i < n_ctx_tiles, 0, 1 + lat // lat_tiles_per_batch)


def _inproj_kernel(*refs):
    xp_ref, xs_ref, mod_ref, w_ref, cos_ref, sin_ref = refs[:6]
    u_ref, nak_ref, nav_ref, dak_ref, dav_ref, wbf_ref = refs[-6:]
    i = pl.program_id(0)
    is_ctx = i < CTX_TILES

    @pl.when(i == 0)
    def _():
        wbf_ref[...] = w_ref[...].astype(BF16)

    mod = mod_ref[...]
    x = jnp.where(is_ctx, xp_ref[...], xs_ref[...])
    h = _ln(x) * (1.0 + mod[1:2]) + mod[0:1]
    u = _dot(h.astype(BF16), wbf_ref[...])
    cos, sin = cos_ref[...], sin_ref[...]
    da_q = _rope(u[:, OFF_DA_Q:OFF_DA_K], cos, sin)
    da_k = _rope(u[:, OFF_DA_K:OFF_DA_V], cos, sin)
    u_ref[:, :OFF_NA_K] = (u[:, :OFF_NA_K] * Q_SCALE).astype(BF16)
    u_ref[:, OFF_NA_K:OFF_DA_Q] = u[:, OFF_NA_K:OFF_DA_Q].astype(BF16)
    u_ref[:, OFF_DA_Q:OFF_DA_K] = (da_q * Q_SCALE).astype(BF16)
    u_ref[:, OFF_DA_K:OFF_DA_V] = da_k.astype(BF16)
    u_ref[:, OFF_DA_V:] = u[:, OFF_DA_V:].astype(BF16)

    @pl.when(is_ctx)
    def _():
        for s in range(SEQ_PER_TILE):
            rows = slice(s * SEQ, (s + 1) * SEQ)
            nak_ref[s] = u[rows, OFF_NA_K:OFF_NA_V]
            nav_ref[s] = u[rows, OFF_NA_V:OFF_SC_B]
            dak_ref[s] = u[rows, OFF_DA_K:OFF_DA_V]
            for j in range(DA_HEADS):
                dav_ref[s, pl.ds(j, SEQ, stride=DA_HEADS), :] = (
                    u[rows, OFF_DA_V + j * 2 * DA_DIM:OFF_DA_V + (j + 1) * 2 * DA_DIM])


def _inproj(xp, xs, mod_l, w_in, l, rope_tabs, cache_bufs=None):
    route = functools.partial(_tile_stream, n_ctx_tiles=CTX_TILES, lat_tiles_per_batch=DEC_SEQ // TM)
    in_specs = [
        pl.BlockSpec((TM, D_MODEL), lambda i: (route(i)[0], 0)),
        pl.BlockSpec((TM, D_MODEL), lambda i: (route(i)[1], 0)),
        pl.BlockSpec((None, 6, D_MODEL), lambda i: (route(i)[2], 0, 0)),
        pl.BlockSpec((None, D_MODEL, IN_WIDTH), lambda i: (l, 0, 0), pipeline_mode=pl.Buffered(1)),
        pl.BlockSpec((TM, DA_QK_WIDTH), lambda i: (jnp.where(i < CTX_TILES, 0, 1 + route(i)[1] % (DEC_SEQ // TM)), 0)),
        pl.BlockSpec((TM, DA_QK_WIDTH), lambda i: (jnp.where(i < CTX_TILES, 0, 1 + route(i)[1] % (DEC_SEQ // TM)), 0)),
    ]
    args = [xp, xs, mod_l, w_in, *rope_tabs]
    out_specs = [pl.BlockSpec((TM, IN_WIDTH), lambda i: (i, 0))]
    out_shape = [jax.ShapeDtypeStruct((xp.shape[0] + xs.shape[0], IN_WIDTH), BF16)]
    for rows, width in ((SEQ, NA_WIDTH), (SEQ, NA_WIDTH), (SEQ, DA_QK_WIDTH), (SEQ * DA_HEADS, 2 * DA_DIM)):
        out_specs.append(pl.BlockSpec((SEQ_PER_TILE, None, rows, width), lambda i: (route(i)[0], l, 0, 0)))
        out_shape.append(jax.ShapeDtypeStruct((BATCH, DEPTH, rows, width), F32))
    aliases = {}
    if cache_bufs is not None:
        aliases = {len(args) + k: 1 + k for k in range(4)}
        in_specs += [pl.BlockSpec(memory_space=pl.ANY)] * 4
        args += list(cache_bufs)
    return pl.pallas_call(
        _inproj_kernel,
        grid=(CTX_TILES + xs.shape[0] // TM,),
        in_specs=in_specs,
        out_specs=out_specs,
        out_shape=out_shape,
        scratch_shapes=[pltpu.VMEM((D_MODEL, IN_WIDTH), BF16)],
        input_output_aliases=aliases,
        compiler_params=_params("arbitrary"),
        name="inproj",
    )(*args)


def _lambda(lam_ref, lam_init):
    lp = lam_ref[...]
    s1 = jnp.sum(lp[0:1] * lp[1:2], axis=-1, keepdims=True)
    s2 = jnp.sum(lp[2:3] * lp[3:4], axis=-1, keepdims=True)
    return jnp.exp(s1) - jnp.exp(s2) + lam_init


def _with_ones(v):
    return jnp.concatenate([v, jnp.ones((v.shape[0], max(v.shape[1], 64)), v.dtype)], axis=1)


def _softmax_pv(score_parts, v_ones, d):
    m = functools.reduce(jnp.maximum, [jnp.max(s, -1, keepdims=True) for s in score_parts])
    e = [jnp.exp2(s - m).astype(BF16) for s in score_parts]
    o = _dot(e[0] if len(e) == 1 else jnp.concatenate(e, axis=1), v_ones)
    return o[:, :d] / o[:, d:d + 1]


def _softmax_pv_small(s, v):
    e = jnp.exp2(s - jnp.max(s, -1, keepdims=True))
    return _dot(e.astype(BF16), v) / jnp.sum(e, -1, keepdims=True)


def _diff_head_norm(o1, o2, lam, gain, lam_init):
    o = o1 - lam * o2
    o = o * lax.rsqrt(jnp.mean(o * o, -1, keepdims=True) + LN_EPS)
    return o * gain * (1.0 - lam_init)


def _gated_conv(b, v, v_prev_row, v_next_row, w, bias):
    n = v.shape[0]
    row = lax.broadcasted_iota(jnp.int32, v.shape, 0)
    prev = jnp.where(row == 0, v_prev_row, pltpu.roll(v, 1, 0))
    nxt = jnp.where(row == n - 1, v_next_row, pltpu.roll(v, n - 1, 0))
    return b * (prev * w[0:1] + v * w[1:2] + nxt * w[2:3] + bias)


def _ctx_mix_kernel(u_ref, lam_ref, gain_ref, cw_ref, cb_ref, o_ref, *, lam_init):
    lam = _lambda(lam_ref, lam_init)
    gain = gain_ref[...]
    for s in range(CTX_SEQ_PER_STEP):
        rows = slice(s * SEQ, (s + 1) * SEQ)
        outs = []
        for h in range(NA_HEADS):
            q = u_ref[rows, OFF_NA_Q + h * HEAD_DIM:OFF_NA_Q + (h + 1) * HEAD_DIM]
            k = u_ref[rows, OFF_NA_K + h * HEAD_DIM:OFF_NA_K + (h + 1) * HEAD_DIM]
            v = u_ref[rows, OFF_NA_V + h * HEAD_DIM:OFF_NA_V + (h + 1) * HEAD_DIM]
            outs.append(_softmax_pv_small(_dot_nt(q, k), v))
        zero_row = jnp.zeros((1, SC_WIDTH), F32)
        vc = u_ref[rows, OFF_SC_C:OFF_SC_X].astype(F32) * u_ref[rows, OFF_SC_X:OFF_DA_Q].astype(F32)
        outs.append(_gated_conv(u_ref[rows, OFF_SC_B:OFF_SC_C].astype(F32), vc, zero_row, zero_row,
                                cw_ref[...], cb_ref[...]))
        for j in range(DA_HEADS):
            v = u_ref[rows, OFF_DA_V + j * 2 * DA_DIM:OFF_DA_V + (j + 1) * 2 * DA_DIM]
            o12 = []
            for mi in (2 * j, 2 * j + 1):
                q = u_ref[rows, OFF_DA_Q + mi * DA_DIM:OFF_DA_Q + (mi + 1) * DA_DIM]
                k = u_ref[rows, OFF_DA_K + mi * DA_DIM:OFF_DA_K + (mi + 1) * DA_DIM]
                o12.append(_softmax_pv_small(_dot_nt(q, k), v))
            outs.append(_diff_head_norm(o12[0], o12[1], lam, gain, lam_init))
        o_ref[rows, :] = jnp.concatenate(outs, axis=-1).astype(BF16)


def _ctx_mix(u, da_lambda, da_norm_g, conv_w, conv_b, l, lam_init):
    step_rows = CTX_SEQ_PER_STEP * SEQ
    return pl.pallas_call(
        functools.partial(_ctx_mix_kernel, lam_init=lam_init),
        grid=(BATCH // CTX_SEQ_PER_STEP,),
        in_specs=[
            pl.BlockSpec((step_rows, IN_WIDTH), lambda b: (b, 0)),
            pl.BlockSpec((None, 4, DA_DIM), lambda b: (l, 0, 0)),
            pl.BlockSpec((None, 1, 2 * DA_DIM), lambda b: (l, 0, 0)),
            pl.BlockSpec((None, 3, SC_WIDTH), lambda b: (l, 0, 0)),
            pl.BlockSpec((None, 1, SC_WIDTH), lambda b: (l, 0, 0)),
        ],
        out_specs=pl.BlockSpec((step_rows, MIX_WIDTH), lambda b: (b, 0)),
        out_shape=jax.ShapeDtypeStruct((BATCH * SEQ, MIX_WIDTH), BF16),
        compiler_params=_params("parallel"),
        name="mix_context",
    )(u, da_lambda, da_norm_g.reshape(DEPTH, 1, 2 * DA_DIM), conv_w, conv_b.reshape(DEPTH, 1, SC_WIDTH))


def _lat_mix_kernel(uq_ref, nak_ref, nav_ref, sc_ref, dakv_ref, cnak_ref, cnav_ref, cdak_ref, cdav_ref,
                    bias_ref, lam_ref, gain_ref, cw_ref, cb_ref, o_ref, kall_ref, vall_ref, cnav1_ref, *, lam_init):
    qt = pl.program_id(1)
    n_qt = pl.num_programs(1)
    lam = _lambda(lam_ref, lam_init)
    gain = gain_ref[...]

    @pl.when(qt == 0)
    def _():
        kall_ref[:DEC_SEQ, :] = dakv_ref[:, :DA_QK_WIDTH]
        kall_ref[DEC_SEQ:, :] = cdak_ref[...]
        for j in range(DA_HEADS):
            vs = slice(j * 2 * DA_DIM, (j + 1) * 2 * DA_DIM)
            cols = slice(j * 4 * DA_DIM, (j + 1) * 4 * DA_DIM)
            vall_ref[:DEC_SEQ, cols] = _with_ones(dakv_ref[:, DA_QK_WIDTH + j * 2 * DA_DIM:DA_QK_WIDTH + (j + 1) * 2 * DA_DIM])
            vall_ref[DEC_SEQ:, cols] = _with_ones(cdav_ref[:, vs])
        for h in range(NA_HEADS):
            cnav1_ref[:, h * 2 * HEAD_DIM:(h + 1) * 2 * HEAD_DIM] = _with_ones(cnav_ref[:, h * HEAD_DIM:(h + 1) * HEAD_DIM])

    outs = []
    key_row0 = jnp.clip(qt * QT_ROWS - NA_WIN_ROWS // 2, 0, GRID_ROWS - NA_KEY_ROWS)
    k0 = pl.multiple_of(key_row0 * GRID_W, GRID_W)
    for h in range(NA_HEADS):
        hs = slice(h * HEAD_DIM, (h + 1) * HEAD_DIM)
        q = uq_ref[:, OFF_NA_Q + h * HEAD_DIM:OFF_NA_Q + (h + 1) * HEAD_DIM]
        k_loc = nak_ref[pl.ds(k0, NA_KEYS), hs]
        v_ones = jnp.concatenate([_with_ones(nav_ref[pl.ds(k0, NA_KEYS), hs]),
                                  cnav1_ref[:, h * 2 * HEAD_DIM:(h + 1) * 2 * HEAD_DIM]], axis=0)
        s_loc = _dot_nt(q, k_loc) + bias_ref[h]
        s_ctx = _dot_nt(q, cnak_ref[:, hs])
        outs.append(_softmax_pv([s_loc, s_ctx], v_ones, HEAD_DIM))
    t0 = pl.multiple_of(qt * QT, QT)
    halo = 16
    before = sc_ref[pl.ds(pl.multiple_of(jnp.maximum(t0 - halo, 0), halo), halo), :].astype(F32)
    after = sc_ref[pl.ds(pl.multiple_of(jnp.minimum(t0 + QT, DEC_SEQ - halo), halo), halo), :].astype(F32)
    cur = sc_ref[pl.ds(t0, QT), :].astype(F32)
    v_prev = before[halo - 1:halo, SC_WIDTH:2 * SC_WIDTH] * before[halo - 1:halo, 2 * SC_WIDTH:]
    v_next = after[0:1, SC_WIDTH:2 * SC_WIDTH] * after[0:1, 2 * SC_WIDTH:]
    v_prev = jnp.where(qt > 0, v_prev, 0.0)
    v_next = jnp.where(qt < n_qt - 1, v_next, 0.0)
    outs.append(_gated_conv(cur[:, :SC_WIDTH], cur[:, SC_WIDTH:2 * SC_WIDTH] * cur[:, 2 * SC_WIDTH:],
                            v_prev, v_next, cw_ref[...], cb_ref[...]))
    for j in range(DA_HEADS):
        v_ones = vall_ref[:, j * 4 * DA_DIM:(j + 1) * 4 * DA_DIM]
        o12 = []
        for mi in (2 * j, 2 * j + 1):
            q = uq_ref[:, OFF_DA_Q + mi * DA_DIM:OFF_DA_Q + (mi + 1) * DA_DIM]
            s = _dot_nt(q, kall_ref[:, mi * DA_DIM:(mi + 1) * DA_DIM])
            o12.append(_softmax_pv([s], v_ones, 2 * DA_DIM))
        outs.append(_diff_head_norm(o12[0], o12[1], lam, gain, lam_init))
    o_ref[...] = jnp.concatenate(outs, axis=-1).astype(BF16)


def _lat_mix(u, caches, bias_tab, da_lambda, da_norm_g, conv_w, conv_b, l, lam_init):
    cnak, cnav, cdak, cdav = caches
    n_qt = DEC_SEQ // QT
    q0 = BATCH * SEQ // QT
    b0 = BATCH * SEQ // DEC_SEQ

    def bias_map(b, qt):
        return (l, 0, jnp.where(qt == 0, 0, jnp.where(qt == n_qt - 1, 2, 1)), 0, 0)

    return pl.pallas_call(
        functools.partial(_lat_mix_kernel, lam_init=lam_init),
        grid=(DEC_BATCH, n_qt),
        in_specs=[
            pl.BlockSpec((QT, IN_WIDTH), lambda b, qt: (q0 + b * n_qt + qt, 0)),
            pl.BlockSpec((DEC_SEQ, NA_WIDTH), lambda b, qt: (b0 + b, OFF_NA_K // NA_WIDTH)),
            pl.BlockSpec((DEC_SEQ, NA_WIDTH), lambda b, qt: (b0 + b, OFF_NA_V // NA_WIDTH)),
            pl.BlockSpec((DEC_SEQ, 3 * SC_WIDTH), lambda b, qt: (b0 + b, OFF_SC_B // (3 * SC_WIDTH))),
            pl.BlockSpec((DEC_SEQ, DA_QK_WIDTH + DA_V_WIDTH), lambda b, qt: (b0 + b, OFF_DA_K // (DA_QK_WIDTH + DA_V_WIDTH))),
            pl.BlockSpec((None, None, PAST_LEN, NA_WIDTH), lambda b, qt: (b, l, 0, 0)),
            pl.BlockSpec((None, None, PAST_LEN, NA_WIDTH), lambda b, qt: (b, l, 0, 0)),
            pl.BlockSpec((None, None, PAST_LEN, DA_QK_WIDTH), lambda b, qt: (b, l, 0, 0)),
            pl.BlockSpec((None, None, PAST_LEN, DA_V_WIDTH), lambda b, qt: (b, l, 0, 0)),
            pl.BlockSpec((None, NA_HEADS, None, QT, NA_KEYS), bias_map),
            pl.BlockSpec((None, 4, DA_DIM), lambda b, qt: (l, 0, 0)),
            pl.BlockSpec((None, 1, 2 * DA_DIM), lambda b, qt: (l, 0, 0)),
            pl.BlockSpec((None, 3, SC_WIDTH), lambda b, qt: (l, 0, 0)),
            pl.BlockSpec((None, 1, SC_WIDTH), lambda b, qt: (l, 0, 0)),
        ],
        out_specs=pl.BlockSpec((QT, MIX_WIDTH), lambda b, qt: (b * n_qt + qt, 0)),
        out_shape=jax.ShapeDtypeStruct((DEC_BATCH * DEC_SEQ, MIX_WIDTH), BF16),
        scratch_shapes=[
            pltpu.VMEM((DEC_SEQ + PAST_LEN, DA_QK_WIDTH), BF16),
            pltpu.VMEM((DEC_SEQ + PAST_LEN, 2 * DA_V_WIDTH), BF16),
            pltpu.VMEM((PAST_LEN, 2 * NA_WIDTH), BF16),
        ],
        compiler_params=_params("parallel", "arbitrary"),
        name="mix_latent",
    )(u, u, u, u, u, cnak, cnav, cdak, cdav, bias_tab, da_lambda,
      da_norm_g.reshape(DEPTH, 1, 2 * DA_DIM), conv_w, conv_b.reshape(DEPTH, 1, SC_WIDTH))


def _first_max(vals):
    m = functools.reduce(jnp.maximum, vals)
    hot, taken = [], None
    for v in vals:
        is_max = v == m
        if taken is None:
            hot.append(is_max)
            taken = is_max
        else:
            hot.append(is_max & ~taken)
            taken = taken | is_max
    return hot, m


def _pick(hot, vals):
    out = vals[-1]
    for h, v in zip(hot[-2::-1], vals[-2::-1]):
        out = jnp.where(h, v, out)
    return out


def _route(h2, h2_bf, wr_ref, br_ref):
    tm = h2.shape[0]
    w = wr_ref[...]
    w_hi = w.astype(BF16)
    w_lo = (w - w_hi.astype(F32)).astype(BF16)
    h_lo = (h2 - h2_bf.astype(F32)).astype(BF16)
    z = _dot_nt(w_hi, h2_bf) + _dot_nt(w_hi, h_lo) + _dot_nt(w_lo, h2_bf)
    scores = jax.nn.sigmoid(z)
    biased = scores + br_ref[...]
    P = [biased[k * N_GROUPS:(k + 1) * N_GROUPS] for k in range(PER_GROUP)]
    S = [scores[k * N_GROUPS:(k + 1) * N_GROUPS] for k in range(PER_GROUP)]
    pair_sums = [P[i] + P[j] for i in range(PER_GROUP) for j in range(i + 1, PER_GROUP)]
    group_score = functools.reduce(jnp.maximum, pair_sums)
    sel, _ = _first_max([group_score[g:g + 1] for g in range(N_GROUPS)])
    c = [_pick(sel, [P[k][g:g + 1] for g in range(N_GROUPS)]) for k in range(PER_GROUP)]
    cs = [_pick(sel, [S[k][g:g + 1] for g in range(N_GROUPS)]) for k in range(PER_GROUP)]
    t1, _ = _first_max(c)
    t2, _ = _first_max([jnp.where(t, -jnp.inf, v) for t, v in zip(t1, c)])
    w1 = functools.reduce(jnp.add, [jnp.where(t, v, 0.0) for t, v in zip(t1, cs)])
    w2 = functools.reduce(jnp.add, [jnp.where(t, v, 0.0) for t, v in zip(t2, cs)])
    total = w1 + w2
    slot_gate = [jnp.where(a, w1 / total, jnp.where(b, w2 / total, 0.0)) for a, b in zip(t1, t2)]
    onehot = jnp.concatenate([s.astype(F32) for s in sel] + [jnp.zeros((8 - N_GROUPS, tm), F32)], axis=0)
    earlier = (lax.broadcasted_iota(jnp.int32, (tm, tm), 0) < lax.broadcasted_iota(jnp.int32, (tm, tm), 1))
    rank = _dot(onehot.astype(BF16), earlier.astype(BF16))
    count = jnp.sum(onehot, axis=1, keepdims=True)
    n_chunks = jnp.floor((count + (MOE_CHUNK - 1)) * (1.0 / MOE_CHUNK))
    pos = jnp.zeros((1, tm), F32)
    start = jnp.zeros((1, 1), F32)
    for g in range(N_GROUPS):
        pos = jnp.where(sel[g], start + rank[g:g + 1], pos)
        start = start + n_chunks[g:g + 1] * MOE_CHUNK
    rows = [pos] + slot_gate + [jnp.zeros((128 - 1 - PER_GROUP, tm), F32)]
    return jnp.concatenate(rows, axis=0), n_chunks


def _post_kernel(mixp_ref, mixs_ref, xp_ref, xs_ref, mod_ref, wout_ref, g_ref, b_ref, wr_ref, br_ref,
                 x1_ref, h2_ref, tok_ref, pos_ref, nch_ref, wbf_ref):
    i = pl.program_id(0)
    is_ctx = i < CTX_TILES

    @pl.when(i == 0)
    def _():
        wbf_ref[...] = wout_ref[...].astype(BF16)

    mod = mod_ref[...]
    y = _dot(jnp.where(is_ctx, mixp_ref[...], mixs_ref[...]), wbf_ref[...])
    x = jnp.where(is_ctx, xp_ref[...], xs_ref[...])
    x1 = _ln(ALPHA * x + mod[2:3] * y) * g_ref[...] + b_ref[...]
    h2 = _ln(x1) * (1.0 + mod[4:5]) + mod[3:4]
    h2_bf = h2.astype(BF16)
    x1_ref[...] = x1
    h2_ref[...] = h2_bf
    rows, n_chunks = _route(h2, h2_bf, wr_ref, br_ref)
    tok_ref[...] = rows.T
    pos_ref[...] = rows[0:8]
    nch_ref[...] = jnp.broadcast_to(n_chunks, (8, 128)).astype(jnp.int32)


def _post(mix_p, mix_s, xp, xs, mod_l, w_out, ln_g, ln_b, wr_t, br_t, l):
    n = xp.shape[0] + xs.shape[0]
    route = functools.partial(_tile_stream, n_ctx_tiles=CTX_TILES, lat_tiles_per_batch=DEC_SEQ // TM_POST)
    return pl.pallas_call(
        _post_kernel,
        grid=(n // TM_POST,),
        in_specs=[
            pl.BlockSpec((TM_POST, MIX_WIDTH), lambda i: (route(i)[0], 0)),
            pl.BlockSpec((TM_POST, MIX_WIDTH), lambda i: (route(i)[1], 0)),
            pl.BlockSpec((TM_POST, D_MODEL), lambda i: (route(i)[0], 0)),
            pl.BlockSpec((TM_POST, D_MODEL), lambda i: (route(i)[1], 0)),
            pl.BlockSpec((None, 6, D_MODEL), lambda i: (route(i)[2], 0, 0)),
            pl.BlockSpec((None, MIX_WIDTH, D_MODEL), lambda i: (l, 0, 0), pipeline_mode=pl.Buffered(1)),
            pl.BlockSpec((None, 1, D_MODEL), lambda i: (l, 0, 0)),
            pl.BlockSpec((None, 1, D_MODEL), lambda i: (l, 0, 0)),
            pl.BlockSpec((N_EXPERTS, D_MODEL), lambda i: (0, 0)),
            pl.BlockSpec((N_EXPERTS, 1), lambda i: (0, 0)),
        ],
        out_specs=[
            pl.BlockSpec((TM_POST, D_MODEL), lambda i: (i, 0)),
            pl.BlockSpec((TM_POST, D_MODEL), lambda i: (i, 0)),
            pl.BlockSpec((TM_POST, 128), lambda i: (i, 0)),
            pl.BlockSpec((None, 8, TM_POST), lambda i: (i, 0, 0)),
            pl.BlockSpec((None, 8, 128), lambda i: (i, 0, 0)),
        ],
        out_shape=[
            jax.ShapeDtypeStruct((n, D_MODEL), F32),
            jax.ShapeDtypeStruct((n, D_MODEL), BF16),
            jax.ShapeDtypeStruct((n, 128), F32),
            jax.ShapeDtypeStruct((n // TM_POST, 8, TM_POST), F32),
            jax.ShapeDtypeStruct((n // TM_POST, 8, 128), jnp.int32),
        ],
        scratch_shapes=[pltpu.VMEM((MIX_WIDTH, D_MODEL), BF16)],
        compiler_params=_params("arbitrary"),
        name="post",
    )(mix_p, mix_s, xp, xs, mod_l, w_out, ln_g.reshape(DEPTH, 1, D_MODEL), ln_b.reshape(DEPTH, 1, D_MODEL), wr_t, br_t)


def _moe_kernel(nch_ref, h_ref, tok_ref, pos_ref, x1_ref, mod_ref, w1f_ref, w3f_ref, w2f_ref, g_ref, b_ref,
                op_ref, os_ref, w1_ref, w3_ref, w2_ref, hs_ref, gs_ref, ys_ref):
    step = pl.program_id(0)

    @pl.when(step < N_EXPERTS)
    def _():
        w1_ref[step] = w1f_ref[...].astype(BF16)
        w3_ref[step] = w3f_ref[...].astype(BF16)
        w2_ref[step] = w2f_ref[...].astype(BF16)

    @pl.when(step >= N_EXPERTS)
    def _():
        i = step - N_EXPERTS
        out = _moe_tile(i, nch_ref, h_ref, tok_ref, pos_ref, x1_ref, mod_ref, w1_ref, w3_ref, w2_ref, g_ref, b_ref,
                        hs_ref, gs_ref, ys_ref)

        @pl.when(i < CTX_TILES)
        def _():
            op_ref[...] = out

        @pl.when(i >= CTX_TILES)
        def _():
            os_ref[...] = out


def _moe_tile(i, nch_ref, h_ref, tok_ref, pos_ref, x1_ref, mod_ref, w1_ref, w3_ref, w2_ref, g_ref, b_ref,
              hs_ref, gs_ref, ys_ref):
    tok = tok_ref[...]
    pos_lane = pos_ref[0:1, :]
    pos_col = tok[:, 0:1]
    slot = lax.broadcasted_iota(jnp.int32, (MOE_SORTED, TM_MOE), 0).astype(F32)
    sort = (slot == pos_lane).astype(BF16)
    tok_hi = tok.astype(BF16)
    tok_lo = (tok - tok_hi.astype(F32)).astype(BF16)
    z = _dot(sort, jnp.concatenate([h_ref[...], tok_hi, tok_lo], axis=1))
    hs_ref[...] = z[:, :D_MODEL].astype(BF16)
    gs_ref[...] = z[:, D_MODEL:D_MODEL + 128] + z[:, D_MODEL + 128:]
    ys_ref[...] = jnp.zeros_like(ys_ref)

    def experts(g, chunk0, n_rows):
        r0 = pl.multiple_of(chunk0 * MOE_CHUNK, MOE_CHUNK)
        rows = hs_ref[pl.ds(r0, n_rows), :]
        gates = gs_ref[pl.ds(r0, n_rows), :]
        acc = jnp.zeros((n_rows, D_MODEL), F32)
        for k in range(PER_GROUP):
            e = g * PER_GROUP + k
            a = _dot(rows, w1_ref[e])
            b = _dot(rows, w3_ref[e])
            hid = a * jax.nn.sigmoid(a) * b * gates[:, 1 + k:2 + k]
            acc = acc + _dot(hid.astype(BF16), w2_ref[e])
        ys_ref[pl.ds(r0, n_rows), :] = acc.astype(BF16)

    def group(g, first):
        n = nch_ref[i * N_GROUPS + g]
        n_blocks = n // MOE_BLOCK
        rest = n - n_blocks * MOE_BLOCK

        def block(c, carry):
            experts(g, first + c * MOE_BLOCK, MOE_BLOCK * MOE_CHUNK)
            return carry

        lax.fori_loop(0, n_blocks, block, 0)
        for m in range(1, MOE_BLOCK):
            @pl.when(rest == m)
            def _(m=m):
                experts(g, first + n_blocks * MOE_BLOCK, m * MOE_CHUNK)
        return first + n

    lax.fori_loop(0, N_GROUPS, group, 0)

    unsort = (lax.broadcasted_iota(jnp.int32, (TM_MOE, MOE_SORTED), 1).astype(F32) == pos_col).astype(BF16)
    y = _dot(unsort, ys_ref[...])
    mod = mod_ref[...]
    return _ln(ALPHA * x1_ref[...] + mod[5:6] * y) * g_ref[...] + b_ref[...]


def _moe(h2, tok, pos, nch, x1, mod_l, w1, w3, w2, ln_g, ln_b, l):
    n_tiles = x1.shape[0] // TM_MOE
    n_ctx = CTX_TILES * TM_MOE
    per_batch = DEC_SEQ // TM_MOE

    def tile(step):
        return jnp.maximum(step - N_EXPERTS, 0)

    def expert(step):
        return jnp.minimum(step, N_EXPERTS - 1)

    def mod_map(step, s):
        return (_tile_stream(tile(step), CTX_TILES, per_batch)[2], 0, 0)

    grid_spec = pltpu.PrefetchScalarGridSpec(
        num_scalar_prefetch=1,
        grid=(N_EXPERTS + n_tiles,),
        in_specs=[
            pl.BlockSpec((TM_MOE, D_MODEL), lambda step, s: (tile(step), 0)),
            pl.BlockSpec((TM_MOE, 128), lambda step, s: (tile(step), 0)),
            pl.BlockSpec((None, 8, TM_MOE), lambda step, s: (tile(step), 0, 0)),
            pl.BlockSpec((TM_MOE, D_MODEL), lambda step, s: (tile(step), 0)),
            pl.BlockSpec((None, 6, D_MODEL), mod_map),
            pl.BlockSpec((None, None, D_MODEL, D_EXPERT), lambda step, s: (l, expert(step), 0, 0)),
            pl.BlockSpec((None, None, D_MODEL, D_EXPERT), lambda step, s: (l, expert(step), 0, 0)),
            pl.BlockSpec((None, None, D_EXPERT, D_MODEL), lambda step, s: (l, expert(step), 0, 0)),
            pl.BlockSpec((None, 1, D_MODEL), lambda step, s: (l, 0, 0)),
            pl.BlockSpec((None, 1, D_MODEL), lambda step, s: (l, 0, 0)),
        ],
        out_specs=[
            pl.BlockSpec((TM_MOE, D_MODEL), lambda step, s: (jnp.minimum(tile(step), CTX_TILES - 1), 0)),
            pl.BlockSpec((TM_MOE, D_MODEL), lambda step, s: (jnp.maximum(tile(step) - CTX_TILES, 0), 0)),
        ],
        scratch_shapes=[
            pltpu.VMEM((N_EXPERTS, D_MODEL, D_EXPERT), BF16),
            pltpu.VMEM((N_EXPERTS, D_MODEL, D_EXPERT), BF16),
            pltpu.VMEM((N_EXPERTS, D_EXPERT, D_MODEL), BF16),
            pltpu.VMEM((MOE_SORTED, D_MODEL), BF16),
            pltpu.VMEM((MOE_SORTED, 128), F32),
            pltpu.VMEM((MOE_SORTED, D_MODEL), BF16),
        ],
    )
    return pl.pallas_call(
        _moe_kernel,
        grid_spec=grid_spec,
        out_shape=[jax.ShapeDtypeStruct((n_ctx, D_MODEL), F32),
                   jax.ShapeDtypeStruct((x1.shape[0] - n_ctx, D_MODEL), F32)],
        compiler_params=_params("arbitrary"),
        name="moe",
    )(nch[:, :N_GROUPS, 0].reshape(-1), h2, tok, pos, x1, mod_l, w1, w3, w2,
      ln_g.reshape(DEPTH, 1, D_MODEL), ln_b.reshape(DEPTH, 1, D_MODEL))


def _rope_tables():
    t = np.arange(DEC_SEQ)
    half = DA_DIM // 4
    inv_freq = ROPE_BASE ** (-np.arange(half, dtype=np.float32) / half)
    cos, sin = [], []
    for pos in (t // GRID_W, t % GRID_W):
        ang = pos.astype(np.float32)[:, None] * inv_freq[None, :]
        c, s = np.cos(ang), np.sin(ang)
        cos += [c, c]
        sin += [-s, s]
    cos = np.tile(np.concatenate(cos, axis=1), (1, 2 * DA_HEADS))
    sin = np.tile(np.concatenate(sin, axis=1), (1, 2 * DA_HEADS))
    cos = np.concatenate([np.ones((TM, DA_QK_WIDTH), np.float32), cos], axis=0)
    sin = np.concatenate([np.zeros((TM, DA_QK_WIDTH), np.float32), sin], axis=0)
    return jnp.asarray(cos, F32), jnp.asarray(sin, F32)


_NA_TILE_KINDS = ((0, 0), (2 * QT_ROWS, 2 * QT_ROWS - NA_WIN_ROWS // 2), (GRID_ROWS - QT_ROWS, GRID_ROWS - NA_KEY_ROWS))


def _na_bias_kernel(by_ref, o_ref):
    qc = lax.broadcasted_iota(jnp.int32, (GRID_W, GRID_W), 0)
    kc = lax.broadcasted_iota(jnp.int32, (GRID_W, GRID_W), 1)
    win_c0 = jnp.clip(qc - NA_WIN_COLS // 2, 0, GRID_W - NA_WIN_COLS)
    in_cols = (kc >= win_c0) & (kc < win_c0 + NA_WIN_COLS)
    masked = jnp.full((GRID_W, GRID_W), -jnp.inf, F32)
    for kind, (r0, key_row0) in enumerate(_NA_TILE_KINDS):
        for a in range(QT_ROWS):
            qr = r0 + a
            win_r0 = min(max(qr - NA_WIN_ROWS // 2, 0), GRID_ROWS - NA_WIN_ROWS)
            blocks = []
            for b in range(NA_KEY_ROWS):
                kr = key_row0 + b
                if win_r0 <= kr < win_r0 + NA_WIN_ROWS:
                    blocks.append(jnp.where(in_cols, by_ref[kr - qr + NA_WIN_ROWS - 1] * LOG2E, masked))
                else:
                    blocks.append(masked)
            o_ref[kind, a * GRID_W:(a + 1) * GRID_W, :] = jnp.concatenate(blocks, axis=1)


def _na_bias_tables(na_rel_bias):
    n_dr, n_dc = 2 * NA_WIN_ROWS - 1, 2 * NA_WIN_COLS - 1
    cols = np.arange(GRID_W)
    d_col = np.clip(cols[None, :] - cols[:, None], 1 - NA_WIN_COLS, NA_WIN_COLS - 1) + NA_WIN_COLS - 1
    col_sel = (d_col[None] == np.arange(n_dc)[:, None, None]).astype(np.float32)
    by_col = jnp.einsum('lhab,bqc->lhaqc', na_rel_bias.astype(F32), col_sel, precision=lax.Precision.HIGHEST)
    return pl.pallas_call(
        _na_bias_kernel,
        grid=(DEPTH, NA_HEADS),
        in_specs=[pl.BlockSpec((None, None, n_dr, GRID_W, GRID_W), lambda l, h: (l, h, 0, 0, 0))],
        out_specs=pl.BlockSpec((None, None, len(_NA_TILE_KINDS), QT, NA_KEYS), lambda l, h: (l, h, 0, 0, 0)),
        out_shape=jax.ShapeDtypeStruct((DEPTH, NA_HEADS, len(_NA_TILE_KINDS), QT, NA_KEYS), F32),
        compiler_params=_params("parallel", "parallel"),
        name="na_bias_table",
    )(by_col)


def kernel(x_prompt, x_sample, cache_na_k, cache_na_v, cache_da_k, cache_da_v, c, c_ctx, w_mod, b_mod, w_in,
           na_rel_bias, sc_conv_w, sc_conv_b, da_lambda, da_norm_g, w_out, ln1_g, ln1_b, w_router, b_router,
           moe_w1, moe_w3, moe_w2, ln2_g, ln2_b):
    xp = x_prompt.reshape(BATCH * SEQ, D_MODEL)
    xs = x_sample.reshape(DEC_BATCH * DEC_SEQ, D_MODEL)

    cond = jnp.concatenate([c_ctx[None, :], c, jnp.zeros((N_COND - 1 - DEC_BATCH, D_MODEL), F32)], axis=0)
    mod = _modulation(cond, w_mod, b_mod).reshape(DEPTH, N_COND, 6, D_MODEL)

    slot_major = np.arange(N_EXPERTS).reshape(N_GROUPS, PER_GROUP).T.reshape(-1)
    wr_t = w_router.T[slot_major]
    br_t = b_router.astype(F32)[slot_major].reshape(N_EXPERTS, 1)

    caches = (
        cache_na_k.reshape(DEC_BATCH, DEPTH, PAST_LEN, NA_WIDTH).astype(BF16),
        cache_na_v.reshape(DEC_BATCH, DEPTH, PAST_LEN, NA_WIDTH).astype(BF16),
        cache_da_k.reshape(DEC_BATCH, DEPTH, PAST_LEN, DA_QK_WIDTH).astype(BF16),
        cache_da_v.reshape(DEC_BATCH, DEPTH, PAST_LEN, DA_V_WIDTH).astype(BF16),
    )
    rope_tabs = _rope_tables()
    bias_tab = _na_bias_tables(na_rel_bias)

    new_caches = None
    for l in range(DEPTH):
        lam_init = 0.8 - 0.6 * math.exp(-0.3 * l)
        mod_l = mod[l]
        u, *new_caches = _inproj(xp, xs, mod_l, w_in, l, rope_tabs, cache_bufs=new_caches)
        mix_p = _ctx_mix(u, da_lambda, da_norm_g, sc_conv_w, sc_conv_b, l, lam_init)
        mix_s = _lat_mix(u, caches, bias_tab, da_lambda, da_norm_g, sc_conv_w, sc_conv_b, l, lam_init)
        x1, h2, *route = _post(mix_p, mix_s, xp, xs, mod_l, w_out, ln1_g, ln1_b, wr_t, br_t, l)
        xp, xs = _moe(h2, *route, x1, mod_l, moe_w1, moe_w3, moe_w2, ln2_g, ln2_b, l)

    nak, nav, dak, dav = new_caches
    return (xp.reshape(BATCH, SEQ, D_MODEL), xs.reshape(DEC_BATCH, DEC_SEQ, D_MODEL),
            nak.reshape(BATCH, DEPTH, SEQ, NA_HEADS, HEAD_DIM), nav.reshape(BATCH, DEPTH, SEQ, NA_HEADS, HEAD_DIM),
            dak.reshape(BATCH, DEPTH, SEQ, 2 * DA_HEADS, DA_DIM), dav.reshape(BATCH, DEPTH, SEQ, DA_HEADS, 2 * DA_DIM))
```

```python
import functools
import math

import numpy as np
import jax
import jax.numpy as jnp
from jax import lax
from jax.experimental import pallas as pl
from jax.experimental.pallas import tpu as pltpu

F32 = jnp.float32
BF16 = jnp.bfloat16

D_MODEL = 1024
BATCH = 16
SEQ = 256
DEPTH = 4
DEC_BATCH = 2
DEC_SEQ = 2048
PAST_LEN = 512
GRID_W = 64
GRID_ROWS = DEC_SEQ // GRID_W
HEAD_DIM = 64
NA_HEADS = 4
NA_WIDTH = NA_HEADS * HEAD_DIM
NA_WIN_ROWS = 8
NA_WIN_COLS = 16
SC_WIDTH = 256
DA_HEADS = 4
DA_DIM = 64
DA_QK_WIDTH = 2 * DA_HEADS * DA_DIM
DA_V_WIDTH = DA_HEADS * 2 * DA_DIM
MIX_WIDTH = NA_WIDTH + SC_WIDTH + DA_V_WIDTH
IN_WIDTH = 3 * NA_WIDTH + 3 * SC_WIDTH + 2 * DA_QK_WIDTH + DA_V_WIDTH
OFF_NA_Q = 0
OFF_NA_K = NA_WIDTH
OFF_NA_V = 2 * NA_WIDTH
OFF_SC_B = 3 * NA_WIDTH
OFF_SC_C = OFF_SC_B + SC_WIDTH
OFF_SC_X = OFF_SC_C + SC_WIDTH
OFF_DA_Q = OFF_SC_X + SC_WIDTH
OFF_DA_K = OFF_DA_Q + DA_QK_WIDTH
OFF_DA_V = OFF_DA_K + DA_QK_WIDTH
ROPE_BASE = 10000.0
N_EXPERTS = 16
N_GROUPS = 4
PER_GROUP = N_EXPERTS // N_GROUPS
D_EXPERT = 256
ALPHA = (2 * DEPTH) ** 0.25
LN_EPS = 1e-5
LOG2E = math.log2(math.e)
Q_SCALE = HEAD_DIM ** -0.5 * LOG2E

N_COND = 8
TM = 512
SEQ_PER_TILE = TM // SEQ
CTX_TILES = BATCH * SEQ // TM
CTX_SEQ_PER_STEP = 2
TM_MOE = 512
TM_POST = TM_MOE
MOE_CHUNK = 64
MOE_BLOCK = 4
MOE_SORTED = TM_MOE + (N_GROUPS - 1) * MOE_CHUNK
assert TM == TM_MOE
QT = 256
QT_ROWS = QT // GRID_W
NA_KEY_ROWS = 12
NA_KEYS = NA_KEY_ROWS * GRID_W
VMEM_LIMIT = 56 * 1024 * 1024


def _dot(a, b):
    return jnp.dot(a, b, preferred_element_type=F32)


def _dot_nt(a, b):
    return lax.dot_general(a, b, (((1,), (1,)), ((), ())), preferred_element_type=F32)


def _ln(x):
    mu = jnp.mean(x, -1, keepdims=True)
    xc = x - mu
    var = jnp.mean(xc * xc, -1, keepdims=True)
    return xc * lax.rsqrt(var + LN_EPS)


def _params(*sem):
    return pltpu.CompilerParams(dimension_semantics=sem, vmem_limit_bytes=VMEM_LIMIT)


def _mod_kernel(cond_ref, w_ref, b_ref, o_ref):
    c = cond_ref[...]
    s = (c * jax.nn.sigmoid(c)).astype(BF16)
    o_ref[...] = _dot(s, w_ref[...].astype(BF16)) + b_ref[...]


def _modulation(cond, w_mod, b_mod):
    tn = 1024
    return pl.pallas_call(
        _mod_kernel,
        grid=(DEPTH, 6 * D_MODEL // tn),
        in_specs=[
            pl.BlockSpec((N_COND, D_MODEL), lambda l, j: (0, 0)),
            pl.BlockSpec((None, D_MODEL, tn), lambda l, j: (l, 0, j)),
            pl.BlockSpec((None, 1, tn), lambda l, j: (l, 0, j)),
        ],
        out_specs=pl.BlockSpec((None, N_COND, tn), lambda l, j: (l, 0, j)),
        out_shape=jax.ShapeDtypeStruct((DEPTH, N_COND, 6 * D_MODEL), F32),
        compiler_params=_params("parallel", "parallel"),
        name="modulation",
    )(cond, w_mod, b_mod.reshape(DEPTH, 1, 6 * D_MODEL))


def _rope(t, cos, sin):
    lane = lax.broadcasted_iota(jnp.int32, t.shape, 1)
    first = (lane // 16) % 2 == 0
    n = t.shape[1]
    swapped = jnp.where(first, pltpu.roll(t, n - 16, 1), pltpu.roll(t, 16, 1))
    return t * cos + swapped * sin


def _tile_stream(i, n_ctx_tiles, lat_tiles_per_batch):
    lat = jnp.maximum(i - n_ctx_tiles, 0)
    return jnp.minimum(i, n_ctx_tiles - 1), lat, jnp.where(i < n_ctx_tiles, 0, 1 + lat // lat_tiles_per_batch)


def _inproj_kernel(*refs):
    xp_ref, xs_ref, mod_ref, w_ref, cos_ref, sin_ref = refs[:6]
    u_ref, nak_ref, nav_ref, dak_ref, dav_ref, wbf_ref = refs[-6:]
    i = pl.program_id(0)
    is_ctx = i < CTX_TILES

    @pl.when(i == 0)
    def _():
        wbf_ref[...] = w_ref[...].astype(BF16)

    mod = mod_ref[...]
    x = jnp.where(is_ctx, xp_ref[...], xs_ref[...])
    h = _ln(x) * (1.0 + mod[1:2]) + mod[0:1]
    u = _dot(h.astype(BF16), wbf_ref[...])
    cos, sin = cos_ref[...], sin_ref[...]
    da_q = _rope(u[:, OFF_DA_Q:OFF_DA_K], cos, sin)
    da_k = _rope(u[:, OFF_DA_K:OFF_DA_V], cos, sin)
    u_ref[:, :OFF_NA_K] = (u[:, :OFF_NA_K] * Q_SCALE).astype(BF16)
    u_ref[:, OFF_NA_K:OFF_DA_Q] = u[:, OFF_NA_K:OFF_DA_Q].astype(BF16)
    u_ref[:, OFF_DA_Q:OFF_DA_K] = (da_q * Q_SCALE).astype(BF16)
    u_ref[:, OFF_DA_K:OFF_DA_V] = da_k.astype(BF16)
    u_ref[:, OFF_DA_V:] = u[:, OFF_DA_V:].astype(BF16)

    @pl.when(is_ctx)
    def _():
        for s in range(SEQ_PER_TILE):
            rows = slice(s * SEQ, (s + 1) * SEQ)
            nak_ref[s] = u[rows, OFF_NA_K:OFF_NA_V]
            nav_ref[s] = u[rows, OFF_NA_V:OFF_SC_B]
            dak_ref[s] = u[rows, OFF_DA_K:OFF_DA_V]
            for j in range(DA_HEADS):
                dav_ref[s, pl.ds(j, SEQ, stride=DA_HEADS), :] = (
                    u[rows, OFF_DA_V + j * 2 * DA_DIM:OFF_DA_V + (j + 1) * 2 * DA_DIM])


def _inproj(xp, xs, mod_l, w_in, l, rope_tabs, cache_bufs=None):
    route = functools.partial(_tile_stream, n_ctx_tiles=CTX_TILES, lat_tiles_per_batch=DEC_SEQ // TM)
    in_specs = [
        pl.BlockSpec((TM, D_MODEL), lambda i: (route(i)[0], 0)),
        pl.BlockSpec((TM, D_MODEL), lambda i: (route(i)[1], 0)),
        pl.BlockSpec((None, 6, D_MODEL), lambda i: (route(i)[2], 0, 0)),
        pl.BlockSpec((None, D_MODEL, IN_WIDTH), lambda i: (l, 0, 0), pipeline_mode=pl.Buffered(1)),
        pl.BlockSpec((TM, DA_QK_WIDTH), lambda i: (jnp.where(i < CTX_TILES, 0, 1 + route(i)[1] % (DEC_SEQ // TM)), 0)),
        pl.BlockSpec((TM, DA_QK_WIDTH), lambda i: (jnp.where(i < CTX_TILES, 0, 1 + route(i)[1] % (DEC_SEQ // TM)), 0)),
    ]
    args = [xp, xs, mod_l, w_in, *rope_tabs]
    out_specs = [pl.BlockSpec((TM, IN_WIDTH), lambda i: (i, 0))]
    out_shape = [jax.ShapeDtypeStruct((xp.shape[0] + xs.shape[0], IN_WIDTH), BF16)]
    for rows, width in ((SEQ, NA_WIDTH), (SEQ, NA_WIDTH), (SEQ, DA_QK_WIDTH), (SEQ * DA_HEADS, 2 * DA_DIM)):
        out_specs.append(pl.BlockSpec((SEQ_PER_TILE, None, rows, width), lambda i: (route(i)[0], l, 0, 0)))
        out_shape.append(jax.ShapeDtypeStruct((BATCH, DEPTH, rows, width), F32))
    aliases = {}
    if cache_bufs is not None:
        aliases = {len(args) + k: 1 + k for k in range(4)}
        in_specs += [pl.BlockSpec(memory_space=pl.ANY)] * 4
        args += list(cache_bufs)
    return pl.pallas_call(
        _inproj_kernel,
        grid=(CTX_TILES + xs.shape[0] // TM,),
        in_specs=in_specs,
        out_specs=out_specs,
        out_shape=out_shape,
        scratch_shapes=[pltpu.VMEM((D_MODEL, IN_WIDTH), BF16)],
        input_output_aliases=aliases,
        compiler_params=_params("arbitrary"),
        name="inproj",
    )(*args)


def _lambda(lam_ref, lam_init):
    lp = lam_ref[...]
    s1 = jnp.sum(lp[0:1] * lp[1:2], axis=-1, keepdims=True)
    s2 = jnp.sum(lp[2:3] * lp[3:4], axis=-1, keepdims=True)
    return jnp.exp(s1) - jnp.exp(s2) + lam_init


def _with_ones(v):
    return jnp.concatenate([v, jnp.ones((v.shape[0], max(v.shape[1], 64)), v.dtype)], axis=1)


def _softmax_pv(score_parts, v_ones, d):
    m = functools.reduce(jnp.maximum, [jnp.max(s, -1, keepdims=True) for s in score_parts])
    e = [jnp.exp2(s - m).astype(BF16) for s in score_parts]
    o = _dot(e[0] if len(e) == 1 else jnp.concatenate(e, axis=1), v_ones)
    return o[:, :d] / o[:, d:d + 1]


def _softmax_pv_small(s, v):
    e = jnp.exp2(s - jnp.max(s, -1, keepdims=True))
    return _dot(e.astype(BF16), v) / jnp.sum(e, -1, keepdims=True)


def _diff_head_norm(o1, o2, lam, gain, lam_init):
    o = o1 - lam * o2
    o = o * lax.rsqrt(jnp.mean(o * o, -1, keepdims=True) + LN_EPS)
    return o * gain * (1.0 - lam_init)


def _gated_conv(b, v, v_prev_row, v_next_row, w, bias):
    n = v.shape[0]
    row = lax.broadcasted_iota(jnp.int32, v.shape, 0)
    prev = jnp.where(row == 0, v_prev_row, pltpu.roll(v, 1, 0))
    nxt = jnp.where(row == n - 1, v_next_row, pltpu.roll(v, n - 1, 0))
    return b * (prev * w[0:1] + v * w[1:2] + nxt * w[2:3] + bias)


def _ctx_mix_kernel(u_ref, lam_ref, gain_ref, cw_ref, cb_ref, o_ref, *, lam_init):
    lam = _lambda(lam_ref, lam_init)
    gain = gain_ref[...]
    for s in range(CTX_SEQ_PER_STEP):
        rows = slice(s * SEQ, (s + 1) * SEQ)
        outs = []
        for h in range(NA_HEADS):
            q = u_ref[rows, OFF_NA_Q + h * HEAD_DIM:OFF_NA_Q + (h + 1) * HEAD_DIM]
            k = u_ref[rows, OFF_NA_K + h * HEAD_DIM:OFF_NA_K + (h + 1) * HEAD_DIM]
            v = u_ref[rows, OFF_NA_V + h * HEAD_DIM:OFF_NA_V + (h + 1) * HEAD_DIM]
            outs.append(_softmax_pv_small(_dot_nt(q, k), v))
        zero_row = jnp.zeros((1, SC_WIDTH), F32)
        vc = u_ref[rows, OFF_SC_C:OFF_SC_X].astype(F32) * u_ref[rows, OFF_SC_X:OFF_DA_Q].astype(F32)
        outs.append(_gated_conv(u_ref[rows, OFF_SC_B:OFF_SC_C].astype(F32), vc, zero_row, zero_row,
                                cw_ref[...], cb_ref[...]))
        for j in range(DA_HEADS):
            v = u_ref[rows, OFF_DA_V + j * 2 * DA_DIM:OFF_DA_V + (j + 1) * 2 * DA_DIM]
            o12 = []
            for mi in (2 * j, 2 * j + 1):
                q = u_ref[rows, OFF_DA_Q + mi * DA_DIM:OFF_DA_Q + (mi + 1) * DA_DIM]
                k = u_ref[rows, OFF_DA_K + mi * DA_DIM:OFF_DA_K + (mi + 1) * DA_DIM]
                o12.append(_softmax_pv_small(_dot_nt(q, k), v))
            outs.append(_diff_head_norm(o12[0], o12[1], lam, gain, lam_init))
        o_ref[rows, :] = jnp.concatenate(outs, axis=-1).astype(BF16)


def _ctx_mix(u, da_lambda, da_norm_g, conv_w, conv_b, l, lam_init):
    step_rows = CTX_SEQ_PER_STEP * SEQ
    return pl.pallas_call(
        functools.partial(_ctx_mix_kernel, lam_init=lam_init),
        grid=(BATCH // CTX_SEQ_PER_STEP,),
        in_specs=[
            pl.BlockSpec((step_rows, IN_WIDTH), lambda b: (b, 0)),
            pl.BlockSpec((None, 4, DA_DIM), lambda b: (l, 0, 0)),
            pl.BlockSpec((None, 1, 2 * DA_DIM), lambda b: (l, 0, 0)),
            pl.BlockSpec((None, 3, SC_WIDTH), lambda b: (l, 0, 0)),
            pl.BlockSpec((None, 1, SC_WIDTH), lambda b: (l, 0, 0)),
        ],
        out_specs=pl.BlockSpec((step_rows, MIX_WIDTH), lambda b: (b, 0)),
        out_shape=jax.ShapeDtypeStruct((BATCH * SEQ, MIX_WIDTH), BF16),
        compiler_params=_params("parallel"),
        name="mix_context",
    )(u, da_lambda, da_norm_g.reshape(DEPTH, 1, 2 * DA_DIM), conv_w, conv_b.reshape(DEPTH, 1, SC_WIDTH))


def _lat_mix_kernel(uq_ref, nak_ref, nav_ref, sc_ref, dakv_ref, cnak_ref, cnav_ref, cdak_ref, cdav_ref,
                    bias_ref, lam_ref, gain_ref, cw_ref, cb_ref, o_ref, kall_ref, vall_ref, cnav1_ref, *, lam_init):
    qt = pl.program_id(1)
    n_qt = pl.num_programs(1)
    lam = _lambda(lam_ref, lam_init)
    gain = gain_ref[...]

    @pl.when(qt == 0)
    def _():
        kall_ref[:DEC_SEQ, :] = dakv_ref[:, :DA_QK_WIDTH]
        kall_ref[DEC_SEQ:, :] = cdak_ref[...]
        for j in range(DA_HEADS):
            vs = slice(j * 2 * DA_DIM, (j + 1) * 2 * DA_DIM)
            cols = slice(j * 4 * DA_DIM, (j + 1) * 4 * DA_DIM)
            vall_ref[:DEC_SEQ, cols] = _with_ones(dakv_ref[:, DA_QK_WIDTH + j * 2 * DA_DIM:DA_QK_WIDTH + (j + 1) * 2 * DA_DIM])
            vall_ref[DEC_SEQ:, cols] = _with_ones(cdav_ref[:, vs])
        for h in range(NA_HEADS):
            cnav1_ref[:, h * 2 * HEAD_DIM:(h + 1) * 2 * HEAD_DIM] = _with_ones(cnav_ref[:, h * HEAD_DIM:(h + 1) * HEAD_DIM])

    outs = []
    key_row0 = jnp.clip(qt * QT_ROWS - NA_WIN_ROWS // 2, 0, GRID_ROWS - NA_KEY_ROWS)
    k0 = pl.multiple_of(key_row0 * GRID_W, GRID_W)
    for h in range(NA_HEADS):
        hs = slice(h * HEAD_DIM, (h + 1) * HEAD_DIM)
        q = uq_ref[:, OFF_NA_Q + h * HEAD_DIM:OFF_NA_Q + (h + 1) * HEAD_DIM]
        k_loc = nak_ref[pl.ds(k0, NA_KEYS), hs]
        v_ones = jnp.concatenate([_with_ones(nav_ref[pl.ds(k0, NA_KEYS), hs]),
                                  cnav1_ref[:, h * 2 * HEAD_DIM:(h + 1) * 2 * HEAD_DIM]], axis=0)
        s_loc = _dot_nt(q, k_loc) + bias_ref[h]
        s_ctx = _dot_nt(q, cnak_ref[:, hs])
        outs.append(_softmax_pv([s_loc, s_ctx], v_ones, HEAD_DIM))
    t0 = pl.multiple_of(qt * QT, QT)
    halo = 16
    before = sc_ref[pl.ds(pl.multiple_of(jnp.maximum(t0 - halo, 0), halo), halo), :].astype(F32)
    after = sc_ref[pl.ds(pl.multiple_of(jnp.minimum(t0 + QT, DEC_SEQ - halo), halo), halo), :].astype(F32)
    cur = sc_ref[pl.ds(t0, QT), :].astype(F32)
    v_prev = before[halo - 1:halo, SC_WIDTH:2 * SC_WIDTH] * before[halo - 1:halo, 2 * SC_WIDTH:]
    v_next = after[0:1, SC_WIDTH:2 * SC_WIDTH] * after[0:1, 2 * SC_WIDTH:]
    v_prev = jnp.where(qt > 0, v_prev, 0.0)
    v_next = jnp.where(qt < n_qt - 1, v_next, 0.0)
    outs.append(_gated_conv(cur[:, :SC_WIDTH], cur[:, SC_WIDTH:2 * SC_WIDTH] * cur[:, 2 * SC_WIDTH:],
                            v_prev, v_next, cw_ref[...], cb_ref[...]))
    for j in range(DA_HEADS):
        v_ones = vall_ref[:, j * 4 * DA_DIM:(j + 1) * 4 * DA_DIM]
        o12 = []
        for mi in (2 * j, 2 * j + 1):
            q = uq_ref[:, OFF_DA_Q + mi * DA_DIM:OFF_DA_Q + (mi + 1) * DA_DIM]
            s = _dot_nt(q, kall_ref[:, mi * DA_DIM:(mi + 1) * DA_DIM])
            o12.append(_softmax_pv([s], v_ones, 2 * DA_DIM))
        outs.append(_diff_head_norm(o12[0], o12[1], lam, gain, lam_init))
    o_ref[...] = jnp.concatenate(outs, axis=-1).astype(BF16)


def _lat_mix(u, caches, bias_tab, da_lambda, da_norm_g, conv_w, conv_b, l, lam_init):
    cnak, cnav, cdak, cdav = caches
    n_qt = DEC_SEQ // QT
    q0 = BATCH * SEQ // QT
    b0 = BATCH * SEQ // DEC_SEQ

    def bias_map(b, qt):
        return (l, 0, jnp.where(qt == 0, 0, jnp.where(qt == n_qt - 1, 2, 1)), 0, 0)

    return pl.pallas_call(
        functools.partial(_lat_mix_kernel, lam_init=lam_init),
        grid=(DEC_BATCH, n_qt),
        in_specs=[
            pl.BlockSpec((QT, IN_WIDTH), lambda b, qt: (q0 + b * n_qt + qt, 0)),
            pl.BlockSpec((DEC_SEQ, NA_WIDTH), lambda b, qt: (b0 + b, OFF_NA_K // NA_WIDTH)),
            pl.BlockSpec((DEC_SEQ, NA_WIDTH), lambda b, qt: (b0 + b, OFF_NA_V // NA_WIDTH)),
            pl.BlockSpec((DEC_SEQ, 3 * SC_WIDTH), lambda b, qt: (b0 + b, OFF_SC_B // (3 * SC_WIDTH))),
            pl.BlockSpec((DEC_SEQ, DA_QK_WIDTH + DA_V_WIDTH), lambda b, qt: (b0 + b, OFF_DA_K // (DA_QK_WIDTH + DA_V_WIDTH))),
            pl.BlockSpec((None, None, PAST_LEN, NA_WIDTH), lambda b, qt: (b, l, 0, 0)),
            pl.BlockSpec((None, None, PAST_LEN, NA_WIDTH), lambda b, qt: (b, l, 0, 0)),
            pl.BlockSpec((None, None, PAST_LEN, DA_QK_WIDTH), lambda b, qt: (b, l, 0, 0)),
            pl.BlockSpec((None, None, PAST_LEN, DA_V_WIDTH), lambda b, qt: (b, l, 0, 0)),
            pl.BlockSpec((None, NA_HEADS, None, QT, NA_KEYS), bias_map),
            pl.BlockSpec((None, 4, DA_DIM), lambda b, qt: (l, 0, 0)),
            pl.BlockSpec((None, 1, 2 * DA_DIM), lambda b, qt: (l, 0, 0)),
            pl.BlockSpec((None, 3, SC_WIDTH), lambda b, qt: (l, 0, 0)),
            pl.BlockSpec((None, 1, SC_WIDTH), lambda b, qt: (l, 0, 0)),
        ],
        out_specs=pl.BlockSpec((QT, MIX_WIDTH), lambda b, qt: (b * n_qt + qt, 0)),
        out_shape=jax.ShapeDtypeStruct((DEC_BATCH * DEC_SEQ, MIX_WIDTH), BF16),
        scratch_shapes=[
            pltpu.VMEM((DEC_SEQ + PAST_LEN, DA_QK_WIDTH), BF16),
            pltpu.VMEM((DEC_SEQ + PAST_LEN, 2 * DA_V_WIDTH), BF16),
            pltpu.VMEM((PAST_LEN, 2 * NA_WIDTH), BF16),
        ],
        compiler_params=_params("parallel", "arbitrary"),
        name="mix_latent",
    )(u, u, u, u, u, cnak, cnav, cdak, cdav, bias_tab, da_lambda,
      da_norm_g.reshape(DEPTH, 1, 2 * DA_DIM), conv_w, conv_b.reshape(DEPTH, 1, SC_WIDTH))


def _first_max(vals):
    m = functools.reduce(jnp.maximum, vals)
    hot, taken = [], None
    for v in vals:
        is_max = v == m
        if taken is None:
            hot.append(is_max)
            taken = is_max
        else:
            hot.append(is_max & ~taken)
            taken = taken | is_max
    return hot, m


def _pick(hot, vals):
    out = vals[-1]
    for h, v in zip(hot[-2::-1], vals[-2::-1]):
        out = jnp.where(h, v, out)
    return out


def _route(h2, h2_bf, wr_ref, br_ref):
    tm = h2.shape[0]
    w = wr_ref[...]
    w_hi = w.astype(BF16)
    w_lo = (w - w_hi.astype(F32)).astype(BF16)
    h_lo = (h2 - h2_bf.astype(F32)).astype(BF16)
    z = _dot_nt(w_hi, h2_bf) + _dot_nt(w_hi, h_lo) + _dot_nt(w_lo, h2_bf)
    scores = jax.nn.sigmoid(z)
    biased = scores + br_ref[...]
    P = [biased[k * N_GROUPS:(k + 1) * N_GROUPS] for k in range(PER_GROUP)]
    S = [scores[k * N_GROUPS:(k + 1) * N_GROUPS] for k in range(PER_GROUP)]
    pair_sums = [P[i] + P[j] for i in range(PER_GROUP) for j in range(i + 1, PER_GROUP)]
    group_score = functools.reduce(jnp.maximum, pair_sums)
    sel, _ = _first_max([group_score[g:g + 1] for g in range(N_GROUPS)])
    c = [_pick(sel, [P[k][g:g + 1] for g in range(N_GROUPS)]) for k in range(PER_GROUP)]
    cs = [_pick(sel, [S[k][g:g + 1] for g in range(N_GROUPS)]) for k in range(PER_GROUP)]
    t1, _ = _first_max(c)
    t2, _ = _first_max([jnp.where(t, -jnp.inf, v) for t, v in zip(t1, c)])
    w1 = functools.reduce(jnp.add, [jnp.where(t, v, 0.0) for t, v in zip(t1, cs)])
    w2 = functools.reduce(jnp.add, [jnp.where(t, v, 0.0) for t, v in zip(t2, cs)])
    total = w1 + w2
    slot_gate = [jnp.where(a, w1 / total, jnp.where(b, w2 / total, 0.0)) for a, b in zip(t1, t2)]
    onehot = jnp.concatenate([s.astype(F32) for s in sel] + [jnp.zeros((8 - N_GROUPS, tm), F32)], axis=0)
    earlier = (lax.broadcasted_iota(jnp.int32, (tm, tm), 0) < lax.broadcasted_iota(jnp.int32, (tm, tm), 1))
    rank = _dot(onehot.astype(BF16), earlier.astype(BF16))
    count = jnp.sum(onehot, axis=1, keepdims=True)
    n_chunks = jnp.floor((count + (MOE_CHUNK - 1)) * (1.0 / MOE_CHUNK))
    pos = jnp.zeros((1, tm), F32)
    start = jnp.zeros((1, 1), F32)
    for g in range(N_GROUPS):
        pos = jnp.where(sel[g], start + rank[g:g + 1], pos)
        start = start + n_chunks[g:g + 1] * MOE_CHUNK
    rows = [pos] + slot_gate + [jnp.zeros((128 - 1 - PER_GROUP, tm), F32)]
    return jnp.concatenate(rows, axis=0), n_chunks


def _post_kernel(mixp_ref, mixs_ref, xp_ref, xs_ref, mod_ref, wout_ref, g_ref, b_ref, wr_ref, br_ref,
                 x1_ref, h2_ref, tok_ref, pos_ref, nch_ref, wbf_ref):
    i = pl.program_id(0)
    is_ctx = i < CTX_TILES

    @pl.when(i == 0)
    def _():
        wbf_ref[...] = wout_ref[...].astype(BF16)

    mod = mod_ref[...]
    y = _dot(jnp.where(is_ctx, mixp_ref[...], mixs_ref[...]), wbf_ref[...])
    x = jnp.where(is_ctx, xp_ref[...], xs_ref[...])
    x1 = _ln(ALPHA * x + mod[2:3] * y) * g_ref[...] + b_ref[...]
    h2 = _ln(x1) * (1.0 + mod[4:5]) + mod[3:4]
    h2_bf = h2.astype(BF16)
    x1_ref[...] = x1
    h2_ref[...] = h2_bf
    rows, n_chunks = _route(h2, h2_bf, wr_ref, br_ref)
    tok_ref[...] = rows.T
    pos_ref[...] = rows[0:8]
    nch_ref[...] = jnp.broadcast_to(n_chunks, (8, 128)).astype(jnp.int32)


def _post(mix_p, mix_s, xp, xs, mod_l, w_out, ln_g, ln_b, wr_t, br_t, l):
    n = xp.shape[0] + xs.shape[0]
    route = functools.partial(_tile_stream, n_ctx_tiles=CTX_TILES, lat_tiles_per_batch=DEC_SEQ // TM_POST)
    return pl.pallas_call(
        _post_kernel,
        grid=(n // TM_POST,),
        in_specs=[
            pl.BlockSpec((TM_POST, MIX_WIDTH), lambda i: (route(i)[0], 0)),
            pl.BlockSpec((TM_POST, MIX_WIDTH), lambda i: (route(i)[1], 0)),
            pl.BlockSpec((TM_POST, D_MODEL), lambda i: (route(i)[0], 0)),
            pl.BlockSpec((TM_POST, D_MODEL), lambda i: (route(i)[1], 0)),
            pl.BlockSpec((None, 6, D_MODEL), lambda i: (route(i)[2], 0, 0)),
            pl.BlockSpec((None, MIX_WIDTH, D_MODEL), lambda i: (l, 0, 0), pipeline_mode=pl.Buffered(1)),
            pl.BlockSpec((None, 1, D_MODEL), lambda i: (l, 0, 0)),
            pl.BlockSpec((None, 1, D_MODEL), lambda i: (l, 0, 0)),
            pl.BlockSpec((N_EXPERTS, D_MODEL), lambda i: (0, 0)),
            pl.BlockSpec((N_EXPERTS, 1), lambda i: (0, 0)),
        ],
        out_specs=[
            pl.BlockSpec((TM_POST, D_MODEL), lambda i: (i, 0)),
            pl.BlockSpec((TM_POST, D_MODEL), lambda i: (i, 0)),
            pl.BlockSpec((TM_POST, 128), lambda i: (i, 0)),
            pl.BlockSpec((None, 8, TM_POST), lambda i: (i, 0, 0)),
            pl.BlockSpec((None, 8, 128), lambda i: (i, 0, 0)),
        ],
        out_shape=[
            jax.ShapeDtypeStruct((n, D_MODEL), F32),
            jax.ShapeDtypeStruct((n, D_MODEL), BF16),
            jax.ShapeDtypeStruct((n, 128), F32),
            jax.ShapeDtypeStruct((n // TM_POST, 8, TM_POST), F32),
            jax.ShapeDtypeStruct((n // TM_POST, 8, 128), jnp.int32),
        ],
        scratch_shapes=[pltpu.VMEM((MIX_WIDTH, D_MODEL), BF16)],
        compiler_params=_params("arbitrary"),
        name="post",
    )(mix_p, mix_s, xp, xs, mod_l, w_out, ln_g.reshape(DEPTH, 1, D_MODEL), ln_b.reshape(DEPTH, 1, D_MODEL), wr_t, br_t)


def _moe_kernel(nch_ref, h_ref, tok_ref, pos_ref, x1_ref, mod_ref, w1f_ref, w3f_ref, w2f_ref, g_ref, b_ref,
                op_ref, os_ref, w13_ref, w2_ref, hs_ref, gs_ref, ys_ref):
    step = pl.program_id(0)
    half = PER_GROUP * D_EXPERT

    for k in range(PER_GROUP):
        @pl.when((step < N_EXPERTS) & (step % PER_GROUP == k))
        def _(k=k):
            g = step // PER_GROUP
            w13_ref[g, :, k * D_EXPERT:(k + 1) * D_EXPERT] = w1f_ref[...].astype(BF16)
            w13_ref[g, :, half + k * D_EXPERT:half + (k + 1) * D_EXPERT] = w3f_ref[...].astype(BF16)
            w2_ref[g, k * D_EXPERT:(k + 1) * D_EXPERT, :] = w2f_ref[...].astype(BF16)

    @pl.when(step >= N_EXPERTS)
    def _():
        i = step - N_EXPERTS
        out = _moe_tile(i, nch_ref, h_ref, tok_ref, pos_ref, x1_ref, mod_ref, w13_ref, w2_ref, g_ref, b_ref,
                        hs_ref, gs_ref, ys_ref)

        @pl.when(i < CTX_TILES)
        def _():
            op_ref[...] = out

        @pl.when(i >= CTX_TILES)
        def _():
            os_ref[...] = out


def _moe_tile(i, nch_ref, h_ref, tok_ref, pos_ref, x1_ref, mod_ref, w13_ref, w2_ref, g_ref, b_ref,
              hs_ref, gs_ref, ys_ref):
    tok = tok_ref[...]
    pos_lane = pos_ref[0:1, :]
    pos_col = tok[:, 0:1]
    slot = lax.broadcasted_iota(jnp.int32, (MOE_SORTED, TM_MOE), 0).astype(F32)
    sort = (slot == pos_lane).astype(BF16)
    tok_hi = tok.astype(BF16)
    tok_lo = (tok - tok_hi.astype(F32)).astype(BF16)
    z = _dot(sort, jnp.concatenate([h_ref[...], tok_hi, tok_lo], axis=1))
    hs_ref[...] = z[:, :D_MODEL].astype(BF16)
    gs_ref[...] = z[:, D_MODEL:D_MODEL + 128] + z[:, D_MODEL + 128:]
    ys_ref[...] = jnp.zeros_like(ys_ref)

    def experts(g, chunk0, n_rows):
        r0 = pl.multiple_of(chunk0 * MOE_CHUNK, MOE_CHUNK)
        rows = hs_ref[pl.ds(r0, n_rows), :]
        gates = gs_ref[pl.ds(r0, n_rows), :]
        half = PER_GROUP * D_EXPERT
        ab = _dot(rows, w13_ref[g])
        hid = []
        for k in range(PER_GROUP):
            a = ab[:, k * D_EXPERT:(k + 1) * D_EXPERT]
            b = ab[:, half + k * D_EXPERT:half + (k + 1) * D_EXPERT]
            hid.append((a * jax.nn.sigmoid(a) * b * gates[:, 1 + k:2 + k]).astype(BF16))
        ys_ref[pl.ds(r0, n_rows), :] = _dot(jnp.concatenate(hid, axis=1), w2_ref[g]).astype(BF16)

    def group(g, first):
        n = nch_ref[i * N_GROUPS + g]
        n_blocks = n // MOE_BLOCK
        rest = n - n_blocks * MOE_BLOCK

        def block(c, carry):
            experts(g, first + c * MOE_BLOCK, MOE_BLOCK * MOE_CHUNK)
            return carry

        lax.fori_loop(0, n_blocks, block, 0)
        for m in range(1, MOE_BLOCK):
            @pl.when(rest == m)
            def _(m=m):
                experts(g, first + n_blocks * MOE_BLOCK, m * MOE_CHUNK)
        return first + n

    lax.fori_loop(0, N_GROUPS, group, 0)

    unsort = (lax.broadcasted_iota(jnp.int32, (TM_MOE, MOE_SORTED), 1).astype(F32) == pos_col).astype(BF16)
    y = _dot(unsort, ys_ref[...])
    mod = mod_ref[...]
    return _ln(ALPHA * x1_ref[...] + mod[5:6] * y) * g_ref[...] + b_ref[...]


def _moe(h2, tok, pos, nch, x1, mod_l, w1, w3, w2, ln_g, ln_b, l):
    n_tiles = x1.shape[0] // TM_MOE
    n_ctx = CTX_TILES * TM_MOE
    per_batch = DEC_SEQ // TM_MOE

    def tile(step):
        return jnp.maximum(step - N_EXPERTS, 0)

    def expert(step):
        return jnp.minimum(step, N_EXPERTS - 1)

    def mod_map(step, s):
        return (_tile_stream(tile(step), CTX_TILES, per_batch)[2], 0, 0)

    grid_spec = pltpu.PrefetchScalarGridSpec(
        num_scalar_prefetch=1,
        grid=(N_EXPERTS + n_tiles,),
        in_specs=[
            pl.BlockSpec((TM_MOE, D_MODEL), lambda step, s: (tile(step), 0)),
            pl.BlockSpec((TM_MOE, 128), lambda step, s: (tile(step), 0)),
            pl.BlockSpec((None, 8, TM_MOE), lambda step, s: (tile(step), 0, 0)),
            pl.BlockSpec((TM_MOE, D_MODEL), lambda step, s: (tile(step), 0)),
            pl.BlockSpec((None, 6, D_MODEL), mod_map),
            pl.BlockSpec((None, None, D_MODEL, D_EXPERT), lambda step, s: (l, expert(step), 0, 0)),
            pl.BlockSpec((None, None, D_MODEL, D_EXPERT), lambda step, s: (l, expert(step), 0, 0)),
            pl.BlockSpec((None, None, D_EXPERT, D_MODEL), lambda step, s: (l, expert(step), 0, 0)),
            pl.BlockSpec((None, 1, D_MODEL), lambda step, s: (l, 0, 0)),
            pl.BlockSpec((None, 1, D_MODEL), lambda step, s: (l, 0, 0)),
        ],
        out_specs=[
            pl.BlockSpec((TM_MOE, D_MODEL), lambda step, s: (jnp.minimum(tile(step), CTX_TILES - 1), 0)),
            pl.BlockSpec((TM_MOE, D_MODEL), lambda step, s: (jnp.maximum(tile(step) - CTX_TILES, 0), 0)),
        ],
        scratch_shapes=[
            pltpu.VMEM((N_GROUPS, D_MODEL, 2 * PER_GROUP * D_EXPERT), BF16),
            pltpu.VMEM((N_GROUPS, PER_GROUP * D_EXPERT, D_MODEL), BF16),
            pltpu.VMEM((MOE_SORTED, D_MODEL), BF16),
            pltpu.VMEM((MOE_SORTED, 128), F32),
            pltpu.VMEM((MOE_SORTED, D_MODEL), BF16),
        ],
    )
    return pl.pallas_call(
        _moe_kernel,
        grid_spec=grid_spec,
        out_shape=[jax.ShapeDtypeStruct((n_ctx, D_MODEL), F32),
                   jax.ShapeDtypeStruct((x1.shape[0] - n_ctx, D_MODEL), F32)],
        compiler_params=_params("arbitrary"),
        name="moe",
    )(nch[:, :N_GROUPS, 0].reshape(-1), h2, tok, pos, x1, mod_l, w1, w3, w2,
      ln_g.reshape(DEPTH, 1, D_MODEL), ln_b.reshape(DEPTH, 1, D_MODEL))


def _rope_tables():
    t = np.arange(DEC_SEQ)
    half = DA_DIM // 4
    inv_freq = ROPE_BASE ** (-np.arange(half, dtype=np.float32) / half)
    cos, sin = [], []
    for pos in (t // GRID_W, t % GRID_W):
        ang = pos.astype(np.float32)[:, None] * inv_freq[None, :]
        c, s = np.cos(ang), np.sin(ang)
        cos += [c, c]
        sin += [-s, s]
    cos = np.tile(np.concatenate(cos, axis=1), (1, 2 * DA_HEADS))
    sin = np.tile(np.concatenate(sin, axis=1), (1, 2 * DA_HEADS))
    cos = np.concatenate([np.ones((TM, DA_QK_WIDTH), np.float32), cos], axis=0)
    sin = np.concatenate([np.zeros((TM, DA_QK_WIDTH), np.float32), sin], axis=0)
    return jnp.asarray(cos, F32), jnp.asarray(sin, F32)


_NA_TILE_KINDS = ((0, 0), (2 * QT_ROWS, 2 * QT_ROWS - NA_WIN_ROWS // 2), (GRID_ROWS - QT_ROWS, GRID_ROWS - NA_KEY_ROWS))


def _na_bias_kernel(by_ref, o_ref):
    qc = lax.broadcasted_iota(jnp.int32, (GRID_W, GRID_W), 0)
    kc = lax.broadcasted_iota(jnp.int32, (GRID_W, GRID_W), 1)
    win_c0 = jnp.clip(qc - NA_WIN_COLS // 2, 0, GRID_W - NA_WIN_COLS)
    in_cols = (kc >= win_c0) & (kc < win_c0 + NA_WIN_COLS)
    masked = jnp.full((GRID_W, GRID_W), -jnp.inf, F32)
    for kind, (r0, key_row0) in enumerate(_NA_TILE_KINDS):
        for a in range(QT_ROWS):
            qr = r0 + a
            win_r0 = min(max(qr - NA_WIN_ROWS // 2, 0), GRID_ROWS - NA_WIN_ROWS)
            blocks = []
            for b in range(NA_KEY_ROWS):
                kr = key_row0 + b
                if win_r0 <= kr < win_r0 + NA_WIN_ROWS:
                    blocks.append(jnp.where(in_cols, by_ref[kr - qr + NA_WIN_ROWS - 1] * LOG2E, masked))
                else:
                    blocks.append(masked)
            o_ref[kind, a * GRID_W:(a + 1) * GRID_W, :] = jnp.concatenate(blocks, axis=1)


def _na_bias_tables(na_rel_bias):
    n_dr, n_dc = 2 * NA_WIN_ROWS - 1, 2 * NA_WIN_COLS - 1
    cols = np.arange(GRID_W)
    d_col = np.clip(cols[None, :] - cols[:, None], 1 - NA_WIN_COLS, NA_WIN_COLS - 1) + NA_WIN_COLS - 1
    col_sel = (d_col[None] == np.arange(n_dc)[:, None, None]).astype(np.float32)
    by_col = jnp.einsum('lhab,bqc->lhaqc', na_rel_bias.astype(F32), col_sel, precision=lax.Precision.HIGHEST)
    return pl.pallas_call(
        _na_bias_kernel,
        grid=(DEPTH, NA_HEADS),
        in_specs=[pl.BlockSpec((None, None, n_dr, GRID_W, GRID_W), lambda l, h: (l, h, 0, 0, 0))],
        out_specs=pl.BlockSpec((None, None, len(_NA_TILE_KINDS), QT, NA_KEYS), lambda l, h: (l, h, 0, 0, 0)),
        out_shape=jax.ShapeDtypeStruct((DEPTH, NA_HEADS, len(_NA_TILE_KINDS), QT, NA_KEYS), F32),
        compiler_params=_params("parallel", "parallel"),
        name="na_bias_table",
    )(by_col)


def kernel(x_prompt, x_sample, cache_na_k, cache_na_v, cache_da_k, cache_da_v, c, c_ctx, w_mod, b_mod, w_in,
           na_rel_bias, sc_conv_w, sc_conv_b, da_lambda, da_norm_g, w_out, ln1_g, ln1_b, w_router, b_router,
           moe_w1, moe_w3, moe_w2, ln2_g, ln2_b):
    xp = x_prompt.reshape(BATCH * SEQ, D_MODEL)
    xs = x_sample.reshape(DEC_BATCH * DEC_SEQ, D_MODEL)

    cond = jnp.concatenate([c_ctx[None, :], c, jnp.zeros((N_COND - 1 - DEC_BATCH, D_MODEL), F32)], axis=0)
    mod = _modulation(cond, w_mod, b_mod).reshape(DEPTH, N_COND, 6, D_MODEL)

    slot_major = np.arange(N_EXPERTS).reshape(N_GROUPS, PER_GROUP).T.reshape(-1)
    wr_t = w_router.T[slot_major]
    br_t = b_router.astype(F32)[slot_major].reshape(N_EXPERTS, 1)

    caches = (
        cache_na_k.reshape(DEC_BATCH, DEPTH, PAST_LEN, NA_WIDTH).astype(BF16),
        cache_na_v.reshape(DEC_BATCH, DEPTH, PAST_LEN, NA_WIDTH).astype(BF16),
        cache_da_k.reshape(DEC_BATCH, DEPTH, PAST_LEN, DA_QK_WIDTH).astype(BF16),
        cache_da_v.reshape(DEC_BATCH, DEPTH, PAST_LEN, DA_V_WIDTH).astype(BF16),
    )
    rope_tabs = _rope_tables()
    bias_tab = _na_bias_tables(na_rel_bias)

    new_caches = None
    for l in range(DEPTH):
        lam_init = 0.8 - 0.6 * math.exp(-0.3 * l)
        mod_l = mod[l]
        u, *new_caches = _inproj(xp, xs, mod_l, w_in, l, rope_tabs, cache_bufs=new_caches)
        mix_p = _ctx_mix(u, da_lambda, da_norm_g, sc_conv_w, sc_conv_b, l, lam_init)
        mix_s = _lat_mix(u, caches, bias_tab, da_lambda, da_norm_g, sc_conv_w, sc_conv_b, l, lam_init)
        x1, h2, *route = _post(mix_p, mix_s, xp, xs, mod_l, w_out, ln1_g, ln1_b, wr_t, br_t, l)
        xp, xs = _moe(h2, *route, x1, mod_l, moe_w1, moe_w3, moe_w2, ln2_g, ln2_b, l)

    nak, nav, dak, dav = new_caches
    return (xp.reshape(BATCH, SEQ, D_MODEL), xs.reshape(DEC_BATCH, DEC_SEQ, D_MODEL),
            nak.reshape(BATCH, DEPTH, SEQ, NA_HEADS, HEAD_DIM), nav.reshape(BATCH, DEPTH, SEQ, NA_HEADS, HEAD_DIM),
            dak.reshape(BATCH, DEPTH, SEQ, 2 * DA_HEADS, DA_DIM), dav.reshape(BATCH, DEPTH, SEQ, DA_HEADS, 2 * DA_DIM))
```

```python
import functools
import math

import numpy as np
import jax
import jax.numpy as jnp
from jax import lax
from jax.experimental import pallas as pl
from jax.experimental.pallas import tpu as pltpu

F32 = jnp.float32
BF16 = jnp.bfloat16

D_MODEL = 1024
BATCH = 16
SEQ = 256
DEPTH = 4
DEC_BATCH = 2
DEC_SEQ = 2048
PAST_LEN = 512
GRID_W = 64
GRID_ROWS = DEC_SEQ // GRID_W
HEAD_DIM = 64
NA_HEADS = 4
NA_WIDTH = NA_HEADS * HEAD_DIM
NA_WIN_ROWS = 8
NA_WIN_COLS = 16
SC_WIDTH = 256
DA_HEADS = 4
DA_DIM = 64
DA_QK_WIDTH = 2 * DA_HEADS * DA_DIM
DA_V_WIDTH = DA_HEADS * 2 * DA_DIM
MIX_WIDTH = NA_WIDTH + SC_WIDTH + DA_V_WIDTH
IN_WIDTH = 3 * NA_WIDTH + 3 * SC_WIDTH + 2 * DA_QK_WIDTH + DA_V_WIDTH
OFF_NA_Q = 0
OFF_NA_K = NA_WIDTH
OFF_NA_V = 2 * NA_WIDTH
OFF_SC_B = 3 * NA_WIDTH
OFF_SC_C = OFF_SC_B + SC_WIDTH
OFF_SC_X = OFF_SC_C + SC_WIDTH
OFF_DA_Q = OFF_SC_X + SC_WIDTH
OFF_DA_K = OFF_DA_Q + DA_QK_WIDTH
OFF_DA_V = OFF_DA_K + DA_QK_WIDTH
ROPE_BASE = 10000.0
N_EXPERTS = 16
N_GROUPS = 4
PER_GROUP = N_EXPERTS // N_GROUPS
D_EXPERT = 256
ALPHA = (2 * DEPTH) ** 0.25
LN_EPS = 1e-5
LOG2E = math.log2(math.e)
Q_SCALE = HEAD_DIM ** -0.5 * LOG2E

N_COND = 8
TM = 512
SEQ_PER_TILE = TM // SEQ
CTX_TILES = BATCH * SEQ // TM
CTX_SEQ_PER_STEP = 2
TM_MOE = 512
TM_POST = TM_MOE
MOE_CHUNK = 32
MOE_BLOCK = 8
MOE_SORTED = TM_MOE + (N_GROUPS - 1) * MOE_CHUNK
assert TM == TM_MOE
QT = 256
QT_ROWS = QT // GRID_W
NA_KEY_ROWS = 12
NA_KEYS = NA_KEY_ROWS * GRID_W
VMEM_LIMIT = 56 * 1024 * 1024


def _dot(a, b):
    return jnp.dot(a, b, preferred_element_type=F32)


def _dot_nt(a, b):
    return lax.dot_general(a, b, (((1,), (1,)), ((), ())), preferred_element_type=F32)


def _ln(x):
    mu = jnp.mean(x, -1, keepdims=True)
    xc = x - mu
    var = jnp.mean(xc * xc, -1, keepdims=True)
    return xc * lax.rsqrt(var + LN_EPS)


def _params(*sem):
    return pltpu.CompilerParams(dimension_semantics=sem, vmem_limit_bytes=VMEM_LIMIT)


def _mod_kernel(cond_ref, w_ref, b_ref, o_ref):
    c = cond_ref[...]
    s = (c * jax.nn.sigmoid(c)).astype(BF16)
    o_ref[...] = _dot(s, w_ref[...].astype(BF16)) + b_ref[...]


def _modulation(cond, w_mod, b_mod):
    tn = 1024
    return pl.pallas_call(
        _mod_kernel,
        grid=(DEPTH, 6 * D_MODEL // tn),
        in_specs=[
            pl.BlockSpec((N_COND, D_MODEL), lambda l, j: (0, 0)),
            pl.BlockSpec((None, D_MODEL, tn), lambda l, j: (l, 0, j)),
            pl.BlockSpec((None, 1, tn), lambda l, j: (l, 0, j)),
        ],
        out_specs=pl.BlockSpec((None, N_COND, tn), lambda l, j: (l, 0, j)),
        out_shape=jax.ShapeDtypeStruct((DEPTH, N_COND, 6 * D_MODEL), F32),
        compiler_params=_params("parallel", "parallel"),
        name="modulation",
    )(cond, w_mod, b_mod.reshape(DEPTH, 1, 6 * D_MODEL))


def _rope(t, cos, sin):
    lane = lax.broadcasted_iota(jnp.int32, t.shape, 1)
    first = (lane // 16) % 2 == 0
    n = t.shape[1]
    swapped = jnp.where(first, pltpu.roll(t, n - 16, 1), pltpu.roll(t, 16, 1))
    return t * cos + swapped * sin


def _tile_stream(i, n_ctx_tiles, lat_tiles_per_batch):
    lat = jnp.maximum(i - n_ctx_tiles, 0)
    return jnp.minimum(i, n_ctx_tiles - 1), lat, jnp.where(i < n_ctx_tiles, 0, 1 + lat // lat_tiles_per_batch)


def _inproj_kernel(*refs):
    xp_ref, xs_ref, mod_ref, w_ref, cos_ref, sin_ref = refs[:6]
    u_ref, nak_ref, nav_ref, dak_ref, dav_ref, wbf_ref = refs[-6:]
    i = pl.program_id(0)
    is_ctx = i < CTX_TILES

    @pl.when(i == 0)
    def _():
        wbf_ref[...] = w_ref[...].astype(BF16)

    mod = mod_ref[...]
    x = jnp.where(is_ctx, xp_ref[...], xs_ref[...])
    h = _ln(x) * (1.0 + mod[1:2]) + mod[0:1]
    u = _dot(h.astype(BF16), wbf_ref[...])
    cos, sin = cos_ref[...], sin_ref[...]
    da_q = _rope(u[:, OFF_DA_Q:OFF_DA_K], cos, sin)
    da_k = _rope(u[:, OFF_DA_K:OFF_DA_V], cos, sin)
    u_ref[:, :OFF_NA_K] = (u[:, :OFF_NA_K] * Q_SCALE).astype(BF16)
    u_ref[:, OFF_NA_K:OFF_DA_Q] = u[:, OFF_NA_K:OFF_DA_Q].astype(BF16)
    u_ref[:, OFF_DA_Q:OFF_DA_K] = (da_q * Q_SCALE).astype(BF16)
    u_ref[:, OFF_DA_K:OFF_DA_V] = da_k.astype(BF16)
    u_ref[:, OFF_DA_V:] = u[:, OFF_DA_V:].astype(BF16)

    @pl.when(is_ctx)
    def _():
        for s in range(SEQ_PER_TILE):
            rows = slice(s * SEQ, (s + 1) * SEQ)
            nak_ref[s] = u[rows, OFF_NA_K:OFF_NA_V]
            nav_ref[s] = u[rows, OFF_NA_V:OFF_SC_B]
            dak_ref[s] = u[rows, OFF_DA_K:OFF_DA_V]
            for j in range(DA_HEADS):
                dav_ref[s, pl.ds(j, SEQ, stride=DA_HEADS), :] = (
                    u[rows, OFF_DA_V + j * 2 * DA_DIM:OFF_DA_V + (j + 1) * 2 * DA_DIM])


def _inproj(xp, xs, mod_l, w_in, l, rope_tabs, cache_bufs=None):
    route = functools.partial(_tile_stream, n_ctx_tiles=CTX_TILES, lat_tiles_per_batch=DEC_SEQ // TM)
    in_specs = [
        pl.BlockSpec((TM, D_MODEL), lambda i: (route(i)[0], 0)),
        pl.BlockSpec((TM, D_MODEL), lambda i: (route(i)[1], 0)),
        pl.BlockSpec((None, 6, D_MODEL), lambda i: (route(i)[2], 0, 0)),
        pl.BlockSpec((None, D_MODEL, IN_WIDTH), lambda i: (l, 0, 0), pipeline_mode=pl.Buffered(1)),
        pl.BlockSpec((TM, DA_QK_WIDTH), lambda i: (jnp.where(i < CTX_TILES, 0, 1 + route(i)[1] % (DEC_SEQ // TM)), 0)),
        pl.BlockSpec((TM, DA_QK_WIDTH), lambda i: (jnp.where(i < CTX_TILES, 0, 1 + route(i)[1] % (DEC_SEQ // TM)), 0)),
    ]
    args = [xp, xs, mod_l, w_in, *rope_tabs]
    out_specs = [pl.BlockSpec((TM, IN_WIDTH), lambda i: (i, 0))]
    out_shape = [jax.ShapeDtypeStruct((xp.shape[0] + xs.shape[0], IN_WIDTH), BF16)]
    for rows, width in ((SEQ, NA_WIDTH), (SEQ, NA_WIDTH), (SEQ, DA_QK_WIDTH), (SEQ * DA_HEADS, 2 * DA_DIM)):
        out_specs.append(pl.BlockSpec((SEQ_PER_TILE, None, rows, width), lambda i: (route(i)[0], l, 0, 0)))
        out_shape.append(jax.ShapeDtypeStruct((BATCH, DEPTH, rows, width), F32))
    aliases = {}
    if cache_bufs is not None:
        aliases = {len(args) + k: 1 + k for k in range(4)}
        in_specs += [pl.BlockSpec(memory_space=pl.ANY)] * 4
        args += list(cache_bufs)
    return pl.pallas_call(
        _inproj_kernel,
        grid=(CTX_TILES + xs.shape[0] // TM,),
        in_specs=in_specs,
        out_specs=out_specs,
        out_shape=out_shape,
        scratch_shapes=[pltpu.VMEM((D_MODEL, IN_WIDTH), BF16)],
        input_output_aliases=aliases,
        compiler_params=_params("arbitrary"),
        name="inproj",
    )(*args)


def _lambda(lam_ref, lam_init):
    lp = lam_ref[...]
    s1 = jnp.sum(lp[0:1] * lp[1:2], axis=-1, keepdims=True)
    s2 = jnp.sum(lp[2:3] * lp[3:4], axis=-1, keepdims=True)
    return jnp.exp(s1) - jnp.exp(s2) + lam_init


def _with_ones(v):
    return jnp.concatenate([v, jnp.ones((v.shape[0], max(v.shape[1], 64)), v.dtype)], axis=1)


def _softmax_pv(score_parts, v_ones, d):
    m = functools.reduce(jnp.maximum, [jnp.max(s, -1, keepdims=True) for s in score_parts])
    e = [jnp.exp2(s - m).astype(BF16) for s in score_parts]
    o = _dot(e[0] if len(e) == 1 else jnp.concatenate(e, axis=1), v_ones)
    return o[:, :d] / o[:, d:d + 1]


def _softmax_pv_small(s, v):
    e = jnp.exp2(s - jnp.max(s, -1, keepdims=True))
    return _dot(e.astype(BF16), v) / jnp.sum(e, -1, keepdims=True)


def _diff_head_norm(o1, o2, lam, gain, lam_init):
    o = o1 - lam * o2
    o = o * lax.rsqrt(jnp.mean(o * o, -1, keepdims=True) + LN_EPS)
    return o * gain * (1.0 - lam_init)


def _gated_conv(b, v, v_prev_row, v_next_row, w, bias):
    n = v.shape[0]
    row = lax.broadcasted_iota(jnp.int32, v.shape, 0)
    prev = jnp.where(row == 0, v_prev_row, pltpu.roll(v, 1, 0))
    nxt = jnp.where(row == n - 1, v_next_row, pltpu.roll(v, n - 1, 0))
    return b * (prev * w[0:1] + v * w[1:2] + nxt * w[2:3] + bias)


def _ctx_mix_kernel(u_ref, lam_ref, gain_ref, cw_ref, cb_ref, o_ref, *, lam_init):
    lam = _lambda(lam_ref, lam_init)
    gain = gain_ref[...]
    for s in range(CTX_SEQ_PER_STEP):
        rows = slice(s * SEQ, (s + 1) * SEQ)
        outs = []
        for h in range(NA_HEADS):
            q = u_ref[rows, OFF_NA_Q + h * HEAD_DIM:OFF_NA_Q + (h + 1) * HEAD_DIM]
            k = u_ref[rows, OFF_NA_K + h * HEAD_DIM:OFF_NA_K + (h + 1) * HEAD_DIM]
            v = u_ref[rows, OFF_NA_V + h * HEAD_DIM:OFF_NA_V + (h + 1) * HEAD_DIM]
            outs.append(_softmax_pv_small(_dot_nt(q, k), v))
        zero_row = jnp.zeros((1, SC_WIDTH), F32)
        vc = u_ref[rows, OFF_SC_C:OFF_SC_X].astype(F32) * u_ref[rows, OFF_SC_X:OFF_DA_Q].astype(F32)
        outs.append(_gated_conv(u_ref[rows, OFF_SC_B:OFF_SC_C].astype(F32), vc, zero_row, zero_row,
                                cw_ref[...], cb_ref[...]))
        for j in range(DA_HEADS):
            v = u_ref[rows, OFF_DA_V + j * 2 * DA_DIM:OFF_DA_V + (j + 1) * 2 * DA_DIM]
            o12 = []
            for mi in (2 * j, 2 * j + 1):
                q = u_ref[rows, OFF_DA_Q + mi * DA_DIM:OFF_DA_Q + (mi + 1) * DA_DIM]
                k = u_ref[rows, OFF_DA_K + mi * DA_DIM:OFF_DA_K + (mi + 1) * DA_DIM]
                o12.append(_softmax_pv_small(_dot_nt(q, k), v))
            outs.append(_diff_head_norm(o12[0], o12[1], lam, gain, lam_init))
        o_ref[rows, :] = jnp.concatenate(outs, axis=-1).astype(BF16)


def _ctx_mix(u, da_lambda, da_norm_g, conv_w, conv_b, l, lam_init):
    step_rows = CTX_SEQ_PER_STEP * SEQ
    return pl.pallas_call(
        functools.partial(_ctx_mix_kernel, lam_init=lam_init),
        grid=(BATCH // CTX_SEQ_PER_STEP,),
        in_specs=[
            pl.BlockSpec((step_rows, IN_WIDTH), lambda b: (b, 0)),
            pl.BlockSpec((None, 4, DA_DIM), lambda b: (l, 0, 0)),
            pl.BlockSpec((None, 1, 2 * DA_DIM), lambda b: (l, 0, 0)),
            pl.BlockSpec((None, 3, SC_WIDTH), lambda b: (l, 0, 0)),
            pl.BlockSpec((None, 1, SC_WIDTH), lambda b: (l, 0, 0)),
        ],
        out_specs=pl.BlockSpec((step_rows, MIX_WIDTH), lambda b: (b, 0)),
        out_shape=jax.ShapeDtypeStruct((BATCH * SEQ, MIX_WIDTH), BF16),
        compiler_params=_params("parallel"),
        name="mix_context",
    )(u, da_lambda, da_norm_g.reshape(DEPTH, 1, 2 * DA_DIM), conv_w, conv_b.reshape(DEPTH, 1, SC_WIDTH))


def _lat_mix_kernel(uq_ref, nak_ref, nav_ref, sc_ref, dakv_ref, cnak_ref, cnav_ref, cdak_ref, cdav_ref,
                    bias_ref, lam_ref, gain_ref, cw_ref, cb_ref, o_ref, kall_ref, vall_ref, cnav1_ref, *, lam_init):
    qt = pl.program_id(1)
    n_qt = pl.num_programs(1)
    lam = _lambda(lam_ref, lam_init)
    gain = gain_ref[...]

    @pl.when(qt == 0)
    def _():
        kall_ref[:DEC_SEQ, :] = dakv_ref[:, :DA_QK_WIDTH]
        kall_ref[DEC_SEQ:, :] = cdak_ref[...]
        for j in range(DA_HEADS):
            vs = slice(j * 2 * DA_DIM, (j + 1) * 2 * DA_DIM)
            cols = slice(j * 4 * DA_DIM, (j + 1) * 4 * DA_DIM)
            vall_ref[:DEC_SEQ, cols] = _with_ones(dakv_ref[:, DA_QK_WIDTH + j * 2 * DA_DIM:DA_QK_WIDTH + (j + 1) * 2 * DA_DIM])
            vall_ref[DEC_SEQ:, cols] = _with_ones(cdav_ref[:, vs])
        for h in range(NA_HEADS):
            cnav1_ref[:, h * 2 * HEAD_DIM:(h + 1) * 2 * HEAD_DIM] = _with_ones(cnav_ref[:, h * HEAD_DIM:(h + 1) * HEAD_DIM])

    outs = []
    key_row0 = jnp.clip(qt * QT_ROWS - NA_WIN_ROWS // 2, 0, GRID_ROWS - NA_KEY_ROWS)
    k0 = pl.multiple_of(key_row0 * GRID_W, GRID_W)
    for h in range(NA_HEADS):
        hs = slice(h * HEAD_DIM, (h + 1) * HEAD_DIM)
        q = uq_ref[:, OFF_NA_Q + h * HEAD_DIM:OFF_NA_Q + (h + 1) * HEAD_DIM]
        k_loc = nak_ref[pl.ds(k0, NA_KEYS), hs]
        v_ones = jnp.concatenate([_with_ones(nav_ref[pl.ds(k0, NA_KEYS), hs]),
                                  cnav1_ref[:, h * 2 * HEAD_DIM:(h + 1) * 2 * HEAD_DIM]], axis=0)
        s_loc = _dot_nt(q, k_loc) + bias_ref[h]
        s_ctx = _dot_nt(q, cnak_ref[:, hs])
        outs.append(_softmax_pv([s_loc, s_ctx], v_ones, HEAD_DIM))
    t0 = pl.multiple_of(qt * QT, QT)
    halo = 16
    before = sc_ref[pl.ds(pl.multiple_of(jnp.maximum(t0 - halo, 0), halo), halo), :].astype(F32)
    after = sc_ref[pl.ds(pl.multiple_of(jnp.minimum(t0 + QT, DEC_SEQ - halo), halo), halo), :].astype(F32)
    cur = sc_ref[pl.ds(t0, QT), :].astype(F32)
    v_prev = before[halo - 1:halo, SC_WIDTH:2 * SC_WIDTH] * before[halo - 1:halo, 2 * SC_WIDTH:]
    v_next = after[0:1, SC_WIDTH:2 * SC_WIDTH] * after[0:1, 2 * SC_WIDTH:]
    v_prev = jnp.where(qt > 0, v_prev, 0.0)
    v_next = jnp.where(qt < n_qt - 1, v_next, 0.0)
    outs.append(_gated_conv(cur[:, :SC_WIDTH], cur[:, SC_WIDTH:2 * SC_WIDTH] * cur[:, 2 * SC_WIDTH:],
                            v_prev, v_next, cw_ref[...], cb_ref[...]))
    for j in range(DA_HEADS):
        v_ones = vall_ref[:, j * 4 * DA_DIM:(j + 1) * 4 * DA_DIM]
        o12 = []
        for mi in (2 * j, 2 * j + 1):
            q = uq_ref[:, OFF_DA_Q + mi * DA_DIM:OFF_DA_Q + (mi + 1) * DA_DIM]
            s = _dot_nt(q, kall_ref[:, mi * DA_DIM:(mi + 1) * DA_DIM])
            o12.append(_softmax_pv([s], v_ones, 2 * DA_DIM))
        outs.append(_diff_head_norm(o12[0], o12[1], lam, gain, lam_init))
    o_ref[...] = jnp.concatenate(outs, axis=-1).astype(BF16)


def _lat_mix(u, caches, bias_tab, da_lambda, da_norm_g, conv_w, conv_b, l, lam_init):
    cnak, cnav, cdak, cdav = caches
    n_qt = DEC_SEQ // QT
    q0 = BATCH * SEQ // QT
    b0 = BATCH * SEQ // DEC_SEQ

    def bias_map(b, qt):
        return (l, 0, jnp.where(qt == 0, 0, jnp.where(qt == n_qt - 1, 2, 1)), 0, 0)

    return pl.pallas_call(
        functools.partial(_lat_mix_kernel, lam_init=lam_init),
        grid=(DEC_BATCH, n_qt),
        in_specs=[
            pl.BlockSpec((QT, IN_WIDTH), lambda b, qt: (q0 + b * n_qt + qt, 0)),
            pl.BlockSpec((DEC_SEQ, NA_WIDTH), lambda b, qt: (b0 + b, OFF_NA_K // NA_WIDTH)),
            pl.BlockSpec((DEC_SEQ, NA_WIDTH), lambda b, qt: (b0 + b, OFF_NA_V // NA_WIDTH)),
            pl.BlockSpec((DEC_SEQ, 3 * SC_WIDTH), lambda b, qt: (b0 + b, OFF_SC_B // (3 * SC_WIDTH))),
            pl.BlockSpec((DEC_SEQ, DA_QK_WIDTH + DA_V_WIDTH), lambda b, qt: (b0 + b, OFF_DA_K // (DA_QK_WIDTH + DA_V_WIDTH))),
            pl.BlockSpec((None, None, PAST_LEN, NA_WIDTH), lambda b, qt: (b, l, 0, 0)),
            pl.BlockSpec((None, None, PAST_LEN, NA_WIDTH), lambda b, qt: (b, l, 0, 0)),
            pl.BlockSpec((None, None, PAST_LEN, DA_QK_WIDTH), lambda b, qt: (b, l, 0, 0)),
            pl.BlockSpec((None, None, PAST_LEN, DA_V_WIDTH), lambda b, qt: (b, l, 0, 0)),
            pl.BlockSpec((None, NA_HEADS, None, QT, NA_KEYS), bias_map),
            pl.BlockSpec((None, 4, DA_DIM), lambda b, qt: (l, 0, 0)),
            pl.BlockSpec((None, 1, 2 * DA_DIM), lambda b, qt: (l, 0, 0)),
            pl.BlockSpec((None, 3, SC_WIDTH), lambda b, qt: (l, 0, 0)),
            pl.BlockSpec((None, 1, SC_WIDTH), lambda b, qt: (l, 0, 0)),
        ],
        out_specs=pl.BlockSpec((QT, MIX_WIDTH), lambda b, qt: (b * n_qt + qt, 0)),
        out_shape=jax.ShapeDtypeStruct((DEC_BATCH * DEC_SEQ, MIX_WIDTH), BF16),
        scratch_shapes=[
            pltpu.VMEM((DEC_SEQ + PAST_LEN, DA_QK_WIDTH), BF16),
            pltpu.VMEM((DEC_SEQ + PAST_LEN, 2 * DA_V_WIDTH), BF16),
            pltpu.VMEM((PAST_LEN, 2 * NA_WIDTH), BF16),
        ],
        compiler_params=_params("parallel", "arbitrary"),
        name="mix_latent",
    )(u, u, u, u, u, cnak, cnav, cdak, cdav, bias_tab, da_lambda,
      da_norm_g.reshape(DEPTH, 1, 2 * DA_DIM), conv_w, conv_b.reshape(DEPTH, 1, SC_WIDTH))


def _first_max(vals):
    m = functools.reduce(jnp.maximum, vals)
    hot, taken = [], None
    for v in vals:
        is_max = v == m
        if taken is None:
            hot.append(is_max)
            taken = is_max
        else:
            hot.append(is_max & ~taken)
            taken = taken | is_max
    return hot, m


def _pick(hot, vals):
    out = vals[-1]
    for h, v in zip(hot[-2::-1], vals[-2::-1]):
        out = jnp.where(h, v, out)
    return out


def _route(h2, h2_bf, wr_ref, br_ref):
    tm = h2.shape[0]
    w = wr_ref[...]
    w_hi = w.astype(BF16)
    w_lo = (w - w_hi.astype(F32)).astype(BF16)
    h_lo = (h2 - h2_bf.astype(F32)).astype(BF16)
    both = _dot_nt(jnp.concatenate([w_hi, w_lo], axis=0), h2_bf)
    z = both[:N_EXPERTS] + both[N_EXPERTS:] + _dot_nt(w_hi, h_lo)
    scores = jax.nn.sigmoid(z)
    biased = scores + br_ref[...]
    P = [biased[k * N_GROUPS:(k + 1) * N_GROUPS] for k in range(PER_GROUP)]
    S = [scores[k * N_GROUPS:(k + 1) * N_GROUPS] for k in range(PER_GROUP)]
    pair_sums = [P[i] + P[j] for i in range(PER_GROUP) for j in range(i + 1, PER_GROUP)]
    group_score = functools.reduce(jnp.maximum, pair_sums)
    sel, _ = _first_max([group_score[g:g + 1] for g in range(N_GROUPS)])
    c = [_pick(sel, [P[k][g:g + 1] for g in range(N_GROUPS)]) for k in range(PER_GROUP)]
    cs = [_pick(sel, [S[k][g:g + 1] for g in range(N_GROUPS)]) for k in range(PER_GROUP)]
    t1, _ = _first_max(c)
    t2, _ = _first_max([jnp.where(t, -jnp.inf, v) for t, v in zip(t1, c)])
    w1 = functools.reduce(jnp.add, [jnp.where(t, v, 0.0) for t, v in zip(t1, cs)])
    w2 = functools.reduce(jnp.add, [jnp.where(t, v, 0.0) for t, v in zip(t2, cs)])
    total = w1 + w2
    slot_gate = [jnp.where(a, w1 / total, jnp.where(b, w2 / total, 0.0)) for a, b in zip(t1, t2)]
    onehot = jnp.concatenate([s.astype(F32) for s in sel] + [jnp.zeros((8 - N_GROUPS, tm), F32)], axis=0)
    earlier = (lax.broadcasted_iota(jnp.int32, (tm, tm), 0) < lax.broadcasted_iota(jnp.int32, (tm, tm), 1))
    rank = _dot(onehot.astype(BF16), earlier.astype(BF16))
    count = jnp.sum(onehot, axis=1, keepdims=True)
    n_chunks = jnp.floor((count + (MOE_CHUNK - 1)) * (1.0 / MOE_CHUNK))
    pos = jnp.zeros((1, tm), F32)
    start = jnp.zeros((1, 1), F32)
    for g in range(N_GROUPS):
        pos = jnp.where(sel[g], start + rank[g:g + 1], pos)
        start = start + n_chunks[g:g + 1] * MOE_CHUNK
    rows = [pos] + slot_gate + [jnp.zeros((128 - 1 - PER_GROUP, tm), F32)]
    return jnp.concatenate(rows, axis=0), n_chunks


def _post_kernel(mixp_ref, mixs_ref, xp_ref, xs_ref, mod_ref, wout_ref, g_ref, b_ref, wr_ref, br_ref,
                 x1_ref, h2_ref, tok_ref, pos_ref, nch_ref, wbf_ref):
    i = pl.program_id(0)
    is_ctx = i < CTX_TILES

    @pl.when(i == 0)
    def _():
        wbf_ref[...] = wout_ref[...].astype(BF16)

    mod = mod_ref[...]
    y = _dot(jnp.where(is_ctx, mixp_ref[...], mixs_ref[...]), wbf_ref[...])
    x = jnp.where(is_ctx, xp_ref[...], xs_ref[...])
    x1 = _ln(ALPHA * x + mod[2:3] * y) * g_ref[...] + b_ref[...]
    h2 = _ln(x1) * (1.0 + mod[4:5]) + mod[3:4]
    h2_bf = h2.astype(BF16)
    x1_ref[...] = x1
    h2_ref[...] = h2_bf
    rows, n_chunks = _route(h2, h2_bf, wr_ref, br_ref)
    tok_ref[...] = rows.T
    pos_ref[...] = rows[0:8]
    nch_ref[...] = jnp.broadcast_to(n_chunks, (8, 128)).astype(jnp.int32)


def _post(mix_p, mix_s, xp, xs, mod_l, w_out, ln_g, ln_b, wr_t, br_t, l):
    n = xp.shape[0] + xs.shape[0]
    route = functools.partial(_tile_stream, n_ctx_tiles=CTX_TILES, lat_tiles_per_batch=DEC_SEQ // TM_POST)
    return pl.pallas_call(
        _post_kernel,
        grid=(n // TM_POST,),
        in_specs=[
            pl.BlockSpec((TM_POST, MIX_WIDTH), lambda i: (route(i)[0], 0)),
            pl.BlockSpec((TM_POST, MIX_WIDTH), lambda i: (route(i)[1], 0)),
            pl.BlockSpec((TM_POST, D_MODEL), lambda i: (route(i)[0], 0)),
            pl.BlockSpec((TM_POST, D_MODEL), lambda i: (route(i)[1], 0)),
            pl.BlockSpec((None, 6, D_MODEL), lambda i: (route(i)[2], 0, 0)),
            pl.BlockSpec((None, MIX_WIDTH, D_MODEL), lambda i: (l, 0, 0), pipeline_mode=pl.Buffered(1)),
            pl.BlockSpec((None, 1, D_MODEL), lambda i: (l, 0, 0)),
            pl.BlockSpec((None, 1, D_MODEL), lambda i: (l, 0, 0)),
            pl.BlockSpec((N_EXPERTS, D_MODEL), lambda i: (0, 0)),
            pl.BlockSpec((N_EXPERTS, 1), lambda i: (0, 0)),
        ],
        out_specs=[
            pl.BlockSpec((TM_POST, D_MODEL), lambda i: (i, 0)),
            pl.BlockSpec((TM_POST, D_MODEL), lambda i: (i, 0)),
            pl.BlockSpec((TM_POST, 128), lambda i: (i, 0)),
            pl.BlockSpec((None, 8, TM_POST), lambda i: (i, 0, 0)),
            pl.BlockSpec((None, 8, 128), lambda i: (i, 0, 0)),
        ],
        out_shape=[
            jax.ShapeDtypeStruct((n, D_MODEL), F32),
            jax.ShapeDtypeStruct((n, D_MODEL), BF16),
            jax.ShapeDtypeStruct((n, 128), F32),
            jax.ShapeDtypeStruct((n // TM_POST, 8, TM_POST), F32),
            jax.ShapeDtypeStruct((n // TM_POST, 8, 128), jnp.int32),
        ],
        scratch_shapes=[pltpu.VMEM((MIX_WIDTH, D_MODEL), BF16)],
        compiler_params=_params("arbitrary"),
        name="post",
    )(mix_p, mix_s, xp, xs, mod_l, w_out, ln_g.reshape(DEPTH, 1, D_MODEL), ln_b.reshape(DEPTH, 1, D_MODEL), wr_t, br_t)


def _moe_kernel(nch_ref, h_ref, tok_ref, pos_ref, x1_ref, mod_ref, w1f_ref, w3f_ref, w2f_ref, g_ref, b_ref,
                op_ref, os_ref, w13_ref, w2_ref, hs_ref, gs_ref, ys_ref):
    step = pl.program_id(0)
    half = PER_GROUP * D_EXPERT

    for k in range(PER_GROUP):
        @pl.when((step < N_EXPERTS) & (step % PER_GROUP == k))
        def _(k=k):
            g = step // PER_GROUP
            w13_ref[g, :, k * D_EXPERT:(k + 1) * D_EXPERT] = w1f_ref[...].astype(BF16)
            w13_ref[g, :, half + k * D_EXPERT:half + (k + 1) * D_EXPERT] = w3f_ref[...].astype(BF16)
            w2_ref[g, k * D_EXPERT:(k + 1) * D_EXPERT, :] = w2f_ref[...].astype(BF16)

    @pl.when(step >= N_EXPERTS)
    def _():
        i = step - N_EXPERTS
        out = _moe_tile(i, nch_ref, h_ref, tok_ref, pos_ref, x1_ref, mod_ref, w13_ref, w2_ref, g_ref, b_ref,
                        hs_ref, gs_ref, ys_ref)

        @pl.when(i < CTX_TILES)
        def _():
            op_ref[...] = out

        @pl.when(i >= CTX_TILES)
        def _():
            os_ref[...] = out


def _moe_tile(i, nch_ref, h_ref, tok_ref, pos_ref, x1_ref, mod_ref, w13_ref, w2_ref, g_ref, b_ref,
              hs_ref, gs_ref, ys_ref):
    tok = tok_ref[...]
    pos_lane = pos_ref[0:1, :]
    pos_col = tok[:, 0:1]
    slot = lax.broadcasted_iota(jnp.int32, (MOE_SORTED, TM_MOE), 0).astype(F32)
    sort = (slot == pos_lane).astype(BF16)
    tok_hi = tok.astype(BF16)
    tok_lo = (tok - tok_hi.astype(F32)).astype(BF16)
    z = _dot(sort, jnp.concatenate([h_ref[...], tok_hi, tok_lo], axis=1))
    hs_ref[...] = z[:, :D_MODEL].astype(BF16)
    gs_ref[...] = z[:, D_MODEL:D_MODEL + 128] + z[:, D_MODEL + 128:]
    ys_ref[...] = jnp.zeros_like(ys_ref)

    def experts(g, chunk0, n_rows):
        r0 = pl.multiple_of(chunk0 * MOE_CHUNK, MOE_CHUNK)
        rows = hs_ref[pl.ds(r0, n_rows), :]
        gates = gs_ref[pl.ds(r0, n_rows), :]
        half = PER_GROUP * D_EXPERT
        ab = _dot(rows, w13_ref[g])
        hid = []
        for k in range(PER_GROUP):
            a = ab[:, k * D_EXPERT:(k + 1) * D_EXPERT]
            b = ab[:, half + k * D_EXPERT:half + (k + 1) * D_EXPERT]
            hid.append((a * jax.nn.sigmoid(a) * b * gates[:, 1 + k:2 + k]).astype(BF16))
        ys_ref[pl.ds(r0, n_rows), :] = _dot(jnp.concatenate(hid, axis=1), w2_ref[g]).astype(BF16)

    def group(g, first):
        n = nch_ref[i * N_GROUPS + g]
        n_blocks = n // MOE_BLOCK
        rest = n - n_blocks * MOE_BLOCK

        def block(c, carry):
            experts(g, first + c * MOE_BLOCK, MOE_BLOCK * MOE_CHUNK)
            return carry

        lax.fori_loop(0, n_blocks, block, 0)
        for m in range(1, MOE_BLOCK):
            @pl.when(rest == m)
            def _(m=m):
                experts(g, first + n_blocks * MOE_BLOCK, m * MOE_CHUNK)
        return first + n

    lax.fori_loop(0, N_GROUPS, group, 0)

    unsort = (lax.broadcasted_iota(jnp.int32, (TM_MOE, MOE_SORTED), 1).astype(F32) == pos_col).astype(BF16)
    y = _dot(unsort, ys_ref[...])
    mod = mod_ref[...]
    return _ln(ALPHA * x1_ref[...] + mod[5:6] * y) * g_ref[...] + b_ref[...]


def _moe(h2, tok, pos, nch, x1, mod_l, w1, w3, w2, ln_g, ln_b, l):
    n_tiles = x1.shape[0] // TM_MOE
    n_ctx = CTX_TILES * TM_MOE
    per_batch = DEC_SEQ // TM_MOE

    def tile(step):
        return jnp.maximum(step - N_EXPERTS, 0)

    def expert(step):
        return jnp.minimum(step, N_EXPERTS - 1)

    def mod_map(step, s):
        return (_tile_stream(tile(step), CTX_TILES, per_batch)[2], 0, 0)

    grid_spec = pltpu.PrefetchScalarGridSpec(
        num_scalar_prefetch=1,
        grid=(N_EXPERTS + n_tiles,),
        in_specs=[
            pl.BlockSpec((TM_MOE, D_MODEL), lambda step, s: (tile(step), 0)),
            pl.BlockSpec((TM_MOE, 128), lambda step, s: (tile(step), 0)),
            pl.BlockSpec((None, 8, TM_MOE), lambda step, s: (tile(step), 0, 0)),
            pl.BlockSpec((TM_MOE, D_MODEL), lambda step, s: (tile(step), 0)),
            pl.BlockSpec((None, 6, D_MODEL), mod_map),
            pl.BlockSpec((None, None, D_MODEL, D_EXPERT), lambda step, s: (l, expert(step), 0, 0)),
            pl.BlockSpec((None, None, D_MODEL, D_EXPERT), lambda step, s: (l, expert(step), 0, 0)),
            pl.BlockSpec((None, None, D_EXPERT, D_MODEL), lambda step, s: (l, expert(step), 0, 0)),
            pl.BlockSpec((None, 1, D_MODEL), lambda step, s: (l, 0, 0)),
            pl.BlockSpec((None, 1, D_MODEL), lambda step, s: (l, 0, 0)),
        ],
        out_specs=[
            pl.BlockSpec((TM_MOE, D_MODEL), lambda step, s: (jnp.minimum(tile(step), CTX_TILES - 1), 0)),
            pl.BlockSpec((TM_MOE, D_MODEL), lambda step, s: (jnp.maximum(tile(step) - CTX_TILES, 0), 0)),
        ],
        scratch_shapes=[
            pltpu.VMEM((N_GROUPS, D_MODEL, 2 * PER_GROUP * D_EXPERT), BF16),
            pltpu.VMEM((N_GROUPS, PER_GROUP * D_EXPERT, D_MODEL), BF16),
            pltpu.VMEM((MOE_SORTED, D_MODEL), BF16),
            pltpu.VMEM((MOE_SORTED, 128), F32),
            pltpu.VMEM((MOE_SORTED, D_MODEL), BF16),
        ],
    )
    return pl.pallas_call(
        _moe_kernel,
        grid_spec=grid_spec,
        out_shape=[jax.ShapeDtypeStruct((n_ctx, D_MODEL), F32),
                   jax.ShapeDtypeStruct((x1.shape[0] - n_ctx, D_MODEL), F32)],
        compiler_params=_params("arbitrary"),
        name="moe",
    )(nch[:, :N_GROUPS, 0].reshape(-1), h2, tok, pos, x1, mod_l, w1, w3, w2,
      ln_g.reshape(DEPTH, 1, D_MODEL), ln_b.reshape(DEPTH, 1, D_MODEL))


def _rope_tables():
    t = np.arange(DEC_SEQ)
    half = DA_DIM // 4
    inv_freq = ROPE_BASE ** (-np.arange(half, dtype=np.float32) / half)
    cos, sin = [], []
    for pos in (t // GRID_W, t % GRID_W):
        ang = pos.astype(np.float32)[:, None] * inv_freq[None, :]
        c, s = np.cos(ang), np.sin(ang)
        cos += [c, c]
        sin += [-s, s]
    cos = np.tile(np.concatenate(cos, axis=1), (1, 2 * DA_HEADS))
    sin = np.tile(np.concatenate(sin, axis=1), (1, 2 * DA_HEADS))
    cos = np.concatenate([np.ones((TM, DA_QK_WIDTH), np.float32), cos], axis=0)
    sin = np.concatenate([np.zeros((TM, DA_QK_WIDTH), np.float32), sin], axis=0)
    return jnp.asarray(cos, F32), jnp.asarray(sin, F32)


_NA_TILE_KINDS = ((0, 0), (2 * QT_ROWS, 2 * QT_ROWS - NA_WIN_ROWS // 2), (GRID_ROWS - QT_ROWS, GRID_ROWS - NA_KEY_ROWS))


def _na_bias_kernel(by_ref, o_ref):
    qc = lax.broadcasted_iota(jnp.int32, (GRID_W, GRID_W), 0)
    kc = lax.broadcasted_iota(jnp.int32, (GRID_W, GRID_W), 1)
    win_c0 = jnp.clip(qc - NA_WIN_COLS // 2, 0, GRID_W - NA_WIN_COLS)
    in_cols = (kc >= win_c0) & (kc < win_c0 + NA_WIN_COLS)
    masked = jnp.full((GRID_W, GRID_W), -jnp.inf, F32)
    for kind, (r0, key_row0) in enumerate(_NA_TILE_KINDS):
        for a in range(QT_ROWS):
            qr = r0 + a
            win_r0 = min(max(qr - NA_WIN_ROWS // 2, 0), GRID_ROWS - NA_WIN_ROWS)
            blocks = []
            for b in range(NA_KEY_ROWS):
                kr = key_row0 + b
                if win_r0 <= kr < win_r0 + NA_WIN_ROWS:
                    blocks.append(jnp.where(in_cols, by_ref[kr - qr + NA_WIN_ROWS - 1] * LOG2E, masked))
                else:
                    blocks.append(masked)
            o_ref[kind, a * GRID_W:(a + 1) * GRID_W, :] = jnp.concatenate(blocks, axis=1)


def _na_bias_tables(na_rel_bias):
    n_dr, n_dc = 2 * NA_WIN_ROWS - 1, 2 * NA_WIN_COLS - 1
    cols = np.arange(GRID_W)
    d_col = np.clip(cols[None, :] - cols[:, None], 1 - NA_WIN_COLS, NA_WIN_COLS - 1) + NA_WIN_COLS - 1
    col_sel = (d_col[None] == np.arange(n_dc)[:, None, None]).astype(np.float32)
    by_col = jnp.einsum('lhab,bqc->lhaqc', na_rel_bias.astype(F32), col_sel, precision=lax.Precision.HIGHEST)
    return pl.pallas_call(
        _na_bias_kernel,
        grid=(DEPTH, NA_HEADS),
        in_specs=[pl.BlockSpec((None, None, n_dr, GRID_W, GRID_W), lambda l, h: (l, h, 0, 0, 0))],
        out_specs=pl.BlockSpec((None, None, len(_NA_TILE_KINDS), QT, NA_KEYS), lambda l, h: (l, h, 0, 0, 0)),
        out_shape=jax.ShapeDtypeStruct((DEPTH, NA_HEADS, len(_NA_TILE_KINDS), QT, NA_KEYS), F32),
        compiler_params=_params("parallel", "parallel"),
        name="na_bias_table",
    )(by_col)


def kernel(x_prompt, x_sample, cache_na_k, cache_na_v, cache_da_k, cache_da_v, c, c_ctx, w_mod, b_mod, w_in,
           na_rel_bias, sc_conv_w, sc_conv_b, da_lambda, da_norm_g, w_out, ln1_g, ln1_b, w_router, b_router,
           moe_w1, moe_w3, moe_w2, ln2_g, ln2_b):
    xp = x_prompt.reshape(BATCH * SEQ, D_MODEL)
    xs = x_sample.reshape(DEC_BATCH * DEC_SEQ, D_MODEL)

    cond = jnp.concatenate([c_ctx[None, :], c, jnp.zeros((N_COND - 1 - DEC_BATCH, D_MODEL), F32)], axis=0)
    mod = _modulation(cond, w_mod, b_mod).reshape(DEPTH, N_COND, 6, D_MODEL)

    slot_major = np.arange(N_EXPERTS).reshape(N_GROUPS, PER_GROUP).T.reshape(-1)
    wr_t = w_router.T[slot_major]
    br_t = b_router.astype(F32)[slot_major].reshape(N_EXPERTS, 1)

    caches = (
        cache_na_k.reshape(DEC_BATCH, DEPTH, PAST_LEN, NA_WIDTH).astype(BF16),
        cache_na_v.reshape(DEC_BATCH, DEPTH, PAST_LEN, NA_WIDTH).astype(BF16),
        cache_da_k.reshape(DEC_BATCH, DEPTH, PAST_LEN, DA_QK_WIDTH).astype(BF16),
        cache_da_v.reshape(DEC_BATCH, DEPTH, PAST_LEN, DA_V_WIDTH).astype(BF16),
    )
    rope_tabs = _rope_tables()
    bias_tab = _na_bias_tables(na_rel_bias)

    new_caches = None
    for l in range(DEPTH):
        lam_init = 0.8 - 0.6 * math.exp(-0.3 * l)
        mod_l = mod[l]
        u, *new_caches = _inproj(xp, xs, mod_l, w_in, l, rope_tabs, cache_bufs=new_caches)
        mix_p = _ctx_mix(u, da_lambda, da_norm_g, sc_conv_w, sc_conv_b, l, lam_init)
        mix_s = _lat_mix(u, caches, bias_tab, da_lambda, da_norm_g, sc_conv_w, sc_conv_b, l, lam_init)
        x1, h2, *route = _post(mix_p, mix_s, xp, xs, mod_l, w_out, ln1_g, ln1_b, wr_t, br_t, l)
        xp, xs = _moe(h2, *route, x1, mod_l, moe_w1, moe_w3, moe_w2, ln2_g, ln2_b, l)

    nak, nav, dak, dav = new_caches
    return (xp.reshape(BATCH, SEQ, D_MODEL), xs.reshape(DEC_BATCH, DEC_SEQ, D_MODEL),
            nak.reshape(BATCH, DEPTH, SEQ, NA_HEADS, HEAD_DIM), nav.reshape(BATCH, DEPTH, SEQ, NA_HEADS, HEAD_DIM),
            dak.reshape(BATCH, DEPTH, SEQ, 2 * DA_HEADS, DA_DIM), dav.reshape(BATCH, DEPTH, SEQ, DA_HEADS, 2 * DA_DIM))
```

```python
import functools
import math

import numpy as np
import jax
import jax.numpy as jnp
from jax import lax
from jax.experimental import pallas as pl
from jax.experimental.pallas import tpu as pltpu

F32 = jnp.float32
BF16 = jnp.bfloat16

D_MODEL = 1024
BATCH = 16
SEQ = 256
DEPTH = 4
DEC_BATCH = 2
DEC_SEQ = 2048
PAST_LEN = 512
GRID_W = 64
GRID_ROWS = DEC_SEQ // GRID_W
HEAD_DIM = 64
NA_HEADS = 4
NA_WIDTH = NA_HEADS * HEAD_DIM
NA_WIN_ROWS = 8
NA_WIN_COLS = 16
SC_WIDTH = 256
DA_HEADS = 4
DA_DIM = 64
DA_QK_WIDTH = 2 * DA_HEADS * DA_DIM
DA_V_WIDTH = DA_HEADS * 2 * DA_DIM
MIX_WIDTH = NA_WIDTH + SC_WIDTH + DA_V_WIDTH
IN_WIDTH = 3 * NA_WIDTH + 3 * SC_WIDTH + 2 * DA_QK_WIDTH + DA_V_WIDTH
OFF_NA_Q = 0
OFF_NA_K = NA_WIDTH
OFF_NA_V = 2 * NA_WIDTH
OFF_SC_B = 3 * NA_WIDTH
OFF_SC_C = OFF_SC_B + SC_WIDTH
OFF_SC_X = OFF_SC_C + SC_WIDTH
OFF_DA_Q = OFF_SC_X + SC_WIDTH
OFF_DA_K = OFF_DA_Q + DA_QK_WIDTH
OFF_DA_V = OFF_DA_K + DA_QK_WIDTH
ROPE_BASE = 10000.0
N_EXPERTS = 16
N_GROUPS = 4
PER_GROUP = N_EXPERTS // N_GROUPS
D_EXPERT = 256
ALPHA = (2 * DEPTH) ** 0.25
LN_EPS = 1e-5
LOG2E = math.log2(math.e)
Q_SCALE = HEAD_DIM ** -0.5 * LOG2E

N_COND = 8
TM = 512
SEQ_PER_TILE = TM // SEQ
CTX_TILES = BATCH * SEQ // TM
CTX_SEQ_PER_STEP = 2
POST_SPLIT = 2
TM_MOE = 512
TM_POST = TM_MOE
MOE_CHUNK = 32
MOE_BLOCK = 8
MOE_SORTED = TM_MOE + (N_GROUPS - 1) * MOE_CHUNK
assert TM == TM_MOE
QT = 256
QT_ROWS = QT // GRID_W
NA_KEY_ROWS = 12
NA_KEYS = NA_KEY_ROWS * GRID_W
VMEM_LIMIT = 56 * 1024 * 1024


def _dot(a, b):
    return jnp.dot(a, b, preferred_element_type=F32)


def _dot_nt(a, b):
    return lax.dot_general(a, b, (((1,), (1,)), ((), ())), preferred_element_type=F32)


def _ln(x):
    mu = jnp.mean(x, -1, keepdims=True)
    xc = x - mu
    var = jnp.mean(xc * xc, -1, keepdims=True)
    return xc * lax.rsqrt(var + LN_EPS)


def _ln_each(xs):
    mus = [jnp.mean(x, -1, keepdims=True) for x in xs]
    xcs = [x - mu for x, mu in zip(xs, mus)]
    vs = [jnp.mean(xc * xc, -1, keepdims=True) for xc in xcs]
    return [xc * lax.rsqrt(v + LN_EPS) for xc, v in zip(xcs, vs)]


def _params(*sem):
    return pltpu.CompilerParams(dimension_semantics=sem, vmem_limit_bytes=VMEM_LIMIT)


def _mod_kernel(cond_ref, w_ref, b_ref, o_ref):
    c = cond_ref[...]
    s = (c * jax.nn.sigmoid(c)).astype(BF16)
    o_ref[...] = _dot(s, w_ref[...].astype(BF16)) + b_ref[...]


def _modulation(cond, w_mod, b_mod):
    tn = 1024
    return pl.pallas_call(
        _mod_kernel,
        grid=(DEPTH, 6 * D_MODEL // tn),
        in_specs=[
            pl.BlockSpec((N_COND, D_MODEL), lambda l, j: (0, 0)),
            pl.BlockSpec((None, D_MODEL, tn), lambda l, j: (l, 0, j)),
            pl.BlockSpec((None, 1, tn), lambda l, j: (l, 0, j)),
        ],
        out_specs=pl.BlockSpec((None, N_COND, tn), lambda l, j: (l, 0, j)),
        out_shape=jax.ShapeDtypeStruct((DEPTH, N_COND, 6 * D_MODEL), F32),
        compiler_params=_params("parallel", "parallel"),
        name="modulation",
    )(cond, w_mod, b_mod.reshape(DEPTH, 1, 6 * D_MODEL))


def _rope(t, cos, sin):
    lane = lax.broadcasted_iota(jnp.int32, t.shape, 1)
    first = (lane // 16) % 2 == 0
    n = t.shape[1]
    swapped = jnp.where(first, pltpu.roll(t, n - 16, 1), pltpu.roll(t, 16, 1))
    return t * cos + swapped * sin


def _tile_stream(i, n_ctx_tiles, lat_tiles_per_batch):
    lat = jnp.maximum(i - n_ctx_tiles, 0)
    return jnp.minimum(i, n_ctx_tiles - 1), lat, jnp.where(i < n_ctx_tiles, 0, 1 + lat // lat_tiles_per_batch)


def _inproj_kernel(*refs):
    xp_ref, xs_ref, mod_ref, w_ref, cos_ref, sin_ref = refs[:6]
    u_ref, nak_ref, nav_ref, dak_ref, dav_ref, wbf_ref = refs[-6:]
    i = pl.program_id(0)
    is_ctx = i < CTX_TILES

    @pl.when(i == 0)
    def _():
        wbf_ref[...] = w_ref[...].astype(BF16)

    mod = mod_ref[...]
    x = jnp.where(is_ctx, xp_ref[...], xs_ref[...])
    h = _ln(x) * (1.0 + mod[1:2]) + mod[0:1]
    u = _dot(h.astype(BF16), wbf_ref[...])
    cos, sin = cos_ref[...], sin_ref[...]
    da_q = _rope(u[:, OFF_DA_Q:OFF_DA_K], cos, sin)
    da_k = _rope(u[:, OFF_DA_K:OFF_DA_V], cos, sin)
    u_ref[:, :OFF_NA_K] = (u[:, :OFF_NA_K] * Q_SCALE).astype(BF16)
    u_ref[:, OFF_NA_K:OFF_DA_Q] = u[:, OFF_NA_K:OFF_DA_Q].astype(BF16)
    u_ref[:, OFF_DA_Q:OFF_DA_K] = (da_q * Q_SCALE).astype(BF16)
    u_ref[:, OFF_DA_K:OFF_DA_V] = da_k.astype(BF16)
    u_ref[:, OFF_DA_V:] = u[:, OFF_DA_V:].astype(BF16)

    @pl.when(is_ctx)
    def _():
        for s in range(SEQ_PER_TILE):
            rows = slice(s * SEQ, (s + 1) * SEQ)
            nak_ref[s] = u[rows, OFF_NA_K:OFF_NA_V]
            nav_ref[s] = u[rows, OFF_NA_V:OFF_SC_B]
            dak_ref[s] = u[rows, OFF_DA_K:OFF_DA_V]
            for j in range(DA_HEADS):
                dav_ref[s, pl.ds(j, SEQ, stride=DA_HEADS), :] = (
                    u[rows, OFF_DA_V + j * 2 * DA_DIM:OFF_DA_V + (j + 1) * 2 * DA_DIM])


def _inproj(xp, xs, mod_l, w_in, l, rope_tabs, cache_bufs=None):
    route = functools.partial(_tile_stream, n_ctx_tiles=CTX_TILES, lat_tiles_per_batch=DEC_SEQ // TM)
    in_specs = [
        pl.BlockSpec((TM, D_MODEL), lambda i: (route(i)[0], 0)),
        pl.BlockSpec((TM, D_MODEL), lambda i: (route(i)[1], 0)),
        pl.BlockSpec((None, 6, D_MODEL), lambda i: (route(i)[2], 0, 0)),
        pl.BlockSpec((None, D_MODEL, IN_WIDTH), lambda i: (l, 0, 0), pipeline_mode=pl.Buffered(1)),
        pl.BlockSpec((TM, DA_QK_WIDTH), lambda i: (jnp.where(i < CTX_TILES, 0, 1 + route(i)[1] % (DEC_SEQ // TM)), 0)),
        pl.BlockSpec((TM, DA_QK_WIDTH), lambda i: (jnp.where(i < CTX_TILES, 0, 1 + route(i)[1] % (DEC_SEQ // TM)), 0)),
    ]
    args = [xp, xs, mod_l, w_in, *rope_tabs]
    out_specs = [pl.BlockSpec((TM, IN_WIDTH), lambda i: (i, 0))]
    out_shape = [jax.ShapeDtypeStruct((xp.shape[0] + xs.shape[0], IN_WIDTH), BF16)]
    for rows, width in ((SEQ, NA_WIDTH), (SEQ, NA_WIDTH), (SEQ, DA_QK_WIDTH), (SEQ * DA_HEADS, 2 * DA_DIM)):
        out_specs.append(pl.BlockSpec((SEQ_PER_TILE, None, rows, width), lambda i: (route(i)[0], l, 0, 0)))
        out_shape.append(jax.ShapeDtypeStruct((BATCH, DEPTH, rows, width), F32))
    aliases = {}
    if cache_bufs is not None:
        aliases = {len(args) + k: 1 + k for k in range(4)}
        in_specs += [pl.BlockSpec(memory_space=pl.ANY)] * 4
        args += list(cache_bufs)
    return pl.pallas_call(
        _inproj_kernel,
        grid=(CTX_TILES + xs.shape[0] // TM,),
        in_specs=in_specs,
        out_specs=out_specs,
        out_shape=out_shape,
        scratch_shapes=[pltpu.VMEM((D_MODEL, IN_WIDTH), BF16)],
        input_output_aliases=aliases,
        compiler_params=_params("arbitrary"),
        name="inproj",
    )(*args)


def _lambda(lam_ref, lam_init):
    lp = lam_ref[...]
    s1 = jnp.sum(lp[0:1] * lp[1:2], axis=-1, keepdims=True)
    s2 = jnp.sum(lp[2:3] * lp[3:4], axis=-1, keepdims=True)
    return jnp.exp(s1) - jnp.exp(s2) + lam_init


def _with_ones(v):
    return jnp.concatenate([v, jnp.ones((v.shape[0], max(v.shape[1], 64)), v.dtype)], axis=1)


def _softmax_pv(score_parts, v_ones, d):
    m = functools.reduce(jnp.maximum, [jnp.max(s, -1, keepdims=True) for s in score_parts])
    e = [jnp.exp2(s - m).astype(BF16) for s in score_parts]
    o = _dot(e[0] if len(e) == 1 else jnp.concatenate(e, axis=1), v_ones)
    return o[:, :d] / o[:, d:d + 1]


def _softmax_pv_each(jobs):
    ms = [functools.reduce(jnp.maximum, [jnp.max(s, -1, keepdims=True) for s in parts]) for parts, _, _ in jobs]
    es = [[jnp.exp2(s - m).astype(BF16) for s in parts] for (parts, _, _), m in zip(jobs, ms)]
    os = [_dot(e[0] if len(e) == 1 else jnp.concatenate(e, axis=1), v) for e, (_, v, _) in zip(es, jobs)]
    return [o[:, :d] / o[:, d:d + 1] for o, (_, _, d) in zip(os, jobs)]


def _diff_head_norm(o1, o2, lam, gain, lam_init):
    o = o1 - lam * o2
    o = o * lax.rsqrt(jnp.mean(o * o, -1, keepdims=True) + LN_EPS)
    return o * gain * (1.0 - lam_init)


def _gated_conv(b, v, v_prev_row, v_next_row, w, bias):
    n = v.shape[0]
    row = lax.broadcasted_iota(jnp.int32, v.shape, 0)
    prev = jnp.where(row == 0, v_prev_row, pltpu.roll(v, 1, 0))
    nxt = jnp.where(row == n - 1, v_next_row, pltpu.roll(v, n - 1, 0))
    return b * (prev * w[0:1] + v * w[1:2] + nxt * w[2:3] + bias)


def _ctx_mix_kernel(u_ref, lam_ref, gain_ref, cw_ref, cb_ref, o_ref, *, lam_init):
    lam = _lambda(lam_ref, lam_init)
    gain = gain_ref[...]
    maps = []
    for s in range(CTX_SEQ_PER_STEP):
        for h in range(NA_HEADS):
            maps.append((s, OFF_NA_Q + h * HEAD_DIM, OFF_NA_K + h * HEAD_DIM, slice(OFF_NA_V + h * HEAD_DIM,
                                                                                    OFF_NA_V + (h + 1) * HEAD_DIM)))
        for mi in range(2 * DA_HEADS):
            j = mi // 2
            maps.append((s, OFF_DA_Q + mi * DA_DIM, OFF_DA_K + mi * DA_DIM,
                         slice(OFF_DA_V + j * 2 * DA_DIM, OFF_DA_V + (j + 1) * 2 * DA_DIM)))
    rows_of = lambda s: slice(s * SEQ, (s + 1) * SEQ)
    scores = [_dot_nt(u_ref[rows_of(s), q0:q0 + HEAD_DIM], u_ref[rows_of(s), k0:k0 + HEAD_DIM])
              for s, q0, k0, _ in maps]
    exps = [jnp.exp2(sc - jnp.max(sc, -1, keepdims=True)) for sc in scores]
    outs = [_dot(e.astype(BF16), u_ref[rows_of(s), vs]) / jnp.sum(e, -1, keepdims=True)
            for e, (s, _, _, vs) in zip(exps, maps)]
    per_seq = NA_HEADS + 2 * DA_HEADS
    zero_row = jnp.zeros((1, SC_WIDTH), F32)
    for s in range(CTX_SEQ_PER_STEP):
        rows = rows_of(s)
        o = outs[s * per_seq:(s + 1) * per_seq]
        vc = u_ref[rows, OFF_SC_C:OFF_SC_X].astype(F32) * u_ref[rows, OFF_SC_X:OFF_DA_Q].astype(F32)
        conv = _gated_conv(u_ref[rows, OFF_SC_B:OFF_SC_C].astype(F32), vc, zero_row, zero_row, cw_ref[...], cb_ref[...])
        da = [_diff_head_norm(o[NA_HEADS + 2 * j], o[NA_HEADS + 2 * j + 1], lam, gain, lam_init)
              for j in range(DA_HEADS)]
        o_ref[rows, :] = jnp.concatenate(o[:NA_HEADS] + [conv] + da, axis=-1).astype(BF16)


def _ctx_mix(u, da_lambda, da_norm_g, conv_w, conv_b, l, lam_init):
    step_rows = CTX_SEQ_PER_STEP * SEQ
    return pl.pallas_call(
        functools.partial(_ctx_mix_kernel, lam_init=lam_init),
        grid=(BATCH // CTX_SEQ_PER_STEP,),
        in_specs=[
            pl.BlockSpec((step_rows, IN_WIDTH), lambda b: (b, 0)),
            pl.BlockSpec((None, 4, DA_DIM), lambda b: (l, 0, 0)),
            pl.BlockSpec((None, 1, 2 * DA_DIM), lambda b: (l, 0, 0)),
            pl.BlockSpec((None, 3, SC_WIDTH), lambda b: (l, 0, 0)),
            pl.BlockSpec((None, 1, SC_WIDTH), lambda b: (l, 0, 0)),
        ],
        out_specs=pl.BlockSpec((step_rows, MIX_WIDTH), lambda b: (b, 0)),
        out_shape=jax.ShapeDtypeStruct((BATCH * SEQ, MIX_WIDTH), BF16),
        compiler_params=_params("parallel"),
        name="mix_context",
    )(u, da_lambda, da_norm_g.reshape(DEPTH, 1, 2 * DA_DIM), conv_w, conv_b.reshape(DEPTH, 1, SC_WIDTH))


def _lat_mix_kernel(uq_ref, nak_ref, nav_ref, sc_ref, dakv_ref, cnak_ref, cnav_ref, cdak_ref, cdav_ref,
                    bias_ref, lam_ref, gain_ref, cw_ref, cb_ref, o_ref, kall_ref, vall_ref, cnav1_ref, *, lam_init):
    qt = pl.program_id(1)
    n_qt = pl.num_programs(1)
    lam = _lambda(lam_ref, lam_init)
    gain = gain_ref[...]

    @pl.when(qt == 0)
    def _():
        kall_ref[:DEC_SEQ, :] = dakv_ref[:, :DA_QK_WIDTH]
        kall_ref[DEC_SEQ:, :] = cdak_ref[...]
        for j in range(DA_HEADS):
            vs = slice(j * 2 * DA_DIM, (j + 1) * 2 * DA_DIM)
            cols = slice(j * 4 * DA_DIM, (j + 1) * 4 * DA_DIM)
            vall_ref[:DEC_SEQ, cols] = _with_ones(dakv_ref[:, DA_QK_WIDTH + j * 2 * DA_DIM:DA_QK_WIDTH + (j + 1) * 2 * DA_DIM])
            vall_ref[DEC_SEQ:, cols] = _with_ones(cdav_ref[:, vs])
        for h in range(NA_HEADS):
            cnav1_ref[:, h * 2 * HEAD_DIM:(h + 1) * 2 * HEAD_DIM] = _with_ones(cnav_ref[:, h * HEAD_DIM:(h + 1) * HEAD_DIM])

    outs = []
    key_row0 = jnp.clip(qt * QT_ROWS - NA_WIN_ROWS // 2, 0, GRID_ROWS - NA_KEY_ROWS)
    k0 = pl.multiple_of(key_row0 * GRID_W, GRID_W)
    jobs = []
    for h in range(NA_HEADS):
        hs = slice(h * HEAD_DIM, (h + 1) * HEAD_DIM)
        q = uq_ref[:, OFF_NA_Q + h * HEAD_DIM:OFF_NA_Q + (h + 1) * HEAD_DIM]
        k_loc = nak_ref[pl.ds(k0, NA_KEYS), hs]
        v_ones = jnp.concatenate([_with_ones(nav_ref[pl.ds(k0, NA_KEYS), hs]),
                                  cnav1_ref[:, h * 2 * HEAD_DIM:(h + 1) * 2 * HEAD_DIM]], axis=0)
        s_loc = _dot_nt(q, k_loc) + bias_ref[h]
        s_ctx = _dot_nt(q, cnak_ref[:, hs])
        jobs.append(([s_loc, s_ctx], v_ones, HEAD_DIM))
    outs += _softmax_pv_each(jobs)
    t0 = pl.multiple_of(qt * QT, QT)
    halo = 16
    before = sc_ref[pl.ds(pl.multiple_of(jnp.maximum(t0 - halo, 0), halo), halo), :].astype(F32)
    after = sc_ref[pl.ds(pl.multiple_of(jnp.minimum(t0 + QT, DEC_SEQ - halo), halo), halo), :].astype(F32)
    cur = sc_ref[pl.ds(t0, QT), :].astype(F32)
    v_prev = before[halo - 1:halo, SC_WIDTH:2 * SC_WIDTH] * before[halo - 1:halo, 2 * SC_WIDTH:]
    v_next = after[0:1, SC_WIDTH:2 * SC_WIDTH] * after[0:1, 2 * SC_WIDTH:]
    v_prev = jnp.where(qt > 0, v_prev, 0.0)
    v_next = jnp.where(qt < n_qt - 1, v_next, 0.0)
    outs.append(_gated_conv(cur[:, :SC_WIDTH], cur[:, SC_WIDTH:2 * SC_WIDTH] * cur[:, 2 * SC_WIDTH:],
                            v_prev, v_next, cw_ref[...], cb_ref[...]))
    for j in range(DA_HEADS):
        v_ones = vall_ref[:, j * 4 * DA_DIM:(j + 1) * 4 * DA_DIM]
        jobs = []
        for mi in (2 * j, 2 * j + 1):
            q = uq_ref[:, OFF_DA_Q + mi * DA_DIM:OFF_DA_Q + (mi + 1) * DA_DIM]
            jobs.append(([_dot_nt(q, kall_ref[:, mi * DA_DIM:(mi + 1) * DA_DIM])], v_ones, 2 * DA_DIM))
        o1, o2 = _softmax_pv_each(jobs)
        outs.append(_diff_head_norm(o1, o2, lam, gain, lam_init))
    o_ref[...] = jnp.concatenate(outs, axis=-1).astype(BF16)


def _lat_mix(u, caches, bias_tab, da_lambda, da_norm_g, conv_w, conv_b, l, lam_init):
    cnak, cnav, cdak, cdav = caches
    n_qt = DEC_SEQ // QT
    q0 = BATCH * SEQ // QT
    b0 = BATCH * SEQ // DEC_SEQ

    def bias_map(b, qt):
        return (l, 0, jnp.where(qt == 0, 0, jnp.where(qt == n_qt - 1, 2, 1)), 0, 0)

    return pl.pallas_call(
        functools.partial(_lat_mix_kernel, lam_init=lam_init),
        grid=(DEC_BATCH, n_qt),
        in_specs=[
            pl.BlockSpec((QT, IN_WIDTH), lambda b, qt: (q0 + b * n_qt + qt, 0)),
            pl.BlockSpec((DEC_SEQ, NA_WIDTH), lambda b, qt: (b0 + b, OFF_NA_K // NA_WIDTH)),
            pl.BlockSpec((DEC_SEQ, NA_WIDTH), lambda b, qt: (b0 + b, OFF_NA_V // NA_WIDTH)),
            pl.BlockSpec((DEC_SEQ, 3 * SC_WIDTH), lambda b, qt: (b0 + b, OFF_SC_B // (3 * SC_WIDTH))),
            pl.BlockSpec((DEC_SEQ, DA_QK_WIDTH + DA_V_WIDTH), lambda b, qt: (b0 + b, OFF_DA_K // (DA_QK_WIDTH + DA_V_WIDTH))),
            pl.BlockSpec((None, None, PAST_LEN, NA_WIDTH), lambda b, qt: (b, l, 0, 0)),
            pl.BlockSpec((None, None, PAST_LEN, NA_WIDTH), lambda b, qt: (b, l, 0, 0)),
            pl.BlockSpec((None, None, PAST_LEN, DA_QK_WIDTH), lambda b, qt: (b, l, 0, 0)),
            pl.BlockSpec((None, None, PAST_LEN, DA_V_WIDTH), lambda b, qt: (b, l, 0, 0)),
            pl.BlockSpec((None, NA_HEADS, None, QT, NA_KEYS), bias_map),
            pl.BlockSpec((None, 4, DA_DIM), lambda b, qt: (l, 0, 0)),
            pl.BlockSpec((None, 1, 2 * DA_DIM), lambda b, qt: (l, 0, 0)),
            pl.BlockSpec((None, 3, SC_WIDTH), lambda b, qt: (l, 0, 0)),
            pl.BlockSpec((None, 1, SC_WIDTH), lambda b, qt: (l, 0, 0)),
        ],
        out_specs=pl.BlockSpec((QT, MIX_WIDTH), lambda b, qt: (b * n_qt + qt, 0)),
        out_shape=jax.ShapeDtypeStruct((DEC_BATCH * DEC_SEQ, MIX_WIDTH), BF16),
        scratch_shapes=[
            pltpu.VMEM((DEC_SEQ + PAST_LEN, DA_QK_WIDTH), BF16),
            pltpu.VMEM((DEC_SEQ + PAST_LEN, 2 * DA_V_WIDTH), BF16),
            pltpu.VMEM((PAST_LEN, 2 * NA_WIDTH), BF16),
        ],
        compiler_params=_params("parallel", "arbitrary"),
        name="mix_latent",
    )(u, u, u, u, u, cnak, cnav, cdak, cdav, bias_tab, da_lambda,
      da_norm_g.reshape(DEPTH, 1, 2 * DA_DIM), conv_w, conv_b.reshape(DEPTH, 1, SC_WIDTH))


def _first_max(vals):
    m = functools.reduce(jnp.maximum, vals)
    hot, taken = [], None
    for v in vals:
        is_max = v == m
        if taken is None:
            hot.append(is_max)
            taken = is_max
        else:
            hot.append(is_max & ~taken)
            taken = taken | is_max
    return hot, m


def _pick(hot, vals):
    out = vals[-1]
    for h, v in zip(hot[-2::-1], vals[-2::-1]):
        out = jnp.where(h, v, out)
    return out


def _route(h2, h2_bf, wr_ref, br_ref):
    tm = h2.shape[0]
    w = wr_ref[...]
    w_hi = w.astype(BF16)
    w_lo = (w - w_hi.astype(F32)).astype(BF16)
    h_lo = (h2 - h2_bf.astype(F32)).astype(BF16)
    both = _dot_nt(jnp.concatenate([w_hi, w_lo], axis=0), h2_bf)
    z = both[:N_EXPERTS] + both[N_EXPERTS:] + _dot_nt(w_hi, h_lo)
    scores = jax.nn.sigmoid(z)
    biased = scores + br_ref[...]
    P = [biased[k * N_GROUPS:(k + 1) * N_GROUPS] for k in range(PER_GROUP)]
    S = [scores[k * N_GROUPS:(k + 1) * N_GROUPS] for k in range(PER_GROUP)]
    pair_sums = [P[i] + P[j] for i in range(PER_GROUP) for j in range(i + 1, PER_GROUP)]
    group_score = functools.reduce(jnp.maximum, pair_sums)
    sel, _ = _first_max([group_score[g:g + 1] for g in range(N_GROUPS)])
    c = [_pick(sel, [P[k][g:g + 1] for g in range(N_GROUPS)]) for k in range(PER_GROUP)]
    cs = [_pick(sel, [S[k][g:g + 1] for g in range(N_GROUPS)]) for k in range(PER_GROUP)]
    t1, _ = _first_max(c)
    t2, _ = _first_max([jnp.where(t, -jnp.inf, v) for t, v in zip(t1, c)])
    w1 = functools.reduce(jnp.add, [jnp.where(t, v, 0.0) for t, v in zip(t1, cs)])
    w2 = functools.reduce(jnp.add, [jnp.where(t, v, 0.0) for t, v in zip(t2, cs)])
    total = w1 + w2
    slot_gate = [jnp.where(a, w1 / total, jnp.where(b, w2 / total, 0.0)) for a, b in zip(t1, t2)]
    onehot = jnp.concatenate([s.astype(F32) for s in sel] + [jnp.zeros((8 - N_GROUPS, tm), F32)], axis=0)
    earlier = (lax.broadcasted_iota(jnp.int32, (tm, tm), 0) < lax.broadcasted_iota(jnp.int32, (tm, tm), 1))
    rank = _dot(onehot.astype(BF16), earlier.astype(BF16))
    count = jnp.sum(onehot, axis=1, keepdims=True)
    n_chunks = jnp.floor((count + (MOE_CHUNK - 1)) * (1.0 / MOE_CHUNK))
    pos = jnp.zeros((1, tm), F32)
    start = jnp.zeros((1, 1), F32)
    for g in range(N_GROUPS):
        pos = jnp.where(sel[g], start + rank[g:g + 1], pos)
        start = start + n_chunks[g:g + 1] * MOE_CHUNK
    rows = [pos] + slot_gate + [jnp.zeros((128 - 1 - PER_GROUP, tm), F32)]
    return jnp.concatenate(rows, axis=0), n_chunks


def _post_kernel(mixp_ref, mixs_ref, xp_ref, xs_ref, mod_ref, wout_ref, g_ref, b_ref, wr_ref, br_ref,
                 x1_ref, h2_ref, tok_ref, pos_ref, nch_ref, wbf_ref):
    i = pl.program_id(0)
    is_ctx = i < CTX_TILES

    @pl.when(i == 0)
    def _():
        wbf_ref[...] = wout_ref[...].astype(BF16)

    mod = mod_ref[...]
    n = TM_POST // POST_SPLIT
    parts = [slice(p * n, (p + 1) * n) for p in range(POST_SPLIT)]
    ys = [_dot(jnp.where(is_ctx, mixp_ref[p, :], mixs_ref[p, :]), wbf_ref[...]) for p in parts]
    xs = [jnp.where(is_ctx, xp_ref[p, :], xs_ref[p, :]) for p in parts]
    x1s = [t * g_ref[...] + b_ref[...] for t in _ln_each([ALPHA * x + mod[2:3] * y for x, y in zip(xs, ys)])]
    h2s = [t * (1.0 + mod[4:5]) + mod[3:4] for t in _ln_each(x1s)]
    for p, x1 in zip(parts, x1s):
        x1_ref[p, :] = x1
    h2 = jnp.concatenate(h2s, axis=0)
    h2_bf = h2.astype(BF16)
    h2_ref[...] = h2_bf
    rows, n_chunks = _route(h2, h2_bf, wr_ref, br_ref)
    tok_ref[...] = rows.T
    pos_ref[...] = rows[0:8]
    nch_ref[...] = jnp.broadcast_to(n_chunks, (8, 128)).astype(jnp.int32)


def _post(mix_p, mix_s, xp, xs, mod_l, w_out, ln_g, ln_b, wr_t, br_t, l):
    n = xp.shape[0] + xs.shape[0]
    route = functools.partial(_tile_stream, n_ctx_tiles=CTX_TILES, lat_tiles_per_batch=DEC_SEQ // TM_POST)
    return pl.pallas_call(
        _post_kernel,
        grid=(n // TM_POST,),
        in_specs=[
            pl.BlockSpec((TM_POST, MIX_WIDTH), lambda i: (route(i)[0], 0)),
            pl.BlockSpec((TM_POST, MIX_WIDTH), lambda i: (route(i)[1], 0)),
            pl.BlockSpec((TM_POST, D_MODEL), lambda i: (route(i)[0], 0)),
            pl.BlockSpec((TM_POST, D_MODEL), lambda i: (route(i)[1], 0)),
            pl.BlockSpec((None, 6, D_MODEL), lambda i: (route(i)[2], 0, 0)),
            pl.BlockSpec((None, MIX_WIDTH, D_MODEL), lambda i: (l, 0, 0), pipeline_mode=pl.Buffered(1)),
            pl.BlockSpec((None, 1, D_MODEL), lambda i: (l, 0, 0)),
            pl.BlockSpec((None, 1, D_MODEL), lambda i: (l, 0, 0)),
            pl.BlockSpec((N_EXPERTS, D_MODEL), lambda i: (0, 0)),
            pl.BlockSpec((N_EXPERTS, 1), lambda i: (0, 0)),
        ],
        out_specs=[
            pl.BlockSpec((TM_POST, D_MODEL), lambda i: (i, 0)),
            pl.BlockSpec((TM_POST, D_MODEL), lambda i: (i, 0)),
            pl.BlockSpec((TM_POST, 128), lambda i: (i, 0)),
            pl.BlockSpec((None, 8, TM_POST), lambda i: (i, 0, 0)),
            pl.BlockSpec((None, 8, 128), lambda i: (i, 0, 0)),
        ],
        out_shape=[
            jax.ShapeDtypeStruct((n, D_MODEL), F32),
            jax.ShapeDtypeStruct((n, D_MODEL), BF16),
            jax.ShapeDtypeStruct((n, 128), F32),
            jax.ShapeDtypeStruct((n // TM_POST, 8, TM_POST), F32),
            jax.ShapeDtypeStruct((n // TM_POST, 8, 128), jnp.int32),
        ],
        scratch_shapes=[pltpu.VMEM((MIX_WIDTH, D_MODEL), BF16)],
        compiler_params=_params("arbitrary"),
        name="post",
    )(mix_p, mix_s, xp, xs, mod_l, w_out, ln_g.reshape(DEPTH, 1, D_MODEL), ln_b.reshape(DEPTH, 1, D_MODEL), wr_t, br_t)


def _moe_kernel(nch_ref, h_ref, tok_ref, pos_ref, x1_ref, mod_ref, w1f_ref, w3f_ref, w2f_ref, g_ref, b_ref,
                op_ref, os_ref, w13_ref, w2_ref, hs_ref, gs_ref, ys_ref):
    step = pl.program_id(0)
    half = PER_GROUP * D_EXPERT

    for k in range(PER_GROUP):
        @pl.when((step < N_EXPERTS) & (step % PER_GROUP == k))
        def _(k=k):
            g = step // PER_GROUP
            w13_ref[g, :, k * D_EXPERT:(k + 1) * D_EXPERT] = w1f_ref[...].astype(BF16)
            w13_ref[g, :, half + k * D_EXPERT:half + (k + 1) * D_EXPERT] = w3f_ref[...].astype(BF16)
            w2_ref[g, k * D_EXPERT:(k + 1) * D_EXPERT, :] = w2f_ref[...].astype(BF16)

    @pl.when(step >= N_EXPERTS)
    def _():
        i = step - N_EXPERTS
        out = _moe_tile(i, nch_ref, h_ref, tok_ref, pos_ref, x1_ref, mod_ref, w13_ref, w2_ref, g_ref, b_ref,
                        hs_ref, gs_ref, ys_ref)

        @pl.when(i < CTX_TILES)
        def _():
            op_ref[...] = out

        @pl.when(i >= CTX_TILES)
        def _():
            os_ref[...] = out


def _moe_tile(i, nch_ref, h_ref, tok_ref, pos_ref, x1_ref, mod_ref, w13_ref, w2_ref, g_ref, b_ref,
              hs_ref, gs_ref, ys_ref):
    tok = tok_ref[...]
    pos_lane = pos_ref[0:1, :]
    pos_col = tok[:, 0:1]
    slot = lax.broadcasted_iota(jnp.int32, (MOE_SORTED, TM_MOE), 0).astype(F32)
    sort = (slot == pos_lane).astype(BF16)
    tok_hi = tok.astype(BF16)
    tok_lo = (tok - tok_hi.astype(F32)).astype(BF16)
    z = _dot(sort, jnp.concatenate([h_ref[...], tok_hi, tok_lo], axis=1))
    hs_ref[...] = z[:, :D_MODEL].astype(BF16)
    gs_ref[...] = z[:, D_MODEL:D_MODEL + 128] + z[:, D_MODEL + 128:]
    ys_ref[...] = jnp.zeros_like(ys_ref)

    def experts(g, chunk0, n_rows):
        r0 = pl.multiple_of(chunk0 * MOE_CHUNK, MOE_CHUNK)
        rows = hs_ref[pl.ds(r0, n_rows), :]
        gates = gs_ref[pl.ds(r0, n_rows), :]
        half = PER_GROUP * D_EXPERT
        ab = _dot(rows, w13_ref[g])
        hid = []
        for k in range(PER_GROUP):
            a = ab[:, k * D_EXPERT:(k + 1) * D_EXPERT]
            b = ab[:, half + k * D_EXPERT:half + (k + 1) * D_EXPERT]
            hid.append((a * jax.nn.sigmoid(a) * b * gates[:, 1 + k:2 + k]).astype(BF16))
        ys_ref[pl.ds(r0, n_rows), :] = _dot(jnp.concatenate(hid, axis=1), w2_ref[g]).astype(BF16)

    def group(g, first):
        n = nch_ref[i * N_GROUPS + g]
        n_blocks = n // MOE_BLOCK
        rest = n - n_blocks * MOE_BLOCK

        def block(c, carry):
            experts(g, first + c * MOE_BLOCK, MOE_BLOCK * MOE_CHUNK)
            return carry

        lax.fori_loop(0, n_blocks, block, 0)
        for m in range(1, MOE_BLOCK):
            @pl.when(rest == m)
            def _(m=m):
                experts(g, first + n_blocks * MOE_BLOCK, m * MOE_CHUNK)
        return first + n

    lax.fori_loop(0, N_GROUPS, group, 0)

    unsort = (lax.broadcasted_iota(jnp.int32, (TM_MOE, MOE_SORTED), 1).astype(F32) == pos_col).astype(BF16)
    y = _dot(unsort, ys_ref[...])
    mod = mod_ref[...]
    return _ln(ALPHA * x1_ref[...] + mod[5:6] * y) * g_ref[...] + b_ref[...]


def _moe(h2, tok, pos, nch, x1, mod_l, w1, w3, w2, ln_g, ln_b, l):
    n_tiles = x1.shape[0] // TM_MOE
    n_ctx = CTX_TILES * TM_MOE
    per_batch = DEC_SEQ // TM_MOE

    def tile(step):
        return jnp.maximum(step - N_EXPERTS, 0)

    def expert(step):
        return jnp.minimum(step, N_EXPERTS - 1)

    def mod_map(step, s):
        return (_tile_stream(tile(step), CTX_TILES, per_batch)[2], 0, 0)

    grid_spec = pltpu.PrefetchScalarGridSpec(
        num_scalar_prefetch=1,
        grid=(N_EXPERTS + n_tiles,),
        in_specs=[
            pl.BlockSpec((TM_MOE, D_MODEL), lambda step, s: (tile(step), 0)),
            pl.BlockSpec((TM_MOE, 128), lambda step, s: (tile(step), 0)),
            pl.BlockSpec((None, 8, TM_MOE), lambda step, s: (tile(step), 0, 0)),
            pl.BlockSpec((TM_MOE, D_MODEL), lambda step, s: (tile(step), 0)),
            pl.BlockSpec((None, 6, D_MODEL), mod_map),
            pl.BlockSpec((None, None, D_MODEL, D_EXPERT), lambda step, s: (l, expert(step), 0, 0)),
            pl.BlockSpec((None, None, D_MODEL, D_EXPERT), lambda step, s: (l, expert(step), 0, 0)),
            pl.BlockSpec((None, None, D_EXPERT, D_MODEL), lambda step, s: (l, expert(step), 0, 0)),
            pl.BlockSpec((None, 1, D_MODEL), lambda step, s: (l, 0, 0)),
            pl.BlockSpec((None, 1, D_MODEL), lambda step, s: (l, 0, 0)),
        ],
        out_specs=[
            pl.BlockSpec((TM_MOE, D_MODEL), lambda step, s: (jnp.minimum(tile(step), CTX_TILES - 1), 0)),
            pl.BlockSpec((TM_MOE, D_MODEL), lambda step, s: (jnp.maximum(tile(step) - CTX_TILES, 0), 0)),
        ],
        scratch_shapes=[
            pltpu.VMEM((N_GROUPS, D_MODEL, 2 * PER_GROUP * D_EXPERT), BF16),
            pltpu.VMEM((N_GROUPS, PER_GROUP * D_EXPERT, D_MODEL), BF16),
            pltpu.VMEM((MOE_SORTED, D_MODEL), BF16),
            pltpu.VMEM((MOE_SORTED, 128), F32),
            pltpu.VMEM((MOE_SORTED, D_MODEL), BF16),
        ],
    )
    return pl.pallas_call(
        _moe_kernel,
        grid_spec=grid_spec,
        out_shape=[jax.ShapeDtypeStruct((n_ctx, D_MODEL), F32),
                   jax.ShapeDtypeStruct((x1.shape[0] - n_ctx, D_MODEL), F32)],
        compiler_params=_params("arbitrary"),
        name="moe",
    )(nch[:, :N_GROUPS, 0].reshape(-1), h2, tok, pos, x1, mod_l, w1, w3, w2,
      ln_g.reshape(DEPTH, 1, D_MODEL), ln_b.reshape(DEPTH, 1, D_MODEL))


def _rope_tables():
    t = np.arange(DEC_SEQ)
    half = DA_DIM // 4
    inv_freq = ROPE_BASE ** (-np.arange(half, dtype=np.float32) / half)
    cos, sin = [], []
    for pos in (t // GRID_W, t % GRID_W):
        ang = pos.astype(np.float32)[:, None] * inv_freq[None, :]
        c, s = np.cos(ang), np.sin(ang)
        cos += [c, c]
        sin += [-s, s]
    cos = np.tile(np.concatenate(cos, axis=1), (1, 2 * DA_HEADS))
    sin = np.tile(np.concatenate(sin, axis=1), (1, 2 * DA_HEADS))
    cos = np.concatenate([np.ones((TM, DA_QK_WIDTH), np.float32), cos], axis=0)
    sin = np.concatenate([np.zeros((TM, DA_QK_WIDTH), np.float32), sin], axis=0)
    return jnp.asarray(cos, F32), jnp.asarray(sin, F32)


_NA_TILE_KINDS = ((0, 0), (2 * QT_ROWS, 2 * QT_ROWS - NA_WIN_ROWS // 2), (GRID_ROWS - QT_ROWS, GRID_ROWS - NA_KEY_ROWS))


def _na_bias_kernel(by_ref, o_ref):
    qc = lax.broadcasted_iota(jnp.int32, (GRID_W, GRID_W), 0)
    kc = lax.broadcasted_iota(jnp.int32, (GRID_W, GRID_W), 1)
    win_c0 = jnp.clip(qc - NA_WIN_COLS // 2, 0, GRID_W - NA_WIN_COLS)
    in_cols = (kc >= win_c0) & (kc < win_c0 + NA_WIN_COLS)
    masked = jnp.full((GRID_W, GRID_W), -jnp.inf, F32)
    for kind, (r0, key_row0) in enumerate(_NA_TILE_KINDS):
        for a in range(QT_ROWS):
            qr = r0 + a
            win_r0 = min(max(qr - NA_WIN_ROWS // 2, 0), GRID_ROWS - NA_WIN_ROWS)
            blocks = []
            for b in range(NA_KEY_ROWS):
                kr = key_row0 + b
                if win_r0 <= kr < win_r0 + NA_WIN_ROWS:
                    blocks.append(jnp.where(in_cols, by_ref[kr - qr + NA_WIN_ROWS - 1] * LOG2E, masked))
                else:
                    blocks.append(masked)
            o_ref[kind, a * GRID_W:(a + 1) * GRID_W, :] = jnp.concatenate(blocks, axis=1)


def _na_bias_tables(na_rel_bias):
    n_dr, n_dc = 2 * NA_WIN_ROWS - 1, 2 * NA_WIN_COLS - 1
    cols = np.arange(GRID_W)
    d_col = np.clip(cols[None, :] - cols[:, None], 1 - NA_WIN_COLS, NA_WIN_COLS - 1) + NA_WIN_COLS - 1
    col_sel = (d_col[None] == np.arange(n_dc)[:, None, None]).astype(np.float32)
    by_col = jnp.einsum('lhab,bqc->lhaqc', na_rel_bias.astype(F32), col_sel, precision=lax.Precision.HIGHEST)
    return pl.pallas_call(
        _na_bias_kernel,
        grid=(DEPTH, NA_HEADS),
        in_specs=[pl.BlockSpec((None, None, n_dr, GRID_W, GRID_W), lambda l, h: (l, h, 0, 0, 0))],
        out_specs=pl.BlockSpec((None, None, len(_NA_TILE_KINDS), QT, NA_KEYS), lambda l, h: (l, h, 0, 0, 0)),
        out_shape=jax.ShapeDtypeStruct((DEPTH, NA_HEADS, len(_NA_TILE_KINDS), QT, NA_KEYS), F32),
        compiler_params=_params("parallel", "parallel"),
        name="na_bias_table",
    )(by_col)


def kernel(x_prompt, x_sample, cache_na_k, cache_na_v, cache_da_k, cache_da_v, c, c_ctx, w_mod, b_mod, w_in,
           na_rel_bias, sc_conv_w, sc_conv_b, da_lambda, da_norm_g, w_out, ln1_g, ln1_b, w_router, b_router,
           moe_w1, moe_w3, moe_w2, ln2_g, ln2_b):
    xp = x_prompt.reshape(BATCH * SEQ, D_MODEL)
    xs = x_sample.reshape(DEC_BATCH * DEC_SEQ, D_MODEL)

    cond = jnp.concatenate([c_ctx[None, :], c, jnp.zeros((N_COND - 1 - DEC_BATCH, D_MODEL), F32)], axis=0)
    mod = _modulation(cond, w_mod, b_mod).reshape(DEPTH, N_COND, 6, D_MODEL)

    slot_major = np.arange(N_EXPERTS).reshape(N_GROUPS, PER_GROUP).T.reshape(-1)
    wr_t = w_router.T[slot_major]
    br_t = b_router.astype(F32)[slot_major].reshape(N_EXPERTS, 1)

    caches = (
        cache_na_k.reshape(DEC_BATCH, DEPTH, PAST_LEN, NA_WIDTH).astype(BF16),
        cache_na_v.reshape(DEC_BATCH, DEPTH, PAST_LEN, NA_WIDTH).astype(BF16),
        cache_da_k.reshape(DEC_BATCH, DEPTH, PAST_LEN, DA_QK_WIDTH).astype(BF16),
        cache_da_v.reshape(DEC_BATCH, DEPTH, PAST_LEN, DA_V_WIDTH).astype(BF16),
    )
    rope_tabs = _rope_tables()
    bias_tab = _na_bias_tables(na_rel_bias)

    new_caches = None
    for l in range(DEPTH):
        lam_init = 0.8 - 0.6 * math.exp(-0.3 * l)
        mod_l = mod[l]
        u, *new_caches = _inproj(xp, xs, mod_l, w_in, l, rope_tabs, cache_bufs=new_caches)
        mix_p = _ctx_mix(u, da_lambda, da_norm_g, sc_conv_w, sc_conv_b, l, lam_init)
        mix_s = _lat_mix(u, caches, bias_tab, da_lambda, da_norm_g, sc_conv_w, sc_conv_b, l, lam_init)
        x1, h2, *route = _post(mix_p, mix_s, xp, xs, mod_l, w_out, ln1_g, ln1_b, wr_t, br_t, l)
        xp, xs = _moe(h2, *route, x1, mod_l, moe_w1, moe_w3, moe_w2, ln2_g, ln2_b, l)

    nak, nav, dak, dav = new_caches
    return (xp.reshape(BATCH, SEQ, D_MODEL), xs.reshape(DEC_BATCH, DEC_SEQ, D_MODEL),
            nak.reshape(BATCH, DEPTH, SEQ, NA_HEADS, HEAD_DIM), nav.reshape(BATCH, DEPTH, SEQ, NA_HEADS, HEAD_DIM),
            dak.reshape(BATCH, DEPTH, SEQ, 2 * DA_HEADS, DA_DIM), dav.reshape(BATCH, DEPTH, SEQ, DA_HEADS, 2 * DA_DIM))
```

```python
import functools
import math

import numpy as np
import jax
import jax.numpy as jnp
from jax import lax
from jax.experimental import pallas as pl
from jax.experimental.pallas import tpu as pltpu

F32 = jnp.float32
BF16 = jnp.bfloat16

D_MODEL = 1024
BATCH = 16
SEQ = 256
DEPTH = 4
DEC_BATCH = 2
DEC_SEQ = 2048
PAST_LEN = 512
GRID_W = 64
GRID_ROWS = DEC_SEQ // GRID_W
HEAD_DIM = 64
NA_HEADS = 4
NA_WIDTH = NA_HEADS * HEAD_DIM
NA_WIN_ROWS = 8
NA_WIN_COLS = 16
SC_WIDTH = 256
DA_HEADS = 4
DA_DIM = 64
DA_QK_WIDTH = 2 * DA_HEADS * DA_DIM
DA_V_WIDTH = DA_HEADS * 2 * DA_DIM
MIX_WIDTH = NA_WIDTH + SC_WIDTH + DA_V_WIDTH
IN_WIDTH = 3 * NA_WIDTH + 3 * SC_WIDTH + 2 * DA_QK_WIDTH + DA_V_WIDTH
OFF_NA_Q = 0
OFF_NA_K = NA_WIDTH
OFF_NA_V = 2 * NA_WIDTH
OFF_SC_B = 3 * NA_WIDTH
OFF_SC_C = OFF_SC_B + SC_WIDTH
OFF_SC_X = OFF_SC_C + SC_WIDTH
OFF_DA_Q = OFF_SC_X + SC_WIDTH
OFF_DA_K = OFF_DA_Q + DA_QK_WIDTH
OFF_DA_V = OFF_DA_K + DA_QK_WIDTH
ROPE_BASE = 10000.0
N_EXPERTS = 16
N_GROUPS = 4
PER_GROUP = N_EXPERTS // N_GROUPS
D_EXPERT = 256
ALPHA = (2 * DEPTH) ** 0.25
LN_EPS = 1e-5
LOG2E = math.log2(math.e)
Q_SCALE = HEAD_DIM ** -0.5 * LOG2E

N_COND = 8
TM = 512
SEQ_PER_TILE = TM // SEQ
CTX_TILES = BATCH * SEQ // TM
CTX_SEQ_PER_STEP = 2
DA_HEADS_PER_STAGE = 2
POST_SPLIT = 2
TM_MOE = 512
TM_POST = TM_MOE
MOE_CHUNK = 32
MOE_BLOCK = 8
MOE_SORTED = TM_MOE + (N_GROUPS - 1) * MOE_CHUNK
assert TM == TM_MOE
QT = 256
QT_ROWS = QT // GRID_W
NA_KEY_ROWS = 12
NA_KEYS = NA_KEY_ROWS * GRID_W
VMEM_LIMIT = 60 * 1024 * 1024


def _dot(a, b):
    return jnp.dot(a, b, preferred_element_type=F32)


def _dot_nt(a, b):
    return lax.dot_general(a, b, (((1,), (1,)), ((), ())), preferred_element_type=F32)


def _ln(x):
    mu = jnp.mean(x, -1, keepdims=True)
    xc = x - mu
    var = jnp.mean(xc * xc, -1, keepdims=True)
    return xc * lax.rsqrt(var + LN_EPS)


def _ln_each(xs):
    mus = [jnp.mean(x, -1, keepdims=True) for x in xs]
    xcs = [x - mu for x, mu in zip(xs, mus)]
    vs = [jnp.mean(xc * xc, -1, keepdims=True) for xc in xcs]
    return [xc * lax.rsqrt(v + LN_EPS) for xc, v in zip(xcs, vs)]


def _params(*sem):
    return pltpu.CompilerParams(dimension_semantics=sem, vmem_limit_bytes=VMEM_LIMIT)


def _mod_kernel(cond_ref, w_ref, b_ref, o_ref):
    c = cond_ref[...]
    s = (c * jax.nn.sigmoid(c)).astype(BF16)
    o_ref[...] = _dot(s, w_ref[...].astype(BF16)) + b_ref[...]


def _modulation(cond, w_mod, b_mod):
    tn = 1024
    return pl.pallas_call(
        _mod_kernel,
        grid=(DEPTH, 6 * D_MODEL // tn),
        in_specs=[
            pl.BlockSpec((N_COND, D_MODEL), lambda l, j: (0, 0)),
            pl.BlockSpec((None, D_MODEL, tn), lambda l, j: (l, 0, j)),
            pl.BlockSpec((None, 1, tn), lambda l, j: (l, 0, j)),
        ],
        out_specs=pl.BlockSpec((None, N_COND, tn), lambda l, j: (l, 0, j)),
        out_shape=jax.ShapeDtypeStruct((DEPTH, N_COND, 6 * D_MODEL), F32),
        compiler_params=_params("parallel", "parallel"),
        name="modulation",
    )(cond, w_mod, b_mod.reshape(DEPTH, 1, 6 * D_MODEL))


def _rope(t, cos, sin):
    lane = lax.broadcasted_iota(jnp.int32, t.shape, 1)
    first = (lane // 16) % 2 == 0
    n = t.shape[1]
    swapped = jnp.where(first, pltpu.roll(t, n - 16, 1), pltpu.roll(t, 16, 1))
    return t * cos + swapped * sin


def _tile_stream(i, n_ctx_tiles, lat_tiles_per_batch):
    lat = jnp.maximum(i - n_ctx_tiles, 0)
    return jnp.minimum(i, n_ctx_tiles - 1), lat, jnp.where(i < n_ctx_tiles, 0, 1 + lat // lat_tiles_per_batch)


def _inproj_kernel(*refs):
    xp_ref, xs_ref, mod_ref, w_ref, cos_ref, sin_ref = refs[:6]
    u_ref, nak_ref, nav_ref, dak_ref, dav_ref, wbf_ref = refs[-6:]
    i = pl.program_id(0)
    is_ctx = i < CTX_TILES

    @pl.when(i == 0)
    def _():
        wbf_ref[...] = w_ref[...].astype(BF16)

    mod = mod_ref[...]
    parts = [slice(s * SEQ, (s + 1) * SEQ) for s in range(SEQ_PER_TILE)]
    hs = [t * (1.0 + mod[1:2]) + mod[0:1]
          for t in _ln_each([jnp.where(is_ctx, xp_ref[p, :], xs_ref[p, :]) for p in parts])]
    us = [_dot(h.astype(BF16), wbf_ref[...]) for h in hs]
    for p, u in zip(parts, us):
        cos, sin = cos_ref[p, :], sin_ref[p, :]
        da_q = _rope(u[:, OFF_DA_Q:OFF_DA_K], cos, sin)
        da_k = _rope(u[:, OFF_DA_K:OFF_DA_V], cos, sin)
        u_ref[p, :OFF_NA_K] = (u[:, :OFF_NA_K] * Q_SCALE).astype(BF16)
        u_ref[p, OFF_NA_K:OFF_DA_Q] = u[:, OFF_NA_K:OFF_DA_Q].astype(BF16)
        u_ref[p, OFF_DA_Q:OFF_DA_K] = (da_q * Q_SCALE).astype(BF16)
        u_ref[p, OFF_DA_K:OFF_DA_V] = da_k.astype(BF16)
        u_ref[p, OFF_DA_V:] = u[:, OFF_DA_V:].astype(BF16)

    @pl.when(is_ctx)
    def _():
        for s, u in enumerate(us):
            nak_ref[s] = u[:, OFF_NA_K:OFF_NA_V]
            nav_ref[s] = u[:, OFF_NA_V:OFF_SC_B]
            dak_ref[s] = u[:, OFF_DA_K:OFF_DA_V]
            for j in range(DA_HEADS):
                dav_ref[s, pl.ds(j, SEQ, stride=DA_HEADS), :] = (
                    u[:, OFF_DA_V + j * 2 * DA_DIM:OFF_DA_V + (j + 1) * 2 * DA_DIM])


def _inproj(xp, xs, mod_l, w_in, l, rope_tabs, cache_bufs=None):
    route = functools.partial(_tile_stream, n_ctx_tiles=CTX_TILES, lat_tiles_per_batch=DEC_SEQ // TM)
    in_specs = [
        pl.BlockSpec((TM, D_MODEL), lambda i: (route(i)[0], 0)),
        pl.BlockSpec((TM, D_MODEL), lambda i: (route(i)[1], 0)),
        pl.BlockSpec((None, 6, D_MODEL), lambda i: (route(i)[2], 0, 0)),
        pl.BlockSpec((None, D_MODEL, IN_WIDTH), lambda i: (l, 0, 0), pipeline_mode=pl.Buffered(1)),
        pl.BlockSpec((TM, DA_QK_WIDTH), lambda i: (jnp.where(i < CTX_TILES, 0, 1 + route(i)[1] % (DEC_SEQ // TM)), 0)),
        pl.BlockSpec((TM, DA_QK_WIDTH), lambda i: (jnp.where(i < CTX_TILES, 0, 1 + route(i)[1] % (DEC_SEQ // TM)), 0)),
    ]
    args = [xp, xs, mod_l, w_in, *rope_tabs]
    out_specs = [pl.BlockSpec((TM, IN_WIDTH), lambda i: (i, 0))]
    out_shape = [jax.ShapeDtypeStruct((xp.shape[0] + xs.shape[0], IN_WIDTH), BF16)]
    for rows, width in ((SEQ, NA_WIDTH), (SEQ, NA_WIDTH), (SEQ, DA_QK_WIDTH), (SEQ * DA_HEADS, 2 * DA_DIM)):
        out_specs.append(pl.BlockSpec((SEQ_PER_TILE, None, rows, width), lambda i: (route(i)[0], l, 0, 0)))
        out_shape.append(jax.ShapeDtypeStruct((BATCH, DEPTH, rows, width), F32))
    aliases = {}
    if cache_bufs is not None:
        aliases = {len(args) + k: 1 + k for k in range(4)}
        in_specs += [pl.BlockSpec(memory_space=pl.ANY)] * 4
        args += list(cache_bufs)
    return pl.pallas_call(
        _inproj_kernel,
        grid=(CTX_TILES + xs.shape[0] // TM,),
        in_specs=in_specs,
        out_specs=out_specs,
        out_shape=out_shape,
        scratch_shapes=[pltpu.VMEM((D_MODEL, IN_WIDTH), BF16)],
        input_output_aliases=aliases,
        compiler_params=_params("arbitrary"),
        name="inproj",
    )(*args)


def _lambda(lam_ref, lam_init):
    lp = lam_ref[...]
    s1 = jnp.sum(lp[0:1] * lp[1:2], axis=-1, keepdims=True)
    s2 = jnp.sum(lp[2:3] * lp[3:4], axis=-1, keepdims=True)
    return jnp.exp(s1) - jnp.exp(s2) + lam_init


def _with_ones(v):
    return jnp.concatenate([v, jnp.ones((v.shape[0], max(v.shape[1], 64)), v.dtype)], axis=1)


def _softmax_pv(score_parts, v_ones, d):
    m = functools.reduce(jnp.maximum, [jnp.max(s, -1, keepdims=True) for s in score_parts])
    e = [jnp.exp2(s - m).astype(BF16) for s in score_parts]
    o = _dot(e[0] if len(e) == 1 else jnp.concatenate(e, axis=1), v_ones)
    return o[:, :d] / o[:, d:d + 1]


def _softmax_pv_each(jobs):
    ms = [functools.reduce(jnp.maximum, [jnp.max(s, -1, keepdims=True) for s in parts]) for parts, _, _ in jobs]
    es = [[jnp.exp2(s - m).astype(BF16) for s in parts] for (parts, _, _), m in zip(jobs, ms)]
    os = [_dot(e[0] if len(e) == 1 else jnp.concatenate(e, axis=1), v) for e, (_, v, _) in zip(es, jobs)]
    return [o[:, :d] / o[:, d:d + 1] for o, (_, _, d) in zip(os, jobs)]


def _diff_head_norm(o1, o2, lam, gain, lam_init):
    o = o1 - lam * o2
    o = o * lax.rsqrt(jnp.mean(o * o, -1, keepdims=True) + LN_EPS)
    return o * gain * (1.0 - lam_init)


def _gated_conv(b, v, v_prev_row, v_next_row, w, bias):
    n = v.shape[0]
    row = lax.broadcasted_iota(jnp.int32, v.shape, 0)
    prev = jnp.where(row == 0, v_prev_row, pltpu.roll(v, 1, 0))
    nxt = jnp.where(row == n - 1, v_next_row, pltpu.roll(v, n - 1, 0))
    return b * (prev * w[0:1] + v * w[1:2] + nxt * w[2:3] + bias)


def _ctx_mix_kernel(u_ref, lam_ref, gain_ref, cw_ref, cb_ref, o_ref, *, lam_init):
    lam = _lambda(lam_ref, lam_init)
    gain = gain_ref[...]
    maps = []
    for s in range(CTX_SEQ_PER_STEP):
        for h in range(NA_HEADS):
            maps.append((s, OFF_NA_Q + h * HEAD_DIM, OFF_NA_K + h * HEAD_DIM, slice(OFF_NA_V + h * HEAD_DIM,
                                                                                    OFF_NA_V + (h + 1) * HEAD_DIM)))
        for mi in range(2 * DA_HEADS):
            j = mi // 2
            maps.append((s, OFF_DA_Q + mi * DA_DIM, OFF_DA_K + mi * DA_DIM,
                         slice(OFF_DA_V + j * 2 * DA_DIM, OFF_DA_V + (j + 1) * 2 * DA_DIM)))
    rows_of = lambda s: slice(s * SEQ, (s + 1) * SEQ)
    scores = [_dot_nt(u_ref[rows_of(s), q0:q0 + HEAD_DIM], u_ref[rows_of(s), k0:k0 + HEAD_DIM])
              for s, q0, k0, _ in maps]
    exps = [jnp.exp2(sc - jnp.max(sc, -1, keepdims=True)) for sc in scores]
    outs = [_dot(e.astype(BF16), u_ref[rows_of(s), vs]) / jnp.sum(e, -1, keepdims=True)
            for e, (s, _, _, vs) in zip(exps, maps)]
    per_seq = NA_HEADS + 2 * DA_HEADS
    zero_row = jnp.zeros((1, SC_WIDTH), F32)
    for s in range(CTX_SEQ_PER_STEP):
        rows = rows_of(s)
        o = outs[s * per_seq:(s + 1) * per_seq]
        vc = u_ref[rows, OFF_SC_C:OFF_SC_X].astype(F32) * u_ref[rows, OFF_SC_X:OFF_DA_Q].astype(F32)
        conv = _gated_conv(u_ref[rows, OFF_SC_B:OFF_SC_C].astype(F32), vc, zero_row, zero_row, cw_ref[...], cb_ref[...])
        da = [_diff_head_norm(o[NA_HEADS + 2 * j], o[NA_HEADS + 2 * j + 1], lam, gain, lam_init)
              for j in range(DA_HEADS)]
        o_ref[rows, :] = jnp.concatenate(o[:NA_HEADS] + [conv] + da, axis=-1).astype(BF16)


def _ctx_mix(u, da_lambda, da_norm_g, conv_w, conv_b, l, lam_init):
    step_rows = CTX_SEQ_PER_STEP * SEQ
    return pl.pallas_call(
        functools.partial(_ctx_mix_kernel, lam_init=lam_init),
        grid=(BATCH // CTX_SEQ_PER_STEP,),
        in_specs=[
            pl.BlockSpec((step_rows, IN_WIDTH), lambda b: (b, 0)),
            pl.BlockSpec((None, 4, DA_DIM), lambda b: (l, 0, 0)),
            pl.BlockSpec((None, 1, 2 * DA_DIM), lambda b: (l, 0, 0)),
            pl.BlockSpec((None, 3, SC_WIDTH), lambda b: (l, 0, 0)),
            pl.BlockSpec((None, 1, SC_WIDTH), lambda b: (l, 0, 0)),
        ],
        out_specs=pl.BlockSpec((step_rows, MIX_WIDTH), lambda b: (b, 0)),
        out_shape=jax.ShapeDtypeStruct((BATCH * SEQ, MIX_WIDTH), BF16),
        compiler_params=_params("parallel"),
        name="mix_context",
    )(u, da_lambda, da_norm_g.reshape(DEPTH, 1, 2 * DA_DIM), conv_w, conv_b.reshape(DEPTH, 1, SC_WIDTH))


def _lat_mix_kernel(uq_ref, nak_ref, nav_ref, sc_ref, dakv_ref, cnak_ref, cnav_ref, cdak_ref, cdav_ref,
                    bias_ref, lam_ref, gain_ref, cw_ref, cb_ref, o_ref, kall_ref, vall_ref, cnav1_ref, *, lam_init):
    qt = pl.program_id(1)
    n_qt = pl.num_programs(1)
    lam = _lambda(lam_ref, lam_init)
    gain = gain_ref[...]

    @pl.when(qt == 0)
    def _():
        kall_ref[:DEC_SEQ, :] = dakv_ref[:, :DA_QK_WIDTH]
        kall_ref[DEC_SEQ:, :] = cdak_ref[...]
        for j in range(DA_HEADS):
            vs = slice(j * 2 * DA_DIM, (j + 1) * 2 * DA_DIM)
            cols = slice(j * 4 * DA_DIM, (j + 1) * 4 * DA_DIM)
            vall_ref[:DEC_SEQ, cols] = _with_ones(dakv_ref[:, DA_QK_WIDTH + j * 2 * DA_DIM:DA_QK_WIDTH + (j + 1) * 2 * DA_DIM])
            vall_ref[DEC_SEQ:, cols] = _with_ones(cdav_ref[:, vs])
        for h in range(NA_HEADS):
            cnav1_ref[:, h * 2 * HEAD_DIM:(h + 1) * 2 * HEAD_DIM] = _with_ones(cnav_ref[:, h * HEAD_DIM:(h + 1) * HEAD_DIM])

    outs = []
    key_row0 = jnp.clip(qt * QT_ROWS - NA_WIN_ROWS // 2, 0, GRID_ROWS - NA_KEY_ROWS)
    k0 = pl.multiple_of(key_row0 * GRID_W, GRID_W)
    jobs = []
    for h in range(NA_HEADS):
        hs = slice(h * HEAD_DIM, (h + 1) * HEAD_DIM)
        q = uq_ref[:, OFF_NA_Q + h * HEAD_DIM:OFF_NA_Q + (h + 1) * HEAD_DIM]
        k_loc = nak_ref[pl.ds(k0, NA_KEYS), hs]
        v_ones = jnp.concatenate([_with_ones(nav_ref[pl.ds(k0, NA_KEYS), hs]),
                                  cnav1_ref[:, h * 2 * HEAD_DIM:(h + 1) * 2 * HEAD_DIM]], axis=0)
        s_loc = _dot_nt(q, k_loc) + bias_ref[h]
        s_ctx = _dot_nt(q, cnak_ref[:, hs])
        jobs.append(([s_loc, s_ctx], v_ones, HEAD_DIM))
    outs += _softmax_pv_each(jobs)
    t0 = pl.multiple_of(qt * QT, QT)
    halo = 16
    before = sc_ref[pl.ds(pl.multiple_of(jnp.maximum(t0 - halo, 0), halo), halo), :].astype(F32)
    after = sc_ref[pl.ds(pl.multiple_of(jnp.minimum(t0 + QT, DEC_SEQ - halo), halo), halo), :].astype(F32)
    cur = sc_ref[pl.ds(t0, QT), :].astype(F32)
    v_prev = before[halo - 1:halo, SC_WIDTH:2 * SC_WIDTH] * before[halo - 1:halo, 2 * SC_WIDTH:]
    v_next = after[0:1, SC_WIDTH:2 * SC_WIDTH] * after[0:1, 2 * SC_WIDTH:]
    v_prev = jnp.where(qt > 0, v_prev, 0.0)
    v_next = jnp.where(qt < n_qt - 1, v_next, 0.0)
    outs.append(_gated_conv(cur[:, :SC_WIDTH], cur[:, SC_WIDTH:2 * SC_WIDTH] * cur[:, 2 * SC_WIDTH:],
                            v_prev, v_next, cw_ref[...], cb_ref[...]))
    for j0 in range(0, DA_HEADS, DA_HEADS_PER_STAGE):
        jobs = []
        for j in range(j0, j0 + DA_HEADS_PER_STAGE):
            v_ones = vall_ref[:, j * 4 * DA_DIM:(j + 1) * 4 * DA_DIM]
            for mi in (2 * j, 2 * j + 1):
                q = uq_ref[:, OFF_DA_Q + mi * DA_DIM:OFF_DA_Q + (mi + 1) * DA_DIM]
                jobs.append(([_dot_nt(q, kall_ref[:, mi * DA_DIM:(mi + 1) * DA_DIM])], v_ones, 2 * DA_DIM))
        o = _softmax_pv_each(jobs)
        for t in range(DA_HEADS_PER_STAGE):
            outs.append(_diff_head_norm(o[2 * t], o[2 * t + 1], lam, gain, lam_init))
    o_ref[...] = jnp.concatenate(outs, axis=-1).astype(BF16)


def _lat_mix(u, caches, bias_tab, da_lambda, da_norm_g, conv_w, conv_b, l, lam_init):
    cnak, cnav, cdak, cdav = caches
    n_qt = DEC_SEQ // QT
    q0 = BATCH * SEQ // QT
    b0 = BATCH * SEQ // DEC_SEQ

    def bias_map(b, qt):
        return (l, 0, jnp.where(qt == 0, 0, jnp.where(qt == n_qt - 1, 2, 1)), 0, 0)

    return pl.pallas_call(
        functools.partial(_lat_mix_kernel, lam_init=lam_init),
        grid=(DEC_BATCH, n_qt),
        in_specs=[
            pl.BlockSpec((QT, IN_WIDTH), lambda b, qt: (q0 + b * n_qt + qt, 0)),
            pl.BlockSpec((DEC_SEQ, NA_WIDTH), lambda b, qt: (b0 + b, OFF_NA_K // NA_WIDTH)),
            pl.BlockSpec((DEC_SEQ, NA_WIDTH), lambda b, qt: (b0 + b, OFF_NA_V // NA_WIDTH)),
            pl.BlockSpec((DEC_SEQ, 3 * SC_WIDTH), lambda b, qt: (b0 + b, OFF_SC_B // (3 * SC_WIDTH))),
            pl.BlockSpec((DEC_SEQ, DA_QK_WIDTH + DA_V_WIDTH), lambda b, qt: (b0 + b, OFF_DA_K // (DA_QK_WIDTH + DA_V_WIDTH))),
            pl.BlockSpec((None, None, PAST_LEN, NA_WIDTH), lambda b, qt: (b, l, 0, 0)),
            pl.BlockSpec((None, None, PAST_LEN, NA_WIDTH), lambda b, qt: (b, l, 0, 0)),
            pl.BlockSpec((None, None, PAST_LEN, DA_QK_WIDTH), lambda b, qt: (b, l, 0, 0)),
            pl.BlockSpec((None, None, PAST_LEN, DA_V_WIDTH), lambda b, qt: (b, l, 0, 0)),
            pl.BlockSpec((None, NA_HEADS, None, QT, NA_KEYS), bias_map),
            pl.BlockSpec((None, 4, DA_DIM), lambda b, qt: (l, 0, 0)),
            pl.BlockSpec((None, 1, 2 * DA_DIM), lambda b, qt: (l, 0, 0)),
            pl.BlockSpec((None, 3, SC_WIDTH), lambda b, qt: (l, 0, 0)),
            pl.BlockSpec((None, 1, SC_WIDTH), lambda b, qt: (l, 0, 0)),
        ],
        out_specs=pl.BlockSpec((QT, MIX_WIDTH), lambda b, qt: (b * n_qt + qt, 0)),
        out_shape=jax.ShapeDtypeStruct((DEC_BATCH * DEC_SEQ, MIX_WIDTH), BF16),
        scratch_shapes=[
            pltpu.VMEM((DEC_SEQ + PAST_LEN, DA_QK_WIDTH), BF16),
            pltpu.VMEM((DEC_SEQ + PAST_LEN, 2 * DA_V_WIDTH), BF16),
            pltpu.VMEM((PAST_LEN, 2 * NA_WIDTH), BF16),
        ],
        compiler_params=_params("parallel", "arbitrary"),
        name="mix_latent",
    )(u, u, u, u, u, cnak, cnav, cdak, cdav, bias_tab, da_lambda,
      da_norm_g.reshape(DEPTH, 1, 2 * DA_DIM), conv_w, conv_b.reshape(DEPTH, 1, SC_WIDTH))


def _first_max(vals):
    m = functools.reduce(jnp.maximum, vals)
    hot, taken = [], None
    for v in vals:
        is_max = v == m
        if taken is None:
            hot.append(is_max)
            taken = is_max
        else:
            hot.append(is_max & ~taken)
            taken = taken | is_max
    return hot, m


def _pick(hot, vals):
    out = vals[-1]
    for h, v in zip(hot[-2::-1], vals[-2::-1]):
        out = jnp.where(h, v, out)
    return out


def _route(h2, h2_bf, wr_ref, br_ref):
    tm = h2.shape[0]
    w = wr_ref[...]
    w_hi = w.astype(BF16)
    w_lo = (w - w_hi.astype(F32)).astype(BF16)
    h_lo = (h2 - h2_bf.astype(F32)).astype(BF16)
    both = _dot_nt(jnp.concatenate([w_hi, w_lo], axis=0), h2_bf)
    z = both[:N_EXPERTS] + both[N_EXPERTS:] + _dot_nt(w_hi, h_lo)
    scores = jax.nn.sigmoid(z)
    biased = scores + br_ref[...]
    P = [biased[k * N_GROUPS:(k + 1) * N_GROUPS] for k in range(PER_GROUP)]
    S = [scores[k * N_GROUPS:(k + 1) * N_GROUPS] for k in range(PER_GROUP)]
    pair_sums = [P[i] + P[j] for i in range(PER_GROUP) for j in range(i + 1, PER_GROUP)]
    group_score = functools.reduce(jnp.maximum, pair_sums)
    sel, _ = _first_max([group_score[g:g + 1] for g in range(N_GROUPS)])
    c = [_pick(sel, [P[k][g:g + 1] for g in range(N_GROUPS)]) for k in range(PER_GROUP)]
    cs = [_pick(sel, [S[k][g:g + 1] for g in range(N_GROUPS)]) for k in range(PER_GROUP)]
    t1, _ = _first_max(c)
    t2, _ = _first_max([jnp.where(t, -jnp.inf, v) for t, v in zip(t1, c)])
    w1 = functools.reduce(jnp.add, [jnp.where(t, v, 0.0) for t, v in zip(t1, cs)])
    w2 = functools.reduce(jnp.add, [jnp.where(t, v, 0.0) for t, v in zip(t2, cs)])
    total = w1 + w2
    slot_gate = [jnp.where(a, w1 / total, jnp.where(b, w2 / total, 0.0)) for a, b in zip(t1, t2)]
    onehot = jnp.concatenate([s.astype(F32) for s in sel] + [jnp.zeros((8 - N_GROUPS, tm), F32)], axis=0)
    earlier = (lax.broadcasted_iota(jnp.int32, (tm, tm), 0) < lax.broadcasted_iota(jnp.int32, (tm, tm), 1))
    rank = _dot(onehot.astype(BF16), earlier.astype(BF16))
    count = jnp.sum(onehot, axis=1, keepdims=True)
    n_chunks = jnp.floor((count + (MOE_CHUNK - 1)) * (1.0 / MOE_CHUNK))
    pos = jnp.zeros((1, tm), F32)
    start = jnp.zeros((1, 1), F32)
    for g in range(N_GROUPS):
        pos = jnp.where(sel[g], start + rank[g:g + 1], pos)
        start = start + n_chunks[g:g + 1] * MOE_CHUNK
    rows = [pos] + slot_gate + [jnp.zeros((128 - 1 - PER_GROUP, tm), F32)]
    return jnp.concatenate(rows, axis=0), n_chunks


def _post_kernel(mixp_ref, mixs_ref, xp_ref, xs_ref, mod_ref, wout_ref, g_ref, b_ref, wr_ref, br_ref,
                 x1_ref, h2_ref, tok_ref, pos_ref, nch_ref, wbf_ref):
    i = pl.program_id(0)
    is_ctx = i < CTX_TILES

    @pl.when(i == 0)
    def _():
        wbf_ref[...] = wout_ref[...].astype(BF16)

    mod = mod_ref[...]
    n = TM_POST // POST_SPLIT
    parts = [slice(p * n, (p + 1) * n) for p in range(POST_SPLIT)]
    ys = [_dot(jnp.where(is_ctx, mixp_ref[p, :], mixs_ref[p, :]), wbf_ref[...]) for p in parts]
    xs = [jnp.where(is_ctx, xp_ref[p, :], xs_ref[p, :]) for p in parts]
    x1s = [t * g_ref[...] + b_ref[...] for t in _ln_each([ALPHA * x + mod[2:3] * y for x, y in zip(xs, ys)])]
    h2s = [t * (1.0 + mod[4:5]) + mod[3:4] for t in _ln_each(x1s)]
    for p, x1 in zip(parts, x1s):
        x1_ref[p, :] = x1
    h2 = jnp.concatenate(h2s, axis=0)
    h2_bf = h2.astype(BF16)
    h2_ref[...] = h2_bf
    rows, n_chunks = _route(h2, h2_bf, wr_ref, br_ref)
    tok_ref[...] = rows.T
    pos_ref[...] = rows[0:8]
    nch_ref[...] = jnp.broadcast_to(n_chunks, (8, 128)).astype(jnp.int32)


def _post(mix_p, mix_s, xp, xs, mod_l, w_out, ln_g, ln_b, wr_t, br_t, l):
    n = xp.shape[0] + xs.shape[0]
    route = functools.partial(_tile_stream, n_ctx_tiles=CTX_TILES, lat_tiles_per_batch=DEC_SEQ // TM_POST)
    return pl.pallas_call(
        _post_kernel,
        grid=(n // TM_POST,),
        in_specs=[
            pl.BlockSpec((TM_POST, MIX_WIDTH), lambda i: (route(i)[0], 0)),
            pl.BlockSpec((TM_POST, MIX_WIDTH), lambda i: (route(i)[1], 0)),
            pl.BlockSpec((TM_POST, D_MODEL), lambda i: (route(i)[0], 0)),
            pl.BlockSpec((TM_POST, D_MODEL), lambda i: (route(i)[1], 0)),
            pl.BlockSpec((None, 6, D_MODEL), lambda i: (route(i)[2], 0, 0)),
            pl.BlockSpec((None, MIX_WIDTH, D_MODEL), lambda i: (l, 0, 0), pipeline_mode=pl.Buffered(1)),
            pl.BlockSpec((None, 1, D_MODEL), lambda i: (l, 0, 0)),
            pl.BlockSpec((None, 1, D_MODEL), lambda i: (l, 0, 0)),
            pl.BlockSpec((N_EXPERTS, D_MODEL), lambda i: (0, 0)),
            pl.BlockSpec((N_EXPERTS, 1), lambda i: (0, 0)),
        ],
        out_specs=[
            pl.BlockSpec((TM_POST, D_MODEL), lambda i: (i, 0)),
            pl.BlockSpec((TM_POST, D_MODEL), lambda i: (i, 0)),
            pl.BlockSpec((TM_POST, 128), lambda i: (i, 0)),
            pl.BlockSpec((None, 8, TM_POST), lambda i: (i, 0, 0)),
            pl.BlockSpec((None, 8, 128), lambda i: (i, 0, 0)),
        ],
        out_shape=[
            jax.ShapeDtypeStruct((n, D_MODEL), F32),
            jax.ShapeDtypeStruct((n, D_MODEL), BF16),
            jax.ShapeDtypeStruct((n, 128), F32),
            jax.ShapeDtypeStruct((n // TM_POST, 8, TM_POST), F32),
            jax.ShapeDtypeStruct((n // TM_POST, 8, 128), jnp.int32),
        ],
        scratch_shapes=[pltpu.VMEM((MIX_WIDTH, D_MODEL), BF16)],
        compiler_params=_params("arbitrary"),
        name="post",
    )(mix_p, mix_s, xp, xs, mod_l, w_out, ln_g.reshape(DEPTH, 1, D_MODEL), ln_b.reshape(DEPTH, 1, D_MODEL), wr_t, br_t)


def _moe_kernel(nch_ref, h_ref, tok_ref, pos_ref, x1_ref, mod_ref, w1f_ref, w3f_ref, w2f_ref, g_ref, b_ref,
                op_ref, os_ref, w13_ref, w2_ref, hs_ref, gs_ref, ys_ref):
    step = pl.program_id(0)
    half = PER_GROUP * D_EXPERT

    for k in range(PER_GROUP):
        @pl.when((step < N_EXPERTS) & (step % PER_GROUP == k))
        def _(k=k):
            g = step // PER_GROUP
            w13_ref[g, :, k * D_EXPERT:(k + 1) * D_EXPERT] = w1f_ref[...].astype(BF16)
            w13_ref[g, :, half + k * D_EXPERT:half + (k + 1) * D_EXPERT] = w3f_ref[...].astype(BF16)
            w2_ref[g, k * D_EXPERT:(k + 1) * D_EXPERT, :] = w2f_ref[...].astype(BF16)

    @pl.when(step >= N_EXPERTS)
    def _():
        i = step - N_EXPERTS
        out = _moe_tile(i, nch_ref, h_ref, tok_ref, pos_ref, x1_ref, mod_ref, w13_ref, w2_ref, g_ref, b_ref,
                        hs_ref, gs_ref, ys_ref)

        @pl.when(i < CTX_TILES)
        def _():
            op_ref[...] = out

        @pl.when(i >= CTX_TILES)
        def _():
            os_ref[...] = out


def _moe_tile(i, nch_ref, h_ref, tok_ref, pos_ref, x1_ref, mod_ref, w13_ref, w2_ref, g_ref, b_ref,
              hs_ref, gs_ref, ys_ref):
    tok = tok_ref[...]
    pos_lane = pos_ref[0:1, :]
    pos_col = tok[:, 0:1]
    slot = lax.broadcasted_iota(jnp.int32, (MOE_SORTED, TM_MOE), 0).astype(F32)
    sort = (slot == pos_lane).astype(BF16)
    tok_hi = tok.astype(BF16)
    tok_lo = (tok - tok_hi.astype(F32)).astype(BF16)
    z = _dot(sort, jnp.concatenate([h_ref[...], tok_hi, tok_lo], axis=1))
    hs_ref[...] = z[:, :D_MODEL].astype(BF16)
    gs_ref[...] = z[:, D_MODEL:D_MODEL + 128] + z[:, D_MODEL + 128:]
    ys_ref[...] = jnp.zeros_like(ys_ref)

    def experts(g, chunk0, n_rows):
        r0 = pl.multiple_of(chunk0 * MOE_CHUNK, MOE_CHUNK)
        rows = hs_ref[pl.ds(r0, n_rows), :]
        gates = gs_ref[pl.ds(r0, n_rows), :]
        half = PER_GROUP * D_EXPERT
        ab = _dot(rows, w13_ref[g])
        hid = []
        for k in range(PER_GROUP):
            a = ab[:, k * D_EXPERT:(k + 1) * D_EXPERT]
            b = ab[:, half + k * D_EXPERT:half + (k + 1) * D_EXPERT]
            hid.append((a * jax.nn.sigmoid(a) * b * gates[:, 1 + k:2 + k]).astype(BF16))
        ys_ref[pl.ds(r0, n_rows), :] = _dot(jnp.concatenate(hid, axis=1), w2_ref[g]).astype(BF16)

    def group(g, first):
        n = nch_ref[i * N_GROUPS + g]
        n_blocks = n // MOE_BLOCK
        rest = n - n_blocks * MOE_BLOCK

        def block(c, carry):
            experts(g, first + c * MOE_BLOCK, MOE_BLOCK * MOE_CHUNK)
            return carry

        lax.fori_loop(0, n_blocks, block, 0)
        for m in range(1, MOE_BLOCK):
            @pl.when(rest == m)
            def _(m=m):
                experts(g, first + n_blocks * MOE_BLOCK, m * MOE_CHUNK)
        return first + n

    lax.fori_loop(0, N_GROUPS, group, 0)

    unsort = (lax.broadcasted_iota(jnp.int32, (TM_MOE, MOE_SORTED), 1).astype(F32) == pos_col).astype(BF16)
    y = _dot(unsort, ys_ref[...])
    mod = mod_ref[...]
    return _ln(ALPHA * x1_ref[...] + mod[5:6] * y) * g_ref[...] + b_ref[...]


def _moe(h2, tok, pos, nch, x1, mod_l, w1, w3, w2, ln_g, ln_b, l):
    n_tiles = x1.shape[0] // TM_MOE
    n_ctx = CTX_TILES * TM_MOE
    per_batch = DEC_SEQ // TM_MOE

    def tile(step):
        return jnp.maximum(step - N_EXPERTS, 0)

    def expert(step):
        return jnp.minimum(step, N_EXPERTS - 1)

    def mod_map(step, s):
        return (_tile_stream(tile(step), CTX_TILES, per_batch)[2], 0, 0)

    grid_spec = pltpu.PrefetchScalarGridSpec(
        num_scalar_prefetch=1,
        grid=(N_EXPERTS + n_tiles,),
        in_specs=[
            pl.BlockSpec((TM_MOE, D_MODEL), lambda step, s: (tile(step), 0)),
            pl.BlockSpec((TM_MOE, 128), lambda step, s: (tile(step), 0)),
            pl.BlockSpec((None, 8, TM_MOE), lambda step, s: (tile(step), 0, 0)),
            pl.BlockSpec((TM_MOE, D_MODEL), lambda step, s: (tile(step), 0)),
            pl.BlockSpec((None, 6, D_MODEL), mod_map),
            pl.BlockSpec((None, None, D_MODEL, D_EXPERT), lambda step, s: (l, expert(step), 0, 0)),
            pl.BlockSpec((None, None, D_MODEL, D_EXPERT), lambda step, s: (l, expert(step), 0, 0)),
            pl.BlockSpec((None, None, D_EXPERT, D_MODEL), lambda step, s: (l, expert(step), 0, 0)),
            pl.BlockSpec((None, 1, D_MODEL), lambda step, s: (l, 0, 0)),
            pl.BlockSpec((None, 1, D_MODEL), lambda step, s: (l, 0, 0)),
        ],
        out_specs=[
            pl.BlockSpec((TM_MOE, D_MODEL), lambda step, s: (jnp.minimum(tile(step), CTX_TILES - 1), 0)),
            pl.BlockSpec((TM_MOE, D_MODEL), lambda step, s: (jnp.maximum(tile(step) - CTX_TILES, 0), 0)),
        ],
        scratch_shapes=[
            pltpu.VMEM((N_GROUPS, D_MODEL, 2 * PER_GROUP * D_EXPERT), BF16),
            pltpu.VMEM((N_GROUPS, PER_GROUP * D_EXPERT, D_MODEL), BF16),
            pltpu.VMEM((MOE_SORTED, D_MODEL), BF16),
            pltpu.VMEM((MOE_SORTED, 128), F32),
            pltpu.VMEM((MOE_SORTED, D_MODEL), BF16),
        ],
    )
    return pl.pallas_call(
        _moe_kernel,
        grid_spec=grid_spec,
        out_shape=[jax.ShapeDtypeStruct((n_ctx, D_MODEL), F32),
                   jax.ShapeDtypeStruct((x1.shape[0] - n_ctx, D_MODEL), F32)],
        compiler_params=_params("arbitrary"),
        name="moe",
    )(nch[:, :N_GROUPS, 0].reshape(-1), h2, tok, pos, x1, mod_l, w1, w3, w2,
      ln_g.reshape(DEPTH, 1, D_MODEL), ln_b.reshape(DEPTH, 1, D_MODEL))


def _rope_tables():
    t = np.arange(DEC_SEQ)
    half = DA_DIM // 4
    inv_freq = ROPE_BASE ** (-np.arange(half, dtype=np.float32) / half)
    cos, sin = [], []
    for pos in (t // GRID_W, t % GRID_W):
        ang = pos.astype(np.float32)[:, None] * inv_freq[None, :]
        c, s = np.cos(ang), np.sin(ang)
        cos += [c, c]
        sin += [-s, s]
    cos = np.tile(np.concatenate(cos, axis=1), (1, 2 * DA_HEADS))
    sin = np.tile(np.concatenate(sin, axis=1), (1, 2 * DA_HEADS))
    cos = np.concatenate([np.ones((TM, DA_QK_WIDTH), np.float32), cos], axis=0)
    sin = np.concatenate([np.zeros((TM, DA_QK_WIDTH), np.float32), sin], axis=0)
    return jnp.asarray(cos, F32), jnp.asarray(sin, F32)


_NA_TILE_KINDS = ((0, 0), (2 * QT_ROWS, 2 * QT_ROWS - NA_WIN_ROWS // 2), (GRID_ROWS - QT_ROWS, GRID_ROWS - NA_KEY_ROWS))


def _na_bias_kernel(by_ref, o_ref):
    qc = lax.broadcasted_iota(jnp.int32, (GRID_W, GRID_W), 0)
    kc = lax.broadcasted_iota(jnp.int32, (GRID_W, GRID_W), 1)
    win_c0 = jnp.clip(qc - NA_WIN_COLS // 2, 0, GRID_W - NA_WIN_COLS)
    in_cols = (kc >= win_c0) & (kc < win_c0 + NA_WIN_COLS)
    masked = jnp.full((GRID_W, GRID_W), -jnp.inf, F32)
    for kind, (r0, key_row0) in enumerate(_NA_TILE_KINDS):
        for a in range(QT_ROWS):
            qr = r0 + a
            win_r0 = min(max(qr - NA_WIN_ROWS // 2, 0), GRID_ROWS - NA_WIN_ROWS)
            blocks = []
            for b in range(NA_KEY_ROWS):
                kr = key_row0 + b
                if win_r0 <= kr < win_r0 + NA_WIN_ROWS:
                    blocks.append(jnp.where(in_cols, by_ref[kr - qr + NA_WIN_ROWS - 1] * LOG2E, masked))
                else:
                    blocks.append(masked)
            o_ref[kind, a * GRID_W:(a + 1) * GRID_W, :] = jnp.concatenate(blocks, axis=1)


def _na_bias_tables(na_rel_bias):
    n_dr, n_dc = 2 * NA_WIN_ROWS - 1, 2 * NA_WIN_COLS - 1
    cols = np.arange(GRID_W)
    d_col = np.clip(cols[None, :] - cols[:, None], 1 - NA_WIN_COLS, NA_WIN_COLS - 1) + NA_WIN_COLS - 1
    col_sel = (d_col[None] == np.arange(n_dc)[:, None, None]).astype(np.float32)
    by_col = jnp.einsum('lhab,bqc->lhaqc', na_rel_bias.astype(F32), col_sel, precision=lax.Precision.HIGHEST)
    return pl.pallas_call(
        _na_bias_kernel,
        grid=(DEPTH, NA_HEADS),
        in_specs=[pl.BlockSpec((None, None, n_dr, GRID_W, GRID_W), lambda l, h: (l, h, 0, 0, 0))],
        out_specs=pl.BlockSpec((None, None, len(_NA_TILE_KINDS), QT, NA_KEYS), lambda l, h: (l, h, 0, 0, 0)),
        out_shape=jax.ShapeDtypeStruct((DEPTH, NA_HEADS, len(_NA_TILE_KINDS), QT, NA_KEYS), F32),
        compiler_params=_params("parallel", "parallel"),
        name="na_bias_table",
    )(by_col)


def kernel(x_prompt, x_sample, cache_na_k, cache_na_v, cache_da_k, cache_da_v, c, c_ctx, w_mod, b_mod, w_in,
           na_rel_bias, sc_conv_w, sc_conv_b, da_lambda, da_norm_g, w_out, ln1_g, ln1_b, w_router, b_router,
           moe_w1, moe_w3, moe_w2, ln2_g, ln2_b):
    xp = x_prompt.reshape(BATCH * SEQ, D_MODEL)
    xs = x_sample.reshape(DEC_BATCH * DEC_SEQ, D_MODEL)

    cond = jnp.concatenate([c_ctx[None, :], c, jnp.zeros((N_COND - 1 - DEC_BATCH, D_MODEL), F32)], axis=0)
    mod = _modulation(cond, w_mod, b_mod).reshape(DEPTH, N_COND, 6, D_MODEL)

    slot_major = np.arange(N_EXPERTS).reshape(N_GROUPS, PER_GROUP).T.reshape(-1)
    wr_t = w_router.T[slot_major]
    br_t = b_router.astype(F32)[slot_major].reshape(N_EXPERTS, 1)

    caches = (
        cache_na_k.reshape(DEC_BATCH, DEPTH, PAST_LEN, NA_WIDTH).astype(BF16),
        cache_na_v.reshape(DEC_BATCH, DEPTH, PAST_LEN, NA_WIDTH).astype(BF16),
        cache_da_k.reshape(DEC_BATCH, DEPTH, PAST_LEN, DA_QK_WIDTH).astype(BF16),
        cache_da_v.reshape(DEC_BATCH, DEPTH, PAST_LEN, DA_V_WIDTH).astype(BF16),
    )
    rope_tabs = _rope_tables()
    bias_tab = _na_bias_tables(na_rel_bias)

    new_caches = None
    for l in range(DEPTH):
        lam_init = 0.8 - 0.6 * math.exp(-0.3 * l)
        mod_l = mod[l]
        u, *new_caches = _inproj(xp, xs, mod_l, w_in, l, rope_tabs, cache_bufs=new_caches)
        mix_p = _ctx_mix(u, da_lambda, da_norm_g, sc_conv_w, sc_conv_b, l, lam_init)
        mix_s = _lat_mix(u, caches, bias_tab, da_lambda, da_norm_g, sc_conv_w, sc_conv_b, l, lam_init)
        x1, h2, *route = _post(mix_p, mix_s, xp, xs, mod_l, w_out, ln1_g, ln1_b, wr_t, br_t, l)
        xp, xs = _moe(h2, *route, x1, mod_l, moe_w1, moe_w3, moe_w2, ln2_g, ln2_b, l)

    nak, nav, dak, dav = new_caches
    return (xp.reshape(BATCH, SEQ, D_MODEL), xs.reshape(DEC_BATCH, DEC_SEQ, D_MODEL),
            nak.reshape(BATCH, DEPTH, SEQ, NA_HEADS, HEAD_DIM), nav.reshape(BATCH, DEPTH, SEQ, NA_HEADS, HEAD_DIM),
            dak.reshape(BATCH, DEPTH, SEQ, 2 * DA_HEADS, DA_DIM), dav.reshape(BATCH, DEPTH, SEQ, DA_HEADS, 2 * DA_DIM))
```

```python
import functools
import math

import numpy as np
import jax
import jax.numpy as jnp
from jax import lax
from jax.experimental import pallas as pl
from jax.experimental.pallas import tpu as pltpu

F32 = jnp.float32
BF16 = jnp.bfloat16

D_MODEL = 1024
BATCH = 16
SEQ = 256
DEPTH = 4
DEC_BATCH = 2
DEC_SEQ = 2048
PAST_LEN = 512
GRID_W = 64
GRID_ROWS = DEC_SEQ // GRID_W
HEAD_DIM = 64
NA_HEADS = 4
NA_WIDTH = NA_HEADS * HEAD_DIM
NA_WIN_ROWS = 8
NA_WIN_COLS = 16
SC_WIDTH = 256
DA_HEADS = 4
DA_DIM = 64
DA_QK_WIDTH = 2 * DA_HEADS * DA_DIM
DA_V_WIDTH = DA_HEADS * 2 * DA_DIM
MIX_WIDTH = NA_WIDTH + SC_WIDTH + DA_V_WIDTH
IN_WIDTH = 3 * NA_WIDTH + 3 * SC_WIDTH + 2 * DA_QK_WIDTH + DA_V_WIDTH
OFF_NA_Q = 0
OFF_NA_K = NA_WIDTH
OFF_NA_V = 2 * NA_WIDTH
OFF_SC_B = 3 * NA_WIDTH
OFF_SC_C = OFF_SC_B + SC_WIDTH
OFF_SC_X = OFF_SC_C + SC_WIDTH
OFF_DA_Q = OFF_SC_X + SC_WIDTH
OFF_DA_K = OFF_DA_Q + DA_QK_WIDTH
OFF_DA_V = OFF_DA_K + DA_QK_WIDTH
ROPE_BASE = 10000.0
N_EXPERTS = 16
N_GROUPS = 4
PER_GROUP = N_EXPERTS // N_GROUPS
D_EXPERT = 256
ALPHA = (2 * DEPTH) ** 0.25
LN_EPS = 1e-5
LOG2E = math.log2(math.e)
Q_SCALE = HEAD_DIM ** -0.5 * LOG2E

N_COND = 8
TM = 512
SEQ_PER_TILE = TM // SEQ
CTX_TILES = BATCH * SEQ // TM
CTX_SEQ_PER_STEP = 2
DA_HEADS_PER_STAGE = 2
POST_SPLIT = 2
TM_MOE = 512
TM_POST = TM_MOE
MOE_CHUNK = 32
MOE_BLOCK = 8
MOE_SORTED = TM_MOE + (N_GROUPS - 1) * MOE_CHUNK
assert TM == TM_MOE
QT = 256
QT_ROWS = QT // GRID_W
NA_KEY_ROWS = 12
NA_KEYS = NA_KEY_ROWS * GRID_W
VMEM_LIMIT = 60 * 1024 * 1024


def _dot(a, b):
    return jnp.dot(a, b, preferred_element_type=F32)


def _dot_nt(a, b):
    return lax.dot_general(a, b, (((1,), (1,)), ((), ())), preferred_element_type=F32)


def _ln(x):
    mu = jnp.mean(x, -1, keepdims=True)
    xc = x - mu
    var = jnp.mean(xc * xc, -1, keepdims=True)
    return xc * lax.rsqrt(var + LN_EPS)


def _ln_each(xs):
    mus = [jnp.mean(x, -1, keepdims=True) for x in xs]
    xcs = [x - mu for x, mu in zip(xs, mus)]
    vs = [jnp.mean(xc * xc, -1, keepdims=True) for xc in xcs]
    return [xc * lax.rsqrt(v + LN_EPS) for xc, v in zip(xcs, vs)]


def _params(*sem):
    return pltpu.CompilerParams(dimension_semantics=sem, vmem_limit_bytes=VMEM_LIMIT)


def _mod_kernel(cond_ref, w_ref, b_ref, o_ref):
    c = cond_ref[...]
    s = (c * jax.nn.sigmoid(c)).astype(BF16)
    o_ref[...] = _dot(s, w_ref[...].astype(BF16)) + b_ref[...]


def _modulation(cond, w_mod, b_mod):
    tn = 1024
    return pl.pallas_call(
        _mod_kernel,
        grid=(DEPTH, 6 * D_MODEL // tn),
        in_specs=[
            pl.BlockSpec((N_COND, D_MODEL), lambda l, j: (0, 0)),
            pl.BlockSpec((None, D_MODEL, tn), lambda l, j: (l, 0, j)),
            pl.BlockSpec((None, 1, tn), lambda l, j: (l, 0, j)),
        ],
        out_specs=pl.BlockSpec((None, N_COND, tn), lambda l, j: (l, 0, j)),
        out_shape=jax.ShapeDtypeStruct((DEPTH, N_COND, 6 * D_MODEL), F32),
        compiler_params=_params("parallel", "parallel"),
        name="modulation",
    )(cond, w_mod, b_mod.reshape(DEPTH, 1, 6 * D_MODEL))


def _rope(t, cos, sin):
    lane = lax.broadcasted_iota(jnp.int32, t.shape, 1)
    first = (lane // 16) % 2 == 0
    n = t.shape[1]
    swapped = jnp.where(first, pltpu.roll(t, n - 16, 1), pltpu.roll(t, 16, 1))
    return t * cos + swapped * sin


def _tile_stream(i, n_ctx_tiles, lat_tiles_per_batch):
    lat = jnp.maximum(i - n_ctx_tiles, 0)
    return jnp.minimum(i, n_ctx_tiles - 1), lat, jnp.where(i < n_ctx_tiles, 0, 1 + lat // lat_tiles_per_batch)


def _inproj_kernel(*refs):
    xp_ref, xs_ref, mod_ref, w_ref, cos_ref, sin_ref = refs[:6]
    u_ref, nak_ref, nav_ref, dak_ref, dav_ref, wbf_ref = refs[-6:]
    i = pl.program_id(0)
    is_ctx = i < CTX_TILES

    @pl.when(i == 0)
    def _():
        wbf_ref[...] = w_ref[...].astype(BF16)

    mod = mod_ref[...]
    parts = [slice(s * SEQ, (s + 1) * SEQ) for s in range(SEQ_PER_TILE)]
    hs = [t * (1.0 + mod[1:2]) + mod[0:1]
          for t in _ln_each([jnp.where(is_ctx, xp_ref[p, :], xs_ref[p, :]) for p in parts])]
    us = [_dot(h.astype(BF16), wbf_ref[...]) for h in hs]
    for p, u in zip(parts, us):
        cos, sin = cos_ref[p, :], sin_ref[p, :]
        da_q = _rope(u[:, OFF_DA_Q:OFF_DA_K], cos, sin)
        da_k = _rope(u[:, OFF_DA_K:OFF_DA_V], cos, sin)
        u_ref[p, :OFF_NA_K] = (u[:, :OFF_NA_K] * Q_SCALE).astype(BF16)
        u_ref[p, OFF_NA_K:OFF_DA_Q] = u[:, OFF_NA_K:OFF_DA_Q].astype(BF16)
        u_ref[p, OFF_DA_Q:OFF_DA_K] = (da_q * Q_SCALE).astype(BF16)
        u_ref[p, OFF_DA_K:OFF_DA_V] = da_k.astype(BF16)
        u_ref[p, OFF_DA_V:] = u[:, OFF_DA_V:].astype(BF16)

    @pl.when(is_ctx)
    def _():
        for s, u in enumerate(us):
            nak_ref[s] = u[:, OFF_NA_K:OFF_NA_V]
            nav_ref[s] = u[:, OFF_NA_V:OFF_SC_B]
            dak_ref[s] = u[:, OFF_DA_K:OFF_DA_V]
            for j in range(DA_HEADS):
                dav_ref[s, pl.ds(j, SEQ, stride=DA_HEADS), :] = (
                    u[:, OFF_DA_V + j * 2 * DA_DIM:OFF_DA_V + (j + 1) * 2 * DA_DIM])


def _inproj(xp, xs, mod_l, w_in, l, rope_tabs, cache_bufs=None):
    route = functools.partial(_tile_stream, n_ctx_tiles=CTX_TILES, lat_tiles_per_batch=DEC_SEQ // TM)
    in_specs = [
        pl.BlockSpec((TM, D_MODEL), lambda i: (route(i)[0], 0)),
        pl.BlockSpec((TM, D_MODEL), lambda i: (route(i)[1], 0)),
        pl.BlockSpec((None, 6, D_MODEL), lambda i: (route(i)[2], 0, 0)),
        pl.BlockSpec((None, D_MODEL, IN_WIDTH), lambda i: (l, 0, 0), pipeline_mode=pl.Buffered(1)),
        pl.BlockSpec((TM, DA_QK_WIDTH), lambda i: (jnp.where(i < CTX_TILES, 0, 1 + route(i)[1] % (DEC_SEQ // TM)), 0)),
        pl.BlockSpec((TM, DA_QK_WIDTH), lambda i: (jnp.where(i < CTX_TILES, 0, 1 + route(i)[1] % (DEC_SEQ // TM)), 0)),
    ]
    args = [xp, xs, mod_l, w_in, *rope_tabs]
    out_specs = [pl.BlockSpec((TM, IN_WIDTH), lambda i: (i, 0))]
    out_shape = [jax.ShapeDtypeStruct((xp.shape[0] + xs.shape[0], IN_WIDTH), BF16)]
    for rows, width in ((SEQ, NA_WIDTH), (SEQ, NA_WIDTH), (SEQ, DA_QK_WIDTH), (SEQ * DA_HEADS, 2 * DA_DIM)):
        out_specs.append(pl.BlockSpec((SEQ_PER_TILE, None, rows, width), lambda i: (route(i)[0], l, 0, 0)))
        out_shape.append(jax.ShapeDtypeStruct((BATCH, DEPTH, rows, width), F32))
    aliases = {}
    if cache_bufs is not None:
        aliases = {len(args) + k: 1 + k for k in range(4)}
        in_specs += [pl.BlockSpec(memory_space=pl.ANY)] * 4
        args += list(cache_bufs)
    return pl.pallas_call(
        _inproj_kernel,
        grid=(CTX_TILES + xs.shape[0] // TM,),
        in_specs=in_specs,
        out_specs=out_specs,
        out_shape=out_shape,
        scratch_shapes=[pltpu.VMEM((D_MODEL, IN_WIDTH), BF16)],
        input_output_aliases=aliases,
        compiler_params=_params("arbitrary"),
        name="inproj",
    )(*args)


def _lambda(lam_ref, lam_init):
    lp = lam_ref[...]
    s1 = jnp.sum(lp[0:1] * lp[1:2], axis=-1, keepdims=True)
    s2 = jnp.sum(lp[2:3] * lp[3:4], axis=-1, keepdims=True)
    return jnp.exp(s1) - jnp.exp(s2) + lam_init


def _with_ones(v):
    return jnp.concatenate([v, jnp.ones((v.shape[0], max(v.shape[1], 64)), v.dtype)], axis=1)


def _softmax_pv(score_parts, v_ones, d):
    m = functools.reduce(jnp.maximum, [jnp.max(s, -1, keepdims=True) for s in score_parts])
    e = [jnp.exp2(s - m).astype(BF16) for s in score_parts]
    o = _dot(e[0] if len(e) == 1 else jnp.concatenate(e, axis=1), v_ones)
    return o[:, :d] / o[:, d:d + 1]


def _softmax_pv_each(jobs):
    ms = [functools.reduce(jnp.maximum, [jnp.max(s, -1, keepdims=True) for s in parts]) for parts, _, _ in jobs]
    es = [[jnp.exp2(s - m).astype(BF16) for s in parts] for (parts, _, _), m in zip(jobs, ms)]
    os = [_dot(e[0] if len(e) == 1 else jnp.concatenate(e, axis=1), v) for e, (_, v, _) in zip(es, jobs)]
    return [o[:, :d] / o[:, d:d + 1] for o, (_, _, d) in zip(os, jobs)]


def _diff_head_norm(o1, o2, lam, gain, lam_init):
    o = o1 - lam * o2
    o = o * lax.rsqrt(jnp.mean(o * o, -1, keepdims=True) + LN_EPS)
    return o * gain * (1.0 - lam_init)


def _gated_conv(b, v, v_prev_row, v_next_row, w, bias):
    n = v.shape[0]
    row = lax.broadcasted_iota(jnp.int32, v.shape, 0)
    prev = jnp.where(row == 0, v_prev_row, pltpu.roll(v, 1, 0))
    nxt = jnp.where(row == n - 1, v_next_row, pltpu.roll(v, n - 1, 0))
    return b * (prev * w[0:1] + v * w[1:2] + nxt * w[2:3] + bias)


def _ctx_mix_kernel(u_ref, lam_ref, gain_ref, cw_ref, cb_ref, o_ref, *, lam_init):
    lam = _lambda(lam_ref, lam_init)
    gain = gain_ref[...]
    maps = []
    for s in range(CTX_SEQ_PER_STEP):
        for h in range(NA_HEADS):
            maps.append((s, OFF_NA_Q + h * HEAD_DIM, OFF_NA_K + h * HEAD_DIM, slice(OFF_NA_V + h * HEAD_DIM,
                                                                                    OFF_NA_V + (h + 1) * HEAD_DIM)))
        for mi in range(2 * DA_HEADS):
            j = mi // 2
            maps.append((s, OFF_DA_Q + mi * DA_DIM, OFF_DA_K + mi * DA_DIM,
                         slice(OFF_DA_V + j * 2 * DA_DIM, OFF_DA_V + (j + 1) * 2 * DA_DIM)))
    rows_of = lambda s: slice(s * SEQ, (s + 1) * SEQ)
    scores = [_dot_nt(u_ref[rows_of(s), q0:q0 + HEAD_DIM], u_ref[rows_of(s), k0:k0 + HEAD_DIM])
              for s, q0, k0, _ in maps]
    exps = [jnp.exp2(sc - jnp.max(sc, -1, keepdims=True)) for sc in scores]
    outs = [_dot(e.astype(BF16), u_ref[rows_of(s), vs]) / jnp.sum(e, -1, keepdims=True)
            for e, (s, _, _, vs) in zip(exps, maps)]
    per_seq = NA_HEADS + 2 * DA_HEADS
    zero_row = jnp.zeros((1, SC_WIDTH), F32)
    for s in range(CTX_SEQ_PER_STEP):
        rows = rows_of(s)
        o = outs[s * per_seq:(s + 1) * per_seq]
        vc = u_ref[rows, OFF_SC_C:OFF_SC_X].astype(F32) * u_ref[rows, OFF_SC_X:OFF_DA_Q].astype(F32)
        conv = _gated_conv(u_ref[rows, OFF_SC_B:OFF_SC_C].astype(F32), vc, zero_row, zero_row, cw_ref[...], cb_ref[...])
        da = [_diff_head_norm(o[NA_HEADS + 2 * j], o[NA_HEADS + 2 * j + 1], lam, gain, lam_init)
              for j in range(DA_HEADS)]
        o_ref[rows, :] = jnp.concatenate(o[:NA_HEADS] + [conv] + da, axis=-1).astype(BF16)


def _ctx_mix(u, da_lambda, da_norm_g, conv_w, conv_b, l, lam_init):
    step_rows = CTX_SEQ_PER_STEP * SEQ
    return pl.pallas_call(
        functools.partial(_ctx_mix_kernel, lam_init=lam_init),
        grid=(BATCH // CTX_SEQ_PER_STEP,),
        in_specs=[
            pl.BlockSpec((step_rows, IN_WIDTH), lambda b: (b, 0)),
            pl.BlockSpec((None, 4, DA_DIM), lambda b: (l, 0, 0)),
            pl.BlockSpec((None, 1, 2 * DA_DIM), lambda b: (l, 0, 0)),
            pl.BlockSpec((None, 3, SC_WIDTH), lambda b: (l, 0, 0)),
            pl.BlockSpec((None, 1, SC_WIDTH), lambda b: (l, 0, 0)),
        ],
        out_specs=pl.BlockSpec((step_rows, MIX_WIDTH), lambda b: (b, 0)),
        out_shape=jax.ShapeDtypeStruct((BATCH * SEQ, MIX_WIDTH), BF16),
        compiler_params=_params("parallel"),
        name="mix_context",
    )(u, da_lambda, da_norm_g.reshape(DEPTH, 1, 2 * DA_DIM), conv_w, conv_b.reshape(DEPTH, 1, SC_WIDTH))


def _lat_mix_kernel(uq_ref, nak_ref, nav_ref, sc_ref, dakv_ref, cnak_ref, cnav_ref, cdak_ref, cdav_ref,
                    bias_ref, lam_ref, gain_ref, cw_ref, cb_ref, o_ref, kall_ref, vall_ref, cnav1_ref, *, lam_init):
    qt = pl.program_id(1)
    n_qt = pl.num_programs(1)
    lam = _lambda(lam_ref, lam_init)
    gain = gain_ref[...]

    @pl.when(qt == 0)
    def _():
        kall_ref[:DEC_SEQ, :] = dakv_ref[:, :DA_QK_WIDTH]
        kall_ref[DEC_SEQ:, :] = cdak_ref[...]
        for j in range(DA_HEADS):
            vs = slice(j * 2 * DA_DIM, (j + 1) * 2 * DA_DIM)
            cols = slice(j * 4 * DA_DIM, (j + 1) * 4 * DA_DIM)
            vall_ref[:DEC_SEQ, cols] = _with_ones(dakv_ref[:, DA_QK_WIDTH + j * 2 * DA_DIM:DA_QK_WIDTH + (j + 1) * 2 * DA_DIM])
            vall_ref[DEC_SEQ:, cols] = _with_ones(cdav_ref[:, vs])
        for h in range(NA_HEADS):
            cnav1_ref[:, h * 2 * HEAD_DIM:(h + 1) * 2 * HEAD_DIM] = _with_ones(cnav_ref[:, h * HEAD_DIM:(h + 1) * HEAD_DIM])

    outs = []
    key_row0 = jnp.clip(qt * QT_ROWS - NA_WIN_ROWS // 2, 0, GRID_ROWS - NA_KEY_ROWS)
    k0 = pl.multiple_of(key_row0 * GRID_W, GRID_W)
    jobs = []
    for h in range(NA_HEADS):
        hs = slice(h * HEAD_DIM, (h + 1) * HEAD_DIM)
        q = uq_ref[:, OFF_NA_Q + h * HEAD_DIM:OFF_NA_Q + (h + 1) * HEAD_DIM]
        k_loc = nak_ref[pl.ds(k0, NA_KEYS), hs]
        v_ones = jnp.concatenate([_with_ones(nav_ref[pl.ds(k0, NA_KEYS), hs]),
                                  cnav1_ref[:, h * 2 * HEAD_DIM:(h + 1) * 2 * HEAD_DIM]], axis=0)
        s_loc = _dot_nt(q, k_loc) + bias_ref[h]
        s_ctx = _dot_nt(q, cnak_ref[:, hs])
        jobs.append(([s_loc, s_ctx], v_ones, HEAD_DIM))
    outs += _softmax_pv_each(jobs)
    t0 = pl.multiple_of(qt * QT, QT)
    halo = 16
    before = sc_ref[pl.ds(pl.multiple_of(jnp.maximum(t0 - halo, 0), halo), halo), :].astype(F32)
    after = sc_ref[pl.ds(pl.multiple_of(jnp.minimum(t0 + QT, DEC_SEQ - halo), halo), halo), :].astype(F32)
    cur = sc_ref[pl.ds(t0, QT), :].astype(F32)
    v_prev = before[halo - 1:halo, SC_WIDTH:2 * SC_WIDTH] * before[halo - 1:halo, 2 * SC_WIDTH:]
    v_next = after[0:1, SC_WIDTH:2 * SC_WIDTH] * after[0:1, 2 * SC_WIDTH:]
    v_prev = jnp.where(qt > 0, v_prev, 0.0)
    v_next = jnp.where(qt < n_qt - 1, v_next, 0.0)
    outs.append(_gated_conv(cur[:, :SC_WIDTH], cur[:, SC_WIDTH:2 * SC_WIDTH] * cur[:, 2 * SC_WIDTH:],
                            v_prev, v_next, cw_ref[...], cb_ref[...]))
    for j0 in range(0, DA_HEADS, DA_HEADS_PER_STAGE):
        jobs = []
        for j in range(j0, j0 + DA_HEADS_PER_STAGE):
            v_ones = vall_ref[:, j * 4 * DA_DIM:(j + 1) * 4 * DA_DIM]
            for mi in (2 * j, 2 * j + 1):
                q = uq_ref[:, OFF_DA_Q + mi * DA_DIM:OFF_DA_Q + (mi + 1) * DA_DIM]
                jobs.append(([_dot_nt(q, kall_ref[:, mi * DA_DIM:(mi + 1) * DA_DIM])], v_ones, 2 * DA_DIM))
        o = _softmax_pv_each(jobs)
        for t in range(DA_HEADS_PER_STAGE):
            outs.append(_diff_head_norm(o[2 * t], o[2 * t + 1], lam, gain, lam_init))
    o_ref[...] = jnp.concatenate(outs, axis=-1).astype(BF16)


def _lat_mix(u, caches, bias_tab, da_lambda, da_norm_g, conv_w, conv_b, l, lam_init):
    cnak, cnav, cdak, cdav = caches
    n_qt = DEC_SEQ // QT
    q0 = BATCH * SEQ // QT
    b0 = BATCH * SEQ // DEC_SEQ

    def bias_map(b, qt):
        return (l, 0, jnp.where(qt == 0, 0, jnp.where(qt == n_qt - 1, 2, 1)), 0, 0)

    return pl.pallas_call(
        functools.partial(_lat_mix_kernel, lam_init=lam_init),
        grid=(DEC_BATCH, n_qt),
        in_specs=[
            pl.BlockSpec((QT, IN_WIDTH), lambda b, qt: (q0 + b * n_qt + qt, 0)),
            pl.BlockSpec((DEC_SEQ, NA_WIDTH), lambda b, qt: (b0 + b, OFF_NA_K // NA_WIDTH)),
            pl.BlockSpec((DEC_SEQ, NA_WIDTH), lambda b, qt: (b0 + b, OFF_NA_V // NA_WIDTH)),
            pl.BlockSpec((DEC_SEQ, 3 * SC_WIDTH), lambda b, qt: (b0 + b, OFF_SC_B // (3 * SC_WIDTH))),
            pl.BlockSpec((DEC_SEQ, DA_QK_WIDTH + DA_V_WIDTH), lambda b, qt: (b0 + b, OFF_DA_K // (DA_QK_WIDTH + DA_V_WIDTH))),
            pl.BlockSpec((None, None, PAST_LEN, NA_WIDTH), lambda b, qt: (b, l, 0, 0)),
            pl.BlockSpec((None, None, PAST_LEN, NA_WIDTH), lambda b, qt: (b, l, 0, 0)),
            pl.BlockSpec((None, None, PAST_LEN, DA_QK_WIDTH), lambda b, qt: (b, l, 0, 0)),
            pl.BlockSpec((None, None, PAST_LEN, DA_V_WIDTH), lambda b, qt: (b, l, 0, 0)),
            pl.BlockSpec((None, NA_HEADS, None, QT, NA_KEYS), bias_map),
            pl.BlockSpec((None, 4, DA_DIM), lambda b, qt: (l, 0, 0)),
            pl.BlockSpec((None, 1, 2 * DA_DIM), lambda b, qt: (l, 0, 0)),
            pl.BlockSpec((None, 3, SC_WIDTH), lambda b, qt: (l, 0, 0)),
            pl.BlockSpec((None, 1, SC_WIDTH), lambda b, qt: (l, 0, 0)),
        ],
        out_specs=pl.BlockSpec((QT, MIX_WIDTH), lambda b, qt: (b * n_qt + qt, 0)),
        out_shape=jax.ShapeDtypeStruct((DEC_BATCH * DEC_SEQ, MIX_WIDTH), BF16),
        scratch_shapes=[
            pltpu.VMEM((DEC_SEQ + PAST_LEN, DA_QK_WIDTH), BF16),
            pltpu.VMEM((DEC_SEQ + PAST_LEN, 2 * DA_V_WIDTH), BF16),
            pltpu.VMEM((PAST_LEN, 2 * NA_WIDTH), BF16),
        ],
        compiler_params=_params("parallel", "arbitrary"),
        name="mix_latent",
    )(u, u, u, u, u, cnak, cnav, cdak, cdav, bias_tab, da_lambda,
      da_norm_g.reshape(DEPTH, 1, 2 * DA_DIM), conv_w, conv_b.reshape(DEPTH, 1, SC_WIDTH))


def _first_max(vals):
    m = functools.reduce(jnp.maximum, vals)
    hot, taken = [], None
    for v in vals:
        is_max = v == m
        if taken is None:
            hot.append(is_max)
            taken = is_max
        else:
            hot.append(is_max & ~taken)
            taken = taken | is_max
    return hot, m


def _pick(hot, vals):
    out = vals[-1]
    for h, v in zip(hot[-2::-1], vals[-2::-1]):
        out = jnp.where(h, v, out)
    return out


def _route(h2, h2_bf, wr_ref, br_ref):
    tm = h2.shape[0]
    w = wr_ref[...]
    w_hi = w.astype(BF16)
    w_lo = (w - w_hi.astype(F32)).astype(BF16)
    h_lo = (h2 - h2_bf.astype(F32)).astype(BF16)
    both = _dot_nt(jnp.concatenate([w_hi, w_lo], axis=0), h2_bf)
    z = both[:N_EXPERTS] + both[N_EXPERTS:] + _dot_nt(w_hi, h_lo)
    scores = jax.nn.sigmoid(z)
    biased = scores + br_ref[...]
    P = [biased[k * N_GROUPS:(k + 1) * N_GROUPS] for k in range(PER_GROUP)]
    S = [scores[k * N_GROUPS:(k + 1) * N_GROUPS] for k in range(PER_GROUP)]
    pair_sums = [P[i] + P[j] for i in range(PER_GROUP) for j in range(i + 1, PER_GROUP)]
    group_score = functools.reduce(jnp.maximum, pair_sums)
    sel, _ = _first_max([group_score[g:g + 1] for g in range(N_GROUPS)])
    c = [_pick(sel, [P[k][g:g + 1] for g in range(N_GROUPS)]) for k in range(PER_GROUP)]
    cs = [_pick(sel, [S[k][g:g + 1] for g in range(N_GROUPS)]) for k in range(PER_GROUP)]
    t1, _ = _first_max(c)
    t2, _ = _first_max([jnp.where(t, -jnp.inf, v) for t, v in zip(t1, c)])
    w1 = functools.reduce(jnp.add, [jnp.where(t, v, 0.0) for t, v in zip(t1, cs)])
    w2 = functools.reduce(jnp.add, [jnp.where(t, v, 0.0) for t, v in zip(t2, cs)])
    total = w1 + w2
    slot_gate = [jnp.where(a, w1 / total, jnp.where(b, w2 / total, 0.0)) for a, b in zip(t1, t2)]
    onehot = jnp.concatenate([s.astype(F32) for s in sel] + [jnp.zeros((8 - N_GROUPS, tm), F32)], axis=0)
    earlier = (lax.broadcasted_iota(jnp.int32, (tm, tm), 0) < lax.broadcasted_iota(jnp.int32, (tm, tm), 1))
    rank = _dot(onehot.astype(BF16), earlier.astype(BF16))
    count = jnp.sum(onehot, axis=1, keepdims=True)
    n_chunks = jnp.floor((count + (MOE_CHUNK - 1)) * (1.0 / MOE_CHUNK))
    pos = jnp.zeros((1, tm), F32)
    start = jnp.zeros((1, 1), F32)
    for g in range(N_GROUPS):
        pos = jnp.where(sel[g], start + rank[g:g + 1], pos)
        start = start + n_chunks[g:g + 1] * MOE_CHUNK
    rows = [pos] + slot_gate + [jnp.zeros((128 - 1 - PER_GROUP, tm), F32)]
    return jnp.concatenate(rows, axis=0), n_chunks


def _post_kernel(mixp_ref, mixs_ref, xp_ref, xs_ref, mod_ref, wout_ref, g_ref, b_ref, wr_ref, br_ref,
                 x1_ref, h2_ref, tok_ref, pos_ref, nch_ref, wbf_ref):
    i = pl.program_id(0)
    is_ctx = i < CTX_TILES

    @pl.when(i == 0)
    def _():
        wbf_ref[...] = wout_ref[...].astype(BF16)

    mod = mod_ref[...]
    n = TM_POST // POST_SPLIT
    parts = [slice(p * n, (p + 1) * n) for p in range(POST_SPLIT)]
    ys = [_dot(jnp.where(is_ctx, mixp_ref[p, :], mixs_ref[p, :]), wbf_ref[...]) for p in parts]
    xs = [jnp.where(is_ctx, xp_ref[p, :], xs_ref[p, :]) for p in parts]
    x1s = [t * g_ref[...] + b_ref[...] for t in _ln_each([ALPHA * x + mod[2:3] * y for x, y in zip(xs, ys)])]
    h2s = [t * (1.0 + mod[4:5]) + mod[3:4] for t in _ln_each(x1s)]
    for p, x1 in zip(parts, x1s):
        x1_ref[p, :] = x1
    h2 = jnp.concatenate(h2s, axis=0)
    h2_bf = h2.astype(BF16)
    h2_ref[...] = h2_bf
    rows, n_chunks = _route(h2, h2_bf, wr_ref, br_ref)
    tok_ref[...] = rows.T
    pos_ref[...] = rows[0:8]
    nch_ref[...] = jnp.broadcast_to(n_chunks, (8, 128)).astype(jnp.int32)


def _post(mix_p, mix_s, xp, xs, mod_l, w_out, ln_g, ln_b, wr_t, br_t, l):
    n = xp.shape[0] + xs.shape[0]
    route = functools.partial(_tile_stream, n_ctx_tiles=CTX_TILES, lat_tiles_per_batch=DEC_SEQ // TM_POST)
    return pl.pallas_call(
        _post_kernel,
        grid=(n // TM_POST,),
        in_specs=[
            pl.BlockSpec((TM_POST, MIX_WIDTH), lambda i: (route(i)[0], 0)),
            pl.BlockSpec((TM_POST, MIX_WIDTH), lambda i: (route(i)[1], 0)),
            pl.BlockSpec((TM_POST, D_MODEL), lambda i: (route(i)[0], 0)),
            pl.BlockSpec((TM_POST, D_MODEL), lambda i: (route(i)[1], 0)),
            pl.BlockSpec((None, 6, D_MODEL), lambda i: (route(i)[2], 0, 0)),
            pl.BlockSpec((None, MIX_WIDTH, D_MODEL), lambda i: (l, 0, 0), pipeline_mode=pl.Buffered(1)),
            pl.BlockSpec((None, 1, D_MODEL), lambda i: (l, 0, 0)),
            pl.BlockSpec((None, 1, D_MODEL), lambda i: (l, 0, 0)),
            pl.BlockSpec((N_EXPERTS, D_MODEL), lambda i: (0, 0)),
            pl.BlockSpec((N_EXPERTS, 1), lambda i: (0, 0)),
        ],
        out_specs=[
            pl.BlockSpec((TM_POST, D_MODEL), lambda i: (i, 0)),
            pl.BlockSpec((TM_POST, D_MODEL), lambda i: (i, 0)),
            pl.BlockSpec((TM_POST, 128), lambda i: (i, 0)),
            pl.BlockSpec((None, 8, TM_POST), lambda i: (i, 0, 0)),
            pl.BlockSpec((None, 8, 128), lambda i: (i, 0, 0)),
        ],
        out_shape=[
            jax.ShapeDtypeStruct((n, D_MODEL), F32),
            jax.ShapeDtypeStruct((n, D_MODEL), BF16),
            jax.ShapeDtypeStruct((n, 128), F32),
            jax.ShapeDtypeStruct((n // TM_POST, 8, TM_POST), F32),
            jax.ShapeDtypeStruct((n // TM_POST, 8, 128), jnp.int32),
        ],
        scratch_shapes=[pltpu.VMEM((MIX_WIDTH, D_MODEL), BF16)],
        compiler_params=_params("arbitrary"),
        name="post",
    )(mix_p, mix_s, xp, xs, mod_l, w_out, ln_g.reshape(DEPTH, 1, D_MODEL), ln_b.reshape(DEPTH, 1, D_MODEL), wr_t, br_t)


def _moe_kernel(nch_ref, h_ref, tokn_ref, pos_ref, tok_ref, x1_ref, mod_ref, w1f_ref, w3f_ref, w2f_ref, g_ref, b_ref,
                op_ref, os_ref, w13_ref, w2_ref, hs_ref, gs_ref, ys_ref):
    step = pl.program_id(0)
    half = PER_GROUP * D_EXPERT

    for k in range(PER_GROUP):
        @pl.when((step < N_EXPERTS) & (step % PER_GROUP == k))
        def _(k=k):
            g = step // PER_GROUP
            w13_ref[g, :, k * D_EXPERT:(k + 1) * D_EXPERT] = w1f_ref[...].astype(BF16)
            w13_ref[g, :, half + k * D_EXPERT:half + (k + 1) * D_EXPERT] = w3f_ref[...].astype(BF16)
            w2_ref[g, k * D_EXPERT:(k + 1) * D_EXPERT, :] = w2f_ref[...].astype(BF16)

    @pl.when(step == N_EXPERTS - 1)
    def _():
        hs, gs = _moe_sort(h_ref, tokn_ref, pos_ref)
        hs_ref[...] = hs
        gs_ref[...] = gs
        ys_ref[...] = jnp.zeros_like(ys_ref)

    @pl.when(step >= N_EXPERTS)
    def _():
        i = step - N_EXPERTS
        _moe_groups(i, nch_ref, w13_ref, w2_ref, hs_ref, gs_ref, ys_ref)
        hs, gs = _moe_sort(h_ref, tokn_ref, pos_ref)
        pos_col = tok_ref[:, 0:1]
        unsort = (lax.broadcasted_iota(jnp.int32, (TM_MOE, MOE_SORTED), 1).astype(F32) == pos_col).astype(BF16)
        y = _dot(unsort, ys_ref[...])
        hs_ref[...] = hs
        gs_ref[...] = gs
        ys_ref[...] = jnp.zeros_like(ys_ref)
        mod = mod_ref[...]
        out = _ln(ALPHA * x1_ref[...] + mod[5:6] * y) * g_ref[...] + b_ref[...]

        @pl.when(i < CTX_TILES)
        def _():
            op_ref[...] = out

        @pl.when(i >= CTX_TILES)
        def _():
            os_ref[...] = out


def _moe_sort(h_ref, tok_ref, pos_ref):
    tok = tok_ref[...]
    slot = lax.broadcasted_iota(jnp.int32, (MOE_SORTED, TM_MOE), 0).astype(F32)
    sort = (slot == pos_ref[0:1, :]).astype(BF16)
    tok_hi = tok.astype(BF16)
    tok_lo = (tok - tok_hi.astype(F32)).astype(BF16)
    z = _dot(sort, jnp.concatenate([h_ref[...], tok_hi, tok_lo], axis=1))
    return z[:, :D_MODEL].astype(BF16), z[:, D_MODEL:D_MODEL + 128] + z[:, D_MODEL + 128:]


def _moe_groups(i, nch_ref, w13_ref, w2_ref, hs_ref, gs_ref, ys_ref):
    def experts(g, chunk0, n_rows):
        r0 = pl.multiple_of(chunk0 * MOE_CHUNK, MOE_CHUNK)
        rows = hs_ref[pl.ds(r0, n_rows), :]
        gates = gs_ref[pl.ds(r0, n_rows), :]
        half = PER_GROUP * D_EXPERT
        ab = _dot(rows, w13_ref[g])
        hid = []
        for k in range(PER_GROUP):
            a = ab[:, k * D_EXPERT:(k + 1) * D_EXPERT]
            b = ab[:, half + k * D_EXPERT:half + (k + 1) * D_EXPERT]
            hid.append((a * jax.nn.sigmoid(a) * b * gates[:, 1 + k:2 + k]).astype(BF16))
        ys_ref[pl.ds(r0, n_rows), :] = _dot(jnp.concatenate(hid, axis=1), w2_ref[g]).astype(BF16)

    def group(g, first):
        n = nch_ref[i * N_GROUPS + g]
        n_blocks = n // MOE_BLOCK
        rest = n - n_blocks * MOE_BLOCK

        def block(c, carry):
            experts(g, first + c * MOE_BLOCK, MOE_BLOCK * MOE_CHUNK)
            return carry

        lax.fori_loop(0, n_blocks, block, 0)
        for m in range(1, MOE_BLOCK):
            @pl.when(rest == m)
            def _(m=m):
                experts(g, first + n_blocks * MOE_BLOCK, m * MOE_CHUNK)
        return first + n

    lax.fori_loop(0, N_GROUPS, group, 0)


def _moe(h2, tok, pos, nch, x1, mod_l, w1, w3, w2, ln_g, ln_b, l):
    n_tiles = x1.shape[0] // TM_MOE
    n_ctx = CTX_TILES * TM_MOE
    per_batch = DEC_SEQ // TM_MOE

    def tile(step):
        return jnp.maximum(step - N_EXPERTS, 0)

    def next_tile(step):
        return jnp.clip(step - (N_EXPERTS - 1), 0, n_tiles - 1)

    def expert(step):
        return jnp.minimum(step, N_EXPERTS - 1)

    def mod_map(step, s):
        return (_tile_stream(tile(step), CTX_TILES, per_batch)[2], 0, 0)

    grid_spec = pltpu.PrefetchScalarGridSpec(
        num_scalar_prefetch=1,
        grid=(N_EXPERTS + n_tiles,),
        in_specs=[
            pl.BlockSpec((TM_MOE, D_MODEL), lambda step, s: (next_tile(step), 0)),
            pl.BlockSpec((TM_MOE, 128), lambda step, s: (next_tile(step), 0)),
            pl.BlockSpec((None, 8, TM_MOE), lambda step, s: (next_tile(step), 0, 0)),
            pl.BlockSpec((TM_MOE, 128), lambda step, s: (tile(step), 0)),
            pl.BlockSpec((TM_MOE, D_MODEL), lambda step, s: (tile(step), 0)),
            pl.BlockSpec((None, 6, D_MODEL), mod_map),
            pl.BlockSpec((None, None, D_MODEL, D_EXPERT), lambda step, s: (l, expert(step), 0, 0)),
            pl.BlockSpec((None, None, D_MODEL, D_EXPERT), lambda step, s: (l, expert(step), 0, 0)),
            pl.BlockSpec((None, None, D_EXPERT, D_MODEL), lambda step, s: (l, expert(step), 0, 0)),
            pl.BlockSpec((None, 1, D_MODEL), lambda step, s: (l, 0, 0)),
            pl.BlockSpec((None, 1, D_MODEL), lambda step, s: (l, 0, 0)),
        ],
        out_specs=[
            pl.BlockSpec((TM_MOE, D_MODEL), lambda step, s: (jnp.minimum(tile(step), CTX_TILES - 1), 0)),
            pl.BlockSpec((TM_MOE, D_MODEL), lambda step, s: (jnp.maximum(tile(step) - CTX_TILES, 0), 0)),
        ],
        scratch_shapes=[
            pltpu.VMEM((N_GROUPS, D_MODEL, 2 * PER_GROUP * D_EXPERT), BF16),
            pltpu.VMEM((N_GROUPS, PER_GROUP * D_EXPERT, D_MODEL), BF16),
            pltpu.VMEM((MOE_SORTED, D_MODEL), BF16),
            pltpu.VMEM((MOE_SORTED, 128), F32),
            pltpu.VMEM((MOE_SORTED, D_MODEL), BF16),
        ],
    )
    return pl.pallas_call(
        _moe_kernel,
        grid_spec=grid_spec,
        out_shape=[jax.ShapeDtypeStruct((n_ctx, D_MODEL), F32),
                   jax.ShapeDtypeStruct((x1.shape[0] - n_ctx, D_MODEL), F32)],
        compiler_params=_params("arbitrary"),
        name="moe",
    )(nch[:, :N_GROUPS, 0].reshape(-1), h2, tok, pos, tok, x1, mod_l, w1, w3, w2,
      ln_g.reshape(DEPTH, 1, D_MODEL), ln_b.reshape(DEPTH, 1, D_MODEL))


def _rope_tables():
    t = np.arange(DEC_SEQ)
    half = DA_DIM // 4
    inv_freq = ROPE_BASE ** (-np.arange(half, dtype=np.float32) / half)
    cos, sin = [], []
    for pos in (t // GRID_W, t % GRID_W):
        ang = pos.astype(np.float32)[:, None] * inv_freq[None, :]
        c, s = np.cos(ang), np.sin(ang)
        cos += [c, c]
        sin += [-s, s]
    cos = np.tile(np.concatenate(cos, axis=1), (1, 2 * DA_HEADS))
    sin = np.tile(np.concatenate(sin, axis=1), (1, 2 * DA_HEADS))
    cos = np.concatenate([np.ones((TM, DA_QK_WIDTH), np.float32), cos], axis=0)
    sin = np.concatenate([np.zeros((TM, DA_QK_WIDTH), np.float32), sin], axis=0)
    return jnp.asarray(cos, F32), jnp.asarray(sin, F32)


_NA_TILE_KINDS = ((0, 0), (2 * QT_ROWS, 2 * QT_ROWS - NA_WIN_ROWS // 2), (GRID_ROWS - QT_ROWS, GRID_ROWS - NA_KEY_ROWS))


def _na_bias_kernel(by_ref, o_ref):
    qc = lax.broadcasted_iota(jnp.int32, (GRID_W, GRID_W), 0)
    kc = lax.broadcasted_iota(jnp.int32, (GRID_W, GRID_W), 1)
    win_c0 = jnp.clip(qc - NA_WIN_COLS // 2, 0, GRID_W - NA_WIN_COLS)
    in_cols = (kc >= win_c0) & (kc < win_c0 + NA_WIN_COLS)
    masked = jnp.full((GRID_W, GRID_W), -jnp.inf, F32)
    for kind, (r0, key_row0) in enumerate(_NA_TILE_KINDS):
        for a in range(QT_ROWS):
            qr = r0 + a
            win_r0 = min(max(qr - NA_WIN_ROWS // 2, 0), GRID_ROWS - NA_WIN_ROWS)
            blocks = []
            for b in range(NA_KEY_ROWS):
                kr = key_row0 + b
                if win_r0 <= kr < win_r0 + NA_WIN_ROWS:
                    blocks.append(jnp.where(in_cols, by_ref[kr - qr + NA_WIN_ROWS - 1] * LOG2E, masked))
                else:
                    blocks.append(masked)
            o_ref[kind, a * GRID_W:(a + 1) * GRID_W, :] = jnp.concatenate(blocks, axis=1)


def _na_bias_tables(na_rel_bias):
    n_dr, n_dc = 2 * NA_WIN_ROWS - 1, 2 * NA_WIN_COLS - 1
    cols = np.arange(GRID_W)
    d_col = np.clip(cols[None, :] - cols[:, None], 1 - NA_WIN_COLS, NA_WIN_COLS - 1) + NA_WIN_COLS - 1
    col_sel = (d_col[None] == np.arange(n_dc)[:, None, None]).astype(np.float32)
    by_col = jnp.einsum('lhab,bqc->lhaqc', na_rel_bias.astype(F32), col_sel, precision=lax.Precision.HIGHEST)
    return pl.pallas_call(
        _na_bias_kernel,
        grid=(DEPTH, NA_HEADS),
        in_specs=[pl.BlockSpec((None, None, n_dr, GRID_W, GRID_W), lambda l, h: (l, h, 0, 0, 0))],
        out_specs=pl.BlockSpec((None, None, len(_NA_TILE_KINDS), QT, NA_KEYS), lambda l, h: (l, h, 0, 0, 0)),
        out_shape=jax.ShapeDtypeStruct((DEPTH, NA_HEADS, len(_NA_TILE_KINDS), QT, NA_KEYS), F32),
        compiler_params=_params("parallel", "parallel"),
        name="na_bias_table",
    )(by_col)


def kernel(x_prompt, x_sample, cache_na_k, cache_na_v, cache_da_k, cache_da_v, c, c_ctx, w_mod, b_mod, w_in,
           na_rel_bias, sc_conv_w, sc_conv_b, da_lambda, da_norm_g, w_out, ln1_g, ln1_b, w_router, b_router,
           moe_w1, moe_w3, moe_w2, ln2_g, ln2_b):
    xp = x_prompt.reshape(BATCH * SEQ, D_MODEL)
    xs = x_sample.reshape(DEC_BATCH * DEC_SEQ, D_MODEL)

    cond = jnp.concatenate([c_ctx[None, :], c, jnp.zeros((N_COND - 1 - DEC_BATCH, D_MODEL), F32)], axis=0)
    mod = _modulation(cond, w_mod, b_mod).reshape(DEPTH, N_COND, 6, D_MODEL)

    slot_major = np.arange(N_EXPERTS).reshape(N_GROUPS, PER_GROUP).T.reshape(-1)
    wr_t = w_router.T[slot_major]
    br_t = b_router.astype(F32)[slot_major].reshape(N_EXPERTS, 1)

    caches = (
        cache_na_k.reshape(DEC_BATCH, DEPTH, PAST_LEN, NA_WIDTH).astype(BF16),
        cache_na_v.reshape(DEC_BATCH, DEPTH, PAST_LEN, NA_WIDTH).astype(BF16),
        cache_da_k.reshape(DEC_BATCH, DEPTH, PAST_LEN, DA_QK_WIDTH).astype(BF16),
        cache_da_v.reshape(DEC_BATCH, DEPTH, PAST_LEN, DA_V_WIDTH).astype(BF16),
    )
    rope_tabs = _rope_tables()
    bias_tab = _na_bias_tables(na_rel_bias)

    new_caches = None
    for l in range(DEPTH):
        lam_init = 0.8 - 0.6 * math.exp(-0.3 * l)
        mod_l = mod[l]
        u, *new_caches = _inproj(xp, xs, mod_l, w_in, l, rope_tabs, cache_bufs=new_caches)
        mix_p = _ctx_mix(u, da_lambda, da_norm_g, sc_conv_w, sc_conv_b, l, lam_init)
        mix_s = _lat_mix(u, caches, bias_tab, da_lambda, da_norm_g, sc_conv_w, sc_conv_b, l, lam_init)
        x1, h2, *route = _post(mix_p, mix_s, xp, xs, mod_l, w_out, ln1_g, ln1_b, wr_t, br_t, l)
        xp, xs = _moe(h2, *route, x1, mod_l, moe_w1, moe_w3, moe_w2, ln2_g, ln2_b, l)

    nak, nav, dak, dav = new_caches
    return (xp.reshape(BATCH, SEQ, D_MODEL), xs.reshape(DEC_BATCH, DEC_SEQ, D_MODEL),
            nak.reshape(BATCH, DEPTH, SEQ, NA_HEADS, HEAD_DIM), nav.reshape(BATCH, DEPTH, SEQ, NA_HEADS, HEAD_DIM),
            dak.reshape(BATCH, DEPTH, SEQ, 2 * DA_HEADS, DA_DIM), dav.reshape(BATCH, DEPTH, SEQ, DA_HEADS, 2 * DA_DIM))
```

```python
import functools
import math

import numpy as np
import jax
import jax.numpy as jnp
from jax import lax
from jax.experimental import pallas as pl
from jax.experimental.pallas import tpu as pltpu

F32 = jnp.float32
BF16 = jnp.bfloat16

D_MODEL = 1024
BATCH = 16
SEQ = 256
DEPTH = 4
DEC_BATCH = 2
DEC_SEQ = 2048
PAST_LEN = 512
GRID_W = 64
GRID_ROWS = DEC_SEQ // GRID_W
HEAD_DIM = 64
NA_HEADS = 4
NA_WIDTH = NA_HEADS * HEAD_DIM
NA_WIN_ROWS = 8
NA_WIN_COLS = 16
SC_WIDTH = 256
DA_HEADS = 4
DA_DIM = 64
DA_QK_WIDTH = 2 * DA_HEADS * DA_DIM
DA_V_WIDTH = DA_HEADS * 2 * DA_DIM
MIX_WIDTH = NA_WIDTH + SC_WIDTH + DA_V_WIDTH
IN_WIDTH = 3 * NA_WIDTH + 3 * SC_WIDTH + 2 * DA_QK_WIDTH + DA_V_WIDTH
OFF_NA_Q = 0
OFF_NA_K = NA_WIDTH
OFF_NA_V = 2 * NA_WIDTH
OFF_SC_B = 3 * NA_WIDTH
OFF_SC_C = OFF_SC_B + SC_WIDTH
OFF_SC_X = OFF_SC_C + SC_WIDTH
OFF_DA_Q = OFF_SC_X + SC_WIDTH
OFF_DA_K = OFF_DA_Q + DA_QK_WIDTH
OFF_DA_V = OFF_DA_K + DA_QK_WIDTH
ROPE_BASE = 10000.0
N_EXPERTS = 16
N_GROUPS = 4
PER_GROUP = N_EXPERTS // N_GROUPS
D_EXPERT = 256
ALPHA = (2 * DEPTH) ** 0.25
LN_EPS = 1e-5
LOG2E = math.log2(math.e)
Q_SCALE = HEAD_DIM ** -0.5 * LOG2E

N_COND = 8
TM = 512
SEQ_PER_TILE = TM // SEQ
CTX_TILES = BATCH * SEQ // TM
CTX_SEQ_PER_STEP = 2
DA_HEADS_PER_STAGE = 2
POST_SPLIT = 2
TM_MOE = 512
TM_POST = TM_MOE
MOE_CHUNK = 32
MOE_BLOCK = 8
MOE_SORTED = TM_MOE + (N_GROUPS - 1) * MOE_CHUNK
assert TM == TM_MOE
QT = 256
QT_ROWS = QT // GRID_W
NA_KEY_ROWS = 12
NA_KEYS = NA_KEY_ROWS * GRID_W
VMEM_LIMIT = 60 * 1024 * 1024


def _dot(a, b):
    return jnp.dot(a, b, preferred_element_type=F32)


def _dot_nt(a, b):
    return lax.dot_general(a, b, (((1,), (1,)), ((), ())), preferred_element_type=F32)


def _ln(x):
    mu = jnp.mean(x, -1, keepdims=True)
    xc = x - mu
    var = jnp.mean(xc * xc, -1, keepdims=True)
    return xc * lax.rsqrt(var + LN_EPS)


def _ln_each(xs):
    mus = [jnp.mean(x, -1, keepdims=True) for x in xs]
    xcs = [x - mu for x, mu in zip(xs, mus)]
    vs = [jnp.mean(xc * xc, -1, keepdims=True) for xc in xcs]
    return [xc * lax.rsqrt(v + LN_EPS) for xc, v in zip(xcs, vs)]


def _params(*sem):
    return pltpu.CompilerParams(dimension_semantics=sem, vmem_limit_bytes=VMEM_LIMIT)


def _mod_kernel(cond_ref, w_ref, b_ref, o_ref):
    c = cond_ref[...]
    s = (c * jax.nn.sigmoid(c)).astype(BF16)
    o_ref[...] = _dot(s, w_ref[...].astype(BF16)) + b_ref[...]


def _modulation(cond, w_mod, b_mod):
    tn = 3 * D_MODEL
    return pl.pallas_call(
        _mod_kernel,
        grid=(DEPTH, 6 * D_MODEL // tn),
        in_specs=[
            pl.BlockSpec((N_COND, D_MODEL), lambda l, j: (0, 0)),
            pl.BlockSpec((None, D_MODEL, tn), lambda l, j: (l, 0, j)),
            pl.BlockSpec((None, 1, tn), lambda l, j: (l, 0, j)),
        ],
        out_specs=pl.BlockSpec((None, N_COND, tn), lambda l, j: (l, 0, j)),
        out_shape=jax.ShapeDtypeStruct((DEPTH, N_COND, 6 * D_MODEL), F32),
        compiler_params=_params("parallel", "parallel"),
        name="modulation",
    )(cond, w_mod, b_mod.reshape(DEPTH, 1, 6 * D_MODEL))


def _rope(t, cos, sin):
    lane = lax.broadcasted_iota(jnp.int32, t.shape, 1)
    first = (lane // 16) % 2 == 0
    n = t.shape[1]
    swapped = jnp.where(first, pltpu.roll(t, n - 16, 1), pltpu.roll(t, 16, 1))
    return t * cos + swapped * sin


def _tile_stream(i, n_ctx_tiles, lat_tiles_per_batch):
    lat = jnp.maximum(i - n_ctx_tiles, 0)
    return jnp.minimum(i, n_ctx_tiles - 1), lat, jnp.where(i < n_ctx_tiles, 0, 1 + lat // lat_tiles_per_batch)


def _inproj_kernel(*refs):
    xp_ref, xs_ref, mod_ref, w_ref, cos_ref, sin_ref = refs[:6]
    u_ref, nak_ref, nav_ref, dak_ref, dav_ref, wbf_ref = refs[-6:]
    i = pl.program_id(0)
    is_ctx = i < CTX_TILES

    @pl.when(i == 0)
    def _():
        wbf_ref[...] = w_ref[...].astype(BF16)

    mod = mod_ref[...]
    parts = [slice(s * SEQ, (s + 1) * SEQ) for s in range(SEQ_PER_TILE)]
    hs = [t * (1.0 + mod[1:2]) + mod[0:1]
          for t in _ln_each([jnp.where(is_ctx, xp_ref[p, :], xs_ref[p, :]) for p in parts])]
    us = [_dot(h.astype(BF16), wbf_ref[...]) for h in hs]
    for p, u in zip(parts, us):
        cos, sin = cos_ref[p, :], sin_ref[p, :]
        da_q = _rope(u[:, OFF_DA_Q:OFF_DA_K], cos, sin)
        da_k = _rope(u[:, OFF_DA_K:OFF_DA_V], cos, sin)
        u_ref[p, :OFF_NA_K] = (u[:, :OFF_NA_K] * Q_SCALE).astype(BF16)
        u_ref[p, OFF_NA_K:OFF_DA_Q] = u[:, OFF_NA_K:OFF_DA_Q].astype(BF16)
        u_ref[p, OFF_DA_Q:OFF_DA_K] = (da_q * Q_SCALE).astype(BF16)
        u_ref[p, OFF_DA_K:OFF_DA_V] = da_k.astype(BF16)
        u_ref[p, OFF_DA_V:] = u[:, OFF_DA_V:].astype(BF16)

    @pl.when(is_ctx)
    def _():
        for s, u in enumerate(us):
            nak_ref[s] = u[:, OFF_NA_K:OFF_NA_V]
            nav_ref[s] = u[:, OFF_NA_V:OFF_SC_B]
            dak_ref[s] = u[:, OFF_DA_K:OFF_DA_V]
            for j in range(DA_HEADS):
                dav_ref[s, pl.ds(j, SEQ, stride=DA_HEADS), :] = (
                    u[:, OFF_DA_V + j * 2 * DA_DIM:OFF_DA_V + (j + 1) * 2 * DA_DIM])


def _inproj(xp, xs, mod_l, w_in, l, rope_tabs, cache_bufs=None):
    route = functools.partial(_tile_stream, n_ctx_tiles=CTX_TILES, lat_tiles_per_batch=DEC_SEQ // TM)
    in_specs = [
        pl.BlockSpec((TM, D_MODEL), lambda i: (route(i)[0], 0)),
        pl.BlockSpec((TM, D_MODEL), lambda i: (route(i)[1], 0)),
        pl.BlockSpec((None, 6, D_MODEL), lambda i: (route(i)[2], 0, 0)),
        pl.BlockSpec((None, D_MODEL, IN_WIDTH), lambda i: (l, 0, 0), pipeline_mode=pl.Buffered(1)),
        pl.BlockSpec((TM, DA_QK_WIDTH), lambda i: (jnp.where(i < CTX_TILES, 0, 1 + route(i)[1] % (DEC_SEQ // TM)), 0)),
        pl.BlockSpec((TM, DA_QK_WIDTH), lambda i: (jnp.where(i < CTX_TILES, 0, 1 + route(i)[1] % (DEC_SEQ // TM)), 0)),
    ]
    args = [xp, xs, mod_l, w_in, *rope_tabs]
    out_specs = [pl.BlockSpec((TM, IN_WIDTH), lambda i: (i, 0))]
    out_shape = [jax.ShapeDtypeStruct((xp.shape[0] + xs.shape[0], IN_WIDTH), BF16)]
    for rows, width in ((SEQ, NA_WIDTH), (SEQ, NA_WIDTH), (SEQ, DA_QK_WIDTH), (SEQ * DA_HEADS, 2 * DA_DIM)):
        out_specs.append(pl.BlockSpec((SEQ_PER_TILE, None, rows, width), lambda i: (route(i)[0], l, 0, 0)))
        out_shape.append(jax.ShapeDtypeStruct((BATCH, DEPTH, rows, width), F32))
    aliases = {}
    if cache_bufs is not None:
        aliases = {len(args) + k: 1 + k for k in range(4)}
        in_specs += [pl.BlockSpec(memory_space=pl.ANY)] * 4
        args += list(cache_bufs)
    return pl.pallas_call(
        _inproj_kernel,
        grid=(CTX_TILES + xs.shape[0] // TM,),
        in_specs=in_specs,
        out_specs=out_specs,
        out_shape=out_shape,
        scratch_shapes=[pltpu.VMEM((D_MODEL, IN_WIDTH), BF16)],
        input_output_aliases=aliases,
        compiler_params=_params("arbitrary"),
        name="inproj",
    )(*args)


def _lambda(lam_ref, lam_init):
    lp = lam_ref[...]
    s1 = jnp.sum(lp[0:1] * lp[1:2], axis=-1, keepdims=True)
    s2 = jnp.sum(lp[2:3] * lp[3:4], axis=-1, keepdims=True)
    return jnp.exp(s1) - jnp.exp(s2) + lam_init


def _with_ones(v):
    return jnp.concatenate([v, jnp.ones((v.shape[0], max(v.shape[1], 64)), v.dtype)], axis=1)


def _softmax_pv_each(jobs):
    ms = [functools.reduce(jnp.maximum, [jnp.max(s, -1, keepdims=True) for s in parts]) for parts, _, _ in jobs]
    es = [[jnp.exp2(s - m).astype(BF16) for s in parts] for (parts, _, _), m in zip(jobs, ms)]
    os = [_dot(e[0] if len(e) == 1 else jnp.concatenate(e, axis=1), v) for e, (_, v, _) in zip(es, jobs)]
    return [o[:, :d] / o[:, d:d + 1] for o, (_, _, d) in zip(os, jobs)]


def _diff_head_norm(o1, o2, lam, gain, lam_init):
    o = o1 - lam * o2
    o = o * lax.rsqrt(jnp.mean(o * o, -1, keepdims=True) + LN_EPS)
    return o * gain * (1.0 - lam_init)


def _gated_conv(b, v, v_prev_row, v_next_row, w, bias):
    n = v.shape[0]
    row = lax.broadcasted_iota(jnp.int32, v.shape, 0)
    prev = jnp.where(row == 0, v_prev_row, pltpu.roll(v, 1, 0))
    nxt = jnp.where(row == n - 1, v_next_row, pltpu.roll(v, n - 1, 0))
    return b * (prev * w[0:1] + v * w[1:2] + nxt * w[2:3] + bias)


def _ctx_mix_kernel(u_ref, lam_ref, gain_ref, cw_ref, cb_ref, o_ref, *, lam_init):
    lam = _lambda(lam_ref, lam_init)
    gain = gain_ref[...]
    maps = []
    for s in range(CTX_SEQ_PER_STEP):
        for h in range(NA_HEADS):
            maps.append((s, OFF_NA_Q + h * HEAD_DIM, OFF_NA_K + h * HEAD_DIM, slice(OFF_NA_V + h * HEAD_DIM,
                                                                                    OFF_NA_V + (h + 1) * HEAD_DIM)))
        for mi in range(2 * DA_HEADS):
            j = mi // 2
            maps.append((s, OFF_DA_Q + mi * DA_DIM, OFF_DA_K + mi * DA_DIM,
                         slice(OFF_DA_V + j * 2 * DA_DIM, OFF_DA_V + (j + 1) * 2 * DA_DIM)))
    rows_of = lambda s: slice(s * SEQ, (s + 1) * SEQ)
    scores = [_dot_nt(u_ref[rows_of(s), q0:q0 + HEAD_DIM], u_ref[rows_of(s), k0:k0 + HEAD_DIM])
              for s, q0, k0, _ in maps]
    exps = [jnp.exp2(sc - jnp.max(sc, -1, keepdims=True)) for sc in scores]
    outs = [_dot(e.astype(BF16), u_ref[rows_of(s), vs]) / jnp.sum(e, -1, keepdims=True)
            for e, (s, _, _, vs) in zip(exps, maps)]
    per_seq = NA_HEADS + 2 * DA_HEADS
    zero_row = jnp.zeros((1, SC_WIDTH), F32)
    for s in range(CTX_SEQ_PER_STEP):
        rows = rows_of(s)
        o = outs[s * per_seq:(s + 1) * per_seq]
        vc = u_ref[rows, OFF_SC_C:OFF_SC_X].astype(F32) * u_ref[rows, OFF_SC_X:OFF_DA_Q].astype(F32)
        conv = _gated_conv(u_ref[rows, OFF_SC_B:OFF_SC_C].astype(F32), vc, zero_row, zero_row, cw_ref[...], cb_ref[...])
        da = [_diff_head_norm(o[NA_HEADS + 2 * j], o[NA_HEADS + 2 * j + 1], lam, gain, lam_init)
              for j in range(DA_HEADS)]
        o_ref[rows, :] = jnp.concatenate(o[:NA_HEADS] + [conv] + da, axis=-1).astype(BF16)


def _ctx_mix(u, da_lambda, da_norm_g, conv_w, conv_b, l, lam_init):
    step_rows = CTX_SEQ_PER_STEP * SEQ
    return pl.pallas_call(
        functools.partial(_ctx_mix_kernel, lam_init=lam_init),
        grid=(BATCH // CTX_SEQ_PER_STEP,),
        in_specs=[
            pl.BlockSpec((step_rows, IN_WIDTH), lambda b: (b, 0)),
            pl.BlockSpec((None, 4, DA_DIM), lambda b: (l, 0, 0)),
            pl.BlockSpec((None, 1, 2 * DA_DIM), lambda b: (l, 0, 0)),
            pl.BlockSpec((None, 3, SC_WIDTH), lambda b: (l, 0, 0)),
            pl.BlockSpec((None, 1, SC_WIDTH), lambda b: (l, 0, 0)),
        ],
        out_specs=pl.BlockSpec((step_rows, MIX_WIDTH), lambda b: (b, 0)),
        out_shape=jax.ShapeDtypeStruct((BATCH * SEQ, MIX_WIDTH), BF16),
        compiler_params=_params("parallel"),
        name="mix_context",
    )(u, da_lambda, da_norm_g.reshape(DEPTH, 1, 2 * DA_DIM), conv_w, conv_b.reshape(DEPTH, 1, SC_WIDTH))


def _lat_mix_kernel(uq_ref, nak_ref, nav_ref, sc_ref, dakv_ref, cnak_ref, cnav_ref, cdak_ref, cdav_ref,
                    bias_ref, lam_ref, gain_ref, cw_ref, cb_ref, o_ref, kall_ref, vall_ref, cnav1_ref, *, lam_init):
    qt = pl.program_id(1)
    n_qt = pl.num_programs(1)
    lam = _lambda(lam_ref, lam_init)
    gain = gain_ref[...]

    @pl.when(qt == 0)
    def _():
        kall_ref[:DEC_SEQ, :] = dakv_ref[:, :DA_QK_WIDTH]
        kall_ref[DEC_SEQ:, :] = cdak_ref[...]
        for j in range(DA_HEADS):
            vs = slice(j * 2 * DA_DIM, (j + 1) * 2 * DA_DIM)
            cols = slice(j * 4 * DA_DIM, (j + 1) * 4 * DA_DIM)
            vall_ref[:DEC_SEQ, cols] = _with_ones(dakv_ref[:, DA_QK_WIDTH + j * 2 * DA_DIM:DA_QK_WIDTH + (j + 1) * 2 * DA_DIM])
            vall_ref[DEC_SEQ:, cols] = _with_ones(cdav_ref[:, vs])
        for h in range(NA_HEADS):
            cnav1_ref[:, h * 2 * HEAD_DIM:(h + 1) * 2 * HEAD_DIM] = _with_ones(cnav_ref[:, h * HEAD_DIM:(h + 1) * HEAD_DIM])

    outs = []
    key_row0 = jnp.clip(qt * QT_ROWS - NA_WIN_ROWS // 2, 0, GRID_ROWS - NA_KEY_ROWS)
    k0 = pl.multiple_of(key_row0 * GRID_W, GRID_W)
    jobs = []
    for h in range(NA_HEADS):
        hs = slice(h * HEAD_DIM, (h + 1) * HEAD_DIM)
        q = uq_ref[:, OFF_NA_Q + h * HEAD_DIM:OFF_NA_Q + (h + 1) * HEAD_DIM]
        k_loc = nak_ref[pl.ds(k0, NA_KEYS), hs]
        v_ones = jnp.concatenate([_with_ones(nav_ref[pl.ds(k0, NA_KEYS), hs]),
                                  cnav1_ref[:, h * 2 * HEAD_DIM:(h + 1) * 2 * HEAD_DIM]], axis=0)
        s_loc = _dot_nt(q, k_loc) + bias_ref[h]
        s_ctx = _dot_nt(q, cnak_ref[:, hs])
        jobs.append(([s_loc, s_ctx], v_ones, HEAD_DIM))
    outs += _softmax_pv_each(jobs)
    t0 = pl.multiple_of(qt * QT, QT)
    halo = 16
    before = sc_ref[pl.ds(pl.multiple_of(jnp.maximum(t0 - halo, 0), halo), halo), :].astype(F32)
    after = sc_ref[pl.ds(pl.multiple_of(jnp.minimum(t0 + QT, DEC_SEQ - halo), halo), halo), :].astype(F32)
    cur = sc_ref[pl.ds(t0, QT), :].astype(F32)
    v_prev = before[halo - 1:halo, SC_WIDTH:2 * SC_WIDTH] * before[halo - 1:halo, 2 * SC_WIDTH:]
    v_next = after[0:1, SC_WIDTH:2 * SC_WIDTH] * after[0:1, 2 * SC_WIDTH:]
    v_prev = jnp.where(qt > 0, v_prev, 0.0)
    v_next = jnp.where(qt < n_qt - 1, v_next, 0.0)
    outs.append(_gated_conv(cur[:, :SC_WIDTH], cur[:, SC_WIDTH:2 * SC_WIDTH] * cur[:, 2 * SC_WIDTH:],
                            v_prev, v_next, cw_ref[...], cb_ref[...]))
    for j0 in range(0, DA_HEADS, DA_HEADS_PER_STAGE):
        jobs = []
        for j in range(j0, j0 + DA_HEADS_PER_STAGE):
            v_ones = vall_ref[:, j * 4 * DA_DIM:(j + 1) * 4 * DA_DIM]
            for mi in (2 * j, 2 * j + 1):
                q = uq_ref[:, OFF_DA_Q + mi * DA_DIM:OFF_DA_Q + (mi + 1) * DA_DIM]
                jobs.append(([_dot_nt(q, kall_ref[:, mi * DA_DIM:(mi + 1) * DA_DIM])], v_ones, 2 * DA_DIM))
        o = _softmax_pv_each(jobs)
        for t in range(DA_HEADS_PER_STAGE):
            outs.append(_diff_head_norm(o[2 * t], o[2 * t + 1], lam, gain, lam_init))
    o_ref[...] = jnp.concatenate(outs, axis=-1).astype(BF16)


def _lat_mix(u, caches, bias_tab, da_lambda, da_norm_g, conv_w, conv_b, l, lam_init):
    cnak, cnav, cdak, cdav = caches
    n_qt = DEC_SEQ // QT
    q0 = BATCH * SEQ // QT
    b0 = BATCH * SEQ // DEC_SEQ

    def bias_map(b, qt):
        return (l, 0, jnp.where(qt == 0, 0, jnp.where(qt == n_qt - 1, 2, 1)), 0, 0)

    return pl.pallas_call(
        functools.partial(_lat_mix_kernel, lam_init=lam_init),
        grid=(DEC_BATCH, n_qt),
        in_specs=[
            pl.BlockSpec((QT, IN_WIDTH), lambda b, qt: (q0 + b * n_qt + qt, 0)),
            pl.BlockSpec((DEC_SEQ, NA_WIDTH), lambda b, qt: (b0 + b, OFF_NA_K // NA_WIDTH)),
            pl.BlockSpec((DEC_SEQ, NA_WIDTH), lambda b, qt: (b0 + b, OFF_NA_V // NA_WIDTH)),
            pl.BlockSpec((DEC_SEQ, 3 * SC_WIDTH), lambda b, qt: (b0 + b, OFF_SC_B // (3 * SC_WIDTH))),
            pl.BlockSpec((DEC_SEQ, DA_QK_WIDTH + DA_V_WIDTH), lambda b, qt: (b0 + b, OFF_DA_K // (DA_QK_WIDTH + DA_V_WIDTH))),
            pl.BlockSpec((None, None, PAST_LEN, NA_WIDTH), lambda b, qt: (b, l, 0, 0)),
            pl.BlockSpec((None, None, PAST_LEN, NA_WIDTH), lambda b, qt: (b, l, 0, 0)),
            pl.BlockSpec((None, None, PAST_LEN, DA_QK_WIDTH), lambda b, qt: (b, l, 0, 0)),
            pl.BlockSpec((None, None, PAST_LEN, DA_V_WIDTH), lambda b, qt: (b, l, 0, 0)),
            pl.BlockSpec((None, NA_HEADS, None, QT, NA_KEYS), bias_map),
            pl.BlockSpec((None, 4, DA_DIM), lambda b, qt: (l, 0, 0)),
            pl.BlockSpec((None, 1, 2 * DA_DIM), lambda b, qt: (l, 0, 0)),
            pl.BlockSpec((None, 3, SC_WIDTH), lambda b, qt: (l, 0, 0)),
            pl.BlockSpec((None, 1, SC_WIDTH), lambda b, qt: (l, 0, 0)),
        ],
        out_specs=pl.BlockSpec((QT, MIX_WIDTH), lambda b, qt: (b * n_qt + qt, 0)),
        out_shape=jax.ShapeDtypeStruct((DEC_BATCH * DEC_SEQ, MIX_WIDTH), BF16),
        scratch_shapes=[
            pltpu.VMEM((DEC_SEQ + PAST_LEN, DA_QK_WIDTH), BF16),
            pltpu.VMEM((DEC_SEQ + PAST_LEN, 2 * DA_V_WIDTH), BF16),
            pltpu.VMEM((PAST_LEN, 2 * NA_WIDTH), BF16),
        ],
        compiler_params=_params("parallel", "arbitrary"),
        name="mix_latent",
    )(u, u, u, u, u, cnak, cnav, cdak, cdav, bias_tab, da_lambda,
      da_norm_g.reshape(DEPTH, 1, 2 * DA_DIM), conv_w, conv_b.reshape(DEPTH, 1, SC_WIDTH))


def _first_max(vals):
    m = functools.reduce(jnp.maximum, vals)
    hot, taken = [], None
    for v in vals:
        is_max = v == m
        if taken is None:
            hot.append(is_max)
            taken = is_max
        else:
            hot.append(is_max & ~taken)
            taken = taken | is_max
    return hot, m


def _pick(hot, vals):
    out = vals[-1]
    for h, v in zip(hot[-2::-1], vals[-2::-1]):
        out = jnp.where(h, v, out)
    return out


def _route(h2, h2_bf, wr_ref, br_ref, earlier_ref):
    tm = h2.shape[0]
    w = wr_ref[...]
    w_hi = w.astype(BF16)
    w_lo = (w - w_hi.astype(F32)).astype(BF16)
    h_lo = (h2 - h2_bf.astype(F32)).astype(BF16)
    both = _dot_nt(jnp.concatenate([w_hi, w_lo], axis=0), h2_bf)
    z = both[:N_EXPERTS] + both[N_EXPERTS:] + _dot_nt(w_hi, h_lo)
    scores = jax.nn.sigmoid(z)
    biased = scores + br_ref[...]
    P = [biased[k * N_GROUPS:(k + 1) * N_GROUPS] for k in range(PER_GROUP)]
    S = [scores[k * N_GROUPS:(k + 1) * N_GROUPS] for k in range(PER_GROUP)]
    pair_sums = [P[i] + P[j] for i in range(PER_GROUP) for j in range(i + 1, PER_GROUP)]
    group_score = functools.reduce(jnp.maximum, pair_sums)
    sel, _ = _first_max([group_score[g:g + 1] for g in range(N_GROUPS)])
    c = [_pick(sel, [P[k][g:g + 1] for g in range(N_GROUPS)]) for k in range(PER_GROUP)]
    cs = [_pick(sel, [S[k][g:g + 1] for g in range(N_GROUPS)]) for k in range(PER_GROUP)]
    t1, _ = _first_max(c)
    t2, _ = _first_max([jnp.where(t, -jnp.inf, v) for t, v in zip(t1, c)])
    w1 = functools.reduce(jnp.add, [jnp.where(t, v, 0.0) for t, v in zip(t1, cs)])
    w2 = functools.reduce(jnp.add, [jnp.where(t, v, 0.0) for t, v in zip(t2, cs)])
    total = w1 + w2
    slot_gate = [jnp.where(a, w1 / total, jnp.where(b, w2 / total, 0.0)) for a, b in zip(t1, t2)]
    onehot = jnp.concatenate([s.astype(F32) for s in sel] + [jnp.zeros((8 - N_GROUPS, tm), F32)], axis=0)
    rank = _dot(onehot.astype(BF16), earlier_ref[...])
    count = jnp.sum(onehot, axis=1, keepdims=True)
    n_chunks = jnp.floor((count + (MOE_CHUNK - 1)) * (1.0 / MOE_CHUNK))
    pos = jnp.zeros((1, tm), F32)
    start = jnp.zeros((1, 1), F32)
    for g in range(N_GROUPS):
        pos = jnp.where(sel[g], start + rank[g:g + 1], pos)
        start = start + n_chunks[g:g + 1] * MOE_CHUNK
    rows = [pos] + slot_gate + [jnp.zeros((128 - 1 - PER_GROUP, tm), F32)]
    return jnp.concatenate(rows, axis=0), n_chunks


def _post_kernel(mixp_ref, mixs_ref, xp_ref, xs_ref, mod_ref, wout_ref, g_ref, b_ref, wr_ref, br_ref,
                 x1_ref, h2_ref, tok_ref, pos_ref, nch_ref, wbf_ref, earlier_ref):
    i = pl.program_id(0)
    is_ctx = i < CTX_TILES

    @pl.when(i == 0)
    def _():
        wbf_ref[...] = wout_ref[...].astype(BF16)
        earlier_ref[...] = (lax.broadcasted_iota(jnp.int32, (TM_POST, TM_POST), 0)
                            < lax.broadcasted_iota(jnp.int32, (TM_POST, TM_POST), 1)).astype(BF16)

    mod = mod_ref[...]
    n = TM_POST // POST_SPLIT
    parts = [slice(p * n, (p + 1) * n) for p in range(POST_SPLIT)]
    ys = [_dot(jnp.where(is_ctx, mixp_ref[p, :], mixs_ref[p, :]), wbf_ref[...]) for p in parts]
    xs = [jnp.where(is_ctx, xp_ref[p, :], xs_ref[p, :]) for p in parts]
    x1s = [t * g_ref[...] + b_ref[...] for t in _ln_each([ALPHA * x + mod[2:3] * y for x, y in zip(xs, ys)])]
    h2s = [t * (1.0 + mod[4:5]) + mod[3:4] for t in _ln_each(x1s)]
    for p, x1 in zip(parts, x1s):
        x1_ref[p, :] = x1
    h2 = jnp.concatenate(h2s, axis=0)
    h2_bf = h2.astype(BF16)
    h2_ref[...] = h2_bf
    rows, n_chunks = _route(h2, h2_bf, wr_ref, br_ref, earlier_ref)
    tok_ref[...] = rows.T
    pos_ref[...] = rows[0:8]
    nch_ref[...] = jnp.broadcast_to(n_chunks, (8, 128)).astype(jnp.int32)


def _post(mix_p, mix_s, xp, xs, mod_l, w_out, ln_g, ln_b, wr_t, br_t, l):
    n = xp.shape[0] + xs.shape[0]
    route = functools.partial(_tile_stream, n_ctx_tiles=CTX_TILES, lat_tiles_per_batch=DEC_SEQ // TM_POST)
    return pl.pallas_call(
        _post_kernel,
        grid=(n // TM_POST,),
        in_specs=[
            pl.BlockSpec((TM_POST, MIX_WIDTH), lambda i: (route(i)[0], 0)),
            pl.BlockSpec((TM_POST, MIX_WIDTH), lambda i: (route(i)[1], 0)),
            pl.BlockSpec((TM_POST, D_MODEL), lambda i: (route(i)[0], 0)),
            pl.BlockSpec((TM_POST, D_MODEL), lambda i: (route(i)[1], 0)),
            pl.BlockSpec((None, 6, D_MODEL), lambda i: (route(i)[2], 0, 0)),
            pl.BlockSpec((None, MIX_WIDTH, D_MODEL), lambda i: (l, 0, 0), pipeline_mode=pl.Buffered(1)),
            pl.BlockSpec((None, 1, D_MODEL), lambda i: (l, 0, 0)),
            pl.BlockSpec((None, 1, D_MODEL), lambda i: (l, 0, 0)),
            pl.BlockSpec((N_EXPERTS, D_MODEL), lambda i: (0, 0)),
            pl.BlockSpec((N_EXPERTS, 1), lambda i: (0, 0)),
        ],
        out_specs=[
            pl.BlockSpec((TM_POST, D_MODEL), lambda i: (i, 0)),
            pl.BlockSpec((TM_POST, D_MODEL), lambda i: (i, 0)),
            pl.BlockSpec((TM_POST, 128), lambda i: (i, 0)),
            pl.BlockSpec((None, 8, TM_POST), lambda i: (i, 0, 0)),
            pl.BlockSpec((None, 8, 128), lambda i: (i, 0, 0)),
        ],
        out_shape=[
            jax.ShapeDtypeStruct((n, D_MODEL), F32),
            jax.ShapeDtypeStruct((n, D_MODEL), BF16),
            jax.ShapeDtypeStruct((n, 128), F32),
            jax.ShapeDtypeStruct((n // TM_POST, 8, TM_POST), F32),
            jax.ShapeDtypeStruct((n // TM_POST, 8, 128), jnp.int32),
        ],
        scratch_shapes=[pltpu.VMEM((MIX_WIDTH, D_MODEL), BF16), pltpu.VMEM((TM_POST, TM_POST), BF16)],
        compiler_params=_params("arbitrary"),
        name="post",
    )(mix_p, mix_s, xp, xs, mod_l, w_out, ln_g.reshape(DEPTH, 1, D_MODEL), ln_b.reshape(DEPTH, 1, D_MODEL), wr_t, br_t)


def _moe_kernel(nch_ref, h_ref, tokn_ref, pos_ref, tok_ref, x1_ref, mod_ref, w1f_ref, w3f_ref, w2f_ref, g_ref, b_ref,
                op_ref, os_ref, w13_ref, w2_ref, hs_ref, gs_ref, ys_ref):
    step = pl.program_id(0)
    half = PER_GROUP * D_EXPERT

    for k in range(PER_GROUP):
        @pl.when((step < N_EXPERTS) & (step % PER_GROUP == k))
        def _(k=k):
            g = step // PER_GROUP
            w13_ref[g, :, k * D_EXPERT:(k + 1) * D_EXPERT] = w1f_ref[...].astype(BF16)
            w13_ref[g, :, half + k * D_EXPERT:half + (k + 1) * D_EXPERT] = w3f_ref[...].astype(BF16)
            w2_ref[g, k * D_EXPERT:(k + 1) * D_EXPERT, :] = w2f_ref[...].astype(BF16)

    @pl.when(step == N_EXPERTS - 1)
    def _():
        hs, gs = _moe_sort(h_ref, tokn_ref, pos_ref)
        hs_ref[...] = hs
        gs_ref[...] = gs
        ys_ref[...] = jnp.zeros_like(ys_ref)

    @pl.when(step >= N_EXPERTS)
    def _():
        i = step - N_EXPERTS
        _moe_groups(i, nch_ref, w13_ref, w2_ref, hs_ref, gs_ref, ys_ref)
        hs, gs = _moe_sort(h_ref, tokn_ref, pos_ref)
        pos_col = tok_ref[:, 0:1]
        unsort = (lax.broadcasted_iota(jnp.int32, (TM_MOE, MOE_SORTED), 1).astype(F32) == pos_col).astype(BF16)
        y = _dot(unsort, ys_ref[...])
        hs_ref[...] = hs
        gs_ref[...] = gs
        ys_ref[...] = jnp.zeros_like(ys_ref)
        mod = mod_ref[...]
        out = _ln(ALPHA * x1_ref[...] + mod[5:6] * y) * g_ref[...] + b_ref[...]

        @pl.when(i < CTX_TILES)
        def _():
            op_ref[...] = out

        @pl.when(i >= CTX_TILES)
        def _():
            os_ref[...] = out


def _moe_sort(h_ref, tok_ref, pos_ref):
    tok = tok_ref[...]
    slot = lax.broadcasted_iota(jnp.int32, (MOE_SORTED, TM_MOE), 0).astype(F32)
    sort = (slot == pos_ref[0:1, :]).astype(BF16)
    tok_hi = tok.astype(BF16)
    tok_lo = (tok - tok_hi.astype(F32)).astype(BF16)
    z = _dot(sort, jnp.concatenate([h_ref[...], tok_hi, tok_lo], axis=1))
    return z[:, :D_MODEL].astype(BF16), z[:, D_MODEL:D_MODEL + 128] + z[:, D_MODEL + 128:]


def _moe_groups(i, nch_ref, w13_ref, w2_ref, hs_ref, gs_ref, ys_ref):
    def experts(g, chunk0, n_rows):
        r0 = pl.multiple_of(chunk0 * MOE_CHUNK, MOE_CHUNK)
        rows = hs_ref[pl.ds(r0, n_rows), :]
        gates = gs_ref[pl.ds(r0, n_rows), :]
        half = PER_GROUP * D_EXPERT
        ab = _dot(rows, w13_ref[g])
        hid = []
        for k in range(PER_GROUP):
            a = ab[:, k * D_EXPERT:(k + 1) * D_EXPERT]
            b = ab[:, half + k * D_EXPERT:half + (k + 1) * D_EXPERT]
            hid.append((a * jax.nn.sigmoid(a) * b * gates[:, 1 + k:2 + k]).astype(BF16))
        ys_ref[pl.ds(r0, n_rows), :] = _dot(jnp.concatenate(hid, axis=1), w2_ref[g]).astype(BF16)

    def group(g, first):
        n = nch_ref[i * N_GROUPS + g]
        n_blocks = n // MOE_BLOCK
        rest = n - n_blocks * MOE_BLOCK

        def block(c, carry):
            experts(g, first + c * MOE_BLOCK, MOE_BLOCK * MOE_CHUNK)
            return carry

        lax.fori_loop(0, n_blocks, block, 0)
        for m in range(1, MOE_BLOCK):
            @pl.when(rest == m)
            def _(m=m):
                experts(g, first + n_blocks * MOE_BLOCK, m * MOE_CHUNK)
        return first + n

    lax.fori_loop(0, N_GROUPS, group, 0)


def _moe(h2, tok, pos, nch, x1, mod_l, w1, w3, w2, ln_g, ln_b, l):
    n_tiles = x1.shape[0] // TM_MOE
    n_ctx = CTX_TILES * TM_MOE
    per_batch = DEC_SEQ // TM_MOE

    def tile(step):
        return jnp.maximum(step - N_EXPERTS, 0)

    def next_tile(step):
        return jnp.clip(step - (N_EXPERTS - 1), 0, n_tiles - 1)

    def expert(step):
        return jnp.minimum(step, N_EXPERTS - 1)

    def mod_map(step, s):
        return (_tile_stream(tile(step), CTX_TILES, per_batch)[2], 0, 0)

    grid_spec = pltpu.PrefetchScalarGridSpec(
        num_scalar_prefetch=1,
        grid=(N_EXPERTS + n_tiles,),
        in_specs=[
            pl.BlockSpec((TM_MOE, D_MODEL), lambda step, s: (next_tile(step), 0)),
            pl.BlockSpec((TM_MOE, 128), lambda step, s: (next_tile(step), 0)),
            pl.BlockSpec((None, 8, TM_MOE), lambda step, s: (next_tile(step), 0, 0)),
            pl.BlockSpec((TM_MOE, 128), lambda step, s: (tile(step), 0)),
            pl.BlockSpec((TM_MOE, D_MODEL), lambda step, s: (tile(step), 0)),
            pl.BlockSpec((None, 6, D_MODEL), mod_map),
            pl.BlockSpec((None, None, D_MODEL, D_EXPERT), lambda step, s: (l, expert(step), 0, 0)),
            pl.BlockSpec((None, None, D_MODEL, D_EXPERT), lambda step, s: (l, expert(step), 0, 0)),
            pl.BlockSpec((None, None, D_EXPERT, D_MODEL), lambda step, s: (l, expert(step), 0, 0)),
            pl.BlockSpec((None, 1, D_MODEL), lambda step, s: (l, 0, 0)),
            pl.BlockSpec((None, 1, D_MODEL), lambda step, s: (l, 0, 0)),
        ],
        out_specs=[
            pl.BlockSpec((TM_MOE, D_MODEL), lambda step, s: (jnp.minimum(tile(step), CTX_TILES - 1), 0)),
            pl.BlockSpec((TM_MOE, D_MODEL), lambda step, s: (jnp.maximum(tile(step) - CTX_TILES, 0), 0)),
        ],
        scratch_shapes=[
            pltpu.VMEM((N_GROUPS, D_MODEL, 2 * PER_GROUP * D_EXPERT), BF16),
            pltpu.VMEM((N_GROUPS, PER_GROUP * D_EXPERT, D_MODEL), BF16),
            pltpu.VMEM((MOE_SORTED, D_MODEL), BF16),
            pltpu.VMEM((MOE_SORTED, 128), F32),
            pltpu.VMEM((MOE_SORTED, D_MODEL), BF16),
        ],
    )
    return pl.pallas_call(
        _moe_kernel,
        grid_spec=grid_spec,
        out_shape=[jax.ShapeDtypeStruct((n_ctx, D_MODEL), F32),
                   jax.ShapeDtypeStruct((x1.shape[0] - n_ctx, D_MODEL), F32)],
        compiler_params=_params("arbitrary"),
        name="moe",
    )(nch[:, :N_GROUPS, 0].reshape(-1), h2, tok, pos, tok, x1, mod_l, w1, w3, w2,
      ln_g.reshape(DEPTH, 1, D_MODEL), ln_b.reshape(DEPTH, 1, D_MODEL))


def _rope_tables():
    t = np.arange(DEC_SEQ)
    half = DA_DIM // 4
    inv_freq = ROPE_BASE ** (-np.arange(half, dtype=np.float32) / half)
    cos, sin = [], []
    for pos in (t // GRID_W, t % GRID_W):
        ang = pos.astype(np.float32)[:, None] * inv_freq[None, :]
        c, s = np.cos(ang), np.sin(ang)
        cos += [c, c]
        sin += [-s, s]
    cos = np.tile(np.concatenate(cos, axis=1), (1, 2 * DA_HEADS))
    sin = np.tile(np.concatenate(sin, axis=1), (1, 2 * DA_HEADS))
    cos = np.concatenate([np.ones((TM, DA_QK_WIDTH), np.float32), cos], axis=0)
    sin = np.concatenate([np.zeros((TM, DA_QK_WIDTH), np.float32), sin], axis=0)
    return jnp.asarray(cos, F32), jnp.asarray(sin, F32)


_NA_TILE_KINDS = ((0, 0), (2 * QT_ROWS, 2 * QT_ROWS - NA_WIN_ROWS // 2), (GRID_ROWS - QT_ROWS, GRID_ROWS - NA_KEY_ROWS))


def _na_bias_kernel(by_ref, o_ref):
    qc = lax.broadcasted_iota(jnp.int32, (GRID_W, GRID_W), 0)
    kc = lax.broadcasted_iota(jnp.int32, (GRID_W, GRID_W), 1)
    win_c0 = jnp.clip(qc - NA_WIN_COLS // 2, 0, GRID_W - NA_WIN_COLS)
    in_cols = (kc >= win_c0) & (kc < win_c0 + NA_WIN_COLS)
    masked = jnp.full((GRID_W, GRID_W), -jnp.inf, F32)
    for kind, (r0, key_row0) in enumerate(_NA_TILE_KINDS):
        for a in range(QT_ROWS):
            qr = r0 + a
            win_r0 = min(max(qr - NA_WIN_ROWS // 2, 0), GRID_ROWS - NA_WIN_ROWS)
            blocks = []
            for b in range(NA_KEY_ROWS):
                kr = key_row0 + b
                if win_r0 <= kr < win_r0 + NA_WIN_ROWS:
                    blocks.append(jnp.where(in_cols, by_ref[kr - qr + NA_WIN_ROWS - 1] * LOG2E, masked))
                else:
                    blocks.append(masked)
            o_ref[kind, a * GRID_W:(a + 1) * GRID_W, :] = jnp.concatenate(blocks, axis=1)


def _na_bias_tables(na_rel_bias):
    n_dr, n_dc = 2 * NA_WIN_ROWS - 1, 2 * NA_WIN_COLS - 1
    cols = np.arange(GRID_W)
    d_col = np.clip(cols[None, :] - cols[:, None], 1 - NA_WIN_COLS, NA_WIN_COLS - 1) + NA_WIN_COLS - 1
    col_sel = (d_col[None] == np.arange(n_dc)[:, None, None]).astype(np.float32)
    by_col = jnp.einsum('lhab,bqc->lhaqc', na_rel_bias.astype(F32), col_sel, precision=lax.Precision.HIGHEST)
    return pl.pallas_call(
        _na_bias_kernel,
        grid=(DEPTH, NA_HEADS),
        in_specs=[pl.BlockSpec((None, None, n_dr, GRID_W, GRID_W), lambda l, h: (l, h, 0, 0, 0))],
        out_specs=pl.BlockSpec((None, None, len(_NA_TILE_KINDS), QT, NA_KEYS), lambda l, h: (l, h, 0, 0, 0)),
        out_shape=jax.ShapeDtypeStruct((DEPTH, NA_HEADS, len(_NA_TILE_KINDS), QT, NA_KEYS), F32),
        compiler_params=_params("parallel", "parallel"),
        name="na_bias_table",
    )(by_col)


def kernel(x_prompt, x_sample, cache_na_k, cache_na_v, cache_da_k, cache_da_v, c, c_ctx, w_mod, b_mod, w_in,
           na_rel_bias, sc_conv_w, sc_conv_b, da_lambda, da_norm_g, w_out, ln1_g, ln1_b, w_router, b_router,
           moe_w1, moe_w3, moe_w2, ln2_g, ln2_b):
    xp = x_prompt.reshape(BATCH * SEQ, D_MODEL)
    xs = x_sample.reshape(DEC_BATCH * DEC_SEQ, D_MODEL)

    cond = jnp.concatenate([c_ctx[None, :], c, jnp.zeros((N_COND - 1 - DEC_BATCH, D_MODEL), F32)], axis=0)
    mod = _modulation(cond, w_mod, b_mod).reshape(DEPTH, N_COND, 6, D_MODEL)

    slot_major = np.arange(N_EXPERTS).reshape(N_GROUPS, PER_GROUP).T.reshape(-1)
    wr_t = w_router.T[slot_major]
    br_t = b_router.astype(F32)[slot_major].reshape(N_EXPERTS, 1)

    caches = (
        cache_na_k.reshape(DEC_BATCH, DEPTH, PAST_LEN, NA_WIDTH).astype(BF16),
        cache_na_v.reshape(DEC_BATCH, DEPTH, PAST_LEN, NA_WIDTH).astype(BF16),
        cache_da_k.reshape(DEC_BATCH, DEPTH, PAST_LEN, DA_QK_WIDTH).astype(BF16),
        cache_da_v.reshape(DEC_BATCH, DEPTH, PAST_LEN, DA_V_WIDTH).astype(BF16),
    )
    rope_tabs = _rope_tables()
    bias_tab = _na_bias_tables(na_rel_bias)

    new_caches = None
    for l in range(DEPTH):
        lam_init = 0.8 - 0.6 * math.exp(-0.3 * l)
        mod_l = mod[l]
        u, *new_caches = _inproj(xp, xs, mod_l, w_in, l, rope_tabs, cache_bufs=new_caches)
        mix_p = _ctx_mix(u, da_lambda, da_norm_g, sc_conv_w, sc_conv_b, l, lam_init)
        mix_s = _lat_mix(u, caches, bias_tab, da_lambda, da_norm_g, sc_conv_w, sc_conv_b, l, lam_init)
        x1, h2, *route = _post(mix_p, mix_s, xp, xs, mod_l, w_out, ln1_g, ln1_b, wr_t, br_t, l)
        xp, xs = _moe(h2, *route, x1, mod_l, moe_w1, moe_w3, moe_w2, ln2_g, ln2_b, l)

    nak, nav, dak, dav = new_caches
    return (xp.reshape(BATCH, SEQ, D_MODEL), xs.reshape(DEC_BATCH, DEC_SEQ, D_MODEL),
            nak.reshape(BATCH, DEPTH, SEQ, NA_HEADS, HEAD_DIM), nav.reshape(BATCH, DEPTH, SEQ, NA_HEADS, HEAD_DIM),
            dak.reshape(BATCH, DEPTH, SEQ, 2 * DA_HEADS, DA_DIM), dav.reshape(BATCH, DEPTH, SEQ, DA_HEADS, 2 * DA_DIM))
```

```python
import functools
import math

import numpy as np
import jax
import jax.numpy as jnp
from jax import lax
from jax.experimental import pallas as pl
from jax.experimental.pallas import tpu as pltpu

F32 = jnp.float32
BF16 = jnp.bfloat16

D_MODEL = 1024
BATCH = 16
SEQ = 256
DEPTH = 4
DEC_BATCH = 2
DEC_SEQ = 2048
PAST_LEN = 512
GRID_W = 64
GRID_ROWS = DEC_SEQ // GRID_W
HEAD_DIM = 64
NA_HEADS = 4
NA_WIDTH = NA_HEADS * HEAD_DIM
NA_WIN_ROWS = 8
NA_WIN_COLS = 16
SC_WIDTH = 256
DA_HEADS = 4
DA_DIM = 64
DA_QK_WIDTH = 2 * DA_HEADS * DA_DIM
DA_V_WIDTH = DA_HEADS * 2 * DA_DIM
MIX_WIDTH = NA_WIDTH + SC_WIDTH + DA_V_WIDTH
IN_WIDTH = 3 * NA_WIDTH + 3 * SC_WIDTH + 2 * DA_QK_WIDTH + DA_V_WIDTH
OFF_NA_Q = 0
OFF_NA_K = NA_WIDTH
OFF_NA_V = 2 * NA_WIDTH
OFF_SC_B = 3 * NA_WIDTH
OFF_SC_C = OFF_SC_B + SC_WIDTH
OFF_SC_X = OFF_SC_C + SC_WIDTH
OFF_DA_Q = OFF_SC_X + SC_WIDTH
OFF_DA_K = OFF_DA_Q + DA_QK_WIDTH
OFF_DA_V = OFF_DA_K + DA_QK_WIDTH
ROPE_BASE = 10000.0
N_EXPERTS = 16
N_GROUPS = 4
PER_GROUP = N_EXPERTS // N_GROUPS
D_EXPERT = 256
ALPHA = (2 * DEPTH) ** 0.25
LN_EPS = 1e-5
LOG2E = math.log2(math.e)
Q_SCALE = HEAD_DIM ** -0.5 * LOG2E

N_COND = 8
TM = 512
SEQ_PER_TILE = TM // SEQ
CTX_TILES = BATCH * SEQ // TM
CTX_SEQ_PER_STEP = 2
DA_HEADS_PER_STAGE = 2
POST_SPLIT = 2
TM_MOE = 512
TM_POST = TM_MOE
MOE_CHUNK = 32
MOE_BLOCK = 8
MOE_SORTED = TM_MOE + (N_GROUPS - 1) * MOE_CHUNK
assert TM == TM_MOE
QT = 256
QT_ROWS = QT // GRID_W
NA_KEY_ROWS = 12
NA_KEYS = NA_KEY_ROWS * GRID_W
VMEM_LIMIT = 60 * 1024 * 1024


def _dot(a, b):
    return jnp.dot(a, b, preferred_element_type=F32)


def _dot_nt(a, b):
    return lax.dot_general(a, b, (((1,), (1,)), ((), ())), preferred_element_type=F32)


def _ln(x):
    mu = jnp.mean(x, -1, keepdims=True)
    xc = x - mu
    var = jnp.mean(xc * xc, -1, keepdims=True)
    return xc * lax.rsqrt(var + LN_EPS)


def _ln_each(xs):
    mus = [jnp.mean(x, -1, keepdims=True) for x in xs]
    xcs = [x - mu for x, mu in zip(xs, mus)]
    vs = [jnp.mean(xc * xc, -1, keepdims=True) for xc in xcs]
    return [xc * lax.rsqrt(v + LN_EPS) for xc, v in zip(xcs, vs)]


def _params(*sem):
    return pltpu.CompilerParams(dimension_semantics=sem, vmem_limit_bytes=VMEM_LIMIT)


def _mod_kernel(cond_ref, w_ref, b_ref, o_ref):
    c = cond_ref[...]
    s = (c * jax.nn.sigmoid(c)).astype(BF16)
    o_ref[...] = _dot(s, w_ref[...].astype(BF16)) + b_ref[...]


def _modulation(cond, w_mod, b_mod):
    tn = 3 * D_MODEL
    return pl.pallas_call(
        _mod_kernel,
        grid=(DEPTH, 6 * D_MODEL // tn),
        in_specs=[
            pl.BlockSpec((N_COND, D_MODEL), lambda l, j: (0, 0)),
            pl.BlockSpec((None, D_MODEL, tn), lambda l, j: (l, 0, j)),
            pl.BlockSpec((None, 1, tn), lambda l, j: (l, 0, j)),
        ],
        out_specs=pl.BlockSpec((None, N_COND, tn), lambda l, j: (l, 0, j)),
        out_shape=jax.ShapeDtypeStruct((DEPTH, N_COND, 6 * D_MODEL), F32),
        compiler_params=_params("parallel", "parallel"),
        name="modulation",
    )(cond, w_mod, b_mod.reshape(DEPTH, 1, 6 * D_MODEL))


def _rope(t, cos, sin):
    lane = lax.broadcasted_iota(jnp.int32, t.shape, 1)
    first = (lane // 16) % 2 == 0
    n = t.shape[1]
    swapped = jnp.where(first, pltpu.roll(t, n - 16, 1), pltpu.roll(t, 16, 1))
    return t * cos + swapped * sin


def _tile_stream(i, n_ctx_tiles, lat_tiles_per_batch):
    lat = jnp.maximum(i - n_ctx_tiles, 0)
    return jnp.minimum(i, n_ctx_tiles - 1), lat, jnp.where(i < n_ctx_tiles, 0, 1 + lat // lat_tiles_per_batch)


def _inproj_kernel(*refs):
    xp_ref, xs_ref, mod_ref, w_ref, cos_ref, sin_ref = refs[:6]
    u_ref, nak_ref, nav_ref, dak_ref, dav_ref, wbf_ref = refs[-6:]
    i = pl.program_id(0)
    is_ctx = i < CTX_TILES

    @pl.when(i == 0)
    def _():
        wbf_ref[...] = w_ref[...].astype(BF16)

    mod = mod_ref[...]
    parts = [slice(s * SEQ, (s + 1) * SEQ) for s in range(SEQ_PER_TILE)]
    hs = [t * (1.0 + mod[1:2]) + mod[0:1]
          for t in _ln_each([jnp.where(is_ctx, xp_ref[p, :], xs_ref[p, :]) for p in parts])]
    us = [_dot(h.astype(BF16), wbf_ref[...]) for h in hs]
    for p, u in zip(parts, us):
        cos, sin = cos_ref[p, :], sin_ref[p, :]
        da_q = _rope(u[:, OFF_DA_Q:OFF_DA_K], cos, sin)
        da_k = _rope(u[:, OFF_DA_K:OFF_DA_V], cos, sin)
        u_ref[p, :OFF_NA_K] = (u[:, :OFF_NA_K] * Q_SCALE).astype(BF16)
        u_ref[p, OFF_NA_K:OFF_DA_Q] = u[:, OFF_NA_K:OFF_DA_Q].astype(BF16)
        u_ref[p, OFF_DA_Q:OFF_DA_K] = (da_q * Q_SCALE).astype(BF16)
        u_ref[p, OFF_DA_K:OFF_DA_V] = da_k.astype(BF16)
        u_ref[p, OFF_DA_V:] = u[:, OFF_DA_V:].astype(BF16)

    @pl.when(is_ctx)
    def _():
        for s, u in enumerate(us):
            nak_ref[s] = u[:, OFF_NA_K:OFF_NA_V]
            nav_ref[s] = u[:, OFF_NA_V:OFF_SC_B]
            dak_ref[s] = u[:, OFF_DA_K:OFF_DA_V]
            for j in range(DA_HEADS):
                dav_ref[s, pl.ds(j, SEQ, stride=DA_HEADS), :] = (
                    u[:, OFF_DA_V + j * 2 * DA_DIM:OFF_DA_V + (j + 1) * 2 * DA_DIM])


def _inproj(xp, xs, mod_l, w_in, l, rope_tabs, cache_bufs=None):
    route = functools.partial(_tile_stream, n_ctx_tiles=CTX_TILES, lat_tiles_per_batch=DEC_SEQ // TM)
    in_specs = [
        pl.BlockSpec((TM, D_MODEL), lambda i: (route(i)[0], 0)),
        pl.BlockSpec((TM, D_MODEL), lambda i: (route(i)[1], 0)),
        pl.BlockSpec((None, 6, D_MODEL), lambda i: (route(i)[2], 0, 0)),
        pl.BlockSpec((None, D_MODEL, IN_WIDTH), lambda i: (l, 0, 0), pipeline_mode=pl.Buffered(1)),
        pl.BlockSpec((TM, DA_QK_WIDTH), lambda i: (jnp.where(i < CTX_TILES, 0, 1 + route(i)[1] % (DEC_SEQ // TM)), 0)),
        pl.BlockSpec((TM, DA_QK_WIDTH), lambda i: (jnp.where(i < CTX_TILES, 0, 1 + route(i)[1] % (DEC_SEQ // TM)), 0)),
    ]
    args = [xp, xs, mod_l, w_in, *rope_tabs]
    out_specs = [pl.BlockSpec((TM, IN_WIDTH), lambda i: (i, 0))]
    out_shape = [jax.ShapeDtypeStruct((xp.shape[0] + xs.shape[0], IN_WIDTH), BF16)]
    for rows, width in ((SEQ, NA_WIDTH), (SEQ, NA_WIDTH), (SEQ, DA_QK_WIDTH), (SEQ * DA_HEADS, 2 * DA_DIM)):
        out_specs.append(pl.BlockSpec((SEQ_PER_TILE, None, rows, width), lambda i: (route(i)[0], l, 0, 0)))
        out_shape.append(jax.ShapeDtypeStruct((BATCH, DEPTH, rows, width), F32))
    aliases = {}
    if cache_bufs is not None:
        aliases = {len(args) + k: 1 + k for k in range(4)}
        in_specs += [pl.BlockSpec(memory_space=pl.ANY)] * 4
        args += list(cache_bufs)
    return pl.pallas_call(
        _inproj_kernel,
        grid=(CTX_TILES + xs.shape[0] // TM,),
        in_specs=in_specs,
        out_specs=out_specs,
        out_shape=out_shape,
        scratch_shapes=[pltpu.VMEM((D_MODEL, IN_WIDTH), BF16)],
        input_output_aliases=aliases,
        compiler_params=_params("arbitrary"),
        name="inproj",
    )(*args)


def _lambda(lam_ref, lam_init):
    lp = lam_ref[...]
    s1 = jnp.sum(lp[0:1] * lp[1:2], axis=-1, keepdims=True)
    s2 = jnp.sum(lp[2:3] * lp[3:4], axis=-1, keepdims=True)
    return jnp.exp(s1) - jnp.exp(s2) + lam_init


def _with_ones(v):
    return jnp.concatenate([v, jnp.ones((v.shape[0], max(v.shape[1], 64)), v.dtype)], axis=1)


def _softmax_pv_each(jobs):
    ms = [functools.reduce(jnp.maximum, [jnp.max(s, -1, keepdims=True) for s in parts]) for parts, _, _ in jobs]
    es = [[jnp.exp2(s - m).astype(BF16) for s in parts] for (parts, _, _), m in zip(jobs, ms)]
    os = [_dot(e[0] if len(e) == 1 else jnp.concatenate(e, axis=1), v) for e, (_, v, _) in zip(es, jobs)]
    return [o[:, :d] / o[:, d:d + 1] for o, (_, _, d) in zip(os, jobs)]


def _diff_head_norm(o1, o2, lam, gain, lam_init):
    o = o1 - lam * o2
    o = o * lax.rsqrt(jnp.mean(o * o, -1, keepdims=True) + LN_EPS)
    return o * gain * (1.0 - lam_init)


def _gated_conv(b, v, v_prev_row, v_next_row, w, bias):
    n = v.shape[0]
    row = lax.broadcasted_iota(jnp.int32, v.shape, 0)
    prev = jnp.where(row == 0, v_prev_row, pltpu.roll(v, 1, 0))
    nxt = jnp.where(row == n - 1, v_next_row, pltpu.roll(v, n - 1, 0))
    return b * (prev * w[0:1] + v * w[1:2] + nxt * w[2:3] + bias)


def _ctx_mix_kernel(u_ref, lam_ref, gain_ref, cw_ref, cb_ref, o_ref, *, lam_init):
    lam = _lambda(lam_ref, lam_init)
    gain = gain_ref[...]
    maps = []
    for s in range(CTX_SEQ_PER_STEP):
        for h in range(NA_HEADS):
            maps.append((s, OFF_NA_Q + h * HEAD_DIM, OFF_NA_K + h * HEAD_DIM, slice(OFF_NA_V + h * HEAD_DIM,
                                                                                    OFF_NA_V + (h + 1) * HEAD_DIM)))
        for mi in range(2 * DA_HEADS):
            j = mi // 2
            maps.append((s, OFF_DA_Q + mi * DA_DIM, OFF_DA_K + mi * DA_DIM,
                         slice(OFF_DA_V + j * 2 * DA_DIM, OFF_DA_V + (j + 1) * 2 * DA_DIM)))
    rows_of = lambda s: slice(s * SEQ, (s + 1) * SEQ)
    scores = [_dot_nt(u_ref[rows_of(s), q0:q0 + HEAD_DIM], u_ref[rows_of(s), k0:k0 + HEAD_DIM])
              for s, q0, k0, _ in maps]
    exps = [jnp.exp2(sc - jnp.max(sc, -1, keepdims=True)) for sc in scores]
    outs = [_dot(e.astype(BF16), u_ref[rows_of(s), vs]) / jnp.sum(e, -1, keepdims=True)
            for e, (s, _, _, vs) in zip(exps, maps)]
    per_seq = NA_HEADS + 2 * DA_HEADS
    zero_row = jnp.zeros((1, SC_WIDTH), F32)
    for s in range(CTX_SEQ_PER_STEP):
        rows = rows_of(s)
        o = outs[s * per_seq:(s + 1) * per_seq]
        vc = u_ref[rows, OFF_SC_C:OFF_SC_X].astype(F32) * u_ref[rows, OFF_SC_X:OFF_DA_Q].astype(F32)
        conv = _gated_conv(u_ref[rows, OFF_SC_B:OFF_SC_C].astype(F32), vc, zero_row, zero_row, cw_ref[...], cb_ref[...])
        da = [_diff_head_norm(o[NA_HEADS + 2 * j], o[NA_HEADS + 2 * j + 1], lam, gain, lam_init)
              for j in range(DA_HEADS)]
        o_ref[rows, :] = jnp.concatenate(o[:NA_HEADS] + [conv] + da, axis=-1).astype(BF16)


def _ctx_mix(u, da_lambda, da_norm_g, conv_w, conv_b, l, lam_init):
    step_rows = CTX_SEQ_PER_STEP * SEQ
    return pl.pallas_call(
        functools.partial(_ctx_mix_kernel, lam_init=lam_init),
        grid=(BATCH // CTX_SEQ_PER_STEP,),
        in_specs=[
            pl.BlockSpec((step_rows, IN_WIDTH), lambda b: (b, 0)),
            pl.BlockSpec((None, 4, DA_DIM), lambda b: (l, 0, 0)),
            pl.BlockSpec((None, 1, 2 * DA_DIM), lambda b: (l, 0, 0)),
            pl.BlockSpec((None, 3, SC_WIDTH), lambda b: (l, 0, 0)),
            pl.BlockSpec((None, 1, SC_WIDTH), lambda b: (l, 0, 0)),
        ],
        out_specs=pl.BlockSpec((step_rows, MIX_WIDTH), lambda b: (b, 0)),
        out_shape=jax.ShapeDtypeStruct((BATCH * SEQ, MIX_WIDTH), BF16),
        compiler_params=_params("parallel"),
        name="mix_context",
    )(u, da_lambda, da_norm_g.reshape(DEPTH, 1, 2 * DA_DIM), conv_w, conv_b.reshape(DEPTH, 1, SC_WIDTH))


def _lat_mix_kernel(uq_ref, nak_ref, nav_ref, sc_ref, dakv_ref, cnak_ref, cnav_ref, cdak_ref, cdav_ref,
                    bias_ref, lam_ref, gain_ref, cw_ref, cb_ref, o_ref, kall_ref, vall_ref, nav1_ref, *, lam_init):
    qt = pl.program_id(1)
    n_qt = pl.num_programs(1)
    lam = _lambda(lam_ref, lam_init)
    gain = gain_ref[...]

    @pl.when(qt == 0)
    def _():
        kall_ref[:DEC_SEQ, :] = dakv_ref[:, :DA_QK_WIDTH]
        kall_ref[DEC_SEQ:, :] = cdak_ref[...]
        for j in range(DA_HEADS):
            vs = slice(j * 2 * DA_DIM, (j + 1) * 2 * DA_DIM)
            cols = slice(j * 4 * DA_DIM, (j + 1) * 4 * DA_DIM)
            vall_ref[:DEC_SEQ, cols] = _with_ones(dakv_ref[:, DA_QK_WIDTH + j * 2 * DA_DIM:DA_QK_WIDTH + (j + 1) * 2 * DA_DIM])
            vall_ref[DEC_SEQ:, cols] = _with_ones(cdav_ref[:, vs])
        for h in range(NA_HEADS):
            hs, cols = slice(h * HEAD_DIM, (h + 1) * HEAD_DIM), slice(h * 2 * HEAD_DIM, (h + 1) * 2 * HEAD_DIM)
            nav1_ref[:DEC_SEQ, cols] = _with_ones(nav_ref[:, hs])
            nav1_ref[DEC_SEQ:, cols] = _with_ones(cnav_ref[:, hs])

    outs = []
    key_row0 = jnp.clip(qt * QT_ROWS - NA_WIN_ROWS // 2, 0, GRID_ROWS - NA_KEY_ROWS)
    k0 = pl.multiple_of(key_row0 * GRID_W, GRID_W)
    jobs = []
    for h in range(NA_HEADS):
        hs = slice(h * HEAD_DIM, (h + 1) * HEAD_DIM)
        q = uq_ref[:, OFF_NA_Q + h * HEAD_DIM:OFF_NA_Q + (h + 1) * HEAD_DIM]
        k_loc = nak_ref[pl.ds(k0, NA_KEYS), hs]
        cols = slice(h * 2 * HEAD_DIM, (h + 1) * 2 * HEAD_DIM)
        v_ones = jnp.concatenate([nav1_ref[pl.ds(k0, NA_KEYS), cols], nav1_ref[DEC_SEQ:, cols]], axis=0)
        s_loc = _dot_nt(q, k_loc) + bias_ref[h]
        s_ctx = _dot_nt(q, cnak_ref[:, hs])
        jobs.append(([s_loc, s_ctx], v_ones, HEAD_DIM))
    outs += _softmax_pv_each(jobs)
    t0 = pl.multiple_of(qt * QT, QT)
    halo = 16
    before = sc_ref[pl.ds(pl.multiple_of(jnp.maximum(t0 - halo, 0), halo), halo), :].astype(F32)
    after = sc_ref[pl.ds(pl.multiple_of(jnp.minimum(t0 + QT, DEC_SEQ - halo), halo), halo), :].astype(F32)
    cur = sc_ref[pl.ds(t0, QT), :].astype(F32)
    v_prev = before[halo - 1:halo, SC_WIDTH:2 * SC_WIDTH] * before[halo - 1:halo, 2 * SC_WIDTH:]
    v_next = after[0:1, SC_WIDTH:2 * SC_WIDTH] * after[0:1, 2 * SC_WIDTH:]
    v_prev = jnp.where(qt > 0, v_prev, 0.0)
    v_next = jnp.where(qt < n_qt - 1, v_next, 0.0)
    outs.append(_gated_conv(cur[:, :SC_WIDTH], cur[:, SC_WIDTH:2 * SC_WIDTH] * cur[:, 2 * SC_WIDTH:],
                            v_prev, v_next, cw_ref[...], cb_ref[...]))
    for j0 in range(0, DA_HEADS, DA_HEADS_PER_STAGE):
        jobs = []
        for j in range(j0, j0 + DA_HEADS_PER_STAGE):
            v_ones = vall_ref[:, j * 4 * DA_DIM:(j + 1) * 4 * DA_DIM]
            for mi in (2 * j, 2 * j + 1):
                q = uq_ref[:, OFF_DA_Q + mi * DA_DIM:OFF_DA_Q + (mi + 1) * DA_DIM]
                jobs.append(([_dot_nt(q, kall_ref[:, mi * DA_DIM:(mi + 1) * DA_DIM])], v_ones, 2 * DA_DIM))
        o = _softmax_pv_each(jobs)
        for t in range(DA_HEADS_PER_STAGE):
            outs.append(_diff_head_norm(o[2 * t], o[2 * t + 1], lam, gain, lam_init))
    o_ref[...] = jnp.concatenate(outs, axis=-1).astype(BF16)


def _lat_mix(u, caches, bias_tab, da_lambda, da_norm_g, conv_w, conv_b, l, lam_init):
    cnak, cnav, cdak, cdav = caches
    n_qt = DEC_SEQ // QT
    q0 = BATCH * SEQ // QT
    b0 = BATCH * SEQ // DEC_SEQ

    def bias_map(b, qt):
        return (l, 0, jnp.where(qt == 0, 0, jnp.where(qt == n_qt - 1, 2, 1)), 0, 0)

    return pl.pallas_call(
        functools.partial(_lat_mix_kernel, lam_init=lam_init),
        grid=(DEC_BATCH, n_qt),
        in_specs=[
            pl.BlockSpec((QT, IN_WIDTH), lambda b, qt: (q0 + b * n_qt + qt, 0)),
            pl.BlockSpec((DEC_SEQ, NA_WIDTH), lambda b, qt: (b0 + b, OFF_NA_K // NA_WIDTH)),
            pl.BlockSpec((DEC_SEQ, NA_WIDTH), lambda b, qt: (b0 + b, OFF_NA_V // NA_WIDTH)),
            pl.BlockSpec((DEC_SEQ, 3 * SC_WIDTH), lambda b, qt: (b0 + b, OFF_SC_B // (3 * SC_WIDTH))),
            pl.BlockSpec((DEC_SEQ, DA_QK_WIDTH + DA_V_WIDTH), lambda b, qt: (b0 + b, OFF_DA_K // (DA_QK_WIDTH + DA_V_WIDTH))),
            pl.BlockSpec((None, None, PAST_LEN, NA_WIDTH), lambda b, qt: (b, l, 0, 0)),
            pl.BlockSpec((None, None, PAST_LEN, NA_WIDTH), lambda b, qt: (b, l, 0, 0)),
            pl.BlockSpec((None, None, PAST_LEN, DA_QK_WIDTH), lambda b, qt: (b, l, 0, 0)),
            pl.BlockSpec((None, None, PAST_LEN, DA_V_WIDTH), lambda b, qt: (b, l, 0, 0)),
            pl.BlockSpec((None, NA_HEADS, None, QT, NA_KEYS), bias_map),
            pl.BlockSpec((None, 4, DA_DIM), lambda b, qt: (l, 0, 0)),
            pl.BlockSpec((None, 1, 2 * DA_DIM), lambda b, qt: (l, 0, 0)),
            pl.BlockSpec((None, 3, SC_WIDTH), lambda b, qt: (l, 0, 0)),
            pl.BlockSpec((None, 1, SC_WIDTH), lambda b, qt: (l, 0, 0)),
        ],
        out_specs=pl.BlockSpec((QT, MIX_WIDTH), lambda b, qt: (b * n_qt + qt, 0)),
        out_shape=jax.ShapeDtypeStruct((DEC_BATCH * DEC_SEQ, MIX_WIDTH), BF16),
        scratch_shapes=[
            pltpu.VMEM((DEC_SEQ + PAST_LEN, DA_QK_WIDTH), BF16),
            pltpu.VMEM((DEC_SEQ + PAST_LEN, 2 * DA_V_WIDTH), BF16),
            pltpu.VMEM((DEC_SEQ + PAST_LEN, 2 * NA_WIDTH), BF16),
        ],
        compiler_params=_params("parallel", "arbitrary"),
        name="mix_latent",
    )(u, u, u, u, u, cnak, cnav, cdak, cdav, bias_tab, da_lambda,
      da_norm_g.reshape(DEPTH, 1, 2 * DA_DIM), conv_w, conv_b.reshape(DEPTH, 1, SC_WIDTH))


def _first_max(vals):
    m = functools.reduce(jnp.maximum, vals)
    hot, taken = [], None
    for v in vals:
        is_max = v == m
        if taken is None:
            hot.append(is_max)
            taken = is_max
        else:
            hot.append(is_max & ~taken)
            taken = taken | is_max
    return hot, m


def _pick(hot, vals):
    out = vals[-1]
    for h, v in zip(hot[-2::-1], vals[-2::-1]):
        out = jnp.where(h, v, out)
    return out


def _route(h2, h2_bf, wr_ref, br_ref, earlier_ref):
    tm = h2.shape[0]
    w = wr_ref[...]
    w_hi = w.astype(BF16)
    w_lo = (w - w_hi.astype(F32)).astype(BF16)
    h_lo = (h2 - h2_bf.astype(F32)).astype(BF16)
    both = _dot_nt(jnp.concatenate([w_hi, w_lo], axis=0), h2_bf)
    z = both[:N_EXPERTS] + both[N_EXPERTS:] + _dot_nt(w_hi, h_lo)
    scores = jax.nn.sigmoid(z)
    biased = scores + br_ref[...]
    P = [biased[k * N_GROUPS:(k + 1) * N_GROUPS] for k in range(PER_GROUP)]
    S = [scores[k * N_GROUPS:(k + 1) * N_GROUPS] for k in range(PER_GROUP)]
    pair_sums = [P[i] + P[j] for i in range(PER_GROUP) for j in range(i + 1, PER_GROUP)]
    group_score = functools.reduce(jnp.maximum, pair_sums)
    sel, _ = _first_max([group_score[g:g + 1] for g in range(N_GROUPS)])
    c = [_pick(sel, [P[k][g:g + 1] for g in range(N_GROUPS)]) for k in range(PER_GROUP)]
    cs = [_pick(sel, [S[k][g:g + 1] for g in range(N_GROUPS)]) for k in range(PER_GROUP)]
    t1, _ = _first_max(c)
    t2, _ = _first_max([jnp.where(t, -jnp.inf, v) for t, v in zip(t1, c)])
    w1 = functools.reduce(jnp.add, [jnp.where(t, v, 0.0) for t, v in zip(t1, cs)])
    w2 = functools.reduce(jnp.add, [jnp.where(t, v, 0.0) for t, v in zip(t2, cs)])
    total = w1 + w2
    slot_gate = [jnp.where(a, w1 / total, jnp.where(b, w2 / total, 0.0)) for a, b in zip(t1, t2)]
    onehot = jnp.concatenate([s.astype(F32) for s in sel] + [jnp.zeros((8 - N_GROUPS, tm), F32)], axis=0)
    rank = _dot(onehot.astype(BF16), earlier_ref[...])
    count = jnp.sum(onehot, axis=1, keepdims=True)
    n_chunks = jnp.floor((count + (MOE_CHUNK - 1)) * (1.0 / MOE_CHUNK))
    pos = jnp.zeros((1, tm), F32)
    start = jnp.zeros((1, 1), F32)
    for g in range(N_GROUPS):
        pos = jnp.where(sel[g], start + rank[g:g + 1], pos)
        start = start + n_chunks[g:g + 1] * MOE_CHUNK
    rows = [pos] + slot_gate + [jnp.zeros((128 - 1 - PER_GROUP, tm), F32)]
    return jnp.concatenate(rows, axis=0), n_chunks


def _post_kernel(mixp_ref, mixs_ref, xp_ref, xs_ref, mod_ref, wout_ref, g_ref, b_ref, wr_ref, br_ref,
                 x1_ref, h2_ref, tok_ref, pos_ref, nch_ref, wbf_ref, earlier_ref):
    i = pl.program_id(0)
    is_ctx = i < CTX_TILES

    @pl.when(i == 0)
    def _():
        wbf_ref[...] = wout_ref[...].astype(BF16)
        earlier_ref[...] = (lax.broadcasted_iota(jnp.int32, (TM_POST, TM_POST), 0)
                            < lax.broadcasted_iota(jnp.int32, (TM_POST, TM_POST), 1)).astype(BF16)

    mod = mod_ref[...]
    n = TM_POST // POST_SPLIT
    parts = [slice(p * n, (p + 1) * n) for p in range(POST_SPLIT)]
    ys = [_dot(jnp.where(is_ctx, mixp_ref[p, :], mixs_ref[p, :]), wbf_ref[...]) for p in parts]
    xs = [jnp.where(is_ctx, xp_ref[p, :], xs_ref[p, :]) for p in parts]
    x1s = [t * g_ref[...] + b_ref[...] for t in _ln_each([ALPHA * x + mod[2:3] * y for x, y in zip(xs, ys)])]
    h2s = [t * (1.0 + mod[4:5]) + mod[3:4] for t in _ln_each(x1s)]
    for p, x1 in zip(parts, x1s):
        x1_ref[p, :] = x1
    h2 = jnp.concatenate(h2s, axis=0)
    h2_bf = h2.astype(BF16)
    h2_ref[...] = h2_bf
    rows, n_chunks = _route(h2, h2_bf, wr_ref, br_ref, earlier_ref)
    tok_ref[...] = rows.T
    pos_ref[...] = rows[0:8]
    nch_ref[...] = jnp.broadcast_to(n_chunks, (8, 128)).astype(jnp.int32)


def _post(mix_p, mix_s, xp, xs, mod_l, w_out, ln_g, ln_b, wr_t, br_t, l):
    n = xp.shape[0] + xs.shape[0]
    route = functools.partial(_tile_stream, n_ctx_tiles=CTX_TILES, lat_tiles_per_batch=DEC_SEQ // TM_POST)
    return pl.pallas_call(
        _post_kernel,
        grid=(n // TM_POST,),
        in_specs=[
            pl.BlockSpec((TM_POST, MIX_WIDTH), lambda i: (route(i)[0], 0)),
            pl.BlockSpec((TM_POST, MIX_WIDTH), lambda i: (route(i)[1], 0)),
            pl.BlockSpec((TM_POST, D_MODEL), lambda i: (route(i)[0], 0)),
            pl.BlockSpec((TM_POST, D_MODEL), lambda i: (route(i)[1], 0)),
            pl.BlockSpec((None, 6, D_MODEL), lambda i: (route(i)[2], 0, 0)),
            pl.BlockSpec((None, MIX_WIDTH, D_MODEL), lambda i: (l, 0, 0), pipeline_mode=pl.Buffered(1)),
            pl.BlockSpec((None, 1, D_MODEL), lambda i: (l, 0, 0)),
            pl.BlockSpec((None, 1, D_MODEL), lambda i: (l, 0, 0)),
            pl.BlockSpec((N_EXPERTS, D_MODEL), lambda i: (0, 0)),
            pl.BlockSpec((N_EXPERTS, 1), lambda i: (0, 0)),
        ],
        out_specs=[
            pl.BlockSpec((TM_POST, D_MODEL), lambda i: (i, 0)),
            pl.BlockSpec((TM_POST, D_MODEL), lambda i: (i, 0)),
            pl.BlockSpec((TM_POST, 128), lambda i: (i, 0)),
            pl.BlockSpec((None, 8, TM_POST), lambda i: (i, 0, 0)),
            pl.BlockSpec((None, 8, 128), lambda i: (i, 0, 0)),
        ],
        out_shape=[
            jax.ShapeDtypeStruct((n, D_MODEL), F32),
            jax.ShapeDtypeStruct((n, D_MODEL), BF16),
            jax.ShapeDtypeStruct((n, 128), F32),
            jax.ShapeDtypeStruct((n // TM_POST, 8, TM_POST), F32),
            jax.ShapeDtypeStruct((n // TM_POST, 8, 128), jnp.int32),
        ],
        scratch_shapes=[pltpu.VMEM((MIX_WIDTH, D_MODEL), BF16), pltpu.VMEM((TM_POST, TM_POST), BF16)],
        compiler_params=_params("arbitrary"),
        name="post",
    )(mix_p, mix_s, xp, xs, mod_l, w_out, ln_g.reshape(DEPTH, 1, D_MODEL), ln_b.reshape(DEPTH, 1, D_MODEL), wr_t, br_t)


def _moe_kernel(nch_ref, h_ref, tokn_ref, pos_ref, tok_ref, x1_ref, mod_ref, w1f_ref, w3f_ref, w2f_ref, g_ref, b_ref,
                op_ref, os_ref, w13_ref, w2_ref, hs_ref, gs_ref, ys_ref):
    step = pl.program_id(0)
    half = PER_GROUP * D_EXPERT

    for k in range(PER_GROUP):
        @pl.when((step < N_EXPERTS) & (step % PER_GROUP == k))
        def _(k=k):
            g = step // PER_GROUP
            w13_ref[g, :, k * D_EXPERT:(k + 1) * D_EXPERT] = w1f_ref[...].astype(BF16)
            w13_ref[g, :, half + k * D_EXPERT:half + (k + 1) * D_EXPERT] = w3f_ref[...].astype(BF16)
            w2_ref[g, k * D_EXPERT:(k + 1) * D_EXPERT, :] = w2f_ref[...].astype(BF16)

    @pl.when(step == N_EXPERTS - 1)
    def _():
        hs, gs = _moe_sort(h_ref, tokn_ref, pos_ref)
        hs_ref[...] = hs
        gs_ref[...] = gs
        ys_ref[...] = jnp.zeros_like(ys_ref)

    @pl.when(step >= N_EXPERTS)
    def _():
        i = step - N_EXPERTS
        _moe_groups(i, nch_ref, w13_ref, w2_ref, hs_ref, gs_ref, ys_ref)
        hs, gs = _moe_sort(h_ref, tokn_ref, pos_ref)
        pos_col = tok_ref[:, 0:1]
        unsort = (lax.broadcasted_iota(jnp.int32, (TM_MOE, MOE_SORTED), 1).astype(F32) == pos_col).astype(BF16)
        y = _dot(unsort, ys_ref[...])
        hs_ref[...] = hs
        gs_ref[...] = gs
        ys_ref[...] = jnp.zeros_like(ys_ref)
        mod = mod_ref[...]
        out = _ln(ALPHA * x1_ref[...] + mod[5:6] * y) * g_ref[...] + b_ref[...]

        @pl.when(i < CTX_TILES)
        def _():
            op_ref[...] = out

        @pl.when(i >= CTX_TILES)
        def _():
            os_ref[...] = out


def _moe_sort(h_ref, tok_ref, pos_ref):
    tok = tok_ref[...]
    slot = lax.broadcasted_iota(jnp.int32, (MOE_SORTED, TM_MOE), 0).astype(F32)
    sort = (slot == pos_ref[0:1, :]).astype(BF16)
    tok_hi = tok.astype(BF16)
    tok_lo = (tok - tok_hi.astype(F32)).astype(BF16)
    z = _dot(sort, jnp.concatenate([h_ref[...], tok_hi, tok_lo], axis=1))
    return z[:, :D_MODEL].astype(BF16), z[:, D_MODEL:D_MODEL + 128] + z[:, D_MODEL + 128:]


def _moe_groups(i, nch_ref, w13_ref, w2_ref, hs_ref, gs_ref, ys_ref):
    def experts(g, chunk0, n_rows):
        r0 = pl.multiple_of(chunk0 * MOE_CHUNK, MOE_CHUNK)
        rows = hs_ref[pl.ds(r0, n_rows), :]
        gates = gs_ref[pl.ds(r0, n_rows), :]
        half = PER_GROUP * D_EXPERT
        ab = _dot(rows, w13_ref[g])
        hid = []
        for k in range(PER_GROUP):
            a = ab[:, k * D_EXPERT:(k + 1) * D_EXPERT]
            b = ab[:, half + k * D_EXPERT:half + (k + 1) * D_EXPERT]
            hid.append((a * jax.nn.sigmoid(a) * b * gates[:, 1 + k:2 + k]).astype(BF16))
        ys_ref[pl.ds(r0, n_rows), :] = _dot(jnp.concatenate(hid, axis=1), w2_ref[g]).astype(BF16)

    def group(g, first):
        n = nch_ref[i * N_GROUPS + g]
        n_blocks = n // MOE_BLOCK
        rest = n - n_blocks * MOE_BLOCK

        def block(c, carry):
            experts(g, first + c * MOE_BLOCK, MOE_BLOCK * MOE_CHUNK)
            return carry

        lax.fori_loop(0, n_blocks, block, 0)
        for m in range(1, MOE_BLOCK):
            @pl.when(rest == m)
            def _(m=m):
                experts(g, first + n_blocks * MOE_BLOCK, m * MOE_CHUNK)
        return first + n

    lax.fori_loop(0, N_GROUPS, group, 0)


def _moe(h2, tok, pos, nch, x1, mod_l, w1, w3, w2, ln_g, ln_b, l):
    n_tiles = x1.shape[0] // TM_MOE
    n_ctx = CTX_TILES * TM_MOE
    per_batch = DEC_SEQ // TM_MOE

    def tile(step):
        return jnp.maximum(step - N_EXPERTS, 0)

    def next_tile(step):
        return jnp.clip(step - (N_EXPERTS - 1), 0, n_tiles - 1)

    def expert(step):
        return jnp.minimum(step, N_EXPERTS - 1)

    def mod_map(step, s):
        return (_tile_stream(tile(step), CTX_TILES, per_batch)[2], 0, 0)

    grid_spec = pltpu.PrefetchScalarGridSpec(
        num_scalar_prefetch=1,
        grid=(N_EXPERTS + n_tiles,),
        in_specs=[
            pl.BlockSpec((TM_MOE, D_MODEL), lambda step, s: (next_tile(step), 0)),
            pl.BlockSpec((TM_MOE, 128), lambda step, s: (next_tile(step), 0)),
            pl.BlockSpec((None, 8, TM_MOE), lambda step, s: (next_tile(step), 0, 0)),
            pl.BlockSpec((TM_MOE, 128), lambda step, s: (tile(step), 0)),
            pl.BlockSpec((TM_MOE, D_MODEL), lambda step, s: (tile(step), 0)),
            pl.BlockSpec((None, 6, D_MODEL), mod_map),
            pl.BlockSpec((None, None, D_MODEL, D_EXPERT), lambda step, s: (l, expert(step), 0, 0)),
            pl.BlockSpec((None, None, D_MODEL, D_EXPERT), lambda step, s: (l, expert(step), 0, 0)),
            pl.BlockSpec((None, None, D_EXPERT, D_MODEL), lambda step, s: (l, expert(step), 0, 0)),
            pl.BlockSpec((None, 1, D_MODEL), lambda step, s: (l, 0, 0)),
            pl.BlockSpec((None, 1, D_MODEL), lambda step, s: (l, 0, 0)),
        ],
        out_specs=[
            pl.BlockSpec((TM_MOE, D_MODEL), lambda step, s: (jnp.minimum(tile(step), CTX_TILES - 1), 0)),
            pl.BlockSpec((TM_MOE, D_MODEL), lambda step, s: (jnp.maximum(tile(step) - CTX_TILES, 0), 0)),
        ],
        scratch_shapes=[
            pltpu.VMEM((N_GROUPS, D_MODEL, 2 * PER_GROUP * D_EXPERT), BF16),
            pltpu.VMEM((N_GROUPS, PER_GROUP * D_EXPERT, D_MODEL), BF16),
            pltpu.VMEM((MOE_SORTED, D_MODEL), BF16),
            pltpu.VMEM((MOE_SORTED, 128), F32),
            pltpu.VMEM((MOE_SORTED, D_MODEL), BF16),
        ],
    )
    return pl.pallas_call(
        _moe_kernel,
        grid_spec=grid_spec,
        out_shape=[jax.ShapeDtypeStruct((n_ctx, D_MODEL), F32),
                   jax.ShapeDtypeStruct((x1.shape[0] - n_ctx, D_MODEL), F32)],
        compiler_params=_params("arbitrary"),
        name="moe",
    )(nch[:, :N_GROUPS, 0].reshape(-1), h2, tok, pos, tok, x1, mod_l, w1, w3, w2,
      ln_g.reshape(DEPTH, 1, D_MODEL), ln_b.reshape(DEPTH, 1, D_MODEL))


def _rope_tables():
    t = np.arange(DEC_SEQ)
    half = DA_DIM // 4
    inv_freq = ROPE_BASE ** (-np.arange(half, dtype=np.float32) / half)
    cos, sin = [], []
    for pos in (t // GRID_W, t % GRID_W):
        ang = pos.astype(np.float32)[:, None] * inv_freq[None, :]
        c, s = np.cos(ang), np.sin(ang)
        cos += [c, c]
        sin += [-s, s]
    cos = np.tile(np.concatenate(cos, axis=1), (1, 2 * DA_HEADS))
    sin = np.tile(np.concatenate(sin, axis=1), (1, 2 * DA_HEADS))
    cos = np.concatenate([np.ones((TM, DA_QK_WIDTH), np.float32), cos], axis=0)
    sin = np.concatenate([np.zeros((TM, DA_QK_WIDTH), np.float32), sin], axis=0)
    return jnp.asarray(cos, F32), jnp.asarray(sin, F32)


_NA_TILE_KINDS = ((0, 0), (2 * QT_ROWS, 2 * QT_ROWS - NA_WIN_ROWS // 2), (GRID_ROWS - QT_ROWS, GRID_ROWS - NA_KEY_ROWS))


def _na_bias_kernel(by_ref, o_ref):
    qc = lax.broadcasted_iota(jnp.int32, (GRID_W, GRID_W), 0)
    kc = lax.broadcasted_iota(jnp.int32, (GRID_W, GRID_W), 1)
    win_c0 = jnp.clip(qc - NA_WIN_COLS // 2, 0, GRID_W - NA_WIN_COLS)
    in_cols = (kc >= win_c0) & (kc < win_c0 + NA_WIN_COLS)
    masked = jnp.full((GRID_W, GRID_W), -jnp.inf, F32)
    for kind, (r0, key_row0) in enumerate(_NA_TILE_KINDS):
        for a in range(QT_ROWS):
            qr = r0 + a
            win_r0 = min(max(qr - NA_WIN_ROWS // 2, 0), GRID_ROWS - NA_WIN_ROWS)
            blocks = []
            for b in range(NA_KEY_ROWS):
                kr = key_row0 + b
                if win_r0 <= kr < win_r0 + NA_WIN_ROWS:
                    blocks.append(jnp.where(in_cols, by_ref[kr - qr + NA_WIN_ROWS - 1] * LOG2E, masked))
                else:
                    blocks.append(masked)
            o_ref[kind, a * GRID_W:(a + 1) * GRID_W, :] = jnp.concatenate(blocks, axis=1)


def _na_bias_tables(na_rel_bias):
    n_dr, n_dc = 2 * NA_WIN_ROWS - 1, 2 * NA_WIN_COLS - 1
    cols = np.arange(GRID_W)
    d_col = np.clip(cols[None, :] - cols[:, None], 1 - NA_WIN_COLS, NA_WIN_COLS - 1) + NA_WIN_COLS - 1
    col_sel = (d_col[None] == np.arange(n_dc)[:, None, None]).astype(np.float32)
    by_col = jnp.einsum('lhab,bqc->lhaqc', na_rel_bias.astype(F32), col_sel, precision=lax.Precision.HIGHEST)
    return pl.pallas_call(
        _na_bias_kernel,
        grid=(DEPTH, NA_HEADS),
        in_specs=[pl.BlockSpec((None, None, n_dr, GRID_W, GRID_W), lambda l, h: (l, h, 0, 0, 0))],
        out_specs=pl.BlockSpec((None, None, len(_NA_TILE_KINDS), QT, NA_KEYS), lambda l, h: (l, h, 0, 0, 0)),
        out_shape=jax.ShapeDtypeStruct((DEPTH, NA_HEADS, len(_NA_TILE_KINDS), QT, NA_KEYS), F32),
        compiler_params=_params("parallel", "parallel"),
        name="na_bias_table",
    )(by_col)


def kernel(x_prompt, x_sample, cache_na_k, cache_na_v, cache_da_k, cache_da_v, c, c_ctx, w_mod, b_mod, w_in,
           na_rel_bias, sc_conv_w, sc_conv_b, da_lambda, da_norm_g, w_out, ln1_g, ln1_b, w_router, b_router,
           moe_w1, moe_w3, moe_w2, ln2_g, ln2_b):
    xp = x_prompt.reshape(BATCH * SEQ, D_MODEL)
    xs = x_sample.reshape(DEC_BATCH * DEC_SEQ, D_MODEL)

    cond = jnp.concatenate([c_ctx[None, :], c, jnp.zeros((N_COND - 1 - DEC_BATCH, D_MODEL), F32)], axis=0)
    mod = _modulation(cond, w_mod, b_mod).reshape(DEPTH, N_COND, 6, D_MODEL)

    slot_major = np.arange(N_EXPERTS).reshape(N_GROUPS, PER_GROUP).T.reshape(-1)
    wr_t = w_router.T[slot_major]
    br_t = b_router.astype(F32)[slot_major].reshape(N_EXPERTS, 1)

    caches = (
        cache_na_k.reshape(DEC_BATCH, DEPTH, PAST_LEN, NA_WIDTH).astype(BF16),
        cache_na_v.reshape(DEC_BATCH, DEPTH, PAST_LEN, NA_WIDTH).astype(BF16),
        cache_da_k.reshape(DEC_BATCH, DEPTH, PAST_LEN, DA_QK_WIDTH).astype(BF16),
        cache_da_v.reshape(DEC_BATCH, DEPTH, PAST_LEN, DA_V_WIDTH).astype(BF16),
    )
    rope_tabs = _rope_tables()
    bias_tab = _na_bias_tables(na_rel_bias)

    new_caches = None
    for l in range(DEPTH):
        lam_init = 0.8 - 0.6 * math.exp(-0.3 * l)
        mod_l = mod[l]
        u, *new_caches = _inproj(xp, xs, mod_l, w_in, l, rope_tabs, cache_bufs=new_caches)
        mix_p = _ctx_mix(u, da_lambda, da_norm_g, sc_conv_w, sc_conv_b, l, lam_init)
        mix_s = _lat_mix(u, caches, bias_tab, da_lambda, da_norm_g, sc_conv_w, sc_conv_b, l, lam_init)
        x1, h2, *route = _post(mix_p, mix_s, xp, xs, mod_l, w_out, ln1_g, ln1_b, wr_t, br_t, l)
        xp, xs = _moe(h2, *route, x1, mod_l, moe_w1, moe_w3, moe_w2, ln2_g, ln2_b, l)

    nak, nav, dak, dav = new_caches
    return (xp.reshape(BATCH, SEQ, D_MODEL), xs.reshape(DEC_BATCH, DEC_SEQ, D_MODEL),
            nak.reshape(BATCH, DEPTH, SEQ, NA_HEADS, HEAD_DIM), nav.reshape(BATCH, DEPTH, SEQ, NA_HEADS, HEAD_DIM),
            dak.reshape(BATCH, DEPTH, SEQ, 2 * DA_HEADS, DA_DIM), dav.reshape(BATCH, DEPTH, SEQ, DA_HEADS, 2 * DA_DIM))
```

```python
import functools
import math

import numpy as np
import jax
import jax.numpy as jnp
from jax import lax
from jax.experimental import pallas as pl
from jax.experimental.pallas import tpu as pltpu

F32 = jnp.float32
BF16 = jnp.bfloat16

D_MODEL = 1024
BATCH = 16
SEQ = 256
DEPTH = 4
DEC_BATCH = 2
DEC_SEQ = 2048
PAST_LEN = 512
GRID_W = 64
GRID_ROWS = DEC_SEQ // GRID_W
HEAD_DIM = 64
NA_HEADS = 4
NA_WIDTH = NA_HEADS * HEAD_DIM
NA_WIN_ROWS = 8
NA_WIN_COLS = 16
SC_WIDTH = 256
DA_HEADS = 4
DA_DIM = 64
DA_QK_WIDTH = 2 * DA_HEADS * DA_DIM
DA_V_WIDTH = DA_HEADS * 2 * DA_DIM
MIX_WIDTH = NA_WIDTH + SC_WIDTH + DA_V_WIDTH
IN_WIDTH = 3 * NA_WIDTH + 3 * SC_WIDTH + 2 * DA_QK_WIDTH + DA_V_WIDTH
OFF_NA_Q = 0
OFF_NA_K = NA_WIDTH
OFF_NA_V = 2 * NA_WIDTH
OFF_SC_B = 3 * NA_WIDTH
OFF_SC_C = OFF_SC_B + SC_WIDTH
OFF_SC_X = OFF_SC_C + SC_WIDTH
OFF_DA_Q = OFF_SC_X + SC_WIDTH
OFF_DA_K = OFF_DA_Q + DA_QK_WIDTH
OFF_DA_V = OFF_DA_K + DA_QK_WIDTH
ROPE_BASE = 10000.0
N_EXPERTS = 16
N_GROUPS = 4
PER_GROUP = N_EXPERTS // N_GROUPS
D_EXPERT = 256
ALPHA = (2 * DEPTH) ** 0.25
LN_EPS = 1e-5
LOG2E = math.log2(math.e)
Q_SCALE = HEAD_DIM ** -0.5 * LOG2E

N_COND = 8
TM = 512
SEQ_PER_TILE = TM // SEQ
CTX_TILES = BATCH * SEQ // TM
CTX_SEQ_PER_STEP = 4
DA_HEADS_PER_STAGE = 2
POST_SPLIT = 2
TM_MOE = 512
TM_POST = TM_MOE
MOE_CHUNK = 32
MOE_BLOCK = 8
MOE_SORTED = TM_MOE + (N_GROUPS - 1) * MOE_CHUNK
assert TM == TM_MOE
QT = 256
QT_ROWS = QT // GRID_W
NA_KEY_ROWS = 12
NA_KEYS = NA_KEY_ROWS * GRID_W
VMEM_LIMIT = 60 * 1024 * 1024
LANES = 128
SUBLANES = 8
BF16_ROWS = 16


def _dot(a, b):
    return jnp.dot(a, b, preferred_element_type=F32)


def _dot_nt(a, b):
    return lax.dot_general(a, b, (((1,), (1,)), ((), ())), preferred_element_type=F32)


def _ln(x):
    mu = jnp.mean(x, -1, keepdims=True)
    xc = x - mu
    var = jnp.mean(xc * xc, -1, keepdims=True)
    return xc * lax.rsqrt(var + LN_EPS)


def _ln_each(xs):
    mus = [jnp.mean(x, -1, keepdims=True) for x in xs]
    xcs = [x - mu for x, mu in zip(xs, mus)]
    vs = [jnp.mean(xc * xc, -1, keepdims=True) for xc in xcs]
    return [xc * lax.rsqrt(v + LN_EPS) for xc, v in zip(xcs, vs)]


def _params(*sem):
    return pltpu.CompilerParams(dimension_semantics=sem, vmem_limit_bytes=VMEM_LIMIT)


def _mod_kernel(cond_ref, w_ref, b_ref, o_ref):
    c = cond_ref[...]
    s = (c * jax.nn.sigmoid(c)).astype(BF16)
    o_ref[...] = _dot(s, w_ref[...].astype(BF16)) + b_ref[...]


def _modulation(cond, w_mod, b_mod):
    tn = 3 * D_MODEL
    return pl.pallas_call(
        _mod_kernel,
        grid=(DEPTH, 6 * D_MODEL // tn),
        in_specs=[
            pl.BlockSpec((N_COND, D_MODEL), lambda l, j: (0, 0)),
            pl.BlockSpec((None, D_MODEL, tn), lambda l, j: (l, 0, j)),
            pl.BlockSpec((None, 1, tn), lambda l, j: (l, 0, j)),
        ],
        out_specs=pl.BlockSpec((None, N_COND, tn), lambda l, j: (l, 0, j)),
        out_shape=jax.ShapeDtypeStruct((DEPTH, N_COND, 6 * D_MODEL), F32),
        compiler_params=_params("parallel", "parallel"),
        name="modulation",
    )(cond, w_mod, b_mod.reshape(DEPTH, 1, 6 * D_MODEL))


def _rope(t, cos, sin):
    lane = lax.broadcasted_iota(jnp.int32, t.shape, 1)
    first = (lane // 16) % 2 == 0
    n = t.shape[1]
    swapped = jnp.where(first, pltpu.roll(t, n - 16, 1), pltpu.roll(t, 16, 1))
    return t * cos + swapped * sin


def _tile_stream(i, n_ctx_tiles, lat_tiles_per_batch):
    lat = jnp.maximum(i - n_ctx_tiles, 0)
    return jnp.minimum(i, n_ctx_tiles - 1), lat, jnp.where(i < n_ctx_tiles, 0, 1 + lat // lat_tiles_per_batch)


def _inproj_kernel(*refs):
    xp_ref, xs_ref, mod_ref, w_ref, cos_ref, sin_ref = refs[:6]
    u_ref, nak_ref, nav_ref, dak_ref, dav_ref, wbf_ref = refs[-6:]
    i = pl.program_id(0)
    is_ctx = i < CTX_TILES

    @pl.when(i == 0)
    def _():
        wbf_ref[...] = w_ref[...].astype(BF16)

    mod = mod_ref[...]
    parts = [slice(s * SEQ, (s + 1) * SEQ) for s in range(SEQ_PER_TILE)]
    hs = [t * (1.0 + mod[1:2]) + mod[0:1]
          for t in _ln_each([jnp.where(is_ctx, xp_ref[p, :], xs_ref[p, :]) for p in parts])]
    us = [_dot(h.astype(BF16), wbf_ref[...]) for h in hs]
    for p, u in zip(parts, us):
        cos, sin = cos_ref[p, :], sin_ref[p, :]
        da_q = _rope(u[:, OFF_DA_Q:OFF_DA_K], cos, sin)
        da_k = _rope(u[:, OFF_DA_K:OFF_DA_V], cos, sin)
        u_ref[p, :OFF_NA_K] = (u[:, :OFF_NA_K] * Q_SCALE).astype(BF16)
        u_ref[p, OFF_NA_K:OFF_DA_Q] = u[:, OFF_NA_K:OFF_DA_Q].astype(BF16)
        u_ref[p, OFF_DA_Q:OFF_DA_K] = (da_q * Q_SCALE).astype(BF16)
        u_ref[p, OFF_DA_K:OFF_DA_V] = da_k.astype(BF16)
        u_ref[p, OFF_DA_V:] = u[:, OFF_DA_V:].astype(BF16)

    kv_t = [(u[:, OFF_NA_K:OFF_NA_V].T, u[:, OFF_NA_V:OFF_SC_B].T, u[:, OFF_DA_K:OFF_DA_V].T) for u in us]

    @pl.when(is_ctx)
    def _():
        for s, u in enumerate(us):
            nak_ref[s], nav_ref[s], dak_ref[s] = kv_t[s]
            for j in range(DA_HEADS):
                dav_ref[s, pl.ds(j, SEQ, stride=DA_HEADS), :] = (
                    u[:, OFF_DA_V + j * 2 * DA_DIM:OFF_DA_V + (j + 1) * 2 * DA_DIM])


def _inproj(xp, xs, mod_l, w_in, l, rope_tabs, cache_bufs=None):
    route = functools.partial(_tile_stream, n_ctx_tiles=CTX_TILES, lat_tiles_per_batch=DEC_SEQ // TM)
    in_specs = [
        pl.BlockSpec((TM, D_MODEL), lambda i: (route(i)[0], 0)),
        pl.BlockSpec((TM, D_MODEL), lambda i: (route(i)[1], 0)),
        pl.BlockSpec((None, 6, D_MODEL), lambda i: (route(i)[2], 0, 0)),
        pl.BlockSpec((None, D_MODEL, IN_WIDTH), lambda i: (l, 0, 0), pipeline_mode=pl.Buffered(1)),
        pl.BlockSpec((TM, DA_QK_WIDTH), lambda i: (jnp.where(i < CTX_TILES, 0, 1 + route(i)[1] % (DEC_SEQ // TM)), 0)),
        pl.BlockSpec((TM, DA_QK_WIDTH), lambda i: (jnp.where(i < CTX_TILES, 0, 1 + route(i)[1] % (DEC_SEQ // TM)), 0)),
    ]
    args = [xp, xs, mod_l, w_in, *rope_tabs]
    out_specs = [pl.BlockSpec((TM, IN_WIDTH), lambda i: (i, 0))]
    out_shape = [jax.ShapeDtypeStruct((xp.shape[0] + xs.shape[0], IN_WIDTH), BF16)]
    for rows, width in ((NA_WIDTH, SEQ), (NA_WIDTH, SEQ), (DA_QK_WIDTH, SEQ), (SEQ * DA_HEADS, 2 * DA_DIM)):
        out_specs.append(pl.BlockSpec((SEQ_PER_TILE, None, rows, width), lambda i: (route(i)[0], l, 0, 0)))
        out_shape.append(jax.ShapeDtypeStruct((BATCH, DEPTH, rows, width), F32))
    aliases = {}
    if cache_bufs is not None:
        aliases = {len(args) + k: 1 + k for k in range(4)}
        in_specs += [pl.BlockSpec(memory_space=pl.ANY)] * 4
        args += list(cache_bufs)
    return pl.pallas_call(
        _inproj_kernel,
        grid=(CTX_TILES + xs.shape[0] // TM,),
        in_specs=in_specs,
        out_specs=out_specs,
        out_shape=out_shape,
        scratch_shapes=[pltpu.VMEM((D_MODEL, IN_WIDTH), BF16)],
        input_output_aliases=aliases,
        compiler_params=_params("arbitrary"),
        name="inproj",
    )(*args)


def _lambda(lam_ref, lam_init):
    lp = lam_ref[...]
    s1 = jnp.sum(lp[0:1] * lp[1:2], axis=-1, keepdims=True)
    s2 = jnp.sum(lp[2:3] * lp[3:4], axis=-1, keepdims=True)
    return jnp.exp(s1) - jnp.exp(s2) + lam_init


def _with_ones(v):
    return jnp.concatenate([v, jnp.ones((v.shape[0], max(v.shape[1], LANES // 2)), v.dtype)], axis=1)


def _softmax_pv_each(jobs):
    ms = [functools.reduce(jnp.maximum, [jnp.max(s, -1, keepdims=True) for s in parts]) for parts, _, _ in jobs]
    es = [[jnp.exp2(s - m).astype(BF16) for s in parts] for (parts, _, _), m in zip(jobs, ms)]
    os = [_dot(e[0] if len(e) == 1 else jnp.concatenate(e, axis=1), v) for e, (_, v, _) in zip(es, jobs)]
    return [o[:, :d] / o[:, d:d + 1] for o, (_, _, d) in zip(os, jobs)]


def _diff_head_norm(o1, o2, lam, gain, lam_init):
    o = o1 - lam * o2
    o = o * lax.rsqrt(jnp.mean(o * o, -1, keepdims=True) + LN_EPS)
    return o * gain * (1.0 - lam_init)


def _gated_conv(b, v, v_prev_row, v_next_row, w, bias):
    n = v.shape[0]
    row = lax.broadcasted_iota(jnp.int32, v.shape, 0)
    prev = jnp.where(row == 0, v_prev_row, pltpu.roll(v, 1, 0))
    nxt = jnp.where(row == n - 1, v_next_row, pltpu.roll(v, n - 1, 0))
    return b * (prev * w[0:1] + v * w[1:2] + nxt * w[2:3] + bias)


def _ctx_mix_kernel(u_ref, lam_ref, gain_ref, cw_ref, cb_ref, o_ref, *, lam_init):
    lam = _lambda(lam_ref, lam_init)
    gain = gain_ref[...]
    maps = []
    for s in range(CTX_SEQ_PER_STEP):
        for h in range(NA_HEADS):
            maps.append((s, OFF_NA_Q + h * HEAD_DIM, OFF_NA_K + h * HEAD_DIM, slice(OFF_NA_V + h * HEAD_DIM,
                                                                                    OFF_NA_V + (h + 1) * HEAD_DIM)))
        for mi in range(2 * DA_HEADS):
            j = mi // 2
            maps.append((s, OFF_DA_Q + mi * DA_DIM, OFF_DA_K + mi * DA_DIM,
                         slice(OFF_DA_V + j * 2 * DA_DIM, OFF_DA_V + (j + 1) * 2 * DA_DIM)))
    rows_of = lambda s: slice(s * SEQ, (s + 1) * SEQ)
    scores = [_dot_nt(u_ref[rows_of(s), q0:q0 + HEAD_DIM], u_ref[rows_of(s), k0:k0 + HEAD_DIM])
              for s, q0, k0, _ in maps]
    exps = [jnp.exp2(sc - jnp.max(sc, -1, keepdims=True)) for sc in scores]
    outs = [_dot(e.astype(BF16), u_ref[rows_of(s), vs]) / jnp.sum(e, -1, keepdims=True)
            for e, (s, _, _, vs) in zip(exps, maps)]
    per_seq = NA_HEADS + 2 * DA_HEADS
    zero_row = jnp.zeros((1, SC_WIDTH), F32)
    for s in range(CTX_SEQ_PER_STEP):
        rows = rows_of(s)
        o = outs[s * per_seq:(s + 1) * per_seq]
        vc = u_ref[rows, OFF_SC_C:OFF_SC_X].astype(F32) * u_ref[rows, OFF_SC_X:OFF_DA_Q].astype(F32)
        conv = _gated_conv(u_ref[rows, OFF_SC_B:OFF_SC_C].astype(F32), vc, zero_row, zero_row, cw_ref[...], cb_ref[...])
        da = [_diff_head_norm(o[NA_HEADS + 2 * j], o[NA_HEADS + 2 * j + 1], lam, gain, lam_init)
              for j in range(DA_HEADS)]
        o_ref[rows, :] = jnp.concatenate(o[:NA_HEADS] + [conv] + da, axis=-1).astype(BF16)


def _ctx_mix(u, da_lambda, da_norm_g, conv_w, conv_b, l, lam_init):
    step_rows = CTX_SEQ_PER_STEP * SEQ
    return pl.pallas_call(
        functools.partial(_ctx_mix_kernel, lam_init=lam_init),
        grid=(BATCH // CTX_SEQ_PER_STEP,),
        in_specs=[
            pl.BlockSpec((step_rows, IN_WIDTH), lambda b: (b, 0)),
            pl.BlockSpec((None, 4, DA_DIM), lambda b: (l, 0, 0)),
            pl.BlockSpec((None, 1, 2 * DA_DIM), lambda b: (l, 0, 0)),
            pl.BlockSpec((None, 3, SC_WIDTH), lambda b: (l, 0, 0)),
            pl.BlockSpec((None, 1, SC_WIDTH), lambda b: (l, 0, 0)),
        ],
        out_specs=pl.BlockSpec((step_rows, MIX_WIDTH), lambda b: (b, 0)),
        out_shape=jax.ShapeDtypeStruct((BATCH * SEQ, MIX_WIDTH), BF16),
        compiler_params=_params("parallel"),
        name="mix_context",
    )(u, da_lambda, da_norm_g.reshape(DEPTH, 1, 2 * DA_DIM), conv_w, conv_b.reshape(DEPTH, 1, SC_WIDTH))


def _lat_mix_kernel(uq_ref, nak_ref, nav_ref, sc_ref, dakv_ref, cnak_ref, cnav_ref, cdak_ref, cdav_ref,
                    bias_ref, lam_ref, gain_ref, cw_ref, cb_ref, o_ref, kall_ref, vall_ref, nav1_ref, *, lam_init):
    qt = pl.program_id(1)
    n_qt = pl.num_programs(1)
    lam = _lambda(lam_ref, lam_init)
    gain = gain_ref[...]

    @pl.when(qt == 0)
    def _():
        kall_ref[:DEC_SEQ, :] = dakv_ref[:, :DA_QK_WIDTH]
        kall_ref[DEC_SEQ:, :] = cdak_ref[...]
        for j in range(DA_HEADS):
            vs = slice(j * 2 * DA_DIM, (j + 1) * 2 * DA_DIM)
            cols = slice(j * 4 * DA_DIM, (j + 1) * 4 * DA_DIM)
            vall_ref[:DEC_SEQ, cols] = _with_ones(dakv_ref[:, DA_QK_WIDTH + j * 2 * DA_DIM:DA_QK_WIDTH + (j + 1) * 2 * DA_DIM])
            vall_ref[DEC_SEQ:, cols] = _with_ones(cdav_ref[:, vs])
        for h in range(NA_HEADS):
            hs, cols = slice(h * HEAD_DIM, (h + 1) * HEAD_DIM), slice(h * 2 * HEAD_DIM, (h + 1) * 2 * HEAD_DIM)
            nav1_ref[:DEC_SEQ, cols] = _with_ones(nav_ref[:, hs])
            nav1_ref[DEC_SEQ:, cols] = _with_ones(cnav_ref[:, hs])

    outs = []
    key_row0 = jnp.clip(qt * QT_ROWS - NA_WIN_ROWS // 2, 0, GRID_ROWS - NA_KEY_ROWS)
    k0 = pl.multiple_of(key_row0 * GRID_W, GRID_W)
    jobs = []
    for h in range(NA_HEADS):
        hs = slice(h * HEAD_DIM, (h + 1) * HEAD_DIM)
        q = uq_ref[:, OFF_NA_Q + h * HEAD_DIM:OFF_NA_Q + (h + 1) * HEAD_DIM]
        k_loc = nak_ref[pl.ds(k0, NA_KEYS), hs]
        cols = slice(h * 2 * HEAD_DIM, (h + 1) * 2 * HEAD_DIM)
        v_ones = jnp.concatenate([nav1_ref[pl.ds(k0, NA_KEYS), cols], nav1_ref[DEC_SEQ:, cols]], axis=0)
        s_loc = _dot_nt(q, k_loc) + bias_ref[h]
        s_ctx = _dot_nt(q, cnak_ref[:, hs])
        jobs.append(([s_loc, s_ctx], v_ones, HEAD_DIM))
    outs += _softmax_pv_each(jobs)
    t0 = pl.multiple_of(qt * QT, QT)
    halo = BF16_ROWS
    before = sc_ref[pl.ds(pl.multiple_of(jnp.maximum(t0 - halo, 0), halo), halo), :].astype(F32)
    after = sc_ref[pl.ds(pl.multiple_of(jnp.minimum(t0 + QT, DEC_SEQ - halo), halo), halo), :].astype(F32)
    cur = sc_ref[pl.ds(t0, QT), :].astype(F32)
    v_prev = before[halo - 1:halo, SC_WIDTH:2 * SC_WIDTH] * before[halo - 1:halo, 2 * SC_WIDTH:]
    v_next = after[0:1, SC_WIDTH:2 * SC_WIDTH] * after[0:1, 2 * SC_WIDTH:]
    v_prev = jnp.where(qt > 0, v_prev, 0.0)
    v_next = jnp.where(qt < n_qt - 1, v_next, 0.0)
    outs.append(_gated_conv(cur[:, :SC_WIDTH], cur[:, SC_WIDTH:2 * SC_WIDTH] * cur[:, 2 * SC_WIDTH:],
                            v_prev, v_next, cw_ref[...], cb_ref[...]))
    for j0 in range(0, DA_HEADS, DA_HEADS_PER_STAGE):
        jobs = []
        for j in range(j0, j0 + DA_HEADS_PER_STAGE):
            v_ones = vall_ref[:, j * 4 * DA_DIM:(j + 1) * 4 * DA_DIM]
            for mi in (2 * j, 2 * j + 1):
                q = uq_ref[:, OFF_DA_Q + mi * DA_DIM:OFF_DA_Q + (mi + 1) * DA_DIM]
                jobs.append(([_dot_nt(q, kall_ref[:, mi * DA_DIM:(mi + 1) * DA_DIM])], v_ones, 2 * DA_DIM))
        o = _softmax_pv_each(jobs)
        for t in range(DA_HEADS_PER_STAGE):
            outs.append(_diff_head_norm(o[2 * t], o[2 * t + 1], lam, gain, lam_init))
    o_ref[...] = jnp.concatenate(outs, axis=-1).astype(BF16)


def _lat_mix(u, caches, bias_tab, da_lambda, da_norm_g, conv_w, conv_b, l, lam_init):
    cnak, cnav, cdak, cdav = caches
    n_qt = DEC_SEQ // QT
    q0 = BATCH * SEQ // QT
    b0 = BATCH * SEQ // DEC_SEQ

    def bias_map(b, qt):
        return (l, 0, jnp.where(qt == 0, 0, jnp.where(qt == n_qt - 1, 2, 1)), 0, 0)

    return pl.pallas_call(
        functools.partial(_lat_mix_kernel, lam_init=lam_init),
        grid=(DEC_BATCH, n_qt),
        in_specs=[
            pl.BlockSpec((QT, IN_WIDTH), lambda b, qt: (q0 + b * n_qt + qt, 0)),
            pl.BlockSpec((DEC_SEQ, NA_WIDTH), lambda b, qt: (b0 + b, OFF_NA_K // NA_WIDTH)),
            pl.BlockSpec((DEC_SEQ, NA_WIDTH), lambda b, qt: (b0 + b, OFF_NA_V // NA_WIDTH)),
            pl.BlockSpec((DEC_SEQ, 3 * SC_WIDTH), lambda b, qt: (b0 + b, OFF_SC_B // (3 * SC_WIDTH))),
            pl.BlockSpec((DEC_SEQ, DA_QK_WIDTH + DA_V_WIDTH), lambda b, qt: (b0 + b, OFF_DA_K // (DA_QK_WIDTH + DA_V_WIDTH))),
            pl.BlockSpec((None, None, PAST_LEN, NA_WIDTH), lambda b, qt: (b, l, 0, 0)),
            pl.BlockSpec((None, None, PAST_LEN, NA_WIDTH), lambda b, qt: (b, l, 0, 0)),
            pl.BlockSpec((None, None, PAST_LEN, DA_QK_WIDTH), lambda b, qt: (b, l, 0, 0)),
            pl.BlockSpec((None, None, PAST_LEN, DA_V_WIDTH), lambda b, qt: (b, l, 0, 0)),
            pl.BlockSpec((None, NA_HEADS, None, QT, NA_KEYS), bias_map),
            pl.BlockSpec((None, 4, DA_DIM), lambda b, qt: (l, 0, 0)),
            pl.BlockSpec((None, 1, 2 * DA_DIM), lambda b, qt: (l, 0, 0)),
            pl.BlockSpec((None, 3, SC_WIDTH), lambda b, qt: (l, 0, 0)),
            pl.BlockSpec((None, 1, SC_WIDTH), lambda b, qt: (l, 0, 0)),
        ],
        out_specs=pl.BlockSpec((QT, MIX_WIDTH), lambda b, qt: (b * n_qt + qt, 0)),
        out_shape=jax.ShapeDtypeStruct((DEC_BATCH * DEC_SEQ, MIX_WIDTH), BF16),
        scratch_shapes=[
            pltpu.VMEM((DEC_SEQ + PAST_LEN, DA_QK_WIDTH), BF16),
            pltpu.VMEM((DEC_SEQ + PAST_LEN, 2 * DA_V_WIDTH), BF16),
            pltpu.VMEM((DEC_SEQ + PAST_LEN, 2 * NA_WIDTH), BF16),
        ],
        compiler_params=_params("parallel", "arbitrary"),
        name="mix_latent",
    )(u, u, u, u, u, cnak, cnav, cdak, cdav, bias_tab, da_lambda,
      da_norm_g.reshape(DEPTH, 1, 2 * DA_DIM), conv_w, conv_b.reshape(DEPTH, 1, SC_WIDTH))


def _first_max(vals):
    m = functools.reduce(jnp.maximum, vals)
    hot, taken = [], None
    for v in vals:
        is_max = v == m
        if taken is None:
            hot.append(is_max)
            taken = is_max
        else:
            hot.append(is_max & ~taken)
            taken = taken | is_max
    return hot, m


def _pick(hot, vals):
    out = vals[-1]
    for h, v in zip(hot[-2::-1], vals[-2::-1]):
        out = jnp.where(h, v, out)
    return out


def _route(h2, h2_bf, wr_ref, br_ref, earlier_ref):
    tm = h2.shape[0]
    w = wr_ref[...]
    w_hi = w.astype(BF16)
    w_lo = (w - w_hi.astype(F32)).astype(BF16)
    h_lo = (h2 - h2_bf.astype(F32)).astype(BF16)
    both = _dot_nt(jnp.concatenate([w_hi, w_lo], axis=0), h2_bf)
    z = both[:N_EXPERTS] + both[N_EXPERTS:] + _dot_nt(w_hi, h_lo)
    scores = jax.nn.sigmoid(z)
    biased = scores + br_ref[...]
    P = [biased[k * N_GROUPS:(k + 1) * N_GROUPS] for k in range(PER_GROUP)]
    S = [scores[k * N_GROUPS:(k + 1) * N_GROUPS] for k in range(PER_GROUP)]
    pair_sums = [P[i] + P[j] for i in range(PER_GROUP) for j in range(i + 1, PER_GROUP)]
    group_score = functools.reduce(jnp.maximum, pair_sums)
    sel, _ = _first_max([group_score[g:g + 1] for g in range(N_GROUPS)])
    c = [_pick(sel, [P[k][g:g + 1] for g in range(N_GROUPS)]) for k in range(PER_GROUP)]
    cs = [_pick(sel, [S[k][g:g + 1] for g in range(N_GROUPS)]) for k in range(PER_GROUP)]
    t1, _ = _first_max(c)
    t2, _ = _first_max([jnp.where(t, -jnp.inf, v) for t, v in zip(t1, c)])
    w1 = functools.reduce(jnp.add, [jnp.where(t, v, 0.0) for t, v in zip(t1, cs)])
    w2 = functools.reduce(jnp.add, [jnp.where(t, v, 0.0) for t, v in zip(t2, cs)])
    total = w1 + w2
    slot_gate = [jnp.where(a, w1 / total, jnp.where(b, w2 / total, 0.0)) for a, b in zip(t1, t2)]
    onehot = jnp.concatenate([s.astype(F32) for s in sel] + [jnp.zeros((SUBLANES - N_GROUPS, tm), F32)], axis=0)
    rank = _dot(onehot.astype(BF16), earlier_ref[...])
    count = jnp.sum(onehot, axis=1, keepdims=True)
    n_chunks = jnp.floor((count + (MOE_CHUNK - 1)) * (1.0 / MOE_CHUNK))
    pos = jnp.zeros((1, tm), F32)
    start = jnp.zeros((1, 1), F32)
    for g in range(N_GROUPS):
        pos = jnp.where(sel[g], start + rank[g:g + 1], pos)
        start = start + n_chunks[g:g + 1] * MOE_CHUNK
    rows = [pos] + slot_gate + [jnp.zeros((LANES - 1 - PER_GROUP, tm), F32)]
    return jnp.concatenate(rows, axis=0), n_chunks


def _post_kernel(mixp_ref, mixs_ref, xp_ref, xs_ref, mod_ref, wout_ref, g_ref, b_ref, wr_ref, br_ref,
                 x1_ref, h2_ref, tok_ref, pos_ref, nch_ref, wbf_ref, earlier_ref):
    i = pl.program_id(0)
    is_ctx = i < CTX_TILES

    @pl.when(i == 0)
    def _():
        wbf_ref[...] = wout_ref[...].astype(BF16)
        earlier_ref[...] = (lax.broadcasted_iota(jnp.int32, (TM_POST, TM_POST), 0)
                            < lax.broadcasted_iota(jnp.int32, (TM_POST, TM_POST), 1)).astype(BF16)

    mod = mod_ref[...]
    n = TM_POST // POST_SPLIT
    parts = [slice(p * n, (p + 1) * n) for p in range(POST_SPLIT)]
    ys = [_dot(jnp.where(is_ctx, mixp_ref[p, :], mixs_ref[p, :]), wbf_ref[...]) for p in parts]
    xs = [jnp.where(is_ctx, xp_ref[p, :], xs_ref[p, :]) for p in parts]
    x1s = [t * g_ref[...] + b_ref[...] for t in _ln_each([ALPHA * x + mod[2:3] * y for x, y in zip(xs, ys)])]
    h2s = [t * (1.0 + mod[4:5]) + mod[3:4] for t in _ln_each(x1s)]
    for p, x1 in zip(parts, x1s):
        x1_ref[p, :] = x1
    h2 = jnp.concatenate(h2s, axis=0)
    h2_bf = h2.astype(BF16)
    h2_ref[...] = h2_bf
    rows, n_chunks = _route(h2, h2_bf, wr_ref, br_ref, earlier_ref)
    tok_ref[...] = rows.T
    pos_ref[...] = rows[0:SUBLANES]
    nch_ref[...] = jnp.broadcast_to(n_chunks, (SUBLANES, LANES)).astype(jnp.int32)


def _post(mix_p, mix_s, xp, xs, mod_l, w_out, ln_g, ln_b, wr_t, br_t, l):
    n = xp.shape[0] + xs.shape[0]
    route = functools.partial(_tile_stream, n_ctx_tiles=CTX_TILES, lat_tiles_per_batch=DEC_SEQ // TM_POST)
    return pl.pallas_call(
        _post_kernel,
        grid=(n // TM_POST,),
        in_specs=[
            pl.BlockSpec((TM_POST, MIX_WIDTH), lambda i: (route(i)[0], 0)),
            pl.BlockSpec((TM_POST, MIX_WIDTH), lambda i: (route(i)[1], 0)),
            pl.BlockSpec((TM_POST, D_MODEL), lambda i: (route(i)[0], 0)),
            pl.BlockSpec((TM_POST, D_MODEL), lambda i: (route(i)[1], 0)),
            pl.BlockSpec((None, 6, D_MODEL), lambda i: (route(i)[2], 0, 0)),
            pl.BlockSpec((None, MIX_WIDTH, D_MODEL), lambda i: (l, 0, 0), pipeline_mode=pl.Buffered(1)),
            pl.BlockSpec((None, 1, D_MODEL), lambda i: (l, 0, 0)),
            pl.BlockSpec((None, 1, D_MODEL), lambda i: (l, 0, 0)),
            pl.BlockSpec((N_EXPERTS, D_MODEL), lambda i: (0, 0)),
            pl.BlockSpec((N_EXPERTS, 1), lambda i: (0, 0)),
        ],
        out_specs=[
            pl.BlockSpec((TM_POST, D_MODEL), lambda i: (i, 0)),
            pl.BlockSpec((TM_POST, D_MODEL), lambda i: (i, 0)),
            pl.BlockSpec((TM_POST, LANES), lambda i: (i, 0)),
            pl.BlockSpec((None, SUBLANES, TM_POST), lambda i: (i, 0, 0)),
            pl.BlockSpec((None, SUBLANES, LANES), lambda i: (i, 0, 0)),
        ],
        out_shape=[
            jax.ShapeDtypeStruct((n, D_MODEL), F32),
            jax.ShapeDtypeStruct((n, D_MODEL), BF16),
            jax.ShapeDtypeStruct((n, LANES), F32),
            jax.ShapeDtypeStruct((n // TM_POST, SUBLANES, TM_POST), F32),
            jax.ShapeDtypeStruct((n // TM_POST, SUBLANES, LANES), jnp.int32),
        ],
        scratch_shapes=[pltpu.VMEM((MIX_WIDTH, D_MODEL), BF16), pltpu.VMEM((TM_POST, TM_POST), BF16)],
        compiler_params=_params("arbitrary"),
        name="post",
    )(mix_p, mix_s, xp, xs, mod_l, w_out, ln_g.reshape(DEPTH, 1, D_MODEL), ln_b.reshape(DEPTH, 1, D_MODEL), wr_t, br_t)


def _moe_kernel(nch_ref, h_ref, tokn_ref, pos_ref, tok_ref, x1_ref, mod_ref, w1f_ref, w3f_ref, w2f_ref, g_ref, b_ref,
                op_ref, os_ref, w13_ref, w2_ref, hs_ref, gs_ref, ys_ref):
    step = pl.program_id(0)
    half = PER_GROUP * D_EXPERT

    for k in range(PER_GROUP):
        @pl.when((step < N_EXPERTS) & (step % PER_GROUP == k))
        def _(k=k):
            g = step // PER_GROUP
            w13_ref[g, :, k * D_EXPERT:(k + 1) * D_EXPERT] = w1f_ref[...].astype(BF16)
            w13_ref[g, :, half + k * D_EXPERT:half + (k + 1) * D_EXPERT] = w3f_ref[...].astype(BF16)
            w2_ref[g, k * D_EXPERT:(k + 1) * D_EXPERT, :] = w2f_ref[...].astype(BF16)

    @pl.when(step == N_EXPERTS - 1)
    def _():
        hs, gs = _moe_sort(h_ref, tokn_ref, pos_ref)
        hs_ref[...] = hs
        gs_ref[...] = gs
        ys_ref[...] = jnp.zeros_like(ys_ref)

    @pl.when(step >= N_EXPERTS)
    def _():
        i = step - N_EXPERTS
        _moe_groups(i, nch_ref, w13_ref, w2_ref, hs_ref, gs_ref, ys_ref)
        hs, gs = _moe_sort(h_ref, tokn_ref, pos_ref)
        pos_col = tok_ref[:, 0:1]
        unsort = (lax.broadcasted_iota(jnp.int32, (TM_MOE, MOE_SORTED), 1).astype(F32) == pos_col).astype(BF16)
        y = _dot(unsort, ys_ref[...])
        hs_ref[...] = hs
        gs_ref[...] = gs
        ys_ref[...] = jnp.zeros_like(ys_ref)
        mod = mod_ref[...]
        out = _ln(ALPHA * x1_ref[...] + mod[5:6] * y) * g_ref[...] + b_ref[...]

        @pl.when(i < CTX_TILES)
        def _():
            op_ref[...] = out

        @pl.when(i >= CTX_TILES)
        def _():
            os_ref[...] = out


def _moe_sort(h_ref, tok_ref, pos_ref):
    tok = tok_ref[...]
    slot = lax.broadcasted_iota(jnp.int32, (MOE_SORTED, TM_MOE), 0).astype(F32)
    sort = (slot == pos_ref[0:1, :]).astype(BF16)
    tok_hi = tok.astype(BF16)
    tok_lo = (tok - tok_hi.astype(F32)).astype(BF16)
    z = _dot(sort, jnp.concatenate([h_ref[...], tok_hi, tok_lo], axis=1))
    return z[:, :D_MODEL].astype(BF16), z[:, D_MODEL:D_MODEL + LANES] + z[:, D_MODEL + LANES:]


def _moe_groups(i, nch_ref, w13_ref, w2_ref, hs_ref, gs_ref, ys_ref):
    def experts(g, chunk0, n_rows):
        r0 = pl.multiple_of(chunk0 * MOE_CHUNK, MOE_CHUNK)
        rows = hs_ref[pl.ds(r0, n_rows), :]
        gates = gs_ref[pl.ds(r0, n_rows), :]
        half = PER_GROUP * D_EXPERT
        ab = _dot(rows, w13_ref[g])
        hid = []
        for k in range(PER_GROUP):
            a = ab[:, k * D_EXPERT:(k + 1) * D_EXPERT]
            b = ab[:, half + k * D_EXPERT:half + (k + 1) * D_EXPERT]
            hid.append((a * jax.nn.sigmoid(a) * b * gates[:, 1 + k:2 + k]).astype(BF16))
        ys_ref[pl.ds(r0, n_rows), :] = _dot(jnp.concatenate(hid, axis=1), w2_ref[g]).astype(BF16)

    def group(g, first):
        n = nch_ref[i * N_GROUPS + g]
        n_blocks = n // MOE_BLOCK
        rest = n - n_blocks * MOE_BLOCK

        def block(c, carry):
            experts(g, first + c * MOE_BLOCK, MOE_BLOCK * MOE_CHUNK)
            return carry

        lax.fori_loop(0, n_blocks, block, 0)
        for m in range(1, MOE_BLOCK):
            @pl.when(rest == m)
            def _(m=m):
                experts(g, first + n_blocks * MOE_BLOCK, m * MOE_CHUNK)
        return first + n

    lax.fori_loop(0, N_GROUPS, group, 0)


def _moe(h2, tok, pos, nch, x1, mod_l, w1, w3, w2, ln_g, ln_b, l):
    n_tiles = x1.shape[0] // TM_MOE
    n_ctx = CTX_TILES * TM_MOE
    per_batch = DEC_SEQ // TM_MOE

    def tile(step):
        return jnp.maximum(step - N_EXPERTS, 0)

    def next_tile(step):
        return jnp.clip(step - (N_EXPERTS - 1), 0, n_tiles - 1)

    def expert(step):
        return jnp.minimum(step, N_EXPERTS - 1)

    def mod_map(step, s):
        return (_tile_stream(tile(step), CTX_TILES, per_batch)[2], 0, 0)

    grid_spec = pltpu.PrefetchScalarGridSpec(
        num_scalar_prefetch=1,
        grid=(N_EXPERTS + n_tiles,),
        in_specs=[
            pl.BlockSpec((TM_MOE, D_MODEL), lambda step, s: (next_tile(step), 0)),
            pl.BlockSpec((TM_MOE, LANES), lambda step, s: (next_tile(step), 0)),
            pl.BlockSpec((None, SUBLANES, TM_MOE), lambda step, s: (next_tile(step), 0, 0)),
            pl.BlockSpec((TM_MOE, LANES), lambda step, s: (tile(step), 0)),
            pl.BlockSpec((TM_MOE, D_MODEL), lambda step, s: (tile(step), 0)),
            pl.BlockSpec((None, 6, D_MODEL), mod_map),
            pl.BlockSpec((None, None, D_MODEL, D_EXPERT), lambda step, s: (l, expert(step), 0, 0)),
            pl.BlockSpec((None, None, D_MODEL, D_EXPERT), lambda step, s: (l, expert(step), 0, 0)),
            pl.BlockSpec((None, None, D_EXPERT, D_MODEL), lambda step, s: (l, expert(step), 0, 0)),
            pl.BlockSpec((None, 1, D_MODEL), lambda step, s: (l, 0, 0)),
            pl.BlockSpec((None, 1, D_MODEL), lambda step, s: (l, 0, 0)),
        ],
        out_specs=[
            pl.BlockSpec((TM_MOE, D_MODEL), lambda step, s: (jnp.minimum(tile(step), CTX_TILES - 1), 0)),
            pl.BlockSpec((TM_MOE, D_MODEL), lambda step, s: (jnp.maximum(tile(step) - CTX_TILES, 0), 0)),
        ],
        scratch_shapes=[
            pltpu.VMEM((N_GROUPS, D_MODEL, 2 * PER_GROUP * D_EXPERT), BF16),
            pltpu.VMEM((N_GROUPS, PER_GROUP * D_EXPERT, D_MODEL), BF16),
            pltpu.VMEM((MOE_SORTED, D_MODEL), BF16),
            pltpu.VMEM((MOE_SORTED, LANES), F32),
            pltpu.VMEM((MOE_SORTED, D_MODEL), BF16),
        ],
    )
    return pl.pallas_call(
        _moe_kernel,
        grid_spec=grid_spec,
        out_shape=[jax.ShapeDtypeStruct((n_ctx, D_MODEL), F32),
                   jax.ShapeDtypeStruct((x1.shape[0] - n_ctx, D_MODEL), F32)],
        compiler_params=_params("arbitrary"),
        name="moe",
    )(nch[:, :N_GROUPS, 0].reshape(-1), h2, tok, pos, tok, x1, mod_l, w1, w3, w2,
      ln_g.reshape(DEPTH, 1, D_MODEL), ln_b.reshape(DEPTH, 1, D_MODEL))


def _rope_tables():
    t = np.arange(DEC_SEQ)
    half = DA_DIM // 4
    inv_freq = ROPE_BASE ** (-np.arange(half, dtype=np.float32) / half)
    cos, sin = [], []
    for pos in (t // GRID_W, t % GRID_W):
        ang = pos.astype(np.float32)[:, None] * inv_freq[None, :]
        c, s = np.cos(ang), np.sin(ang)
        cos += [c, c]
        sin += [-s, s]
    cos = np.tile(np.concatenate(cos, axis=1), (1, 2 * DA_HEADS))
    sin = np.tile(np.concatenate(sin, axis=1), (1, 2 * DA_HEADS))
    cos = np.concatenate([np.ones((TM, DA_QK_WIDTH), np.float32), cos], axis=0)
    sin = np.concatenate([np.zeros((TM, DA_QK_WIDTH), np.float32), sin], axis=0)
    return jnp.asarray(cos, F32), jnp.asarray(sin, F32)


_NA_TILE_KINDS = ((0, 0), (2 * QT_ROWS, 2 * QT_ROWS - NA_WIN_ROWS // 2), (GRID_ROWS - QT_ROWS, GRID_ROWS - NA_KEY_ROWS))


def _na_bias_kernel(by_ref, o_ref):
    qc = lax.broadcasted_iota(jnp.int32, (GRID_W, GRID_W), 0)
    kc = lax.broadcasted_iota(jnp.int32, (GRID_W, GRID_W), 1)
    win_c0 = jnp.clip(qc - NA_WIN_COLS // 2, 0, GRID_W - NA_WIN_COLS)
    in_cols = (kc >= win_c0) & (kc < win_c0 + NA_WIN_COLS)
    masked = jnp.full((GRID_W, GRID_W), -jnp.inf, F32)
    for kind, (r0, key_row0) in enumerate(_NA_TILE_KINDS):
        for a in range(QT_ROWS):
            qr = r0 + a
            win_r0 = min(max(qr - NA_WIN_ROWS // 2, 0), GRID_ROWS - NA_WIN_ROWS)
            blocks = []
            for b in range(NA_KEY_ROWS):
                kr = key_row0 + b
                if win_r0 <= kr < win_r0 + NA_WIN_ROWS:
                    blocks.append(jnp.where(in_cols, by_ref[kr - qr + NA_WIN_ROWS - 1] * LOG2E, masked))
                else:
                    blocks.append(masked)
            o_ref[kind, a * GRID_W:(a + 1) * GRID_W, :] = jnp.concatenate(blocks, axis=1)


def _na_bias_tables(na_rel_bias):
    n_dr, n_dc = 2 * NA_WIN_ROWS - 1, 2 * NA_WIN_COLS - 1
    cols = np.arange(GRID_W)
    d_col = np.clip(cols[None, :] - cols[:, None], 1 - NA_WIN_COLS, NA_WIN_COLS - 1) + NA_WIN_COLS - 1
    col_sel = (d_col[None] == np.arange(n_dc)[:, None, None]).astype(np.float32)
    by_col = jnp.einsum('lhab,bqc->lhaqc', na_rel_bias.astype(F32), col_sel, precision=lax.Precision.HIGHEST)
    return pl.pallas_call(
        _na_bias_kernel,
        grid=(DEPTH, NA_HEADS),
        in_specs=[pl.BlockSpec((None, None, n_dr, GRID_W, GRID_W), lambda l, h: (l, h, 0, 0, 0))],
        out_specs=pl.BlockSpec((None, None, len(_NA_TILE_KINDS), QT, NA_KEYS), lambda l, h: (l, h, 0, 0, 0)),
        out_shape=jax.ShapeDtypeStruct((DEPTH, NA_HEADS, len(_NA_TILE_KINDS), QT, NA_KEYS), F32),
        compiler_params=_params("parallel", "parallel"),
        name="na_bias_table",
    )(by_col)


def kernel(x_prompt, x_sample, cache_na_k, cache_na_v, cache_da_k, cache_da_v, c, c_ctx, w_mod, b_mod, w_in,
           na_rel_bias, sc_conv_w, sc_conv_b, da_lambda, da_norm_g, w_out, ln1_g, ln1_b, w_router, b_router,
           moe_w1, moe_w3, moe_w2, ln2_g, ln2_b):
    xp = x_prompt.reshape(BATCH * SEQ, D_MODEL)
    xs = x_sample.reshape(DEC_BATCH * DEC_SEQ, D_MODEL)

    cond = jnp.concatenate([c_ctx[None, :], c, jnp.zeros((N_COND - 1 - DEC_BATCH, D_MODEL), F32)], axis=0)
    mod = _modulation(cond, w_mod, b_mod).reshape(DEPTH, N_COND, 6, D_MODEL)

    slot_major = np.arange(N_EXPERTS).reshape(N_GROUPS, PER_GROUP).T.reshape(-1)
    wr_t = w_router.T[slot_major]
    br_t = b_router.astype(F32)[slot_major].reshape(N_EXPERTS, 1)

    caches = (
        cache_na_k.reshape(DEC_BATCH, DEPTH, PAST_LEN, NA_WIDTH).astype(BF16),
        cache_na_v.reshape(DEC_BATCH, DEPTH, PAST_LEN, NA_WIDTH).astype(BF16),
        cache_da_k.reshape(DEC_BATCH, DEPTH, PAST_LEN, DA_QK_WIDTH).astype(BF16),
        cache_da_v.reshape(DEC_BATCH, DEPTH, PAST_LEN, DA_V_WIDTH).astype(BF16),
    )
    rope_tabs = _rope_tables()
    bias_tab = _na_bias_tables(na_rel_bias)

    new_caches = None
    for l in range(DEPTH):
        lam_init = 0.8 - 0.6 * math.exp(-0.3 * l)
        mod_l = mod[l]
        u, *new_caches = _inproj(xp, xs, mod_l, w_in, l, rope_tabs, cache_bufs=new_caches)
        mix_p = _ctx_mix(u, da_lambda, da_norm_g, sc_conv_w, sc_conv_b, l, lam_init)
        mix_s = _lat_mix(u, caches, bias_tab, da_lambda, da_norm_g, sc_conv_w, sc_conv_b, l, lam_init)
        x1, h2, *route = _post(mix_p, mix_s, xp, xs, mod_l, w_out, ln1_g, ln1_b, wr_t, br_t, l)
        xp, xs = _moe(h2, *route, x1, mod_l, moe_w1, moe_w3, moe_w2, ln2_g, ln2_b, l)

    nak, nav, dak, dav = new_caches

    def token_major(t, heads, dim):
        return t.reshape(BATCH, DEPTH, heads, dim, SEQ).transpose(0, 1, 4, 2, 3)

    return (xp.reshape(BATCH, SEQ, D_MODEL), xs.reshape(DEC_BATCH, DEC_SEQ, D_MODEL),
            token_major(nak, NA_HEADS, HEAD_DIM), token_major(nav, NA_HEADS, HEAD_DIM),
            token_major(dak, 2 * DA_HEADS, DA_DIM), dav.reshape(BATCH, DEPTH, SEQ, DA_HEADS, 2 * DA_DIM))
```
